```python
import jax, jax.numpy as jnp
from jax import lax
import numpy as np

D_MODEL = 2048
BATCH = 16
SEQ = 256
DEPTH = 2
DEC_BATCH = 4
DEC_SEQ = 4096
PAST_LEN = 512

GRID_W = 64
GROUP_W = D_MODEL // 4
MIX_W = 4 * GROUP_W
MLA_HEADS = 4
MLA_NOPE = 128
MLA_ROPE = 64
MLA_V = 128
MLA_QK = MLA_NOPE + MLA_ROPE
Q_LORA = 448
KV_LORA = 128
ROPE_THETA = 10000.0
ROPE_AXIS = MLA_ROPE // 2
Q_BLOCK = 128
CONV_W = GROUP_W
CONV_K = 31
GMLP_W = GROUP_W
GMLP_HEADS = 4
GMLP_HC = GMLP_W // GMLP_HEADS
CHUNK = 128
SC_W = GROUP_W
SC_K = 3
IN_COLS = Q_LORA + KV_LORA + MLA_ROPE + 2 * CONV_W + 2 * GMLP_W + 3 * SC_W
N_EXPERTS = 32
N_GROUPS = 8
EXPERTS_PER_GROUP = N_EXPERTS // N_GROUPS
TOP_K = 2
D_EXPERT = 512
MOE_BLOCK = 128
NORM_EPS = 1e-6

kernel_name = 'hybrid_mla_conformer_gmlp_shortconv_moe_dit'


def rmsnorm(x, g):
    xf = x.astype(jnp.float32)
    y = xf * lax.rsqrt(jnp.mean(xf * xf, axis=-1, keepdims=True) + NORM_EPS)
    return (y * g.astype(jnp.float32)).astype(x.dtype)


def layernorm(x, g, b):
    xf = x.astype(jnp.float32)
    mu = jnp.mean(xf, axis=-1, keepdims=True)
    var = jnp.mean(jnp.square(xf - mu), axis=-1, keepdims=True)
    y = (xf - mu) * lax.rsqrt(var + NORM_EPS)
    return (y * g.astype(jnp.float32) + b.astype(jnp.float32)).astype(x.dtype)


def axial_angles(n_tokens):
    rows = n_tokens // GRID_W
    row = jnp.repeat(jnp.arange(rows, dtype=jnp.float32), GRID_W)
    col = jnp.tile(jnp.arange(GRID_W, dtype=jnp.float32), rows)
    n_freq = ROPE_AXIS // 2
    inv = ROPE_THETA ** (-jnp.arange(n_freq, dtype=jnp.float32) / n_freq)
    return row[:, None] * inv[None, :], col[:, None] * inv[None, :]


def rotate(x, ang):
    cos = jnp.cos(ang)[None, :, None, :]
    sin = jnp.sin(ang)[None, :, None, :]
    x1, x2 = jnp.split(x, 2, axis=-1)
    return jnp.concatenate([x1 * cos - x2 * sin, x2 * cos + x1 * sin], axis=-1)


def rope_tail(x, ang_row, ang_col):
    x_nope = x[..., :MLA_NOPE]
    x_row, x_col = jnp.split(x[..., MLA_NOPE:].astype(jnp.float32), 2, axis=-1)
    return jnp.concatenate([x_nope, rotate(x_row, ang_row).astype(x.dtype),
                            rotate(x_col, ang_col).astype(x.dtype)], axis=-1)


def attention(q, k, v):
    b, s, h, d = q.shape
    nb = s // Q_BLOCK
    qb = jnp.moveaxis(q.reshape(b, nb, Q_BLOCK, h, d), 1, 0)
    scale = MLA_QK ** -0.5

    def block(qi):
        sc = jnp.einsum('bqhd,bkhd->bhqk', qi, k).astype(jnp.float32) * scale
        pr = jax.nn.softmax(sc, axis=-1).astype(v.dtype)
        return jnp.einsum('bhqk,bkhd->bqhd', pr, v)

    o = lax.map(block, qb)
    return jnp.moveaxis(o, 0, 1).reshape(b, s, h, v.shape[-1])


def dwconv(x, w):
    pad = w.shape[0] // 2
    return lax.conv_general_dilated(
        x, w[:, None, :].astype(x.dtype), window_strides=(1,), padding=((pad, pad),),
        dimension_numbers=('NWC', 'WIO', 'NWC'), feature_group_count=x.shape[-1])


def mla_keys(ckv, krope, p):
    b, s, _ = ckv.shape
    kv = (ckv @ p['w_ukv']).reshape(b, s, MLA_HEADS, MLA_NOPE + MLA_V)
    k_nope, v = jnp.split(kv, [MLA_NOPE], axis=-1)
    k_rope = jnp.broadcast_to(krope[:, :, None, :], (b, s, MLA_HEADS, MLA_ROPE)).astype(k_nope.dtype)
    k = rmsnorm(jnp.concatenate([k_nope, k_rope], axis=-1), p['g_qk_k'])
    return k, v


def token_mixers(h, p, ctx_ckv, ctx_krope):
    b, s, _ = h.shape
    proj = h @ p['w_in']
    o1 = Q_LORA
    o2 = o1 + KV_LORA
    o3 = o2 + MLA_ROPE
    o4 = o3 + 2 * CONV_W
    o5 = o4 + 2 * GMLP_W
    cq, ckv, krope, conv_in, gmlp_in, sc_in = jnp.split(proj, [o1, o2, o3, o4, o5], axis=-1)

    q = (rmsnorm(cq, p['g_q_lora']) @ p['w_uq']).reshape(b, s, MLA_HEADS, MLA_QK)
    q = rmsnorm(q, p['g_qk_q'])
    ckv = rmsnorm(ckv, p['g_kv_lora'])
    k, v = mla_keys(ckv, krope, p)
    if ctx_ckv is None:
        o_a = attention(q, k, v)
    else:
        ang_row, ang_col = axial_angles(s)
        q = rope_tail(q, ang_row, ang_col)
        k = rope_tail(k, ang_row, ang_col)
        k_ctx, v_ctx = mla_keys(ctx_ckv.astype(h.dtype), ctx_krope.astype(h.dtype), p)
        o_a = attention(q, jnp.concatenate([k_ctx, k], axis=1), jnp.concatenate([v_ctx, v], axis=1))
    o_a = o_a.reshape(b, s, MLA_HEADS * MLA_V)

    a, ga = jnp.split(conv_in, 2, axis=-1)
    z = a * jax.nn.sigmoid(ga)
    z = dwconv(z, p['w_dw31']) + p['b_dw31']
    z = jax.nn.silu(layernorm(z, p['g_conv_ln'], p['b_conv_ln']))
    o_b = z @ p['w_pw'] + p['b_pw']

    u, vg = jnp.split(gmlp_in, 2, axis=-1)
    vg = layernorm(vg, p['g_sgu_ln'], p['b_sgu_ln']).reshape(b, s // CHUNK, CHUNK, GMLP_HEADS, GMLP_HC)
    mixed = jnp.einsum('hpq,bnqhc->bnphc', p['w_spatial'], vg) + p['b_spatial'].T[None, None, :, :, None]
    o_c = u * mixed.reshape(b, s, GMLP_W)

    bg, cg, xg = jnp.split(sc_in, 3, axis=-1)
    o_d = bg * dwconv(cg * xg, p['w_sc3'])

    mix = jnp.concatenate([o_a, o_b, o_c, o_d], axis=-1) @ p['w_out']
    return mix, ckv, krope


def moe_ffn(h, router_w, router_b, w1, w3, w2):
    b, s, d = h.shape
    t = b * s
    xt = h.reshape(t, d)
    scores = jax.nn.sigmoid((xt @ router_w).astype(jnp.float32))
    sel = scores + router_b.astype(jnp.float32)
    grp = sel.reshape(t, N_GROUPS, EXPERTS_PER_GROUP)
    g_score = lax.top_k(grp, 2)[0].sum(axis=-1)
    g_best = jnp.argmax(g_score, axis=-1)
    in_grp = (jnp.arange(N_EXPERTS) // EXPERTS_PER_GROUP)[None, :] == g_best[:, None]
    _, eidx = lax.top_k(jnp.where(in_grp, sel, -jnp.inf), TOP_K)
    gate = jnp.take_along_axis(scores, eidx, axis=-1)
    gate = gate / jnp.sum(gate, axis=-1, keepdims=True)

    n_assign = t * TOP_K
    e_flat = eidx.reshape(n_assign)
    tok_flat = jnp.arange(n_assign) // TOP_K
    g_flat = gate.reshape(n_assign)
    order = jnp.argsort(e_flat)
    e_sorted = e_flat[order]
    counts = jnp.bincount(e_flat, length=N_EXPERTS)
    start = jnp.cumsum(counts) - counts
    padded = (counts + MOE_BLOCK - 1) // MOE_BLOCK * MOE_BLOCK
    pad_end = jnp.cumsum(padded)
    pad_start = pad_end - padded
    pos = pad_start[e_sorted] + (jnp.arange(n_assign) - start[e_sorted])
    n_buf = n_assign + N_EXPERTS * MOE_BLOCK
    n_blk = n_buf // MOE_BLOCK
    buf_tok = jnp.full((n_buf,), t, dtype=jnp.int32).at[pos].set(tok_flat[order].astype(jnp.int32))
    buf_gate = jnp.zeros((n_buf,), jnp.float32).at[pos].set(g_flat[order])
    blk_exp = jnp.minimum(jnp.searchsorted(pad_end, jnp.arange(n_blk) * MOE_BLOCK, side='right'), N_EXPERTS - 1)
    x_pad = jnp.concatenate([xt, jnp.zeros((1, d), xt.dtype)], axis=0)
    xb = x_pad[buf_tok].reshape(n_blk, MOE_BLOCK, d)

    def expert_block(args):
        xblk, e = args
        return (jax.nn.silu(xblk @ w1[e]) * (xblk @ w3[e])) @ w2[e]

    yb = lax.map(expert_block, (xb, blk_exp)).reshape(n_buf, d)
    y = jax.ops.segment_sum(yb * buf_gate[:, None].astype(yb.dtype), buf_tok, num_segments=t + 1)[:t]
    return y.reshape(b, s, d)


def trunk_layer(x, mod, p, router_w, router_b, ctx_ckv, ctx_krope):
    shift1, scale1, gate1, shift2, scale2, gate2 = [m[:, None, :] for m in jnp.split(mod.astype(x.dtype), 6, axis=-1)]
    h = rmsnorm(x, p['g_norm1']) * (1 + scale1) + shift1
    mix, ckv, krope = token_mixers(h, p, ctx_ckv, ctx_krope)
    x = x + gate1 * mix
    h = rmsnorm(x, p['g_norm2']) * (1 + scale2) + shift2
    x = x + gate2 * moe_ffn(h, router_w, router_b, p['w1'], p['w3'], p['w2'])
    return x, ckv, krope


def setup_inputs(seed: int = 0) -> dict:
    key = jax.random.key(seed)
    ks = iter(jax.random.split(key, 48))
    f32 = jnp.float32

    def nrm(shape, scale):
        return jax.random.normal(next(ks), shape, f32) * scale

    def gain(shape):
        return 1.0 + 0.05 * jax.random.normal(next(ks), shape, f32)

    def bias(shape):
        return 0.02 * jax.random.normal(next(ks), shape, f32)

    return {
        'x_prompt': nrm((BATCH, SEQ, D_MODEL), 1.0),
        'x_sample': nrm((DEC_BATCH, DEC_SEQ, D_MODEL), 1.0),
        'cache_mla_ckv': nrm((DEC_BATCH, DEPTH, PAST_LEN, KV_LORA), 1.0),
        'cache_mla_krope': nrm((DEC_BATCH, DEPTH, PAST_LEN, MLA_ROPE), 1.0),
        'c': nrm((DEC_BATCH, D_MODEL), 1.0),
        'c_ctx': nrm((D_MODEL,), 1.0),
        'w_mod': nrm((DEPTH, D_MODEL, 6 * D_MODEL), 0.5 * D_MODEL ** -0.5),
        'b_mod': bias((DEPTH, 6 * D_MODEL)),
        'g_norm1': gain((DEPTH, D_MODEL)),
        'g_norm2': gain((DEPTH, D_MODEL)),
        'w_in': nrm((DEPTH, D_MODEL, IN_COLS), D_MODEL ** -0.5),
        'g_q_lora': gain((DEPTH, Q_LORA)),
        'w_uq': nrm((DEPTH, Q_LORA, MLA_HEADS * MLA_QK), Q_LORA ** -0.5),
        'g_kv_lora': gain((DEPTH, KV_LORA)),
        'w_ukv': nrm((DEPTH, KV_LORA, MLA_HEADS * (MLA_NOPE + MLA_V)), KV_LORA ** -0.5),
        'g_qk_q': gain((DEPTH, MLA_QK)),
        'g_qk_k': gain((DEPTH, MLA_QK)),
        'w_dw31': nrm((DEPTH, CONV_K, CONV_W), CONV_K ** -0.5),
        'b_dw31': bias((DEPTH, CONV_W)),
        'g_conv_ln': gain((DEPTH, CONV_W)),
        'b_conv_ln': bias((DEPTH, CONV_W)),
        'w_pw': nrm((DEPTH, CONV_W, CONV_W), CONV_W ** -0.5),
        'b_pw': bias((DEPTH, CONV_W)),
        'g_sgu_ln': gain((DEPTH, GMLP_W)),
        'b_sgu_ln': bias((DEPTH, GMLP_W)),
        'w_spatial': nrm((DEPTH, GMLP_HEADS, CHUNK, CHUNK), CHUNK ** -0.5),
        'b_spatial': gain((DEPTH, GMLP_HEADS, CHUNK)),
        'w_sc3': nrm((DEPTH, SC_K, SC_W), SC_K ** -0.5),
        'w_out': nrm((DEPTH, MIX_W, D_MODEL), MIX_W ** -0.5),
        'router_w': nrm((D_MODEL, N_EXPERTS), D_MODEL ** -0.5),
        'router_b': nrm((N_EXPERTS,), 0.01),
        'w1': nrm((DEPTH, N_EXPERTS, D_MODEL, D_EXPERT), D_MODEL ** -0.5),
        'w3': nrm((DEPTH, N_EXPERTS, D_MODEL, D_EXPERT), D_MODEL ** -0.5),
        'w2': nrm((DEPTH, N_EXPERTS, D_EXPERT, D_MODEL), D_EXPERT ** -0.5),
    }


def reference(x_prompt, x_sample, cache_mla_ckv, cache_mla_krope, c, c_ctx,
              w_mod, b_mod, g_norm1, g_norm2, w_in, g_q_lora, w_uq, g_kv_lora, w_ukv,
              g_qk_q, g_qk_k, w_dw31, b_dw31, g_conv_ln, b_conv_ln, w_pw, b_pw,
              g_sgu_ln, b_sgu_ln, w_spatial, b_spatial, w_sc3, w_out,
              router_w, router_b, w1, w3, w2):
    y_prompt = x_prompt
    y_sample = x_sample
    ckv_states = []
    krope_states = []
    for l in range(DEPTH):
        p = {
            'g_norm1': g_norm1[l], 'g_norm2': g_norm2[l], 'w_in': w_in[l],
            'g_q_lora': g_q_lora[l], 'w_uq': w_uq[l], 'g_kv_lora': g_kv_lora[l], 'w_ukv': w_ukv[l],
            'g_qk_q': g_qk_q[l], 'g_qk_k': g_qk_k[l],
            'w_dw31': w_dw31[l], 'b_dw31': b_dw31[l], 'g_conv_ln': g_conv_ln[l], 'b_conv_ln': b_conv_ln[l],
            'w_pw': w_pw[l], 'b_pw': b_pw[l],
            'g_sgu_ln': g_sgu_ln[l], 'b_sgu_ln': b_sgu_ln[l], 'w_spatial': w_spatial[l], 'b_spatial': b_spatial[l],
            'w_sc3': w_sc3[l], 'w_out': w_out[l],
            'w1': w1[l], 'w3': w3[l], 'w2': w2[l],
        }
        mod_ctx = (jax.nn.silu(c_ctx) @ w_mod[l] + b_mod[l])[None, :]
        mod_lat = jax.nn.silu(c) @ w_mod[l] + b_mod[l]
        y_prompt, ckv_l, krope_l = trunk_layer(y_prompt, mod_ctx, p, router_w, router_b, None, None)
        ckv_states.append(ckv_l)
        krope_states.append(krope_l)
        y_sample, _, _ = trunk_layer(y_sample, mod_lat, p, router_w, router_b,
                                     cache_mla_ckv[:, l], cache_mla_krope[:, l])
    state_mla_ckv = jnp.stack(ckv_states, axis=1)
    state_mla_krope = jnp.stack(krope_states, axis=1)
    return (y_prompt, y_sample, state_mla_ckv, state_mla_krope)
```

```python
import functools

import jax
import jax.numpy as jnp
from jax import lax
from jax.experimental import pallas as pl
from jax.experimental.pallas import tpu as pltpu

F32 = jnp.float32
BF16 = jnp.bfloat16
U32 = jnp.uint32

MLA_HEADS = 4
MLA_NOPE = 128
MLA_ROPE = 64
MLA_V = 128
MLA_QK = MLA_NOPE + MLA_ROPE
HEAD_PAD = 256
GRID_W = 64
ROPE_THETA = 10000.0
CONV_K = 31
SC_K = 3
CHUNK = 128
GMLP_HEADS = 4
N_GROUPS = 8
GROUP_SIZE = 4
N_PAIRS = 6
NORM_EPS = 1e-6

LANE = 128
HALO = 16
PROJ_W = 4096
KV_W = 256
IN_TN = 512
MOE_BM = 256
GATE_W = 128
VMEM_CAP = 56 * 1024 * 1024


def _cparams(sem, vmem_mb):
    return pltpu.CompilerParams(dimension_semantics=sem,
                                vmem_limit_bytes=min(vmem_mb * 1024 * 1024, VMEM_CAP))


def _inv_rms(x, n):
    return lax.rsqrt(jnp.sum(x * x, axis=-1, keepdims=True) * (1.0 / n) + NORM_EPS)


def _layernorm(x, g, b):
    mu = jnp.mean(x, axis=-1, keepdims=True)
    xc = x - mu
    var = jnp.mean(xc * xc, axis=-1, keepdims=True)
    return xc * lax.rsqrt(var + NORM_EPS) * g + b


def _split_bf16(x):
    hi = x.astype(BF16)
    lo = (x - hi.astype(F32)).astype(BF16)
    return hi, lo


def _pack_rows(x):
    n = x.shape[1] // 2
    hi = pltpu.bitcast(x[:, :n].astype(BF16).astype(F32), U32)
    lo = pltpu.bitcast(x[:, n:].astype(BF16).astype(F32), U32)
    return (hi & jnp.uint32(0xFFFF0000)) | (lo >> 16)


def _unpack_rows(u):
    a = pltpu.bitcast(u & jnp.uint32(0xFFFF0000), F32)
    b = pltpu.bitcast(u << 16, F32)
    return a, b


def _mod_kernel(c_ref, w_ref, b_ref, o_ref):
    c = c_ref[...]
    a_hi, a_lo = _split_bf16(c * jax.nn.sigmoid(c))
    w_hi, w_lo = _split_bf16(w_ref[...])
    acc = jnp.dot(a_hi, w_hi, preferred_element_type=F32)
    acc += jnp.dot(a_lo, w_hi, preferred_element_type=F32)
    acc += jnp.dot(a_hi, w_lo, preferred_element_type=F32)
    o_ref[...] = acc + b_ref[...]


def _modulation(cvec, w_mod, b_mod):
    depth, d, n = w_mod.shape
    tn = 1024
    return pl.pallas_call(
        _mod_kernel,
        grid=(depth, n // tn),
        in_specs=[pl.BlockSpec((8, d), lambda l, j: (0, 0)),
                  pl.BlockSpec((None, d, tn), lambda l, j: (l, 0, j)),
                  pl.BlockSpec((None, 1, tn), lambda l, j: (l, 0, j))],
        out_specs=pl.BlockSpec((None, 8, tn), lambda l, j: (l, 0, j)),
        out_shape=jax.ShapeDtypeStruct((depth, 8, n), F32),
        compiler_params=_cparams(("arbitrary", "arbitrary"), 40),
        name="modulation",
    )(cvec, w_mod, b_mod.reshape(depth, 1, n))


def _in_proj_kernel(npt, xp_ref, xs_ref, mod_ref, g_ref, w_ref, proj_ref, kv_ref, h_scr):
    i = pl.program_id(0)
    j = pl.program_id(1)
    nj = pl.num_programs(1)

    @pl.when(j == 0)
    def _():
        x = jnp.where(i < npt, xp_ref[...], xs_ref[...])
        y = x * _inv_rms(x, x.shape[-1]) * g_ref[...]
        h_scr[...] = (y * (1.0 + mod_ref[1:2, :]) + mod_ref[0:1, :]).astype(BF16)

    acc = jnp.dot(h_scr[...], w_ref[...], preferred_element_type=F32)

    @pl.when(j < nj - 1)
    def _():
        proj_ref[...] = acc.astype(BF16)

    @pl.when(j == nj - 1)
    def _():
        kv_ref[...] = acc[:, :KV_W]


def _in_proj(xp, xs, mod_l, g, w, tm, mod_index):
    tp, d = xp.shape
    t = tp + xs.shape[0]
    npt = tp // tm
    n = w.shape[1]
    nj = n // IN_TN
    return pl.pallas_call(
        functools.partial(_in_proj_kernel, npt),
        grid=(t // tm, nj),
        in_specs=[pl.BlockSpec((tm, d), lambda i, j: (jnp.minimum(i, npt - 1), 0)),
                  pl.BlockSpec((tm, d), lambda i, j: (jnp.maximum(i - npt, 0), 0)),
                  pl.BlockSpec((None, 6, d), lambda i, j: (mod_index(i, tm), 0, 0)),
                  pl.BlockSpec((1, d), lambda i, j: (0, 0)),
                  pl.BlockSpec((d, IN_TN), lambda i, j: (0, j))],
        out_specs=[pl.BlockSpec((tm, IN_TN), lambda i, j: (i, jnp.minimum(j, nj - 2))),
                   pl.BlockSpec((tm, KV_W), lambda i, j: (i, 0))],
        out_shape=[jax.ShapeDtypeStruct((t, PROJ_W), BF16),
                   jax.ShapeDtypeStruct((t, KV_W), F32)],
        scratch_shapes=[pltpu.VMEM((tm, d), BF16)],
        compiler_params=_cparams(("arbitrary", "arbitrary"), 48),
        name="in_proj",
    )(xp, xs, mod_l, g, w)


def _rope_tile(t, cos, sin):
    lane = lax.broadcasted_iota(jnp.int32, t.shape, 1)
    first = (lane & 16) == 0
    swapped = jnp.where(first, pltpu.roll(t, LANE - 16, 1), pltpu.roll(t, 16, 1))
    return t * cos + swapped * sin


def _keys_values(ckv_n, krope, w_ukv_ref, gk_ref, cos, sin, k_ref, v_ref):
    kvf = jnp.dot(ckv_n.astype(BF16), w_ukv_ref[...], preferred_element_type=F32)
    kr_ss = jnp.sum(krope * krope, axis=-1, keepdims=True)
    gk = gk_ref[...]
    for h in range(MLA_HEADS):
        kn = kvf[:, h * MLA_NOPE:(h + 1) * MLA_NOPE]
        r = lax.rsqrt((jnp.sum(kn * kn, axis=-1, keepdims=True) + kr_ss) * (1.0 / MLA_QK) + NORM_EPS)
        tail = krope * r * gk[:, MLA_NOPE:]
        if cos is not None:
            tail = _rope_tile(tail, cos, sin)
        k_ref[:, h * HEAD_PAD:h * HEAD_PAD + MLA_NOPE] = (kn * r * gk[:, :MLA_NOPE]).astype(BF16)
        k_ref[:, h * HEAD_PAD + MLA_NOPE:(h + 1) * HEAD_PAD] = tail.astype(BF16)
    v_ref[...] = kvf[:, MLA_HEADS * MLA_NOPE:].astype(BF16)


def _qkv_kernel(npt, q_lora, cq_ref, kv_ref, cos_ref, sin_ref, w_uq_ref, w_ukv_ref, gql_ref, gkv_ref,
                gq_ref, gk_ref, q_ref, k_ref, v_ref, st_ref):
    i = pl.program_id(0)
    cos = cos_ref[...]
    sin = sin_ref[...]
    cq = cq_ref[...].astype(F32)
    cqn = cq * _inv_rms(cq, q_lora) * gql_ref[...]
    qf = jnp.dot(cqn.astype(BF16), w_uq_ref[...], preferred_element_type=F32)
    gq = gq_ref[...]
    scale = MLA_QK ** -0.5
    for h in range(MLA_HEADS):
        qh = qf[:, h * HEAD_PAD:(h + 1) * HEAD_PAD]
        qh = qh * (_inv_rms(qh, MLA_QK) * scale) * gq
        q_ref[:, h * HEAD_PAD:h * HEAD_PAD + MLA_NOPE] = qh[:, :MLA_NOPE].astype(BF16)
        q_ref[:, h * HEAD_PAD + MLA_NOPE:(h + 1) * HEAD_PAD] = _rope_tile(qh[:, MLA_NOPE:], cos, sin).astype(BF16)

    kv = kv_ref[...]
    ckv = kv[:, :MLA_NOPE]
    krope = kv[:, MLA_NOPE:]
    ckv_n = ckv * _inv_rms(ckv, ckv.shape[-1]) * gkv_ref[...]
    _keys_values(ckv_n, krope, w_ukv_ref, gk_ref, cos, sin, k_ref, v_ref)

    @pl.when(i < npt)
    def _():
        st_ref[:, :MLA_NOPE] = ckv_n
        st_ref[:, MLA_NOPE:] = krope


def _qkv(proj, kv, cos_t, sin_t, w_uq, w_ukv, gql, gkv, gq, gk, tp, dec_seq, q_lora, tm):
    t = proj.shape[0]
    npt = tp // tm
    nseq = dec_seq // tm
    cq_blk = (PROJ_W - 512) // 512

    def tab(i):
        return (jnp.where(i < npt, nseq, (i - npt) % nseq), 0)

    const = lambda i: (0, 0)
    return pl.pallas_call(
        functools.partial(_qkv_kernel, npt, q_lora),
        grid=(t // tm,),
        in_specs=[pl.BlockSpec((tm, 512), lambda i: (i, cq_blk)),
                  pl.BlockSpec((tm, KV_W), lambda i: (i, 0)),
                  pl.BlockSpec((tm, LANE), tab),
                  pl.BlockSpec((tm, LANE), tab),
                  pl.BlockSpec(w_uq.shape, const),
                  pl.BlockSpec(w_ukv.shape, const),
                  pl.BlockSpec(gql.shape, const),
                  pl.BlockSpec(gkv.shape, const),
                  pl.BlockSpec(gq.shape, const),
                  pl.BlockSpec(gk.shape, const)],
        out_specs=[pl.BlockSpec((tm, MLA_HEADS * HEAD_PAD), lambda i: (i, 0)),
                   pl.BlockSpec((tm, MLA_HEADS * HEAD_PAD), lambda i: (i, 0)),
                   pl.BlockSpec((tm, MLA_HEADS * MLA_V), lambda i: (i, 0)),
                   pl.BlockSpec((tm, KV_W), lambda i: (jnp.minimum(i, npt - 1), 0))],
        out_shape=[jax.ShapeDtypeStruct((t, MLA_HEADS * HEAD_PAD), BF16),
                   jax.ShapeDtypeStruct((t, MLA_HEADS * HEAD_PAD), BF16),
                   jax.ShapeDtypeStruct((t, MLA_HEADS * MLA_V), BF16),
                   jax.ShapeDtypeStruct((tp, KV_W), F32)],
        compiler_params=_cparams(("arbitrary",), 40),
        name="qkv",
    )(proj, kv, cos_t, sin_t, w_uq, w_ukv, gql, gkv, gq, gk)


def _ctx_kv_kernel(kv_ref, w_ukv_ref, gk_ref, k_ref, v_ref):
    kv = kv_ref[...]
    _keys_values(kv[:, :MLA_NOPE], kv[:, MLA_NOPE:], w_ukv_ref, gk_ref, None, None, k_ref, v_ref)


def _ctx_kv(kvc, w_ukv, gk, tm):
    r = kvc.shape[0]
    const = lambda i: (0, 0)
    return pl.pallas_call(
        _ctx_kv_kernel,
        grid=(r // tm,),
        in_specs=[pl.BlockSpec((tm, KV_W), lambda i: (i, 0)),
                  pl.BlockSpec(w_ukv.shape, const),
                  pl.BlockSpec(gk.shape, const)],
        out_specs=[pl.BlockSpec((tm, MLA_HEADS * HEAD_PAD), lambda i: (i, 0)),
                   pl.BlockSpec((tm, MLA_HEADS * MLA_V), lambda i: (i, 0))],
        out_shape=[jax.ShapeDtypeStruct((r, MLA_HEADS * HEAD_PAD), BF16),
                   jax.ShapeDtypeStruct((r, MLA_HEADS * MLA_V), BF16)],
        compiler_params=_cparams(("arbitrary",), 32),
        name="ctx_kv",
    )(kvc, w_ukv, gk)


_NT = (((1,), (1,)), ((), ()))


def _attn_kernel(has_ctx, *refs):
    if has_ctx:
        q_ref, kc_ref, vc_ref, k_ref, v_ref, o_ref = refs
    else:
        q_ref, k_ref, v_ref, o_ref = refs
    q = q_ref[...]
    s = lax.dot_general(q, k_ref[...], _NT, preferred_element_type=F32)
    m = jnp.max(s, axis=-1, keepdims=True)
    if has_ctx:
        sc = lax.dot_general(q, kc_ref[...], _NT, preferred_element_type=F32)
        m = jnp.maximum(m, jnp.max(sc, axis=-1, keepdims=True))
    p = jnp.exp(s - m)
    l = jnp.sum(p, axis=-1, keepdims=True)
    o = jnp.dot(p.astype(BF16), v_ref[...], preferred_element_type=F32)
    if has_ctx:
        pc = jnp.exp(sc - m)
        l = l + jnp.sum(pc, axis=-1, keepdims=True)
        o = o + jnp.dot(pc.astype(BF16), vc_ref[...], preferred_element_type=F32)
    o_ref[...] = (o / l).astype(BF16)


def _attention_latent(q, k, v, kc, vc, tp, nb, dec_seq, past, tq):
    t = q.shape[0]
    nq = dec_seq // tq
    row0 = tp // tq
    kblk0 = tp // dec_seq
    return pl.pallas_call(
        functools.partial(_attn_kernel, True),
        grid=(nb, MLA_HEADS, nq),
        in_specs=[pl.BlockSpec((tq, HEAD_PAD), lambda b, h, i: (row0 + b * nq + i, h)),
                  pl.BlockSpec((past, HEAD_PAD), lambda b, h, i: (b, h)),
                  pl.BlockSpec((past, MLA_V), lambda b, h, i: (b, h)),
                  pl.BlockSpec((dec_seq, HEAD_PAD), lambda b, h, i: (kblk0 + b, h)),
                  pl.BlockSpec((dec_seq, MLA_V), lambda b, h, i: (kblk0 + b, h))],
        out_specs=pl.BlockSpec((tq, MLA_V), lambda b, h, i: (b * nq + i, h)),
        out_shape=jax.ShapeDtypeStruct((t - tp, MLA_HEADS * MLA_V), BF16),
        compiler_params=_cparams(("arbitrary", "arbitrary", "arbitrary"), 48),
        name="attn_latent",
    )(q, kc, vc, k, v)


def _attention_context(q, k, v, nseq, seq):
    return pl.pallas_call(
        functools.partial(_attn_kernel, False),
        grid=(nseq, MLA_HEADS),
        in_specs=[pl.BlockSpec((seq, HEAD_PAD), lambda b, h: (b, h)),
                  pl.BlockSpec((seq, HEAD_PAD), lambda b, h: (b, h)),
                  pl.BlockSpec((seq, MLA_V), lambda b, h: (b, h))],
        out_specs=pl.BlockSpec((seq, MLA_V), lambda b, h: (b, h)),
        out_shape=jax.ShapeDtypeStruct((nseq * seq, MLA_HEADS * MLA_V), BF16),
        compiler_params=_cparams(("arbitrary", "arbitrary"), 32),
        name="attn_context",
    )(q, k, v)


def _mixers_kernel(npt, p_tiles, s_tiles,
                   cv_ref, cvp_ref, cvn_ref, gm_ref, sc_ref, scp_ref, scn_ref, bg_ref,
                   wdw_ref, bdw_ref, gcl_ref, bcl_ref, wpw_ref, bpw_ref,
                   gsl_ref, bsl_ref, wsp_ref, bsp_ref, wsc_ref,
                   o_ref, zs, ys):
    i = pl.program_id(0)
    tm = cv_ref.shape[0]
    cw = cv_ref.shape[1] // 2
    pos = jnp.where(i < npt, i % p_tiles, (i - npt) % s_tiles)
    n_tiles = jnp.where(i < npt, p_tiles, s_tiles)
    keep_prev = (pos > 0).astype(F32)
    keep_next = (pos < n_tiles - 1).astype(F32)

    def glu(ref):
        a = ref[:, :cw].astype(F32)
        g = ref[:, cw:].astype(F32)
        return a * jax.nn.sigmoid(g)

    def prod(ref):
        return ref[:, :cw].astype(F32) * ref[:, cw:].astype(F32)

    zs[0:HALO, :] = glu(cvp_ref) * keep_prev
    zs[HALO:HALO + tm, :] = glu(cv_ref)
    zs[HALO + tm:, :] = glu(cvn_ref) * keep_next
    pad = CONV_K // 2
    half = tm // 2
    for c in range(cw // LANE):
        cs = slice(c * LANE, (c + 1) * LANE)
        for rh in range(2):
            base = rh * half
            acc = jnp.zeros((half, LANE), F32)
            for r in range(8):
                zr = zs[pl.ds(base + r, half + 24), cs]
                for a in range(4):
                    tap = 8 * a + r - (HALO - pad)
                    if 0 <= tap < CONV_K:
                        acc = acc + zr[8 * a:8 * a + half, :] * wdw_ref[tap:tap + 1, cs]
            ys[base:base + half, cs] = acc
    z = ys[0:tm, :] + bdw_ref[...]
    z = _layernorm(z, gcl_ref[...], bcl_ref[...])
    z = z * jax.nn.sigmoid(z)
    o_b = jnp.dot(z.astype(BF16), wpw_ref[...], preferred_element_type=F32) + bpw_ref[...]
    o_ref[:, 0:cw] = o_b.astype(BF16)

    u = gm_ref[:, :cw].astype(F32)
    vg = _layernorm(gm_ref[:, cw:].astype(F32), gsl_ref[...], bsl_ref[...]).astype(BF16)
    hc = cw // GMLP_HEADS
    for n in range(tm // CHUNK):
        rows = slice(n * CHUNK, (n + 1) * CHUNK)
        for h in range(GMLP_HEADS):
            cols = slice(h * hc, (h + 1) * hc)
            mixed = jnp.dot(wsp_ref[h], vg[rows, cols], preferred_element_type=F32) + bsp_ref[:, cols]
            o_ref[rows, cw + h * hc:cw + (h + 1) * hc] = (u[rows, cols] * mixed).astype(BF16)

    ys[0:HALO, :] = prod(scp_ref) * keep_prev
    ys[HALO:HALO + tm, :] = prod(sc_ref)
    ys[HALO + tm:, :] = prod(scn_ref) * keep_next
    acc = jnp.zeros((tm, cw), F32)
    for tap in range(SC_K):
        acc = acc + ys[pl.ds(HALO - SC_K // 2 + tap, tm), :] * wsc_ref[tap:tap + 1, :]
    o_ref[:, 2 * cw:3 * cw] = (bg_ref[...].astype(F32) * acc).astype(BF16)


def _mixers(proj, weights, tp, seq, dec_seq, tm):
    t = proj.shape[0]
    cw = 512
    npt = tp // tm
    hb = tm // HALO
    last = t // HALO - 1
    const2 = lambda i: (0, 0)

    def main(col):
        return pl.BlockSpec((tm, 2 * cw), lambda i: (i, col))

    def prev(col):
        return pl.BlockSpec((HALO, 2 * cw), lambda i: (jnp.maximum(i * hb - 1, 0), col))

    def nxt(col):
        return pl.BlockSpec((HALO, 2 * cw), lambda i: (jnp.minimum((i + 1) * hb, last), col))

    w_specs = [pl.BlockSpec(w.shape, const2 if w.ndim == 2 else (lambda i: (0, 0, 0))) for w in weights]
    return pl.pallas_call(
        functools.partial(_mixers_kernel, npt, seq // tm, dec_seq // tm),
        grid=(t // tm,),
        in_specs=[main(0), prev(0), nxt(0), main(1), main(2), prev(2), nxt(2),
                  pl.BlockSpec((tm, cw), lambda i: (i, 6))] + w_specs,
        out_specs=pl.BlockSpec((tm, 3 * cw), lambda i: (i, 0)),
        out_shape=jax.ShapeDtypeStruct((t, 3 * cw), BF16),
        scratch_shapes=[pltpu.VMEM((tm + 2 * HALO, cw), F32), pltpu.VMEM((tm + 2 * HALO, cw), F32)],
        compiler_params=_cparams(("arbitrary",), 40),
        name="mixers",
    )(proj, proj, proj, proj, proj, proj, proj, proj, *weights)


def _out_proj_kernel(npt, oap_ref, oas_ref, ob_ref, xp_ref, xs_ref, mod_ref, g_ref, wo_ref, wr_ref,
                     xn_ref, hp_ref, r_ref):
    i = pl.program_id(0)
    ka = oap_ref.shape[1]
    x = jnp.where(i < npt, xp_ref[...], xs_ref[...])
    oa = jnp.where(i < npt, oap_ref[...], oas_ref[...])
    mix = jnp.dot(oa, wo_ref[0:ka, :], preferred_element_type=F32)
    mix += jnp.dot(ob_ref[...], wo_ref[ka:, :], preferred_element_type=F32)
    xn = x + mod_ref[2:3, :] * mix
    xn_ref[...] = xn
    h = xn * _inv_rms(xn, xn.shape[-1]) * g_ref[...]
    h = h * (1.0 + mod_ref[4:5, :]) + mod_ref[3:4, :]
    hp_ref[...] = _pack_rows(h)
    h_hi, h_lo = _split_bf16(h)
    r_ref[...] = (jnp.dot(h_hi, wr_ref[...], preferred_element_type=F32)
                  + jnp.dot(h_lo, wr_ref[...], preferred_element_type=F32))


def _out_proj(oap, oas, ob, xp, xs, mod_l, g, wo, wr, tm, mod_index):
    tp, d = xp.shape
    t = ob.shape[0]
    npt = tp // tm
    const = lambda i: (0, 0)

    def pair(w):
        return [pl.BlockSpec((tm, w), lambda i: (jnp.minimum(i, npt - 1), 0)),
                pl.BlockSpec((tm, w), lambda i: (jnp.maximum(i - npt, 0), 0))]

    return pl.pallas_call(
        functools.partial(_out_proj_kernel, npt),
        grid=(t // tm,),
        in_specs=pair(oap.shape[1]) + [pl.BlockSpec((tm, ob.shape[1]), lambda i: (i, 0))] + pair(d) + [
                  pl.BlockSpec((None, 6, d), lambda i: (mod_index(i, tm), 0, 0)),
                  pl.BlockSpec((1, d), const),
                  pl.BlockSpec(wo.shape, const),
                  pl.BlockSpec(wr.shape, const)],
        out_specs=[pl.BlockSpec((tm, d), lambda i: (i, 0)),
                   pl.BlockSpec((tm, d // 2), lambda i: (i, 0)),
                   pl.BlockSpec((tm, LANE), lambda i: (i, 0))],
        out_shape=[jax.ShapeDtypeStruct((t, d), F32),
                   jax.ShapeDtypeStruct((t, d // 2), U32),
                   jax.ShapeDtypeStruct((t, LANE), F32)],
        compiler_params=_cparams(("arbitrary",), 48),
        name="out_proj",
    )(oap, oas, ob, xp, xs, mod_l, g, wo, wr)


def _route(r, router_b, n_blk):
    t = r.shape[0]
    n_exp = router_b.shape[0]
    scores = jax.nn.sigmoid(r[:, :n_exp] + r[:, n_exp:2 * n_exp])
    sel = scores + router_b.astype(F32)
    grp = sel.reshape(t, N_GROUPS, GROUP_SIZE)
    g_score = lax.top_k(grp, 2)[0].sum(axis=-1)
    g_best = jnp.argmax(g_score, axis=-1).astype(jnp.int32)
    in_grp = (jnp.arange(n_exp) // GROUP_SIZE)[None, :] == g_best[:, None]
    _, eidx = lax.top_k(jnp.where(in_grp, sel, -jnp.inf), 2)
    gate = jnp.take_along_axis(scores, eidx, axis=-1)
    gate = gate / jnp.sum(gate, axis=-1, keepdims=True)
    swap = eidx[:, 0] > eidx[:, 1]
    e_lo = jnp.where(swap, eidx[:, 1], eidx[:, 0]).astype(jnp.int32)
    e_hi = jnp.where(swap, eidx[:, 0], eidx[:, 1]).astype(jnp.int32)
    g_lo = jnp.where(swap, gate[:, 1], gate[:, 0])
    g_hi = jnp.where(swap, gate[:, 0], gate[:, 1])
    r_lo = e_lo % GROUP_SIZE
    r_hi = e_hi % GROUP_SIZE
    pair = r_lo * 3 - (r_lo * (r_lo - 1)) // 2 + (r_hi - r_lo - 1)
    bucket = g_best * N_PAIRS + pair
    n_bucket = N_GROUPS * N_PAIRS

    onehot = (bucket[:, None] == jnp.arange(n_bucket, dtype=jnp.int32)[None, :]).astype(jnp.int32)
    csum = jnp.cumsum(onehot, axis=0)
    counts = csum[-1]
    rank = jnp.sum((csum - onehot) * onehot, axis=1)
    padded = (counts + MOE_BM - 1) // MOE_BM * MOE_BM
    pad_end = jnp.cumsum(padded)
    pad_start = pad_end - padded
    pos = (jnp.sum(onehot * pad_start[None, :], axis=1) + rank).astype(jnp.int32)

    n_used = (pad_end[-1] // MOE_BM).astype(jnp.int32)
    blk = jnp.minimum(jnp.arange(n_blk, dtype=jnp.int32), n_used - 1)
    blk_bucket = jnp.minimum(jnp.searchsorted(pad_end, blk * MOE_BM, side='right'), n_bucket - 1).astype(jnp.int32)
    pair_lo = jnp.array([0, 0, 0, 1, 1, 2], jnp.int32)
    pair_hi = jnp.array([1, 2, 3, 2, 3, 3], jnp.int32)
    blk_a = (blk_bucket // N_PAIRS) * GROUP_SIZE + pair_lo[blk_bucket % N_PAIRS]
    blk_b = (blk_bucket // N_PAIRS) * GROUP_SIZE + pair_hi[blk_bucket % N_PAIRS]
    gates = jnp.zeros((t, GATE_W), F32).at[:, 0].set(g_lo).at[:, 1].set(g_hi)
    return pos, gates, blk, blk_a, blk_b, n_used.reshape(1)


def _dispatch_kernel(pos_ref, hp_ref, gt_ref, init_ref, xs_ref, buf, sem):
    del init_ref
    tm = hp_ref.shape[0]
    w = hp_ref.shape[1]
    buf[:, :w] = hp_ref[...]
    buf[:, w:] = pltpu.bitcast(gt_ref[...], U32)

    def row_copy(r):
        return pltpu.make_async_copy(buf.at[pl.ds(r, 1)], xs_ref.at[pl.ds(pos_ref[0, 0, r], 1)], sem)

    def start(r, carry):
        row_copy(r).start()
        return carry

    def wait(r, carry):
        row_copy(r).wait()
        return carry

    lax.fori_loop(0, tm, start, 0, unroll=8)
    lax.fori_loop(0, tm, wait, 0, unroll=8)


def _dispatch(hp, gates, pos, n_buf, tm):
    t, w = hp.shape
    init = jnp.zeros((n_buf, w + GATE_W), U32)
    return pl.pallas_call(
        _dispatch_kernel,
        grid=(t // tm,),
        in_specs=[pl.BlockSpec((1, 1, tm), lambda i: (i, 0, 0), memory_space=pltpu.SMEM),
                  pl.BlockSpec((tm, w), lambda i: (i, 0)),
                  pl.BlockSpec((tm, GATE_W), lambda i: (i, 0)),
                  pl.BlockSpec(memory_space=pl.ANY)],
        out_specs=pl.BlockSpec(memory_space=pl.ANY),
        out_shape=jax.ShapeDtypeStruct((n_buf, w + GATE_W), U32),
        scratch_shapes=[pltpu.VMEM((tm, w + GATE_W), U32), pltpu.SemaphoreType.DMA(())],
        input_output_aliases={3: 0},
        compiler_params=_cparams(("arbitrary",), 32),
        name="dispatch",
    )(pos.reshape(t // tm, 1, tm), hp, gates, init)


def _moe_kernel(blk_ref, ea_ref, eb_ref, nu_ref, x_ref, w13a_ref, w13b_ref, w2a_ref, w2b_ref, y_ref):
    del blk_ref, ea_ref, eb_ref
    i = pl.program_id(0)

    @pl.when(i < nu_ref[0])
    def _():
        w = x_ref.shape[1] - GATE_W
        xa, xb = _unpack_rows(x_ref[:, :w])
        x = jnp.concatenate([xa.astype(BF16), xb.astype(BF16)], axis=1)
        gates = pltpu.bitcast(x_ref[:, w:], F32)
        de = w2a_ref.shape[0]

        def hidden(w13_ref, g):
            h = jnp.dot(x, w13_ref[...], preferred_element_type=F32)
            a = h[:, :de]
            return (a * jax.nn.sigmoid(a) * h[:, de:] * g).astype(BF16)

        y = jnp.dot(hidden(w13a_ref, gates[:, 0:1]), w2a_ref[...], preferred_element_type=F32)
        y += jnp.dot(hidden(w13b_ref, gates[:, 1:2]), w2b_ref[...], preferred_element_type=F32)
        y_ref[...] = _pack_rows(y)

    @pl.when(i >= nu_ref[0])
    def _():
        y_ref[...] = jnp.zeros_like(y_ref)


def _moe(xs, w13, w2, blk, blk_a, blk_b, n_used):
    n_buf, wx = xs.shape
    n_blk = n_buf // MOE_BM
    _, d, de2 = w13.shape
    de = de2 // 2
    grid_spec = pltpu.PrefetchScalarGridSpec(
        num_scalar_prefetch=4,
        grid=(n_blk,),
        in_specs=[pl.BlockSpec((MOE_BM, wx), lambda i, blk, ea, eb, nu: (blk[i], 0)),
                  pl.BlockSpec((None, d, de2), lambda i, blk, ea, eb, nu: (ea[i], 0, 0)),
                  pl.BlockSpec((None, d, de2), lambda i, blk, ea, eb, nu: (eb[i], 0, 0)),
                  pl.BlockSpec((None, de, d), lambda i, blk, ea, eb, nu: (ea[i], 0, 0)),
                  pl.BlockSpec((None, de, d), lambda i, blk, ea, eb, nu: (eb[i], 0, 0))],
        out_specs=pl.BlockSpec((MOE_BM, d // 2), lambda i, blk, ea, eb, nu: (i, 0)),
    )
    return pl.pallas_call(
        _moe_kernel,
        grid_spec=grid_spec,
        out_shape=jax.ShapeDtypeStruct((n_buf, d // 2), U32),
        compiler_params=_cparams(("arbitrary",), 48),
        name="moe",
    )(blk, blk_a, blk_b, n_used, xs, w13, w13, w2, w2)


def _combine_kernel(npt, pos_ref, xn_ref, mod_ref, yb_ref, op_ref, os_ref, buf, sem):
    i = pl.program_id(0)
    tm = xn_ref.shape[0]

    def row_copy(r):
        return pltpu.make_async_copy(yb_ref.at[pl.ds(pos_ref[0, 0, r], 1)], buf.at[pl.ds(r, 1)], sem)

    def start(r, carry):
        row_copy(r).start()
        return carry

    def wait(r, carry):
        row_copy(r).wait()
        return carry

    lax.fori_loop(0, tm, start, 0, unroll=8)
    lax.fori_loop(0, tm, wait, 0, unroll=8)
    ya, yb = _unpack_rows(buf[...])
    out = xn_ref[...] + mod_ref[5:6, :] * jnp.concatenate([ya, yb], axis=1)

    @pl.when(i < npt)
    def _():
        op_ref[...] = out

    @pl.when(i >= npt)
    def _():
        os_ref[...] = out


def _combine(xn, yb, pos, mod_l, tp, tm, mod_index):
    t, d = xn.shape
    npt = tp // tm
    return pl.pallas_call(
        functools.partial(_combine_kernel, npt),
        grid=(t // tm,),
        in_specs=[pl.BlockSpec((1, 1, tm), lambda i: (i, 0, 0), memory_space=pltpu.SMEM),
                  pl.BlockSpec((tm, d), lambda i: (i, 0)),
                  pl.BlockSpec((None, 6, d), lambda i: (mod_index(i, tm), 0, 0)),
                  pl.BlockSpec(memory_space=pl.ANY)],
        out_specs=[pl.BlockSpec((tm, d), lambda i: (jnp.minimum(i, npt - 1), 0)),
                   pl.BlockSpec((tm, d), lambda i: (jnp.maximum(i - npt, 0), 0))],
        out_shape=[jax.ShapeDtypeStruct((tp, d), F32),
                   jax.ShapeDtypeStruct((t - tp, d), F32)],
        scratch_shapes=[pltpu.VMEM((tm, d // 2), U32), pltpu.SemaphoreType.DMA(())],
        compiler_params=_cparams(("arbitrary",), 32),
        name="combine",
    )(pos.reshape(t // tm, 1, tm), xn, mod_l, yb)


def _pad_cols(x, n):
    return jnp.pad(x, ((0, 0), (0, n - x.shape[1])))


def _rope_tables(dec_seq, tm):
    n_freq = MLA_ROPE // 4
    pos = jnp.arange(dec_seq, dtype=jnp.int32)
    row = (pos // GRID_W).astype(F32)
    col = (pos % GRID_W).astype(F32)
    inv = ROPE_THETA ** (-jnp.arange(n_freq, dtype=F32) / n_freq)
    ar = row[:, None] * inv[None, :]
    ac = col[:, None] * inv[None, :]
    ones = jnp.ones((dec_seq, LANE - MLA_ROPE), F32)
    cos = jnp.concatenate([jnp.cos(ar), jnp.cos(ar), jnp.cos(ac), jnp.cos(ac), ones], axis=1)
    sin = jnp.concatenate([-jnp.sin(ar), jnp.sin(ar), -jnp.sin(ac), jnp.sin(ac), 0.0 * ones], axis=1)
    ident_c = jnp.ones((tm, LANE), F32)
    ident_s = jnp.zeros((tm, LANE), F32)
    return jnp.concatenate([cos, ident_c], axis=0), jnp.concatenate([sin, ident_s], axis=0)


def _layer_params(l, q_lora, kv_lora, w_in, g_q_lora, w_uq, g_kv_lora, w_ukv, g_qk_q, g_qk_k,
                  w_dw31, b_dw31, g_conv_ln, b_conv_ln, w_pw, b_pw, g_sgu_ln, b_sgu_ln,
                  w_spatial, b_spatial, w_sc3, w_out):
    d = w_in.shape[1]
    o1 = q_lora
    o2 = o1 + kv_lora
    o3 = o2 + MLA_ROPE
    o4 = o3 + 1024
    o5 = o4 + 1024
    w = w_in[l]
    zeros = lambda n: jnp.zeros((d, n), w.dtype)
    w_in_p = jnp.concatenate(
        [w[:, o3:o4], w[:, o4:o5], w[:, o5 + 512:o5 + 1536], w[:, o5:o5 + 512],
         w[:, :o1], zeros(512 - q_lora),
         w[:, o1:o2], w[:, o2:o3], zeros(IN_TN - kv_lora - MLA_ROPE)], axis=1).astype(BF16)

    uq = w_uq[l].reshape(q_lora, MLA_HEADS, MLA_QK)
    uq = jnp.pad(uq, ((0, 512 - q_lora), (0, 0), (0, HEAD_PAD - MLA_QK)))
    w_uq_p = uq.reshape(512, MLA_HEADS * HEAD_PAD).astype(BF16)
    ukv = w_ukv[l].reshape(kv_lora, MLA_HEADS, MLA_NOPE + MLA_V)
    w_ukv_p = jnp.concatenate([ukv[:, :, :MLA_NOPE].reshape(kv_lora, -1),
                               ukv[:, :, MLA_NOPE:].reshape(kv_lora, -1)], axis=1).astype(BF16)
    row = lambda v: v.reshape(1, -1).astype(F32)
    qkv_w = (w_uq_p, w_ukv_p, _pad_cols(row(g_q_lora[l]), 512), row(g_kv_lora[l]),
             _pad_cols(row(g_qk_q[l]), HEAD_PAD), _pad_cols(row(g_qk_k[l]), HEAD_PAD))

    hc = w_pw.shape[1] // GMLP_HEADS
    bsp = jnp.repeat(b_spatial[l].T.astype(F32), hc, axis=1)
    mix_w = (jnp.pad(w_dw31[l].astype(F32), ((0, 32 - CONV_K), (0, 0))), row(b_dw31[l]),
             row(g_conv_ln[l]), row(b_conv_ln[l]), w_pw[l].astype(BF16), row(b_pw[l]),
             row(g_sgu_ln[l]), row(b_sgu_ln[l]), w_spatial[l].astype(BF16), bsp,
             jnp.pad(w_sc3[l].astype(F32), ((0, 8 - SC_K), (0, 0))))
    return w_in_p, qkv_w, mix_w, w_out[l].astype(BF16)


def _tile(limit, *sizes):
    tm = limit
    while any(s % tm for s in sizes):
        tm //= 2
    return tm


def kernel(x_prompt, x_sample, cache_mla_ckv, cache_mla_krope, c, c_ctx, w_mod, b_mod, g_norm1, g_norm2, w_in, g_q_lora, w_uq, g_kv_lora, w_ukv, g_qk_q, g_qk_k, w_dw31, b_dw31, g_conv_ln, b_conv_ln, w_pw, b_pw, g_sgu_ln, b_sgu_ln, w_spatial, b_spatial, w_sc3, w_out, router_w, router_b, w1, w3, w2):
    nseq, seq, d = x_prompt.shape
    nb, dec_seq, _ = x_sample.shape
    depth = w_mod.shape[0]
    past = cache_mla_ckv.shape[2]
    q_lora = g_q_lora.shape[1]
    kv_lora = g_kv_lora.shape[1]
    n_exp = router_w.shape[1]
    tp = nseq * seq
    ts = nb * dec_seq
    t = tp + ts
    assert tp % dec_seq == 0 and seq % CHUNK == 0 and dec_seq % GRID_W == 0
    assert nb + 1 <= 8 and kv_lora == MLA_NOPE and n_exp == N_GROUPS * GROUP_SIZE

    tm_in = _tile(512, tp, dec_seq)
    tm_qkv = _tile(512, tp, dec_seq)
    tm_row = _tile(256, seq, dec_seq)
    tq = _tile(256, dec_seq)

    def mod_index(i, tm):
        npt = tp // tm
        return jnp.where(i < npt, 0, 1 + (i - npt) // (dec_seq // tm))

    cvec = jnp.concatenate([c_ctx[None, :], c, jnp.zeros((8 - 1 - nb, d), F32)], axis=0)
    mod = _modulation(cvec, w_mod, b_mod).reshape(depth, 8, 6, d)
    cos_t, sin_t = _rope_tables(dec_seq, tm_qkv)

    rw_hi = router_w.astype(BF16)
    rw_lo = (router_w - rw_hi.astype(F32)).astype(BF16)
    wr = _pad_cols(jnp.concatenate([rw_hi, rw_lo], axis=1), LANE)
    w13 = jnp.concatenate([w1, w3], axis=-1).astype(BF16)
    w2b = w2.astype(BF16)
    n_blk = (t + N_GROUPS * N_PAIRS * (MOE_BM - 1) + MOE_BM - 1) // MOE_BM

    xp = x_prompt.reshape(tp, d)
    xs = x_sample.reshape(ts, d)
    ckv_states = []
    krope_states = []
    for l in range(depth):
        w_in_p, qkv_w, mix_w, w_out_b = _layer_params(
            l, q_lora, kv_lora, w_in, g_q_lora, w_uq, g_kv_lora, w_ukv, g_qk_q, g_qk_k,
            w_dw31, b_dw31, g_conv_ln, b_conv_ln, w_pw, b_pw, g_sgu_ln, b_sgu_ln,
            w_spatial, b_spatial, w_sc3, w_out)
        mod_l = mod[l]

        proj, kv = _in_proj(xp, xs, mod_l, g_norm1[l].reshape(1, d), w_in_p, tm_in, mod_index)
        q, k, v, state = _qkv(proj, kv, cos_t, sin_t, *qkv_w, tp, dec_seq, q_lora, tm_qkv)
        kvc = jnp.concatenate([cache_mla_ckv[:, l], cache_mla_krope[:, l],
                               jnp.zeros((nb, past, KV_W - kv_lora - MLA_ROPE), F32)], axis=-1)
        kc, vc = _ctx_kv(kvc.reshape(nb * past, KV_W), qkv_w[1], qkv_w[5], _tile(512, past))
        o_as = _attention_latent(q, k, v, kc, vc, tp, nb, dec_seq, past, tq)
        o_ap = _attention_context(q, k, v, nseq, seq)
        o_bcd = _mixers(proj, mix_w, tp, seq, dec_seq, tm_row)

        xn, hp, r = _out_proj(o_ap, o_as, o_bcd, xp, xs, mod_l, g_norm2[l].reshape(1, d), w_out_b, wr,
                              tm_row, mod_index)
        pos, gates, blk, blk_a, blk_b, n_used = _route(r, router_b, n_blk)
        xd = _dispatch(hp, gates, pos, n_blk * MOE_BM, tm_row)
        yb = _moe(xd, w13[l], w2b[l], blk, blk_a, blk_b, n_used)
        xp, xs = _combine(xn, yb, pos, mod_l, tp, tm_row, mod_index)

        ckv_states.append(state[:, :kv_lora].reshape(nseq, seq, kv_lora))
        krope_states.append(state[:, kv_lora:kv_lora + MLA_ROPE].reshape(nseq, seq, MLA_ROPE))

    return (xp.reshape(nseq, seq, d), xs.reshape(nb, dec_seq, d),
            jnp.stack(ckv_states, axis=1), jnp.stack(krope_states, axis=1))
```

```python
import functools

import jax
import jax.numpy as jnp
from jax import lax
from jax.experimental import pallas as pl
from jax.experimental.pallas import tpu as pltpu

F32 = jnp.float32
BF16 = jnp.bfloat16
U32 = jnp.uint32

MLA_HEADS = 4
MLA_NOPE = 128
MLA_ROPE = 64
MLA_V = 128
MLA_QK = MLA_NOPE + MLA_ROPE
HEAD_PAD = 256
GRID_W = 64
ROPE_THETA = 10000.0
CONV_K = 31
SC_K = 3
CHUNK = 128
GMLP_HEADS = 4
N_GROUPS = 8
GROUP_SIZE = 4
N_PAIRS = 6
NORM_EPS = 1e-6

LANE = 128
HALO = 16
PROJ_W = 4096
KV_W = 256
IN_TN = 512
MOE_BM = 256
GATE_W = 128
VMEM_CAP = 56 * 1024 * 1024


def _cparams(sem, vmem_mb):
    return pltpu.CompilerParams(dimension_semantics=sem,
                                vmem_limit_bytes=min(vmem_mb * 1024 * 1024, VMEM_CAP))


def _inv_rms(x, n):
    return lax.rsqrt(jnp.sum(x * x, axis=-1, keepdims=True) * (1.0 / n) + NORM_EPS)


def _layernorm(x, g, b):
    mu = jnp.mean(x, axis=-1, keepdims=True)
    xc = x - mu
    var = jnp.mean(xc * xc, axis=-1, keepdims=True)
    return xc * lax.rsqrt(var + NORM_EPS) * g + b


def _split_bf16(x):
    hi = x.astype(BF16)
    lo = (x - hi.astype(F32)).astype(BF16)
    return hi, lo


def _pack_rows(x):
    n = x.shape[1] // 2
    hi = pltpu.bitcast(x[:, :n].astype(BF16).astype(F32), U32)
    lo = pltpu.bitcast(x[:, n:].astype(BF16).astype(F32), U32)
    return (hi & jnp.uint32(0xFFFF0000)) | (lo >> 16)


def _unpack_rows(u):
    a = pltpu.bitcast(u & jnp.uint32(0xFFFF0000), F32)
    b = pltpu.bitcast(u << 16, F32)
    return a, b


def _mod_kernel(c_ref, w_ref, b_ref, o_ref):
    c = c_ref[...]
    a_hi, a_lo = _split_bf16(c * jax.nn.sigmoid(c))
    w_hi, w_lo = _split_bf16(w_ref[...])
    acc = jnp.dot(a_hi, w_hi, preferred_element_type=F32)
    acc += jnp.dot(a_lo, w_hi, preferred_element_type=F32)
    acc += jnp.dot(a_hi, w_lo, preferred_element_type=F32)
    o_ref[...] = acc + b_ref[...]


def _modulation(cvec, w_mod, b_mod):
    depth, d, n = w_mod.shape
    tn = 1024
    return pl.pallas_call(
        _mod_kernel,
        grid=(depth, n // tn),
        in_specs=[pl.BlockSpec((8, d), lambda l, j: (0, 0)),
                  pl.BlockSpec((None, d, tn), lambda l, j: (l, 0, j)),
                  pl.BlockSpec((None, 1, tn), lambda l, j: (l, 0, j))],
        out_specs=pl.BlockSpec((None, 8, tn), lambda l, j: (l, 0, j)),
        out_shape=jax.ShapeDtypeStruct((depth, 8, n), F32),
        compiler_params=_cparams(("arbitrary", "arbitrary"), 40),
        name="modulation",
    )(cvec, w_mod, b_mod.reshape(depth, 1, n))


def _in_proj_kernel(npt, xp_ref, xs_ref, mod_ref, g_ref, w_ref, proj_ref, kv_ref, h_scr):
    i = pl.program_id(0)
    j = pl.program_id(1)
    nj = pl.num_programs(1)

    @pl.when(j == 0)
    def _():
        x = jnp.where(i < npt, xp_ref[...], xs_ref[...])
        y = x * _inv_rms(x, x.shape[-1]) * g_ref[...]
        h_scr[...] = (y * (1.0 + mod_ref[1:2, :]) + mod_ref[0:1, :]).astype(BF16)

    acc = jnp.dot(h_scr[...], w_ref[...], preferred_element_type=F32)

    @pl.when(j < nj - 1)
    def _():
        proj_ref[...] = acc.astype(BF16)

    @pl.when(j == nj - 1)
    def _():
        kv_ref[...] = acc[:, :KV_W]


def _in_proj(xp, xs, mod_l, g, w, tm, mod_index):
    tp, d = xp.shape
    t = tp + xs.shape[0]
    npt = tp // tm
    n = w.shape[1]
    nj = n // IN_TN
    return pl.pallas_call(
        functools.partial(_in_proj_kernel, npt),
        grid=(t // tm, nj),
        in_specs=[pl.BlockSpec((tm, d), lambda i, j: (jnp.minimum(i, npt - 1), 0)),
                  pl.BlockSpec((tm, d), lambda i, j: (jnp.maximum(i - npt, 0), 0)),
                  pl.BlockSpec((None, 6, d), lambda i, j: (mod_index(i, tm), 0, 0)),
                  pl.BlockSpec((1, d), lambda i, j: (0, 0)),
                  pl.BlockSpec((d, IN_TN), lambda i, j: (0, j))],
        out_specs=[pl.BlockSpec((tm, IN_TN), lambda i, j: (i, jnp.minimum(j, nj - 2))),
                   pl.BlockSpec((tm, KV_W), lambda i, j: (i, 0))],
        out_shape=[jax.ShapeDtypeStruct((t, PROJ_W), BF16),
                   jax.ShapeDtypeStruct((t, KV_W), F32)],
        scratch_shapes=[pltpu.VMEM((tm, d), BF16)],
        compiler_params=_cparams(("arbitrary", "arbitrary"), 48),
        name="in_proj",
    )(xp, xs, mod_l, g, w)


def _rope_tile(t, cos, sin):
    lane = lax.broadcasted_iota(jnp.int32, t.shape, 1)
    first = (lane & 16) == 0
    swapped = jnp.where(first, pltpu.roll(t, LANE - 16, 1), pltpu.roll(t, 16, 1))
    return t * cos + swapped * sin


def _keys_values(ckv_n, krope, w_ukv_ref, gk_ref, cos, sin, k_ref, v_ref):
    kvf = jnp.dot(ckv_n.astype(BF16), w_ukv_ref[...], preferred_element_type=F32)
    kr_ss = jnp.sum(krope * krope, axis=-1, keepdims=True)
    gk = gk_ref[...]
    for h in range(MLA_HEADS):
        kn = kvf[:, h * MLA_NOPE:(h + 1) * MLA_NOPE]
        r = lax.rsqrt((jnp.sum(kn * kn, axis=-1, keepdims=True) + kr_ss) * (1.0 / MLA_QK) + NORM_EPS)
        tail = krope * r * gk[:, MLA_NOPE:]
        if cos is not None:
            tail = _rope_tile(tail, cos, sin)
        k_ref[:, h * HEAD_PAD:h * HEAD_PAD + MLA_NOPE] = (kn * r * gk[:, :MLA_NOPE]).astype(BF16)
        k_ref[:, h * HEAD_PAD + MLA_NOPE:(h + 1) * HEAD_PAD] = tail.astype(BF16)
    v_ref[...] = kvf[:, MLA_HEADS * MLA_NOPE:].astype(BF16)


def _qkv_kernel(npt, q_lora, cq_ref, kv_ref, cos_ref, sin_ref, w_uq_ref, w_ukv_ref, gql_ref, gkv_ref,
                gq_ref, gk_ref, q_ref, k_ref, v_ref, st_ref):
    i = pl.program_id(0)
    cos = cos_ref[...]
    sin = sin_ref[...]
    cq = cq_ref[...].astype(F32)
    cqn = cq * _inv_rms(cq, q_lora) * gql_ref[...]
    qf = jnp.dot(cqn.astype(BF16), w_uq_ref[...], preferred_element_type=F32)
    gq = gq_ref[...]
    scale = MLA_QK ** -0.5
    for h in range(MLA_HEADS):
        qh = qf[:, h * HEAD_PAD:(h + 1) * HEAD_PAD]
        qh = qh * (_inv_rms(qh, MLA_QK) * scale) * gq
        q_ref[:, h * HEAD_PAD:h * HEAD_PAD + MLA_NOPE] = qh[:, :MLA_NOPE].astype(BF16)
        q_ref[:, h * HEAD_PAD + MLA_NOPE:(h + 1) * HEAD_PAD] = _rope_tile(qh[:, MLA_NOPE:], cos, sin).astype(BF16)

    kv = kv_ref[...]
    ckv = kv[:, :MLA_NOPE]
    krope = kv[:, MLA_NOPE:]
    ckv_n = ckv * _inv_rms(ckv, ckv.shape[-1]) * gkv_ref[...]
    _keys_values(ckv_n, krope, w_ukv_ref, gk_ref, cos, sin, k_ref, v_ref)

    @pl.when(i < npt)
    def _():
        st_ref[:, :MLA_NOPE] = ckv_n
        st_ref[:, MLA_NOPE:] = krope


def _qkv(proj, kv, cos_t, sin_t, w_uq, w_ukv, gql, gkv, gq, gk, tp, dec_seq, q_lora, tm):
    t = proj.shape[0]
    npt = tp // tm
    nseq = dec_seq // tm
    cq_blk = (PROJ_W - 512) // 512

    def tab(i):
        return (jnp.where(i < npt, nseq, (i - npt) % nseq), 0)

    const = lambda i: (0, 0)
    return pl.pallas_call(
        functools.partial(_qkv_kernel, npt, q_lora),
        grid=(t // tm,),
        in_specs=[pl.BlockSpec((tm, 512), lambda i: (i, cq_blk)),
                  pl.BlockSpec((tm, KV_W), lambda i: (i, 0)),
                  pl.BlockSpec((tm, LANE), tab),
                  pl.BlockSpec((tm, LANE), tab),
                  pl.BlockSpec(w_uq.shape, const),
                  pl.BlockSpec(w_ukv.shape, const),
                  pl.BlockSpec(gql.shape, const),
                  pl.BlockSpec(gkv.shape, const),
                  pl.BlockSpec(gq.shape, const),
                  pl.BlockSpec(gk.shape, const)],
        out_specs=[pl.BlockSpec((tm, MLA_HEADS * HEAD_PAD), lambda i: (i, 0)),
                   pl.BlockSpec((tm, MLA_HEADS * HEAD_PAD), lambda i: (i, 0)),
                   pl.BlockSpec((tm, MLA_HEADS * MLA_V), lambda i: (i, 0)),
                   pl.BlockSpec((tm, KV_W), lambda i: (jnp.minimum(i, npt - 1), 0))],
        out_shape=[jax.ShapeDtypeStruct((t, MLA_HEADS * HEAD_PAD), BF16),
                   jax.ShapeDtypeStruct((t, MLA_HEADS * HEAD_PAD), BF16),
                   jax.ShapeDtypeStruct((t, MLA_HEADS * MLA_V), BF16),
                   jax.ShapeDtypeStruct((tp, KV_W), F32)],
        compiler_params=_cparams(("arbitrary",), 40),
        name="qkv",
    )(proj, kv, cos_t, sin_t, w_uq, w_ukv, gql, gkv, gq, gk)


def _ctx_kv_kernel(kv_ref, w_ukv_ref, gk_ref, k_ref, v_ref):
    kv = kv_ref[...]
    _keys_values(kv[:, :MLA_NOPE], kv[:, MLA_NOPE:], w_ukv_ref, gk_ref, None, None, k_ref, v_ref)


def _ctx_kv(kvc, w_ukv, gk, tm):
    r = kvc.shape[0]
    const = lambda i: (0, 0)
    return pl.pallas_call(
        _ctx_kv_kernel,
        grid=(r // tm,),
        in_specs=[pl.BlockSpec((tm, KV_W), lambda i: (i, 0)),
                  pl.BlockSpec(w_ukv.shape, const),
                  pl.BlockSpec(gk.shape, const)],
        out_specs=[pl.BlockSpec((tm, MLA_HEADS * HEAD_PAD), lambda i: (i, 0)),
                   pl.BlockSpec((tm, MLA_HEADS * MLA_V), lambda i: (i, 0))],
        out_shape=[jax.ShapeDtypeStruct((r, MLA_HEADS * HEAD_PAD), BF16),
                   jax.ShapeDtypeStruct((r, MLA_HEADS * MLA_V), BF16)],
        compiler_params=_cparams(("arbitrary",), 32),
        name="ctx_kv",
    )(kvc, w_ukv, gk)


_NT = (((1,), (1,)), ((), ()))


def _attn_kernel(has_ctx, *refs):
    if has_ctx:
        q_ref, kc_ref, vc_ref, k_ref, v_ref, o_ref = refs
    else:
        q_ref, k_ref, v_ref, o_ref = refs
    q = q_ref[...]
    s = lax.dot_general(q, k_ref[...], _NT, preferred_element_type=F32)
    m = jnp.max(s, axis=-1, keepdims=True)
    if has_ctx:
        sc = lax.dot_general(q, kc_ref[...], _NT, preferred_element_type=F32)
        m = jnp.maximum(m, jnp.max(sc, axis=-1, keepdims=True))
    p = jnp.exp(s - m)
    l = jnp.sum(p, axis=-1, keepdims=True)
    o = jnp.dot(p.astype(BF16), v_ref[...], preferred_element_type=F32)
    if has_ctx:
        pc = jnp.exp(sc - m)
        l = l + jnp.sum(pc, axis=-1, keepdims=True)
        o = o + jnp.dot(pc.astype(BF16), vc_ref[...], preferred_element_type=F32)
    o_ref[...] = (o / l).astype(BF16)


def _attention_latent(q, k, v, kc, vc, tp, nb, dec_seq, past, tq):
    t = q.shape[0]
    nq = dec_seq // tq
    row0 = tp // tq
    kblk0 = tp // dec_seq
    return pl.pallas_call(
        functools.partial(_attn_kernel, True),
        grid=(nb, MLA_HEADS, nq),
        in_specs=[pl.BlockSpec((tq, HEAD_PAD), lambda b, h, i: (row0 + b * nq + i, h)),
                  pl.BlockSpec((past, HEAD_PAD), lambda b, h, i: (b, h)),
                  pl.BlockSpec((past, MLA_V), lambda b, h, i: (b, h)),
                  pl.BlockSpec((dec_seq, HEAD_PAD), lambda b, h, i: (kblk0 + b, h)),
                  pl.BlockSpec((dec_seq, MLA_V), lambda b, h, i: (kblk0 + b, h))],
        out_specs=pl.BlockSpec((tq, MLA_V), lambda b, h, i: (b * nq + i, h)),
        out_shape=jax.ShapeDtypeStruct((t - tp, MLA_HEADS * MLA_V), BF16),
        compiler_params=_cparams(("arbitrary", "arbitrary", "arbitrary"), 48),
        name="attn_latent",
    )(q, kc, vc, k, v)


def _attention_context(q, k, v, nseq, seq):
    return pl.pallas_call(
        functools.partial(_attn_kernel, False),
        grid=(nseq, MLA_HEADS),
        in_specs=[pl.BlockSpec((seq, HEAD_PAD), lambda b, h: (b, h)),
                  pl.BlockSpec((seq, HEAD_PAD), lambda b, h: (b, h)),
                  pl.BlockSpec((seq, MLA_V), lambda b, h: (b, h))],
        out_specs=pl.BlockSpec((seq, MLA_V), lambda b, h: (b, h)),
        out_shape=jax.ShapeDtypeStruct((nseq * seq, MLA_HEADS * MLA_V), BF16),
        compiler_params=_cparams(("arbitrary", "arbitrary"), 32),
        name="attn_context",
    )(q, k, v)


def _mixers_kernel(npt, p_tiles, s_tiles,
                   cv_ref, cvp_ref, cvn_ref, gm_ref, sc_ref, scp_ref, scn_ref, bg_ref,
                   wdw_ref, bdw_ref, gcl_ref, bcl_ref, wpw_ref, bpw_ref,
                   gsl_ref, bsl_ref, wsp_ref, bsp_ref, wsc_ref,
                   o_ref, zs, ys):
    i = pl.program_id(0)
    tm = cv_ref.shape[0]
    cw = cv_ref.shape[1] // 2
    pos = jnp.where(i < npt, i % p_tiles, (i - npt) % s_tiles)
    n_tiles = jnp.where(i < npt, p_tiles, s_tiles)
    keep_prev = (pos > 0).astype(F32)
    keep_next = (pos < n_tiles - 1).astype(F32)

    def glu(ref):
        a = ref[:, :cw].astype(F32)
        g = ref[:, cw:].astype(F32)
        return a * jax.nn.sigmoid(g)

    def prod(ref):
        return ref[:, :cw].astype(F32) * ref[:, cw:].astype(F32)

    zs[0:HALO, :] = glu(cvp_ref) * keep_prev
    zs[HALO:HALO + tm, :] = glu(cv_ref)
    zs[HALO + tm:, :] = glu(cvn_ref) * keep_next
    pad = CONV_K // 2
    half = tm // 2
    for c in range(cw // LANE):
        cs = slice(c * LANE, (c + 1) * LANE)
        for rh in range(2):
            base = rh * half
            acc = jnp.zeros((half, LANE), F32)
            for r in range(8):
                zr = zs[pl.ds(base + r, half + 24), cs]
                for a in range(4):
                    tap = 8 * a + r - (HALO - pad)
                    if 0 <= tap < CONV_K:
                        acc = acc + zr[8 * a:8 * a + half, :] * wdw_ref[tap:tap + 1, cs]
            ys[base:base + half, cs] = acc
    z = ys[0:tm, :] + bdw_ref[...]
    z = _layernorm(z, gcl_ref[...], bcl_ref[...])
    z = z * jax.nn.sigmoid(z)
    o_b = jnp.dot(z.astype(BF16), wpw_ref[...], preferred_element_type=F32) + bpw_ref[...]
    o_ref[:, 0:cw] = o_b.astype(BF16)

    u = gm_ref[:, :cw].astype(F32)
    vg = _layernorm(gm_ref[:, cw:].astype(F32), gsl_ref[...], bsl_ref[...]).astype(BF16)
    hc = cw // GMLP_HEADS
    for n in range(tm // CHUNK):
        rows = slice(n * CHUNK, (n + 1) * CHUNK)
        for h in range(GMLP_HEADS):
            cols = slice(h * hc, (h + 1) * hc)
            mixed = jnp.dot(wsp_ref[h], vg[rows, cols], preferred_element_type=F32) + bsp_ref[:, cols]
            o_ref[rows, cw + h * hc:cw + (h + 1) * hc] = (u[rows, cols] * mixed).astype(BF16)

    ys[0:HALO, :] = prod(scp_ref) * keep_prev
    ys[HALO:HALO + tm, :] = prod(sc_ref)
    ys[HALO + tm:, :] = prod(scn_ref) * keep_next
    acc = jnp.zeros((tm, cw), F32)
    for tap in range(SC_K):
        acc = acc + ys[pl.ds(HALO - SC_K // 2 + tap, tm), :] * wsc_ref[tap:tap + 1, :]
    o_ref[:, 2 * cw:3 * cw] = (bg_ref[...].astype(F32) * acc).astype(BF16)


def _mixers(proj, weights, tp, seq, dec_seq, tm):
    t = proj.shape[0]
    cw = 512
    npt = tp // tm
    hb = tm // HALO
    last = t // HALO - 1
    const2 = lambda i: (0, 0)

    def main(col):
        return pl.BlockSpec((tm, 2 * cw), lambda i: (i, col))

    def prev(col):
        return pl.BlockSpec((HALO, 2 * cw), lambda i: (jnp.maximum(i * hb - 1, 0), col))

    def nxt(col):
        return pl.BlockSpec((HALO, 2 * cw), lambda i: (jnp.minimum((i + 1) * hb, last), col))

    w_specs = [pl.BlockSpec(w.shape, const2 if w.ndim == 2 else (lambda i: (0, 0, 0))) for w in weights]
    return pl.pallas_call(
        functools.partial(_mixers_kernel, npt, seq // tm, dec_seq // tm),
        grid=(t // tm,),
        in_specs=[main(0), prev(0), nxt(0), main(1), main(2), prev(2), nxt(2),
                  pl.BlockSpec((tm, cw), lambda i: (i, 6))] + w_specs,
        out_specs=pl.BlockSpec((tm, 3 * cw), lambda i: (i, 0)),
        out_shape=jax.ShapeDtypeStruct((t, 3 * cw), BF16),
        scratch_shapes=[pltpu.VMEM((tm + 2 * HALO, cw), F32), pltpu.VMEM((tm + 2 * HALO, cw), F32)],
        compiler_params=_cparams(("arbitrary",), 40),
        name="mixers",
    )(proj, proj, proj, proj, proj, proj, proj, proj, *weights)


def _out_proj_kernel(npt, oap_ref, oas_ref, ob_ref, xp_ref, xs_ref, mod_ref, g_ref, wo_ref, wr_ref,
                     xn_ref, hp_ref, r_ref):
    i = pl.program_id(0)
    ka = oap_ref.shape[1]
    x = jnp.where(i < npt, xp_ref[...], xs_ref[...])
    oa = jnp.where(i < npt, oap_ref[...], oas_ref[...])
    mix = jnp.dot(oa, wo_ref[0:ka, :], preferred_element_type=F32)
    mix += jnp.dot(ob_ref[...], wo_ref[ka:, :], preferred_element_type=F32)
    xn = x + mod_ref[2:3, :] * mix
    xn_ref[...] = xn
    h = xn * _inv_rms(xn, xn.shape[-1]) * g_ref[...]
    h = h * (1.0 + mod_ref[4:5, :]) + mod_ref[3:4, :]
    hp_ref[...] = _pack_rows(h)
    h_hi, h_lo = _split_bf16(h)
    r_ref[...] = (lax.dot_general(wr_ref[...], h_hi, _NT, preferred_element_type=F32)
                  + lax.dot_general(wr_ref[...], h_lo, _NT, preferred_element_type=F32))


def _out_proj(oap, oas, ob, xp, xs, mod_l, g, wo, wr, tm, mod_index):
    tp, d = xp.shape
    t = ob.shape[0]
    npt = tp // tm
    const = lambda i: (0, 0)

    def pair(w):
        return [pl.BlockSpec((tm, w), lambda i: (jnp.minimum(i, npt - 1), 0)),
                pl.BlockSpec((tm, w), lambda i: (jnp.maximum(i - npt, 0), 0))]

    return pl.pallas_call(
        functools.partial(_out_proj_kernel, npt),
        grid=(t // tm,),
        in_specs=pair(oap.shape[1]) + [pl.BlockSpec((tm, ob.shape[1]), lambda i: (i, 0))] + pair(d) + [
                  pl.BlockSpec((None, 6, d), lambda i: (mod_index(i, tm), 0, 0)),
                  pl.BlockSpec((1, d), const),
                  pl.BlockSpec(wo.shape, const),
                  pl.BlockSpec(wr.shape, const)],
        out_specs=[pl.BlockSpec((tm, d), lambda i: (i, 0)),
                   pl.BlockSpec((tm, d // 2), lambda i: (i, 0)),
                   pl.BlockSpec((LANE, tm), lambda i: (0, i))],
        out_shape=[jax.ShapeDtypeStruct((t, d), F32),
                   jax.ShapeDtypeStruct((t, d // 2), U32),
                   jax.ShapeDtypeStruct((LANE, t), F32)],
        compiler_params=_cparams(("arbitrary",), 48),
        name="out_proj",
    )(oap, oas, ob, xp, xs, mod_l, g, wo, wr)


N_BUCKET_PAD = 64


def _route_kernel(rt_ref, b_ref, tri_ref, bucket_ref, rank_ref, gate_ref, cnt_ref, run):
    i = pl.program_id(0)
    tm = rt_ref.shape[1]
    ne = N_GROUPS * GROUP_SIZE

    @pl.when(i == 0)
    def _():
        run[...] = jnp.zeros_like(run)

    sc = jax.nn.sigmoid(rt_ref[0:ne, :] + rt_ref[ne:2 * ne, :])
    sel = sc + b_ref[...]
    s = [sel[r * N_GROUPS:(r + 1) * N_GROUPS, :] for r in range(GROUP_SIZE)]
    c = [sc[r * N_GROUPS:(r + 1) * N_GROUPS, :] for r in range(GROUP_SIZE)]
    hi01, lo01 = jnp.maximum(s[0], s[1]), jnp.minimum(s[0], s[1])
    hi23, lo23 = jnp.maximum(s[2], s[3]), jnp.minimum(s[2], s[3])
    g_score = jnp.maximum(hi01, hi23) + jnp.maximum(jnp.minimum(hi01, hi23), jnp.maximum(lo01, lo23))
    grp = lax.broadcasted_iota(jnp.int32, g_score.shape, 0)
    g_best = jnp.min(jnp.where(g_score == jnp.max(g_score, axis=0, keepdims=True), grp, N_GROUPS),
                     axis=0, keepdims=True)
    own = grp == g_best
    v = [jnp.sum(jnp.where(own, s[r], 0.0), axis=0, keepdims=True) for r in range(GROUP_SIZE)]
    w = [jnp.sum(jnp.where(own, c[r], 0.0), axis=0, keepdims=True) for r in range(GROUP_SIZE)]

    def first_max(vals):
        best, idx, gate = vals[0], jnp.zeros_like(g_best), w[0]
        for r in range(1, GROUP_SIZE):
            upd = vals[r] > best
            best = jnp.where(upd, vals[r], best)
            idx = jnp.where(upd, r, idx)
            gate = jnp.where(upd, w[r], gate)
        return idx, gate

    r1, w1 = first_max(v)
    r2, w2 = first_max([jnp.where(r1 == r, -jnp.inf, v[r]) for r in range(GROUP_SIZE)])
    wsum = w1 + w2
    swap = r1 > r2
    r_lo = jnp.where(swap, r2, r1)
    r_hi = jnp.where(swap, r1, r2)
    gate_ref[0:1, :] = jnp.where(swap, w2, w1) / wsum
    gate_ref[1:2, :] = jnp.where(swap, w1, w2) / wsum
    pair = r_lo * 3 - ((r_lo * (r_lo - 1)) >> 1) + (r_hi - r_lo - 1)
    bucket = g_best * N_PAIRS + pair
    bucket_ref[...] = bucket

    onehot = (lax.broadcasted_iota(jnp.int32, (N_BUCKET_PAD, tm), 0) == bucket).astype(F32)
    before = jnp.dot(onehot.astype(BF16), tri_ref[...], preferred_element_type=F32) + run[...]
    rank_ref[...] = jnp.sum(onehot * before, axis=0, keepdims=True).astype(jnp.int32)
    run[...] += jnp.sum(onehot, axis=1, keepdims=True)

    @pl.when(i == pl.num_programs(0) - 1)
    def _():
        cnt_ref[...] = run[...]


def _route(rt, router_b, n_blk, tm):
    t = rt.shape[1]
    ne = router_b.shape[0]
    b_col = router_b.astype(F32).reshape(N_GROUPS, GROUP_SIZE).T.reshape(ne, 1)
    tri = (jnp.arange(tm)[:, None] < jnp.arange(tm)[None, :]).astype(BF16)
    bucket, rank, gate, cnt = pl.pallas_call(
        _route_kernel,
        grid=(t // tm,),
        in_specs=[pl.BlockSpec((LANE, tm), lambda i: (0, i)),
                  pl.BlockSpec((ne, 1), lambda i: (0, 0)),
                  pl.BlockSpec((tm, tm), lambda i: (0, 0))],
        out_specs=[pl.BlockSpec((1, tm), lambda i: (0, i)),
                   pl.BlockSpec((1, tm), lambda i: (0, i)),
                   pl.BlockSpec((2, tm), lambda i: (0, i)),
                   pl.BlockSpec((N_BUCKET_PAD, 1), lambda i: (0, 0))],
        out_shape=[jax.ShapeDtypeStruct((1, t), jnp.int32),
                   jax.ShapeDtypeStruct((1, t), jnp.int32),
                   jax.ShapeDtypeStruct((2, t), F32),
                   jax.ShapeDtypeStruct((N_BUCKET_PAD, 1), F32)],
        scratch_shapes=[pltpu.VMEM((N_BUCKET_PAD, 1), F32)],
        compiler_params=_cparams(("arbitrary",), 32),
        name="route",
    )(rt, b_col, tri)

    n_bucket = N_GROUPS * N_PAIRS
    counts = cnt[:n_bucket, 0].astype(jnp.int32)
    padded = (counts + MOE_BM - 1) // MOE_BM * MOE_BM
    pad_end = jnp.cumsum(padded)
    pad_start = pad_end - padded
    bucket = bucket[0]
    sel = bucket[:, None] == jnp.arange(n_bucket, dtype=jnp.int32)[None, :]
    pos = jnp.sum(jnp.where(sel, pad_start[None, :], 0), axis=1).astype(jnp.int32) + rank[0]
    n_used = (pad_end[-1] // MOE_BM).astype(jnp.int32)
    blk = jnp.minimum(jnp.arange(n_blk, dtype=jnp.int32), n_used - 1)
    blk_bucket = jnp.sum((pad_end[None, :] <= (blk * MOE_BM)[:, None]).astype(jnp.int32), axis=1)
    blk_bucket = jnp.minimum(blk_bucket, n_bucket - 1)
    pair_lo = jnp.array([0, 0, 0, 1, 1, 2], jnp.int32)
    pair_hi = jnp.array([1, 2, 3, 2, 3, 3], jnp.int32)
    blk_pair = blk_bucket % N_PAIRS
    first = (blk_bucket // N_PAIRS) * GROUP_SIZE
    is_pair = blk_pair[:, None] == jnp.arange(N_PAIRS, dtype=jnp.int32)[None, :]
    blk_a = first + jnp.sum(jnp.where(is_pair, pair_lo[None, :], 0), axis=1)
    blk_b = first + jnp.sum(jnp.where(is_pair, pair_hi[None, :], 0), axis=1)
    gates = jnp.pad(gate.T, ((0, 0), (0, GATE_W - 2)))
    return pos, gates, blk, blk_a.astype(jnp.int32), blk_b.astype(jnp.int32), n_used.reshape(1)


def _dispatch_kernel(pos_ref, hp_ref, gt_ref, init_ref, xs_ref, buf, sem):
    del init_ref
    tm = hp_ref.shape[0]
    w = hp_ref.shape[1]
    buf[:, :w] = hp_ref[...]
    buf[:, w:] = pltpu.bitcast(gt_ref[...], U32)

    def row_copy(r):
        return pltpu.make_async_copy(buf.at[pl.ds(r, 1)], xs_ref.at[pl.ds(pos_ref[0, 0, r], 1)], sem)

    def start(r, carry):
        row_copy(r).start()
        return carry

    def wait(r, carry):
        row_copy(r).wait()
        return carry

    lax.fori_loop(0, tm, start, 0, unroll=8)
    lax.fori_loop(0, tm, wait, 0, unroll=8)


def _dispatch(hp, gates, pos, n_buf, tm):
    t, w = hp.shape
    init = jnp.zeros((n_buf, w + GATE_W), U32)
    return pl.pallas_call(
        _dispatch_kernel,
        grid=(t // tm,),
        in_specs=[pl.BlockSpec((1, 1, tm), lambda i: (i, 0, 0), memory_space=pltpu.SMEM),
                  pl.BlockSpec((tm, w), lambda i: (i, 0)),
                  pl.BlockSpec((tm, GATE_W), lambda i: (i, 0)),
                  pl.BlockSpec(memory_space=pl.ANY)],
        out_specs=pl.BlockSpec(memory_space=pl.ANY),
        out_shape=jax.ShapeDtypeStruct((n_buf, w + GATE_W), U32),
        scratch_shapes=[pltpu.VMEM((tm, w + GATE_W), U32), pltpu.SemaphoreType.DMA(())],
        input_output_aliases={3: 0},
        compiler_params=_cparams(("arbitrary",), 32),
        name="dispatch",
    )(pos.reshape(t // tm, 1, tm), hp, gates, init)


def _moe_kernel(blk_ref, ea_ref, eb_ref, nu_ref, x_ref, w13a_ref, w13b_ref, w2a_ref, w2b_ref, y_ref):
    del blk_ref, ea_ref, eb_ref
    i = pl.program_id(0)

    @pl.when(i < nu_ref[0])
    def _():
        w = x_ref.shape[1] - GATE_W
        xa, xb = _unpack_rows(x_ref[:, :w])
        x = jnp.concatenate([xa.astype(BF16), xb.astype(BF16)], axis=1)
        gates = pltpu.bitcast(x_ref[:, w:], F32)
        de = w2a_ref.shape[0]

        def hidden(w13_ref, g):
            h = jnp.dot(x, w13_ref[...], preferred_element_type=F32)
            a = h[:, :de]
            return (a * jax.nn.sigmoid(a) * h[:, de:] * g).astype(BF16)

        y = jnp.dot(hidden(w13a_ref, gates[:, 0:1]), w2a_ref[...], preferred_element_type=F32)
        y += jnp.dot(hidden(w13b_ref, gates[:, 1:2]), w2b_ref[...], preferred_element_type=F32)
        y_ref[...] = _pack_rows(y)

    @pl.when(i >= nu_ref[0])
    def _():
        y_ref[...] = jnp.zeros_like(y_ref)


def _moe(xs, w13, w2, blk, blk_a, blk_b, n_used):
    n_buf, wx = xs.shape
    n_blk = n_buf // MOE_BM
    _, d, de2 = w13.shape
    de = de2 // 2
    grid_spec = pltpu.PrefetchScalarGridSpec(
        num_scalar_prefetch=4,
        grid=(n_blk,),
        in_specs=[pl.BlockSpec((MOE_BM, wx), lambda i, blk, ea, eb, nu: (blk[i], 0)),
                  pl.BlockSpec((None, d, de2), lambda i, blk, ea, eb, nu: (ea[i], 0, 0)),
                  pl.BlockSpec((None, d, de2), lambda i, blk, ea, eb, nu: (eb[i], 0, 0)),
                  pl.BlockSpec((None, de, d), lambda i, blk, ea, eb, nu: (ea[i], 0, 0)),
                  pl.BlockSpec((None, de, d), lambda i, blk, ea, eb, nu: (eb[i], 0, 0))],
        out_specs=pl.BlockSpec((MOE_BM, d // 2), lambda i, blk, ea, eb, nu: (i, 0)),
    )
    return pl.pallas_call(
        _moe_kernel,
        grid_spec=grid_spec,
        out_shape=jax.ShapeDtypeStruct((n_buf, d // 2), U32),
        compiler_params=_cparams(("arbitrary",), 48),
        name="moe",
    )(blk, blk_a, blk_b, n_used, xs, w13, w13, w2, w2)


def _combine_kernel(npt, pos_ref, xn_ref, mod_ref, yb_ref, op_ref, os_ref, buf, sem):
    i = pl.program_id(0)
    tm = xn_ref.shape[0]

    def row_copy(r):
        return pltpu.make_async_copy(yb_ref.at[pl.ds(pos_ref[0, 0, r], 1)], buf.at[pl.ds(r, 1)], sem)

    def start(r, carry):
        row_copy(r).start()
        return carry

    def wait(r, carry):
        row_copy(r).wait()
        return carry

    lax.fori_loop(0, tm, start, 0, unroll=8)
    lax.fori_loop(0, tm, wait, 0, unroll=8)
    ya, yb = _unpack_rows(buf[...])
    out = xn_ref[...] + mod_ref[5:6, :] * jnp.concatenate([ya, yb], axis=1)

    @pl.when(i < npt)
    def _():
        op_ref[...] = out

    @pl.when(i >= npt)
    def _():
        os_ref[...] = out


def _combine(xn, yb, pos, mod_l, tp, tm, mod_index):
    t, d = xn.shape
    npt = tp // tm
    return pl.pallas_call(
        functools.partial(_combine_kernel, npt),
        grid=(t // tm,),
        in_specs=[pl.BlockSpec((1, 1, tm), lambda i: (i, 0, 0), memory_space=pltpu.SMEM),
                  pl.BlockSpec((tm, d), lambda i: (i, 0)),
                  pl.BlockSpec((None, 6, d), lambda i: (mod_index(i, tm), 0, 0)),
                  pl.BlockSpec(memory_space=pl.ANY)],
        out_specs=[pl.BlockSpec((tm, d), lambda i: (jnp.minimum(i, npt - 1), 0)),
                   pl.BlockSpec((tm, d), lambda i: (jnp.maximum(i - npt, 0), 0))],
        out_shape=[jax.ShapeDtypeStruct((tp, d), F32),
                   jax.ShapeDtypeStruct((t - tp, d), F32)],
        scratch_shapes=[pltpu.VMEM((tm, d // 2), U32), pltpu.SemaphoreType.DMA(())],
        compiler_params=_cparams(("arbitrary",), 32),
        name="combine",
    )(pos.reshape(t // tm, 1, tm), xn, mod_l, yb)


def _pad_cols(x, n):
    return jnp.pad(x, ((0, 0), (0, n - x.shape[1])))


def _rope_tables(dec_seq, tm):
    n_freq = MLA_ROPE // 4
    pos = jnp.arange(dec_seq, dtype=jnp.int32)
    row = (pos // GRID_W).astype(F32)
    col = (pos % GRID_W).astype(F32)
    inv = ROPE_THETA ** (-jnp.arange(n_freq, dtype=F32) / n_freq)
    ar = row[:, None] * inv[None, :]
    ac = col[:, None] * inv[None, :]
    ones = jnp.ones((dec_seq, LANE - MLA_ROPE), F32)
    cos = jnp.concatenate([jnp.cos(ar), jnp.cos(ar), jnp.cos(ac), jnp.cos(ac), ones], axis=1)
    sin = jnp.concatenate([-jnp.sin(ar), jnp.sin(ar), -jnp.sin(ac), jnp.sin(ac), 0.0 * ones], axis=1)
    ident_c = jnp.ones((tm, LANE), F32)
    ident_s = jnp.zeros((tm, LANE), F32)
    return jnp.concatenate([cos, ident_c], axis=0), jnp.concatenate([sin, ident_s], axis=0)


def _layer_params(l, q_lora, kv_lora, w_in, g_q_lora, w_uq, g_kv_lora, w_ukv, g_qk_q, g_qk_k,
                  w_dw31, b_dw31, g_conv_ln, b_conv_ln, w_pw, b_pw, g_sgu_ln, b_sgu_ln,
                  w_spatial, b_spatial, w_sc3, w_out):
    d = w_in.shape[1]
    o1 = q_lora
    o2 = o1 + kv_lora
    o3 = o2 + MLA_ROPE
    o4 = o3 + 1024
    o5 = o4 + 1024
    w = w_in[l]
    zeros = lambda n: jnp.zeros((d, n), w.dtype)
    w_in_p = jnp.concatenate(
        [w[:, o3:o4], w[:, o4:o5], w[:, o5 + 512:o5 + 1536], w[:, o5:o5 + 512],
         w[:, :o1], zeros(512 - q_lora),
         w[:, o1:o2], w[:, o2:o3], zeros(IN_TN - kv_lora - MLA_ROPE)], axis=1).astype(BF16)

    uq = w_uq[l].reshape(q_lora, MLA_HEADS, MLA_QK)
    uq = jnp.pad(uq, ((0, 512 - q_lora), (0, 0), (0, HEAD_PAD - MLA_QK)))
    w_uq_p = uq.reshape(512, MLA_HEADS * HEAD_PAD).astype(BF16)
    ukv = w_ukv[l].reshape(kv_lora, MLA_HEADS, MLA_NOPE + MLA_V)
    w_ukv_p = jnp.concatenate([ukv[:, :, :MLA_NOPE].reshape(kv_lora, -1),
                               ukv[:, :, MLA_NOPE:].reshape(kv_lora, -1)], axis=1).astype(BF16)
    row = lambda v: v.reshape(1, -1).astype(F32)
    qkv_w = (w_uq_p, w_ukv_p, _pad_cols(row(g_q_lora[l]), 512), row(g_kv_lora[l]),
             _pad_cols(row(g_qk_q[l]), HEAD_PAD), _pad_cols(row(g_qk_k[l]), HEAD_PAD))

    hc = w_pw.shape[1] // GMLP_HEADS
    bsp = jnp.repeat(b_spatial[l].T.astype(F32), hc, axis=1)
    mix_w = (jnp.pad(w_dw31[l].astype(F32), ((0, 32 - CONV_K), (0, 0))), row(b_dw31[l]),
             row(g_conv_ln[l]), row(b_conv_ln[l]), w_pw[l].astype(BF16), row(b_pw[l]),
             row(g_sgu_ln[l]), row(b_sgu_ln[l]), w_spatial[l].astype(BF16), bsp,
             jnp.pad(w_sc3[l].astype(F32), ((0, 8 - SC_K), (0, 0))))
    return w_in_p, qkv_w, mix_w, w_out[l].astype(BF16)


def _tile(limit, *sizes):
    tm = limit
    while any(s % tm for s in sizes):
        tm //= 2
    return tm


def kernel(x_prompt, x_sample, cache_mla_ckv, cache_mla_krope, c, c_ctx, w_mod, b_mod, g_norm1, g_norm2, w_in, g_q_lora, w_uq, g_kv_lora, w_ukv, g_qk_q, g_qk_k, w_dw31, b_dw31, g_conv_ln, b_conv_ln, w_pw, b_pw, g_sgu_ln, b_sgu_ln, w_spatial, b_spatial, w_sc3, w_out, router_w, router_b, w1, w3, w2):
    nseq, seq, d = x_prompt.shape
    nb, dec_seq, _ = x_sample.shape
    depth = w_mod.shape[0]
    past = cache_mla_ckv.shape[2]
    q_lora = g_q_lora.shape[1]
    kv_lora = g_kv_lora.shape[1]
    n_exp = router_w.shape[1]
    tp = nseq * seq
    ts = nb * dec_seq
    t = tp + ts
    assert tp % dec_seq == 0 and seq % CHUNK == 0 and dec_seq % GRID_W == 0
    assert nb + 1 <= 8 and kv_lora == MLA_NOPE and n_exp == N_GROUPS * GROUP_SIZE

    tm_in = _tile(512, tp, dec_seq)
    tm_qkv = _tile(512, tp, dec_seq)
    tm_row = _tile(256, seq, dec_seq)
    tq = _tile(256, dec_seq)

    def mod_index(i, tm):
        npt = tp // tm
        return jnp.where(i < npt, 0, 1 + (i - npt) // (dec_seq // tm))

    cvec = jnp.concatenate([c_ctx[None, :], c, jnp.zeros((8 - 1 - nb, d), F32)], axis=0)
    mod = _modulation(cvec, w_mod, b_mod).reshape(depth, 8, 6, d)
    cos_t, sin_t = _rope_tables(dec_seq, tm_qkv)

    rw = router_w.T.reshape(N_GROUPS, GROUP_SIZE, d).transpose(1, 0, 2).reshape(n_exp, d)
    rw_hi = rw.astype(BF16)
    rw_lo = (rw - rw_hi.astype(F32)).astype(BF16)
    wr = jnp.pad(jnp.concatenate([rw_hi, rw_lo], axis=0), ((0, LANE - 2 * n_exp), (0, 0)))
    w13 = jnp.concatenate([w1, w3], axis=-1).astype(BF16)
    w2b = w2.astype(BF16)
    n_blk = (t + N_GROUPS * N_PAIRS * (MOE_BM - 1) + MOE_BM - 1) // MOE_BM

    xp = x_prompt.reshape(tp, d)
    xs = x_sample.reshape(ts, d)
    ckv_states = []
    krope_states = []
    for l in range(depth):
        w_in_p, qkv_w, mix_w, w_out_b = _layer_params(
            l, q_lora, kv_lora, w_in, g_q_lora, w_uq, g_kv_lora, w_ukv, g_qk_q, g_qk_k,
            w_dw31, b_dw31, g_conv_ln, b_conv_ln, w_pw, b_pw, g_sgu_ln, b_sgu_ln,
            w_spatial, b_spatial, w_sc3, w_out)
        mod_l = mod[l]

        proj, kv = _in_proj(xp, xs, mod_l, g_norm1[l].reshape(1, d), w_in_p, tm_in, mod_index)
        q, k, v, state = _qkv(proj, kv, cos_t, sin_t, *qkv_w, tp, dec_seq, q_lora, tm_qkv)
        kvc = jnp.concatenate([cache_mla_ckv[:, l], cache_mla_krope[:, l],
                               jnp.zeros((nb, past, KV_W - kv_lora - MLA_ROPE), F32)], axis=-1)
        kc, vc = _ctx_kv(kvc.reshape(nb * past, KV_W), qkv_w[1], qkv_w[5], _tile(512, past))
        o_as = _attention_latent(q, k, v, kc, vc, tp, nb, dec_seq, past, tq)
        o_ap = _attention_context(q, k, v, nseq, seq)
        o_bcd = _mixers(proj, mix_w, tp, seq, dec_seq, tm_row)

        xn, hp, r = _out_proj(o_ap, o_as, o_bcd, xp, xs, mod_l, g_norm2[l].reshape(1, d), w_out_b, wr,
                              tm_row, mod_index)
        pos, gates, blk, blk_a, blk_b, n_used = _route(r, router_b, n_blk, tm_qkv)
        xd = _dispatch(hp, gates, pos, n_blk * MOE_BM, tm_row)
        yb = _moe(xd, w13[l], w2b[l], blk, blk_a, blk_b, n_used)
        xp, xs = _combine(xn, yb, pos, mod_l, tp, tm_row, mod_index)

        ckv_states.append(state[:, :kv_lora].reshape(nseq, seq, kv_lora))
        krope_states.append(state[:, kv_lora:kv_lora + MLA_ROPE].reshape(nseq, seq, MLA_ROPE))

    return (xp.reshape(nseq, seq, d), xs.reshape(nb, dec_seq, d),
            jnp.stack(ckv_states, axis=1), jnp.stack(krope_states, axis=1))
```

```python
import functools

import jax
import jax.numpy as jnp
from jax import lax
from jax.experimental import pallas as pl
from jax.experimental.pallas import tpu as pltpu

F32 = jnp.float32
BF16 = jnp.bfloat16
U32 = jnp.uint32

MLA_HEADS = 4
MLA_NOPE = 128
MLA_ROPE = 64
MLA_V = 128
MLA_QK = MLA_NOPE + MLA_ROPE
HEAD_PAD = 256
GRID_W = 64
ROPE_THETA = 10000.0
CONV_K = 31
SC_K = 3
CHUNK = 128
GMLP_HEADS = 4
N_GROUPS = 8
GROUP_SIZE = 4
N_PAIRS = 6
NORM_EPS = 1e-6

LANE = 128
HALO = 16
PROJ_W = 4608
CQ_COL = 3584
KV_COL = 4096
KV_W = 256
IN_TN = 1536
MOE_BM = 256
OUT_SUB = 256
ATTN_KCHUNK = 1024
LOG2_E = 1.4426950408889634
GATE_W = 128
VMEM_CAP = 56 * 1024 * 1024


def _cparams(sem, vmem_mb):
    return pltpu.CompilerParams(dimension_semantics=sem,
                                vmem_limit_bytes=min(vmem_mb * 1024 * 1024, VMEM_CAP))


def _inv_rms(x, n):
    return lax.rsqrt(jnp.sum(x * x, axis=-1, keepdims=True) * (1.0 / n) + NORM_EPS)


def _layernorm(x, g, b):
    mu = jnp.mean(x, axis=-1, keepdims=True)
    xc = x - mu
    var = jnp.mean(xc * xc, axis=-1, keepdims=True)
    return xc * lax.rsqrt(var + NORM_EPS) * g + b


def _split_bf16(x):
    hi = x.astype(BF16)
    lo = (x - hi.astype(F32)).astype(BF16)
    return hi, lo


def _pack_rows(x):
    n = x.shape[1] // 2
    hi = pltpu.bitcast(x[:, :n].astype(BF16).astype(F32), U32)
    lo = pltpu.bitcast(x[:, n:].astype(BF16).astype(F32), U32)
    return (hi & jnp.uint32(0xFFFF0000)) | (lo >> 16)


def _unpack_rows(u):
    a = pltpu.bitcast(u & jnp.uint32(0xFFFF0000), F32)
    b = pltpu.bitcast(u << 16, F32)
    return a, b


def _mod_kernel(c_ref, w_ref, b_ref, o_ref):
    c = c_ref[...]
    a_hi, a_lo = _split_bf16(c * jax.nn.sigmoid(c))
    w_hi, w_lo = _split_bf16(w_ref[...])
    acc = jnp.dot(a_hi, w_hi, preferred_element_type=F32)
    acc += jnp.dot(a_lo, w_hi, preferred_element_type=F32)
    acc += jnp.dot(a_hi, w_lo, preferred_element_type=F32)
    o_ref[...] = acc + b_ref[...]


def _modulation(cvec, w_mod, b_mod):
    depth, d, n = w_mod.shape
    tn = 1024
    return pl.pallas_call(
        _mod_kernel,
        grid=(depth, n // tn),
        in_specs=[pl.BlockSpec((8, d), lambda l, j: (0, 0)),
                  pl.BlockSpec((None, d, tn), lambda l, j: (l, 0, j)),
                  pl.BlockSpec((None, 1, tn), lambda l, j: (l, 0, j))],
        out_specs=pl.BlockSpec((None, 8, tn), lambda l, j: (l, 0, j)),
        out_shape=jax.ShapeDtypeStruct((depth, 8, n), F32),
        compiler_params=_cparams(("arbitrary", "arbitrary"), 40),
        name="modulation",
    )(cvec, w_mod, b_mod.reshape(depth, 1, n))


def _in_proj_kernel(npt, xp_ref, xs_ref, mod_ref, g_ref, w_ref, proj_ref, kv_ref, h_scr):
    i = pl.program_id(0)
    j = pl.program_id(1)
    nj = pl.num_programs(1)

    @pl.when(j == 0)
    def _():
        x = jnp.where(i < npt, xp_ref[...], xs_ref[...])
        y = x * _inv_rms(x, x.shape[-1]) * g_ref[...]
        h_scr[...] = (y * (1.0 + mod_ref[1:2, :]) + mod_ref[0:1, :]).astype(BF16)

    acc = jnp.dot(h_scr[...], w_ref[...], preferred_element_type=F32)
    proj_ref[...] = acc.astype(BF16)

    @pl.when(j == nj - 1)
    def _():
        kv_ref[...] = acc[:, KV_COL % IN_TN:KV_COL % IN_TN + KV_W]


def _in_proj(xp, xs, mod_l, g, w, tm, mod_index):
    tp, d = xp.shape
    t = tp + xs.shape[0]
    npt = tp // tm
    n = w.shape[1]
    nj = n // IN_TN
    return pl.pallas_call(
        functools.partial(_in_proj_kernel, npt),
        grid=(t // tm, nj),
        in_specs=[pl.BlockSpec((tm, d), lambda i, j: (jnp.minimum(i, npt - 1), 0)),
                  pl.BlockSpec((tm, d), lambda i, j: (jnp.maximum(i - npt, 0), 0)),
                  pl.BlockSpec((None, 6, d), lambda i, j: (mod_index(i, tm), 0, 0)),
                  pl.BlockSpec((1, d), lambda i, j: (0, 0)),
                  pl.BlockSpec((d, IN_TN), lambda i, j: (0, j))],
        out_specs=[pl.BlockSpec((tm, IN_TN), lambda i, j: (i, j)),
                   pl.BlockSpec((tm, KV_W), lambda i, j: (i, 0))],
        out_shape=[jax.ShapeDtypeStruct((t, PROJ_W), BF16),
                   jax.ShapeDtypeStruct((t, KV_W), F32)],
        scratch_shapes=[pltpu.VMEM((tm, d), BF16)],
        compiler_params=_cparams(("arbitrary", "arbitrary"), 48),
        name="in_proj",
    )(xp, xs, mod_l, g, w)


def _rope_tile(t, cos, sin):
    lane = lax.broadcasted_iota(jnp.int32, t.shape, 1)
    first = (lane & 16) == 0
    swapped = jnp.where(first, pltpu.roll(t, LANE - 16, 1), pltpu.roll(t, 16, 1))
    return t * cos + swapped * sin


def _keys_values(ckv_n, krope, w_ukv_ref, gk_ref, cos, sin, k_ref, v_ref):
    kvf = jnp.dot(ckv_n.astype(BF16), w_ukv_ref[...], preferred_element_type=F32)
    kr_ss = jnp.sum(krope * krope, axis=-1, keepdims=True)
    gk = gk_ref[...]
    tail = krope * gk[:, MLA_NOPE:]
    if cos is not None:
        tail = _rope_tile(tail, cos, sin)
    for h in range(MLA_HEADS):
        kn = kvf[:, h * MLA_NOPE:(h + 1) * MLA_NOPE]
        r = lax.rsqrt((jnp.sum(kn * kn, axis=-1, keepdims=True) + kr_ss) * (1.0 / MLA_QK) + NORM_EPS)
        k_ref[:, h * HEAD_PAD:h * HEAD_PAD + MLA_NOPE] = (kn * r * gk[:, :MLA_NOPE]).astype(BF16)
        k_ref[:, h * HEAD_PAD + MLA_NOPE:(h + 1) * HEAD_PAD] = (tail * r).astype(BF16)
    v_ref[...] = kvf[:, MLA_HEADS * MLA_NOPE:].astype(BF16)


def _qkv_kernel(npt, q_lora, cq_ref, kv_ref, cos_ref, sin_ref, w_uq_ref, w_ukv_ref, gql_ref, gkv_ref,
                gq_ref, gk_ref, q_ref, k_ref, v_ref, st_ref):
    i = pl.program_id(0)
    cos = cos_ref[...]
    sin = sin_ref[...]
    cq = cq_ref[...].astype(F32)
    cqn = cq * _inv_rms(cq, q_lora) * gql_ref[...]
    qf = jnp.dot(cqn.astype(BF16), w_uq_ref[...], preferred_element_type=F32)
    gq = gq_ref[...]
    scale = MLA_QK ** -0.5 * LOG2_E
    for h in range(MLA_HEADS):
        qh = qf[:, h * HEAD_PAD:(h + 1) * HEAD_PAD]
        qh = qh * (_inv_rms(qh, MLA_QK) * scale) * gq
        q_ref[:, h * HEAD_PAD:h * HEAD_PAD + MLA_NOPE] = qh[:, :MLA_NOPE].astype(BF16)
        q_ref[:, h * HEAD_PAD + MLA_NOPE:(h + 1) * HEAD_PAD] = _rope_tile(qh[:, MLA_NOPE:], cos, sin).astype(BF16)

    kv = kv_ref[...]
    ckv = kv[:, :MLA_NOPE]
    krope = kv[:, MLA_NOPE:]
    ckv_n = ckv * _inv_rms(ckv, ckv.shape[-1]) * gkv_ref[...]
    _keys_values(ckv_n, krope, w_ukv_ref, gk_ref, cos, sin, k_ref, v_ref)

    @pl.when(i < npt)
    def _():
        st_ref[:, :MLA_NOPE] = ckv_n
        st_ref[:, MLA_NOPE:] = krope


def _qkv(proj, kv, cos_t, sin_t, w_uq, w_ukv, gql, gkv, gq, gk, tp, dec_seq, q_lora, tm):
    t = proj.shape[0]
    npt = tp // tm
    nseq = dec_seq // tm
    cq_blk = CQ_COL // 512

    def tab(i):
        return (jnp.where(i < npt, nseq, (i - npt) % nseq), 0)

    const = lambda i: (0, 0)
    return pl.pallas_call(
        functools.partial(_qkv_kernel, npt, q_lora),
        grid=(t // tm,),
        in_specs=[pl.BlockSpec((tm, 512), lambda i: (i, cq_blk)),
                  pl.BlockSpec((tm, KV_W), lambda i: (i, 0)),
                  pl.BlockSpec((tm, LANE), tab),
                  pl.BlockSpec((tm, LANE), tab),
                  pl.BlockSpec(w_uq.shape, const),
                  pl.BlockSpec(w_ukv.shape, const),
                  pl.BlockSpec(gql.shape, const),
                  pl.BlockSpec(gkv.shape, const),
                  pl.BlockSpec(gq.shape, const),
                  pl.BlockSpec(gk.shape, const)],
        out_specs=[pl.BlockSpec((tm, MLA_HEADS * HEAD_PAD), lambda i: (i, 0)),
                   pl.BlockSpec((tm, MLA_HEADS * HEAD_PAD), lambda i: (i, 0)),
                   pl.BlockSpec((tm, MLA_HEADS * MLA_V), lambda i: (i, 0)),
                   pl.BlockSpec((tm, KV_W), lambda i: (jnp.minimum(i, npt - 1), 0))],
        out_shape=[jax.ShapeDtypeStruct((t, MLA_HEADS * HEAD_PAD), BF16),
                   jax.ShapeDtypeStruct((t, MLA_HEADS * HEAD_PAD), BF16),
                   jax.ShapeDtypeStruct((t, MLA_HEADS * MLA_V), BF16),
                   jax.ShapeDtypeStruct((tp, KV_W), F32)],
        compiler_params=_cparams(("arbitrary",), 40),
        name="qkv",
    )(proj, kv, cos_t, sin_t, w_uq, w_ukv, gql, gkv, gq, gk)


def _ctx_kv_kernel(kv_ref, w_ukv_ref, gk_ref, k_ref, v_ref):
    kv = kv_ref[...]
    _keys_values(kv[:, :MLA_NOPE], kv[:, MLA_NOPE:], w_ukv_ref, gk_ref, None, None, k_ref, v_ref)


def _ctx_kv(kvc, w_ukv, gk, tm):
    r = kvc.shape[0]
    const = lambda i: (0, 0)
    return pl.pallas_call(
        _ctx_kv_kernel,
        grid=(r // tm,),
        in_specs=[pl.BlockSpec((tm, KV_W), lambda i: (i, 0)),
                  pl.BlockSpec(w_ukv.shape, const),
                  pl.BlockSpec(gk.shape, const)],
        out_specs=[pl.BlockSpec((tm, MLA_HEADS * HEAD_PAD), lambda i: (i, 0)),
                   pl.BlockSpec((tm, MLA_HEADS * MLA_V), lambda i: (i, 0))],
        out_shape=[jax.ShapeDtypeStruct((r, MLA_HEADS * HEAD_PAD), BF16),
                   jax.ShapeDtypeStruct((r, MLA_HEADS * MLA_V), BF16)],
        compiler_params=_cparams(("arbitrary",), 32),
        name="ctx_kv",
    )(kvc, w_ukv, gk)


_NT = (((1,), (1,)), ((), ()))


def _attn_kernel(has_ctx, *refs):
    if has_ctx:
        q_ref, kc_ref, vc_ref, k_ref, v_ref, o_ref = refs
    else:
        q_ref, k_ref, v_ref, o_ref = refs
    n_own = k_ref.shape[0]
    step = min(n_own, ATTN_KCHUNK)
    chunks = [(k_ref, v_ref, c * step, step) for c in range(n_own // step)]
    if has_ctx:
        chunks = [(kc_ref, vc_ref, 0, kc_ref.shape[0])] + chunks
    for h in range(q_ref.shape[1] // HEAD_PAD):
        qk = slice(h * HEAD_PAD, (h + 1) * HEAD_PAD)
        hv = slice(h * MLA_V, (h + 1) * MLA_V)
        q = q_ref[:, qk]
        m = l = acc = None
        for kr, vr, off, n in chunks:
            s = lax.dot_general(q, kr[off:off + n, qk], _NT, preferred_element_type=F32)
            mc = jnp.max(s, axis=-1, keepdims=True)
            m_new = mc if m is None else jnp.maximum(m, mc)
            p = jnp.exp2(s - m_new)
            pv = jnp.dot(p.astype(BF16), vr[off:off + n, hv], preferred_element_type=F32)
            ps = jnp.sum(p, axis=-1, keepdims=True)
            if m is None:
                l, acc = ps, pv
            else:
                alpha = jnp.exp2(m - m_new)
                l = alpha * l + ps
                acc = alpha * acc + pv
            m = m_new
        o_ref[:, hv] = (acc / l).astype(BF16)


def _attention_latent(q, k, v, kc, vc, tp, nb, dec_seq, past, tq):
    t = q.shape[0]
    nq = dec_seq // tq
    row0 = tp // tq
    kblk0 = tp // dec_seq
    return pl.pallas_call(
        functools.partial(_attn_kernel, True),
        grid=(nb, MLA_HEADS, nq),
        in_specs=[pl.BlockSpec((tq, HEAD_PAD), lambda b, h, i: (row0 + b * nq + i, h)),
                  pl.BlockSpec((past, HEAD_PAD), lambda b, h, i: (b, h)),
                  pl.BlockSpec((past, MLA_V), lambda b, h, i: (b, h)),
                  pl.BlockSpec((dec_seq, HEAD_PAD), lambda b, h, i: (kblk0 + b, h)),
                  pl.BlockSpec((dec_seq, MLA_V), lambda b, h, i: (kblk0 + b, h))],
        out_specs=pl.BlockSpec((tq, MLA_V), lambda b, h, i: (b * nq + i, h)),
        out_shape=jax.ShapeDtypeStruct((t - tp, MLA_HEADS * MLA_V), BF16),
        compiler_params=_cparams(("arbitrary", "arbitrary", "arbitrary"), 48),
        name="attn_latent",
    )(q, kc, vc, k, v)


def _attention_context(q, k, v, nseq, seq):
    return pl.pallas_call(
        functools.partial(_attn_kernel, False),
        grid=(nseq,),
        in_specs=[pl.BlockSpec((seq, MLA_HEADS * HEAD_PAD), lambda b: (b, 0)),
                  pl.BlockSpec((seq, MLA_HEADS * HEAD_PAD), lambda b: (b, 0)),
                  pl.BlockSpec((seq, MLA_HEADS * MLA_V), lambda b: (b, 0))],
        out_specs=pl.BlockSpec((seq, MLA_HEADS * MLA_V), lambda b: (b, 0)),
        out_shape=jax.ShapeDtypeStruct((nseq * seq, MLA_HEADS * MLA_V), BF16),
        compiler_params=_cparams(("arbitrary",), 32),
        name="attn_context",
    )(q, k, v)


def _mixers_kernel(npt, p_tiles, s_tiles,
                   cv_ref, cvp_ref, cvn_ref, gm_ref, sc_ref, scp_ref, scn_ref, bg_ref,
                   wdw_ref, bdw_ref, gcl_ref, bcl_ref, wpw_ref, bpw_ref,
                   gsl_ref, bsl_ref, wsp_ref, bsp_ref, wsc_ref,
                   o_ref, zs, ys, zsh):
    i = pl.program_id(0)
    tm = cv_ref.shape[0]
    cw = cv_ref.shape[1] // 2
    pos = jnp.where(i < npt, i % p_tiles, (i - npt) % s_tiles)
    n_tiles = jnp.where(i < npt, p_tiles, s_tiles)
    keep_prev = (pos > 0).astype(F32)
    keep_next = (pos < n_tiles - 1).astype(F32)

    def glu(ref):
        a = ref[:, :cw].astype(F32)
        g = ref[:, cw:].astype(F32)
        return a * jax.nn.sigmoid(g)

    def prod(ref):
        return ref[:, :cw].astype(F32) * ref[:, cw:].astype(F32)

    zs[0:HALO, :] = glu(cvp_ref) * keep_prev
    zs[HALO:HALO + tm, :] = glu(cv_ref)
    zs[HALO + tm:, :] = glu(cvn_ref) * keep_next
    pad = CONV_K // 2
    half = tm // 2
    for r in range(1, 8):
        zsh[r - 1] = zs[pl.ds(r, tm + 24), :]
    for c in range(cw // LANE):
        cs = slice(c * LANE, (c + 1) * LANE)
        for rh in range(2):
            base = rh * half
            acc = jnp.zeros((half, LANE), F32)
            for r in range(8):
                for a in range(4):
                    tap = 8 * a + r - (HALO - pad)
                    if 0 <= tap < CONV_K:
                        rows = slice(base + 8 * a, base + 8 * a + half)
                        src = zs[rows, cs] if r == 0 else zsh[r - 1, rows, cs]
                        acc = acc + src * wdw_ref[tap:tap + 1, cs]
            ys[base:base + half, cs] = acc
    z = ys[0:tm, :] + bdw_ref[...]
    z = _layernorm(z, gcl_ref[...], bcl_ref[...])
    z = z * jax.nn.sigmoid(z)
    o_b = jnp.dot(z.astype(BF16), wpw_ref[...], preferred_element_type=F32) + bpw_ref[...]
    o_ref[:, 0:cw] = o_b.astype(BF16)

    u = gm_ref[:, :cw].astype(F32)
    vg = _layernorm(gm_ref[:, cw:].astype(F32), gsl_ref[...], bsl_ref[...]).astype(BF16)
    hc = cw // GMLP_HEADS
    for n in range(tm // CHUNK):
        rows = slice(n * CHUNK, (n + 1) * CHUNK)
        for h in range(GMLP_HEADS):
            cols = slice(h * hc, (h + 1) * hc)
            mixed = jnp.dot(wsp_ref[h], vg[rows, cols], preferred_element_type=F32) + bsp_ref[:, cols]
            o_ref[rows, cw + h * hc:cw + (h + 1) * hc] = (u[rows, cols] * mixed).astype(BF16)

    ys[0:HALO, :] = prod(scp_ref) * keep_prev
    ys[HALO:HALO + tm, :] = prod(sc_ref)
    ys[HALO + tm:, :] = prod(scn_ref) * keep_next
    acc = jnp.zeros((tm, cw), F32)
    for tap in range(SC_K):
        acc = acc + ys[pl.ds(HALO - SC_K // 2 + tap, tm), :] * wsc_ref[tap:tap + 1, :]
    o_ref[:, 2 * cw:3 * cw] = (bg_ref[...].astype(F32) * acc).astype(BF16)


def _mixers(proj, weights, tp, seq, dec_seq, tm):
    t = proj.shape[0]
    cw = 512
    npt = tp // tm
    hb = tm // HALO
    last = t // HALO - 1
    const2 = lambda i: (0, 0)

    def main(col):
        return pl.BlockSpec((tm, 2 * cw), lambda i: (i, col))

    def prev(col):
        return pl.BlockSpec((HALO, 2 * cw), lambda i: (jnp.maximum(i * hb - 1, 0), col))

    def nxt(col):
        return pl.BlockSpec((HALO, 2 * cw), lambda i: (jnp.minimum((i + 1) * hb, last), col))

    w_specs = [pl.BlockSpec(w.shape, const2 if w.ndim == 2 else (lambda i: (0, 0, 0))) for w in weights]
    return pl.pallas_call(
        functools.partial(_mixers_kernel, npt, seq // tm, dec_seq // tm),
        grid=(t // tm,),
        in_specs=[main(0), prev(0), nxt(0), main(1), main(2), prev(2), nxt(2),
                  pl.BlockSpec((tm, cw), lambda i: (i, 6))] + w_specs,
        out_specs=pl.BlockSpec((tm, 3 * cw), lambda i: (i, 0)),
        out_shape=jax.ShapeDtypeStruct((t, 3 * cw), BF16),
        scratch_shapes=[pltpu.VMEM((tm + 2 * HALO, cw), F32), pltpu.VMEM((tm + 2 * HALO, cw), F32),
                        pltpu.VMEM((7, tm + 24, cw), F32)],
        compiler_params=_cparams(("arbitrary",), 40),
        name="mixers",
    )(proj, proj, proj, proj, proj, proj, proj, proj, *weights)


def _out_proj_kernel(npt, oap_ref, oas_ref, ob_ref, xp_ref, xs_ref, mod_ref, g_ref, wo_ref, wr_ref,
                     xn_ref, hp_ref, r_ref):
    i = pl.program_id(0)
    ka = oap_ref.shape[1]
    tm = ob_ref.shape[0]
    sub = min(tm, OUT_SUB)
    for s in range(tm // sub):
        rows = slice(s * sub, (s + 1) * sub)
        x = jnp.where(i < npt, xp_ref[rows, :], xs_ref[rows, :])
        oa = jnp.where(i < npt, oap_ref[rows, :], oas_ref[rows, :])
        mix = jnp.dot(oa, wo_ref[0:ka, :], preferred_element_type=F32)
        mix += jnp.dot(ob_ref[rows, :], wo_ref[ka:, :], preferred_element_type=F32)
        xn = x + mod_ref[2:3, :] * mix
        xn_ref[rows, :] = xn
        h = xn * _inv_rms(xn, xn.shape[-1]) * g_ref[...]
        h = h * (1.0 + mod_ref[4:5, :]) + mod_ref[3:4, :]
        hp_ref[rows, :] = _pack_rows(h)
        h_hi, h_lo = _split_bf16(h)
        r_ref[:, rows] = (lax.dot_general(wr_ref[...], h_hi, _NT, preferred_element_type=F32)
                          + lax.dot_general(wr_ref[...], h_lo, _NT, preferred_element_type=F32))


def _out_proj(oap, oas, ob, xp, xs, mod_l, g, wo, wr, tm, mod_index):
    tp, d = xp.shape
    t = ob.shape[0]
    npt = tp // tm
    const = lambda i: (0, 0)

    def pair(w):
        return [pl.BlockSpec((tm, w), lambda i: (jnp.minimum(i, npt - 1), 0)),
                pl.BlockSpec((tm, w), lambda i: (jnp.maximum(i - npt, 0), 0))]

    return pl.pallas_call(
        functools.partial(_out_proj_kernel, npt),
        grid=(t // tm,),
        in_specs=pair(oap.shape[1]) + [pl.BlockSpec((tm, ob.shape[1]), lambda i: (i, 0))] + pair(d) + [
                  pl.BlockSpec((None, 6, d), lambda i: (mod_index(i, tm), 0, 0)),
                  pl.BlockSpec((1, d), const),
                  pl.BlockSpec(wo.shape, const, pipeline_mode=pl.Buffered(1)),
                  pl.BlockSpec(wr.shape, const, pipeline_mode=pl.Buffered(1))],
        out_specs=[pl.BlockSpec((tm, d), lambda i: (i, 0)),
                   pl.BlockSpec((tm, d // 2), lambda i: (i, 0)),
                   pl.BlockSpec((LANE, tm), lambda i: (0, i))],
        out_shape=[jax.ShapeDtypeStruct((t, d), F32),
                   jax.ShapeDtypeStruct((t, d // 2), U32),
                   jax.ShapeDtypeStruct((LANE, t), F32)],
        compiler_params=_cparams(("arbitrary",), 48),
        name="out_proj",
    )(oap, oas, ob, xp, xs, mod_l, g, wo, wr)


N_BUCKET_PAD = 64


def _route_kernel(rt_ref, b_ref, tri_ref, bucket_ref, rank_ref, gate_ref, cnt_ref, run):
    i = pl.program_id(0)
    tm = rt_ref.shape[1]
    ne = N_GROUPS * GROUP_SIZE

    @pl.when(i == 0)
    def _():
        run[...] = jnp.zeros_like(run)

    sc = jax.nn.sigmoid(rt_ref[0:ne, :] + rt_ref[ne:2 * ne, :])
    sel = sc + b_ref[...]
    s = [sel[r * N_GROUPS:(r + 1) * N_GROUPS, :] for r in range(GROUP_SIZE)]
    c = [sc[r * N_GROUPS:(r + 1) * N_GROUPS, :] for r in range(GROUP_SIZE)]
    hi01, lo01 = jnp.maximum(s[0], s[1]), jnp.minimum(s[0], s[1])
    hi23, lo23 = jnp.maximum(s[2], s[3]), jnp.minimum(s[2], s[3])
    g_score = jnp.maximum(hi01, hi23) + jnp.maximum(jnp.minimum(hi01, hi23), jnp.maximum(lo01, lo23))
    grp = lax.broadcasted_iota(jnp.int32, g_score.shape, 0)
    g_best = jnp.min(jnp.where(g_score == jnp.max(g_score, axis=0, keepdims=True), grp, N_GROUPS),
                     axis=0, keepdims=True)
    own = grp == g_best
    v = [jnp.sum(jnp.where(own, s[r], 0.0), axis=0, keepdims=True) for r in range(GROUP_SIZE)]
    w = [jnp.sum(jnp.where(own, c[r], 0.0), axis=0, keepdims=True) for r in range(GROUP_SIZE)]

    def first_max(vals):
        best, idx, gate = vals[0], jnp.zeros_like(g_best), w[0]
        for r in range(1, GROUP_SIZE):
            upd = vals[r] > best
            best = jnp.where(upd, vals[r], best)
            idx = jnp.where(upd, r, idx)
            gate = jnp.where(upd, w[r], gate)
        return idx, gate

    r1, w1 = first_max(v)
    r2, w2 = first_max([jnp.where(r1 == r, -jnp.inf, v[r]) for r in range(GROUP_SIZE)])
    wsum = w1 + w2
    swap = r1 > r2
    r_lo = jnp.where(swap, r2, r1)
    r_hi = jnp.where(swap, r1, r2)
    gate_ref[0:1, :] = jnp.where(swap, w2, w1) / wsum
    gate_ref[1:2, :] = jnp.where(swap, w1, w2) / wsum
    pair = r_lo * 3 - ((r_lo * (r_lo - 1)) >> 1) + (r_hi - r_lo - 1)
    bucket = g_best * N_PAIRS + pair
    bucket_ref[...] = bucket

    onehot = (lax.broadcasted_iota(jnp.int32, (N_BUCKET_PAD, tm), 0) == bucket).astype(F32)
    before = jnp.dot(onehot.astype(BF16), tri_ref[...], preferred_element_type=F32) + run[...]
    rank_ref[...] = jnp.sum(onehot * before, axis=0, keepdims=True).astype(jnp.int32)
    run[...] += jnp.sum(onehot, axis=1, keepdims=True)

    @pl.when(i == pl.num_programs(0) - 1)
    def _():
        cnt_ref[...] = run[...]


def _route(rt, router_b, n_blk, tm):
    t = rt.shape[1]
    ne = router_b.shape[0]
    b_col = router_b.astype(F32).reshape(N_GROUPS, GROUP_SIZE).T.reshape(ne, 1)
    tri = (jnp.arange(tm)[:, None] < jnp.arange(tm)[None, :]).astype(BF16)
    bucket, rank, gate, cnt = pl.pallas_call(
        _route_kernel,
        grid=(t // tm,),
        in_specs=[pl.BlockSpec((LANE, tm), lambda i: (0, i)),
                  pl.BlockSpec((ne, 1), lambda i: (0, 0)),
                  pl.BlockSpec((tm, tm), lambda i: (0, 0))],
        out_specs=[pl.BlockSpec((1, tm), lambda i: (0, i)),
                   pl.BlockSpec((1, tm), lambda i: (0, i)),
                   pl.BlockSpec((2, tm), lambda i: (0, i)),
                   pl.BlockSpec((N_BUCKET_PAD, 1), lambda i: (0, 0))],
        out_shape=[jax.ShapeDtypeStruct((1, t), jnp.int32),
                   jax.ShapeDtypeStruct((1, t), jnp.int32),
                   jax.ShapeDtypeStruct((2, t), F32),
                   jax.ShapeDtypeStruct((N_BUCKET_PAD, 1), F32)],
        scratch_shapes=[pltpu.VMEM((N_BUCKET_PAD, 1), F32)],
        compiler_params=_cparams(("arbitrary",), 32),
        name="route",
    )(rt, b_col, tri)

    n_bucket = N_GROUPS * N_PAIRS
    counts = cnt[:n_bucket, 0].astype(jnp.int32)
    padded = (counts + MOE_BM - 1) // MOE_BM * MOE_BM
    pad_end = jnp.cumsum(padded)
    pad_start = pad_end - padded
    bucket = bucket[0]
    sel = bucket[:, None] == jnp.arange(n_bucket, dtype=jnp.int32)[None, :]
    pos = jnp.sum(jnp.where(sel, pad_start[None, :], 0), axis=1).astype(jnp.int32) + rank[0]
    n_used = (pad_end[-1] // MOE_BM).astype(jnp.int32)
    blk = jnp.minimum(jnp.arange(n_blk, dtype=jnp.int32), n_used - 1)
    blk_bucket = jnp.sum((pad_end[None, :] <= (blk * MOE_BM)[:, None]).astype(jnp.int32), axis=1)
    blk_bucket = jnp.minimum(blk_bucket, n_bucket - 1)
    pair_lo = jnp.array([0, 0, 0, 1, 1, 2], jnp.int32)
    pair_hi = jnp.array([1, 2, 3, 2, 3, 3], jnp.int32)
    blk_pair = blk_bucket % N_PAIRS
    first = (blk_bucket // N_PAIRS) * GROUP_SIZE
    is_pair = blk_pair[:, None] == jnp.arange(N_PAIRS, dtype=jnp.int32)[None, :]
    blk_a = first + jnp.sum(jnp.where(is_pair, pair_lo[None, :], 0), axis=1)
    blk_b = first + jnp.sum(jnp.where(is_pair, pair_hi[None, :], 0), axis=1)
    gates = jnp.pad(gate.T, ((0, 0), (0, GATE_W - 2)))
    return pos, gates, blk, blk_a.astype(jnp.int32), blk_b.astype(jnp.int32), n_used.reshape(1)


def _dispatch_kernel(pos_ref, hp_ref, gt_ref, init_ref, xs_ref, buf, sem):
    del init_ref
    i = pl.program_id(0)
    slot = i % 2
    tm = hp_ref.shape[0]
    w = hp_ref.shape[1]
    buf[slot, :, :w] = hp_ref[...]
    buf[slot, :, w:] = pltpu.bitcast(gt_ref[...], U32)

    def start(r, carry):
        pltpu.make_async_copy(buf.at[slot, pl.ds(r, 1)], xs_ref.at[pl.ds(pos_ref[0, 0, r], 1)],
                              sem.at[slot]).start()
        return carry

    def wait_all(s):
        def wait(r, carry):
            pltpu.make_async_copy(buf.at[s, pl.ds(0, 1)], xs_ref.at[pl.ds(0, 1)], sem.at[s]).wait()
            return carry
        lax.fori_loop(0, tm, wait, 0, unroll=8)

    lax.fori_loop(0, tm, start, 0, unroll=8)

    @pl.when(i > 0)
    def _():
        wait_all(1 - slot)

    @pl.when(i == pl.num_programs(0) - 1)
    def _():
        wait_all(slot)


def _dispatch(hp, gates, pos, n_buf, tm):
    t, w = hp.shape
    init = jnp.zeros((n_buf, w + GATE_W), U32)
    return pl.pallas_call(
        _dispatch_kernel,
        grid=(t // tm,),
        in_specs=[pl.BlockSpec((1, 1, tm), lambda i: (i, 0, 0), memory_space=pltpu.SMEM),
                  pl.BlockSpec((tm, w), lambda i: (i, 0)),
                  pl.BlockSpec((tm, GATE_W), lambda i: (i, 0)),
                  pl.BlockSpec(memory_space=pl.ANY)],
        out_specs=pl.BlockSpec(memory_space=pl.ANY),
        out_shape=jax.ShapeDtypeStruct((n_buf, w + GATE_W), U32),
        scratch_shapes=[pltpu.VMEM((2, tm, w + GATE_W), U32), pltpu.SemaphoreType.DMA((2,))],
        input_output_aliases={3: 0},
        compiler_params=_cparams(("arbitrary",), 32),
        name="dispatch",
    )(pos.reshape(t // tm, 1, tm), hp, gates, init)


def _moe_kernel(blk_ref, ea_ref, eb_ref, nu_ref, x_ref, w1a_ref, w3a_ref, w2a_ref, w1b_ref, w3b_ref, w2b_ref,
                y_ref):
    del blk_ref, ea_ref, eb_ref
    i = pl.program_id(0)

    @pl.when(i < nu_ref[0])
    def _():
        w = x_ref.shape[1] - GATE_W
        xa, xb = _unpack_rows(x_ref[:, :w])
        x = jnp.concatenate([xa.astype(BF16), xb.astype(BF16)], axis=1)
        gates = pltpu.bitcast(x_ref[:, w:], F32)

        def hidden(w1_ref, w3_ref, g):
            a = jnp.dot(x, w1_ref[...], preferred_element_type=F32)
            b = jnp.dot(x, w3_ref[...], preferred_element_type=F32)
            return (a * jax.nn.sigmoid(a) * b * g).astype(BF16)

        y = jnp.dot(hidden(w1a_ref, w3a_ref, gates[:, 0:1]), w2a_ref[...], preferred_element_type=F32)
        y += jnp.dot(hidden(w1b_ref, w3b_ref, gates[:, 1:2]), w2b_ref[...], preferred_element_type=F32)
        y_ref[...] = _pack_rows(y)

    @pl.when(i >= nu_ref[0])
    def _():
        y_ref[...] = jnp.zeros_like(y_ref)


def _moe(xs, w1, w3, w2, layer, blk, blk_a, blk_b, n_used):
    n_buf, wx = xs.shape
    n_blk = n_buf // MOE_BM
    _, _, d, de = w1.shape

    def expert(second, rows, cols):
        if second:
            return pl.BlockSpec((None, None, rows, cols), lambda i, blk, ea, eb, nu: (layer, eb[i], 0, 0))
        return pl.BlockSpec((None, None, rows, cols), lambda i, blk, ea, eb, nu: (layer, ea[i], 0, 0))

    grid_spec = pltpu.PrefetchScalarGridSpec(
        num_scalar_prefetch=4,
        grid=(n_blk,),
        in_specs=[pl.BlockSpec((MOE_BM, wx), lambda i, blk, ea, eb, nu: (blk[i], 0)),
                  expert(False, d, de), expert(False, d, de), expert(False, de, d),
                  expert(True, d, de), expert(True, d, de), expert(True, de, d)],
        out_specs=pl.BlockSpec((MOE_BM, d // 2), lambda i, blk, ea, eb, nu: (i, 0)),
    )
    return pl.pallas_call(
        _moe_kernel,
        grid_spec=grid_spec,
        out_shape=jax.ShapeDtypeStruct((n_buf, d // 2), U32),
        compiler_params=_cparams(("arbitrary",), 48),
        name="moe",
    )(blk, blk_a, blk_b, n_used, xs, w1, w3, w2, w1, w3, w2)


def _combine_kernel(npt, pos_ref, nxt_ref, xn_ref, mod_ref, yb_ref, op_ref, os_ref, buf, sem):
    i = pl.program_id(0)
    slot = i % 2
    tm = xn_ref.shape[0]

    def gather(idx_ref, s):
        def start(r, carry):
            pltpu.make_async_copy(yb_ref.at[pl.ds(idx_ref[0, 0, r], 1)], buf.at[s, pl.ds(r, 1)],
                                  sem.at[s]).start()
            return carry
        lax.fori_loop(0, tm, start, 0, unroll=8)

    @pl.when(i == 0)
    def _():
        gather(pos_ref, slot)

    @pl.when(i + 1 < pl.num_programs(0))
    def _():
        gather(nxt_ref, 1 - slot)

    def wait(r, carry):
        pltpu.make_async_copy(yb_ref.at[pl.ds(0, 1)], buf.at[slot, pl.ds(0, 1)], sem.at[slot]).wait()
        return carry

    lax.fori_loop(0, tm, wait, 0, unroll=8)
    ya, yb = _unpack_rows(buf[slot])
    out = xn_ref[...] + mod_ref[5:6, :] * jnp.concatenate([ya, yb], axis=1)

    @pl.when(i < npt)
    def _():
        op_ref[...] = out

    @pl.when(i >= npt)
    def _():
        os_ref[...] = out


def _combine(xn, yb, pos, mod_l, tp, tm, mod_index):
    t, d = xn.shape
    npt = tp // tm
    n = t // tm
    pos3 = pos.reshape(n, 1, tm)
    return pl.pallas_call(
        functools.partial(_combine_kernel, npt),
        grid=(n,),
        in_specs=[pl.BlockSpec((1, 1, tm), lambda i: (i, 0, 0), memory_space=pltpu.SMEM),
                  pl.BlockSpec((1, 1, tm), lambda i: (jnp.minimum(i + 1, n - 1), 0, 0), memory_space=pltpu.SMEM),
                  pl.BlockSpec((tm, d), lambda i: (i, 0)),
                  pl.BlockSpec((None, 6, d), lambda i: (mod_index(i, tm), 0, 0)),
                  pl.BlockSpec(memory_space=pl.ANY)],
        out_specs=[pl.BlockSpec((tm, d), lambda i: (jnp.minimum(i, npt - 1), 0)),
                   pl.BlockSpec((tm, d), lambda i: (jnp.maximum(i - npt, 0), 0))],
        out_shape=[jax.ShapeDtypeStruct((tp, d), F32),
                   jax.ShapeDtypeStruct((t - tp, d), F32)],
        scratch_shapes=[pltpu.VMEM((2, tm, d // 2), U32), pltpu.SemaphoreType.DMA((2,))],
        compiler_params=_cparams(("arbitrary",), 32),
        name="combine",
    )(pos3, pos3, xn, mod_l, yb)


def _pad_cols(x, n):
    return jnp.pad(x, ((0, 0), (0, n - x.shape[1])))


def _rope_tables(dec_seq, tm):
    n_freq = MLA_ROPE // 4
    pos = jnp.arange(dec_seq, dtype=jnp.int32)
    row = (pos // GRID_W).astype(F32)
    col = (pos % GRID_W).astype(F32)
    inv = ROPE_THETA ** (-jnp.arange(n_freq, dtype=F32) / n_freq)
    ar = row[:, None] * inv[None, :]
    ac = col[:, None] * inv[None, :]
    ones = jnp.ones((dec_seq, LANE - MLA_ROPE), F32)
    cos = jnp.concatenate([jnp.cos(ar), jnp.cos(ar), jnp.cos(ac), jnp.cos(ac), ones], axis=1)
    sin = jnp.concatenate([-jnp.sin(ar), jnp.sin(ar), -jnp.sin(ac), jnp.sin(ac), 0.0 * ones], axis=1)
    ident_c = jnp.ones((tm, LANE), F32)
    ident_s = jnp.zeros((tm, LANE), F32)
    return jnp.concatenate([cos, ident_c], axis=0), jnp.concatenate([sin, ident_s], axis=0)


def _layer_params(l, q_lora, kv_lora, w_in, g_q_lora, w_uq, g_kv_lora, w_ukv, g_qk_q, g_qk_k,
                  w_dw31, b_dw31, g_conv_ln, b_conv_ln, w_pw, b_pw, g_sgu_ln, b_sgu_ln,
                  w_spatial, b_spatial, w_sc3, w_out):
    d = w_in.shape[1]
    o1 = q_lora
    o2 = o1 + kv_lora
    o3 = o2 + MLA_ROPE
    o4 = o3 + 1024
    o5 = o4 + 1024
    w = w_in[l]
    zeros = lambda n: jnp.zeros((d, n), w.dtype)
    w_in_p = jnp.concatenate(
        [w[:, o3:o4], w[:, o4:o5], w[:, o5 + 512:o5 + 1536], w[:, o5:o5 + 512],
         w[:, :o1], zeros(512 - q_lora),
         w[:, o1:o2], w[:, o2:o3], zeros(PROJ_W - KV_COL - kv_lora - MLA_ROPE)], axis=1).astype(BF16)

    uq = w_uq[l].reshape(q_lora, MLA_HEADS, MLA_QK)
    uq = jnp.pad(uq, ((0, 512 - q_lora), (0, 0), (0, HEAD_PAD - MLA_QK)))
    w_uq_p = uq.reshape(512, MLA_HEADS * HEAD_PAD).astype(BF16)
    ukv = w_ukv[l].reshape(kv_lora, MLA_HEADS, MLA_NOPE + MLA_V)
    w_ukv_p = jnp.concatenate([ukv[:, :, :MLA_NOPE].reshape(kv_lora, -1),
                               ukv[:, :, MLA_NOPE:].reshape(kv_lora, -1)], axis=1).astype(BF16)
    row = lambda v: v.reshape(1, -1).astype(F32)
    qkv_w = (w_uq_p, w_ukv_p, _pad_cols(row(g_q_lora[l]), 512), row(g_kv_lora[l]),
             _pad_cols(row(g_qk_q[l]), HEAD_PAD), _pad_cols(row(g_qk_k[l]), HEAD_PAD))

    hc = w_pw.shape[1] // GMLP_HEADS
    bsp = jnp.repeat(b_spatial[l].T.astype(F32), hc, axis=1)
    mix_w = (jnp.pad(w_dw31[l].astype(F32), ((0, 32 - CONV_K), (0, 0))), row(b_dw31[l]),
             row(g_conv_ln[l]), row(b_conv_ln[l]), w_pw[l].astype(BF16), row(b_pw[l]),
             row(g_sgu_ln[l]), row(b_sgu_ln[l]), w_spatial[l].astype(BF16), bsp,
             jnp.pad(w_sc3[l].astype(F32), ((0, 8 - SC_K), (0, 0))))
    return w_in_p, qkv_w, mix_w, w_out[l].astype(BF16)


def _tile(limit, *sizes):
    tm = limit
    while any(s % tm for s in sizes):
        tm //= 2
    return tm


def kernel(x_prompt, x_sample, cache_mla_ckv, cache_mla_krope, c, c_ctx, w_mod, b_mod, g_norm1, g_norm2, w_in, g_q_lora, w_uq, g_kv_lora, w_ukv, g_qk_q, g_qk_k, w_dw31, b_dw31, g_conv_ln, b_conv_ln, w_pw, b_pw, g_sgu_ln, b_sgu_ln, w_spatial, b_spatial, w_sc3, w_out, router_w, router_b, w1, w3, w2):
    nseq, seq, d = x_prompt.shape
    nb, dec_seq, _ = x_sample.shape
    depth = w_mod.shape[0]
    past = cache_mla_ckv.shape[2]
    q_lora = g_q_lora.shape[1]
    kv_lora = g_kv_lora.shape[1]
    n_exp = router_w.shape[1]
    tp = nseq * seq
    ts = nb * dec_seq
    t = tp + ts
    assert tp % dec_seq == 0 and seq % CHUNK == 0 and dec_seq % GRID_W == 0
    assert nb + 1 <= 8 and kv_lora == MLA_NOPE and n_exp == N_GROUPS * GROUP_SIZE

    tm_in = _tile(512, tp, dec_seq)
    tm_qkv = _tile(512, tp, dec_seq)
    tm_row = _tile(256, seq, dec_seq)
    tq = _tile(1024, dec_seq)

    def mod_index(i, tm):
        npt = tp // tm
        return jnp.where(i < npt, 0, 1 + (i - npt) // (dec_seq // tm))

    cvec = jnp.concatenate([c_ctx[None, :], c, jnp.zeros((8 - 1 - nb, d), F32)], axis=0)
    mod = _modulation(cvec, w_mod, b_mod).reshape(depth, 8, 6, d)
    cos_t, sin_t = _rope_tables(dec_seq, tm_qkv)

    rw = router_w.T.reshape(N_GROUPS, GROUP_SIZE, d).transpose(1, 0, 2).reshape(n_exp, d)
    rw_hi = rw.astype(BF16)
    rw_lo = (rw - rw_hi.astype(F32)).astype(BF16)
    wr = jnp.pad(jnp.concatenate([rw_hi, rw_lo], axis=0), ((0, LANE - 2 * n_exp), (0, 0)))
    w1b, w3b, w2b = w1.astype(BF16), w3.astype(BF16), w2.astype(BF16)
    n_blk = (t + N_GROUPS * N_PAIRS * (MOE_BM - 1) + MOE_BM - 1) // MOE_BM

    xp = x_prompt.reshape(tp, d)
    xs = x_sample.reshape(ts, d)
    ckv_states = []
    krope_states = []
    for l in range(depth):
        w_in_p, qkv_w, mix_w, w_out_b = _layer_params(
            l, q_lora, kv_lora, w_in, g_q_lora, w_uq, g_kv_lora, w_ukv, g_qk_q, g_qk_k,
            w_dw31, b_dw31, g_conv_ln, b_conv_ln, w_pw, b_pw, g_sgu_ln, b_sgu_ln,
            w_spatial, b_spatial, w_sc3, w_out)
        mod_l = mod[l]

        proj, kv = _in_proj(xp, xs, mod_l, g_norm1[l].reshape(1, d), w_in_p, tm_in, mod_index)
        q, k, v, state = _qkv(proj, kv, cos_t, sin_t, *qkv_w, tp, dec_seq, q_lora, tm_qkv)
        kvc = jnp.concatenate([cache_mla_ckv[:, l], cache_mla_krope[:, l],
                               jnp.zeros((nb, past, KV_W - kv_lora - MLA_ROPE), F32)], axis=-1)
        kc, vc = _ctx_kv(kvc.reshape(nb * past, KV_W), qkv_w[1], qkv_w[5], _tile(512, past))
        o_as = _attention_latent(q, k, v, kc, vc, tp, nb, dec_seq, past, tq)
        o_ap = _attention_context(q, k, v, nseq, seq)
        o_bcd = _mixers(proj, mix_w, tp, seq, dec_seq, tm_row)

        xn, hp, r = _out_proj(o_ap, o_as, o_bcd, xp, xs, mod_l, g_norm2[l].reshape(1, d), w_out_b, wr,
                              tm_qkv, mod_index)
        pos, gates, blk, blk_a, blk_b, n_used = _route(r, router_b, n_blk, tm_qkv)
        xd = _dispatch(hp, gates, pos, n_blk * MOE_BM, tm_row)
        yb = _moe(xd, w1b, w3b, w2b, l, blk, blk_a, blk_b, n_used)
        xp, xs = _combine(xn, yb, pos, mod_l, tp, tm_row, mod_index)

        ckv_states.append(state[:, :kv_lora].reshape(nseq, seq, kv_lora))
        krope_states.append(state[:, kv_lora:kv_lora + MLA_ROPE].reshape(nseq, seq, MLA_ROPE))

    return (xp.reshape(nseq, seq, d), xs.reshape(nb, dec_seq, d),
            jnp.stack(ckv_states, axis=1), jnp.stack(krope_states, axis=1))
```

```python
import functools

import jax
import jax.numpy as jnp
from jax import lax
from jax.experimental import pallas as pl
from jax.experimental.pallas import tpu as pltpu

F32 = jnp.float32
BF16 = jnp.bfloat16
U32 = jnp.uint32

MLA_HEADS = 4
MLA_NOPE = 128
MLA_ROPE = 64
MLA_V = 128
MLA_QK = MLA_NOPE + MLA_ROPE
HEAD_PAD = 256
GRID_W = 64
ROPE_THETA = 10000.0
CONV_K = 31
SC_K = 3
CHUNK = 128
GMLP_HEADS = 4
N_GROUPS = 8
GROUP_SIZE = 4
N_PAIRS = 6
NORM_EPS = 1e-6

LANE = 128
HALO = 16
PROJ_W = 4608
CQ_COL = 3584
KV_COL = 4096
KV_W = 256
IN_TN = 1536
MOE_BM = 256
N_DMA_PRIORITIES = 2
IN_SUB = 512
OUT_SUB = 256
ATTN_KCHUNK = 1024
LOG2_E = 1.4426950408889634
GATE_W = 128
VMEM_CAP = 56 * 1024 * 1024


def _cparams(sem, vmem_mb):
    return pltpu.CompilerParams(dimension_semantics=sem,
                                vmem_limit_bytes=min(vmem_mb * 1024 * 1024, VMEM_CAP))


def _inv_rms(x, n):
    return lax.rsqrt(jnp.sum(x * x, axis=-1, keepdims=True) * (1.0 / n) + NORM_EPS)


def _layernorm(x, g, b):
    mu = jnp.mean(x, axis=-1, keepdims=True)
    xc = x - mu
    var = jnp.mean(xc * xc, axis=-1, keepdims=True)
    return xc * lax.rsqrt(var + NORM_EPS) * g + b


def _split_bf16(x):
    hi = x.astype(BF16)
    lo = (x - hi.astype(F32)).astype(BF16)
    return hi, lo


def _pack_rows(x):
    n = x.shape[1] // 2
    hi = pltpu.bitcast(x[:, :n].astype(BF16).astype(F32), U32)
    lo = pltpu.bitcast(x[:, n:].astype(BF16).astype(F32), U32)
    return (hi & jnp.uint32(0xFFFF0000)) | (lo >> 16)


def _unpack_rows(u):
    a = pltpu.bitcast(u & jnp.uint32(0xFFFF0000), F32)
    b = pltpu.bitcast(u << 16, F32)
    return a, b


def _mod_kernel(c_ref, w_ref, b_ref, o_ref):
    c = c_ref[...]
    a_hi, a_lo = _split_bf16(c * jax.nn.sigmoid(c))
    w_hi, w_lo = _split_bf16(w_ref[...])
    acc = jnp.dot(a_hi, w_hi, preferred_element_type=F32)
    acc += jnp.dot(a_lo, w_hi, preferred_element_type=F32)
    acc += jnp.dot(a_hi, w_lo, preferred_element_type=F32)
    o_ref[...] = acc + b_ref[...]


def _modulation(cvec, w_mod, b_mod):
    depth, d, n = w_mod.shape
    tn = 1024
    return pl.pallas_call(
        _mod_kernel,
        grid=(depth, n // tn),
        in_specs=[pl.BlockSpec((8, d), lambda l, j: (0, 0)),
                  pl.BlockSpec((None, d, tn), lambda l, j: (l, 0, j)),
                  pl.BlockSpec((None, 1, tn), lambda l, j: (l, 0, j))],
        out_specs=pl.BlockSpec((None, 8, tn), lambda l, j: (l, 0, j)),
        out_shape=jax.ShapeDtypeStruct((depth, 8, n), F32),
        compiler_params=_cparams(("arbitrary", "arbitrary"), 40),
        name="modulation",
    )(cvec, w_mod, b_mod.reshape(depth, 1, n))


def _in_proj_kernel(npt, xp_ref, xs_ref, mod_ref, g_ref, w_ref, proj_ref, kv_ref):
    i = pl.program_id(0)
    tm = proj_ref.shape[0]
    sub = min(tm, IN_SUB)
    for s in range(tm // sub):
        rows = slice(s * sub, (s + 1) * sub)
        x = jnp.where(i < npt, xp_ref[rows, :], xs_ref[rows, :])
        y = x * _inv_rms(x, x.shape[-1]) * g_ref[...]
        h = (y * (1.0 + mod_ref[1:2, :]) + mod_ref[0:1, :]).astype(BF16)
        for j in range(PROJ_W // IN_TN):
            acc = jnp.dot(h, w_ref[:, j * IN_TN:(j + 1) * IN_TN], preferred_element_type=F32)
            proj_ref[rows, j * IN_TN:(j + 1) * IN_TN] = acc.astype(BF16)
            if j == KV_COL // IN_TN:
                kv_ref[rows, :] = acc[:, KV_COL % IN_TN:KV_COL % IN_TN + KV_W]


def _in_proj(xp, xs, mod_l, g, w, tm, mod_index):
    tp, d = xp.shape
    t = tp + xs.shape[0]
    npt = tp // tm
    return pl.pallas_call(
        functools.partial(_in_proj_kernel, npt),
        grid=(t // tm,),
        in_specs=[pl.BlockSpec((tm, d), lambda i: (jnp.minimum(i, npt - 1), 0)),
                  pl.BlockSpec((tm, d), lambda i: (jnp.maximum(i - npt, 0), 0)),
                  pl.BlockSpec((None, 6, d), lambda i: (mod_index(i, tm), 0, 0)),
                  pl.BlockSpec((1, d), lambda i: (0, 0)),
                  pl.BlockSpec(w.shape, lambda i: (0, 0), pipeline_mode=pl.Buffered(1))],
        out_specs=[pl.BlockSpec((tm, PROJ_W), lambda i: (i, 0)),
                   pl.BlockSpec((tm, KV_W), lambda i: (i, 0))],
        out_shape=[jax.ShapeDtypeStruct((t, PROJ_W), BF16),
                   jax.ShapeDtypeStruct((t, KV_W), F32)],
        compiler_params=_cparams(("arbitrary",), 56),
        name="in_proj",
    )(xp, xs, mod_l, g, w)


def _rope_tile(t, cos, sin):
    lane = lax.broadcasted_iota(jnp.int32, t.shape, 1)
    first = (lane & 16) == 0
    swapped = jnp.where(first, pltpu.roll(t, LANE - 16, 1), pltpu.roll(t, 16, 1))
    return t * cos + swapped * sin


def _keys_values(ckv_n, krope, w_ukv_ref, gk_ref, cos, sin, k_ref, v_ref):
    kvf = jnp.dot(ckv_n.astype(BF16), w_ukv_ref[...], preferred_element_type=F32)
    kr_ss = jnp.sum(krope * krope, axis=-1, keepdims=True)
    gk = gk_ref[...]
    tail = krope * gk[:, MLA_NOPE:]
    if cos is not None:
        tail = _rope_tile(tail, cos, sin)
    for h in range(MLA_HEADS):
        kn = kvf[:, h * MLA_NOPE:(h + 1) * MLA_NOPE]
        r = lax.rsqrt((jnp.sum(kn * kn, axis=-1, keepdims=True) + kr_ss) * (1.0 / MLA_QK) + NORM_EPS)
        k_ref[:, h * HEAD_PAD:h * HEAD_PAD + MLA_NOPE] = (kn * r * gk[:, :MLA_NOPE]).astype(BF16)
        k_ref[:, h * HEAD_PAD + MLA_NOPE:(h + 1) * HEAD_PAD] = (tail * r).astype(BF16)
    v_ref[...] = kvf[:, MLA_HEADS * MLA_NOPE:].astype(BF16)


def _qkv_kernel(npt, q_lora, cq_ref, kv_ref, cos_ref, sin_ref, w_uq_ref, w_ukv_ref, gql_ref, gkv_ref,
                gq_ref, gk_ref, q_ref, k_ref, v_ref, st_ref):
    i = pl.program_id(0)
    cos = cos_ref[...]
    sin = sin_ref[...]
    cq = cq_ref[...].astype(F32)
    cqn = cq * _inv_rms(cq, q_lora) * gql_ref[...]
    qf = jnp.dot(cqn.astype(BF16), w_uq_ref[...], preferred_element_type=F32)
    gq = gq_ref[...]
    scale = MLA_QK ** -0.5 * LOG2_E
    for h in range(MLA_HEADS):
        qh = qf[:, h * HEAD_PAD:(h + 1) * HEAD_PAD]
        qh = qh * (_inv_rms(qh, MLA_QK) * scale) * gq
        q_ref[:, h * HEAD_PAD:h * HEAD_PAD + MLA_NOPE] = qh[:, :MLA_NOPE].astype(BF16)
        q_ref[:, h * HEAD_PAD + MLA_NOPE:(h + 1) * HEAD_PAD] = _rope_tile(qh[:, MLA_NOPE:], cos, sin).astype(BF16)

    kv = kv_ref[...]
    ckv = kv[:, :MLA_NOPE]
    krope = kv[:, MLA_NOPE:]
    ckv_n = ckv * _inv_rms(ckv, ckv.shape[-1]) * gkv_ref[...]
    _keys_values(ckv_n, krope, w_ukv_ref, gk_ref, cos, sin, k_ref, v_ref)

    @pl.when(i < npt)
    def _():
        st_ref[:, :MLA_NOPE] = ckv_n
        st_ref[:, MLA_NOPE:] = krope


def _qkv(proj, kv, cos_t, sin_t, w_uq, w_ukv, gql, gkv, gq, gk, tp, dec_seq, q_lora, tm):
    t = proj.shape[0]
    npt = tp // tm
    nseq = dec_seq // tm
    cq_blk = CQ_COL // 512

    def tab(i):
        return (jnp.where(i < npt, nseq, (i - npt) % nseq), 0)

    const = lambda i: (0, 0)
    return pl.pallas_call(
        functools.partial(_qkv_kernel, npt, q_lora),
        grid=(t // tm,),
        in_specs=[pl.BlockSpec((tm, 512), lambda i: (i, cq_blk)),
                  pl.BlockSpec((tm, KV_W), lambda i: (i, 0)),
                  pl.BlockSpec((tm, LANE), tab),
                  pl.BlockSpec((tm, LANE), tab),
                  pl.BlockSpec(w_uq.shape, const),
                  pl.BlockSpec(w_ukv.shape, const),
                  pl.BlockSpec(gql.shape, const),
                  pl.BlockSpec(gkv.shape, const),
                  pl.BlockSpec(gq.shape, const),
                  pl.BlockSpec(gk.shape, const)],
        out_specs=[pl.BlockSpec((tm, MLA_HEADS * HEAD_PAD), lambda i: (i, 0)),
                   pl.BlockSpec((tm, MLA_HEADS * HEAD_PAD), lambda i: (i, 0)),
                   pl.BlockSpec((tm, MLA_HEADS * MLA_V), lambda i: (i, 0)),
                   pl.BlockSpec((tm, KV_W), lambda i: (jnp.minimum(i, npt - 1), 0))],
        out_shape=[jax.ShapeDtypeStruct((t, MLA_HEADS * HEAD_PAD), BF16),
                   jax.ShapeDtypeStruct((t, MLA_HEADS * HEAD_PAD), BF16),
                   jax.ShapeDtypeStruct((t, MLA_HEADS * MLA_V), BF16),
                   jax.ShapeDtypeStruct((tp, KV_W), F32)],
        compiler_params=_cparams(("arbitrary",), 40),
        name="qkv",
    )(proj, kv, cos_t, sin_t, w_uq, w_ukv, gql, gkv, gq, gk)


def _ctx_kv_kernel(kv_ref, w_ukv_ref, gk_ref, k_ref, v_ref):
    kv = kv_ref[...]
    _keys_values(kv[:, :MLA_NOPE], kv[:, MLA_NOPE:], w_ukv_ref, gk_ref, None, None, k_ref, v_ref)


def _ctx_kv(kvc, w_ukv, gk, tm):
    r = kvc.shape[0]
    const = lambda i: (0, 0)
    return pl.pallas_call(
        _ctx_kv_kernel,
        grid=(r // tm,),
        in_specs=[pl.BlockSpec((tm, KV_W), lambda i: (i, 0)),
                  pl.BlockSpec(w_ukv.shape, const),
                  pl.BlockSpec(gk.shape, const)],
        out_specs=[pl.BlockSpec((tm, MLA_HEADS * HEAD_PAD), lambda i: (i, 0)),
                   pl.BlockSpec((tm, MLA_HEADS * MLA_V), lambda i: (i, 0))],
        out_shape=[jax.ShapeDtypeStruct((r, MLA_HEADS * HEAD_PAD), BF16),
                   jax.ShapeDtypeStruct((r, MLA_HEADS * MLA_V), BF16)],
        compiler_params=_cparams(("arbitrary",), 32),
        name="ctx_kv",
    )(kvc, w_ukv, gk)


_NT = (((1,), (1,)), ((), ()))


def _attn_kernel(has_ctx, *refs):
    if has_ctx:
        q_ref, kc_ref, vc_ref, k_ref, v_ref, w1_ref, w3_ref, w2_ref, o_ref, w1o_ref, w3o_ref, w2o_ref = refs
        w1o_ref[...] = w1_ref[...].astype(BF16)
        w3o_ref[...] = w3_ref[...].astype(BF16)
        w2o_ref[...] = w2_ref[...].astype(BF16)
    else:
        q_ref, k_ref, v_ref, o_ref = refs
    n_own = k_ref.shape[0]
    step = min(n_own, ATTN_KCHUNK)
    chunks = [(k_ref, v_ref, c * step, step) for c in range(n_own // step)]
    if has_ctx:
        chunks = [(kc_ref, vc_ref, 0, kc_ref.shape[0])] + chunks
    for h in range(q_ref.shape[1] // HEAD_PAD):
        qk = slice(h * HEAD_PAD, (h + 1) * HEAD_PAD)
        hv = slice(h * MLA_V, (h + 1) * MLA_V)
        q = q_ref[:, qk]
        m = l = acc = None
        for kr, vr, off, n in chunks:
            s = lax.dot_general(q, kr[off:off + n, qk], _NT, preferred_element_type=F32)
            mc = jnp.max(s, axis=-1, keepdims=True)
            m_new = mc if m is None else jnp.maximum(m, mc)
            p = jnp.exp2(s - m_new)
            pv = jnp.dot(p.astype(BF16), vr[off:off + n, hv], preferred_element_type=F32)
            ps = jnp.sum(p, axis=-1, keepdims=True)
            if m is None:
                l, acc = ps, pv
            else:
                alpha = jnp.exp2(m - m_new)
                l = alpha * l + ps
                acc = alpha * acc + pv
            m = m_new
        o_ref[:, hv] = (acc / l).astype(BF16)


def _attention_latent(q, k, v, kc, vc, w1, w3, w2, layer, tp, nb, dec_seq, past, tq):
    t = q.shape[0]
    nq = dec_seq // tq
    row0 = tp // tq
    kblk0 = tp // dec_seq
    steps = nb * MLA_HEADS * nq
    depth, n_exp, d, de = w1.shape
    flat = [w1.reshape(depth, n_exp * d, de), w3.reshape(depth, n_exp * d, de), w2.reshape(depth, n_exp * de, d)]
    assert all(w.shape[1] % (8 * steps) == 0 for w in flat)
    step = lambda b, h, i: (b * MLA_HEADS + h) * nq + i
    w_in = [pl.BlockSpec((None, w.shape[1] // steps, w.shape[2]), lambda b, h, i: (layer, step(b, h, i), 0))
            for w in flat]
    w_out = [pl.BlockSpec((w.shape[1] // steps, w.shape[2]), lambda b, h, i: (step(b, h, i), 0)) for w in flat]
    o, w1b, w3b, w2b = pl.pallas_call(
        functools.partial(_attn_kernel, True),
        grid=(nb, MLA_HEADS, nq),
        in_specs=[pl.BlockSpec((tq, HEAD_PAD), lambda b, h, i: (row0 + b * nq + i, h)),
                  pl.BlockSpec((past, HEAD_PAD), lambda b, h, i: (b, h)),
                  pl.BlockSpec((past, MLA_V), lambda b, h, i: (b, h)),
                  pl.BlockSpec((dec_seq, HEAD_PAD), lambda b, h, i: (kblk0 + b, h)),
                  pl.BlockSpec((dec_seq, MLA_V), lambda b, h, i: (kblk0 + b, h))] + w_in,
        out_specs=[pl.BlockSpec((tq, MLA_V), lambda b, h, i: (b * nq + i, h))] + w_out,
        out_shape=[jax.ShapeDtypeStruct((t - tp, MLA_HEADS * MLA_V), BF16)]
                  + [jax.ShapeDtypeStruct(w.shape[1:], BF16) for w in flat],
        compiler_params=_cparams(("arbitrary", "arbitrary", "arbitrary"), 56),
        name="attn_latent",
    )(q, kc, vc, k, v, *flat)
    return o, w1b.reshape(n_exp, d, de), w3b.reshape(n_exp, d, de), w2b.reshape(n_exp, de, d)


def _attention_context(q, k, v, nseq, seq):
    return pl.pallas_call(
        functools.partial(_attn_kernel, False),
        grid=(nseq,),
        in_specs=[pl.BlockSpec((seq, MLA_HEADS * HEAD_PAD), lambda b: (b, 0)),
                  pl.BlockSpec((seq, MLA_HEADS * HEAD_PAD), lambda b: (b, 0)),
                  pl.BlockSpec((seq, MLA_HEADS * MLA_V), lambda b: (b, 0))],
        out_specs=pl.BlockSpec((seq, MLA_HEADS * MLA_V), lambda b: (b, 0)),
        out_shape=jax.ShapeDtypeStruct((nseq * seq, MLA_HEADS * MLA_V), BF16),
        compiler_params=_cparams(("arbitrary",), 32),
        name="attn_context",
    )(q, k, v)


def _mixers_kernel(npt, p_tiles, s_tiles,
                   cv_ref, cvp_ref, cvn_ref, gm_ref, sc_ref, scp_ref, scn_ref, bg_ref,
                   wdw_ref, bdw_ref, gcl_ref, bcl_ref, wpw_ref, bpw_ref,
                   gsl_ref, bsl_ref, wsp_ref, bsp_ref, wsc_ref,
                   o_ref, zs, ys, zsh):
    i = pl.program_id(0)
    tm = cv_ref.shape[0]
    cw = cv_ref.shape[1] // 2
    pos = jnp.where(i < npt, i % p_tiles, (i - npt) % s_tiles)
    n_tiles = jnp.where(i < npt, p_tiles, s_tiles)
    keep_prev = (pos > 0).astype(F32)
    keep_next = (pos < n_tiles - 1).astype(F32)

    def glu(ref):
        a = ref[:, :cw].astype(F32)
        g = ref[:, cw:].astype(F32)
        return a * jax.nn.sigmoid(g)

    def prod(ref):
        return ref[:, :cw].astype(F32) * ref[:, cw:].astype(F32)

    zs[0:HALO, :] = glu(cvp_ref) * keep_prev
    zs[HALO:HALO + tm, :] = glu(cv_ref)
    zs[HALO + tm:, :] = glu(cvn_ref) * keep_next
    pad = CONV_K // 2
    half = tm // 2
    for r in range(1, 8):
        zsh[r - 1] = zs[pl.ds(r, tm + 24), :]
    for c in range(cw // LANE):
        cs = slice(c * LANE, (c + 1) * LANE)
        for rh in range(2):
            base = rh * half
            acc = jnp.zeros((half, LANE), F32)
            for r in range(8):
                for a in range(4):
                    tap = 8 * a + r - (HALO - pad)
                    if 0 <= tap < CONV_K:
                        rows = slice(base + 8 * a, base + 8 * a + half)
                        src = zs[rows, cs] if r == 0 else zsh[r - 1, rows, cs]
                        acc = acc + src * wdw_ref[tap:tap + 1, cs]
            ys[base:base + half, cs] = acc
    z = ys[0:tm, :] + bdw_ref[...]
    z = _layernorm(z, gcl_ref[...], bcl_ref[...])
    z = z * jax.nn.sigmoid(z)
    o_b = jnp.dot(z.astype(BF16), wpw_ref[...], preferred_element_type=F32) + bpw_ref[...]
    o_ref[:, 0:cw] = o_b.astype(BF16)

    u = gm_ref[:, :cw].astype(F32)
    vg = _layernorm(gm_ref[:, cw:].astype(F32), gsl_ref[...], bsl_ref[...]).astype(BF16)
    hc = cw // GMLP_HEADS
    for n in range(tm // CHUNK):
        rows = slice(n * CHUNK, (n + 1) * CHUNK)
        for h in range(GMLP_HEADS):
            cols = slice(h * hc, (h + 1) * hc)
            mixed = jnp.dot(wsp_ref[h], vg[rows, cols], preferred_element_type=F32) + bsp_ref[:, cols]
            o_ref[rows, cw + h * hc:cw + (h + 1) * hc] = (u[rows, cols] * mixed).astype(BF16)

    ys[0:HALO, :] = prod(scp_ref) * keep_prev
    ys[HALO:HALO + tm, :] = prod(sc_ref)
    ys[HALO + tm:, :] = prod(scn_ref) * keep_next
    acc = jnp.zeros((tm, cw), F32)
    for tap in range(SC_K):
        acc = acc + ys[pl.ds(HALO - SC_K // 2 + tap, tm), :] * wsc_ref[tap:tap + 1, :]
    o_ref[:, 2 * cw:3 * cw] = (bg_ref[...].astype(F32) * acc).astype(BF16)


def _mixers(proj, weights, tp, seq, dec_seq, tm):
    t = proj.shape[0]
    cw = 512
    npt = tp // tm
    hb = tm // HALO
    last = t // HALO - 1
    const2 = lambda i: (0, 0)

    def main(col):
        return pl.BlockSpec((tm, 2 * cw), lambda i: (i, col))

    def prev(col):
        return pl.BlockSpec((HALO, 2 * cw), lambda i: (jnp.maximum(i * hb - 1, 0), col))

    def nxt(col):
        return pl.BlockSpec((HALO, 2 * cw), lambda i: (jnp.minimum((i + 1) * hb, last), col))

    w_specs = [pl.BlockSpec(w.shape, const2 if w.ndim == 2 else (lambda i: (0, 0, 0))) for w in weights]
    return pl.pallas_call(
        functools.partial(_mixers_kernel, npt, seq // tm, dec_seq // tm),
        grid=(t // tm,),
        in_specs=[main(0), prev(0), nxt(0), main(1), main(2), prev(2), nxt(2),
                  pl.BlockSpec((tm, cw), lambda i: (i, 6))] + w_specs,
        out_specs=pl.BlockSpec((tm, 3 * cw), lambda i: (i, 0)),
        out_shape=jax.ShapeDtypeStruct((t, 3 * cw), BF16),
        scratch_shapes=[pltpu.VMEM((tm + 2 * HALO, cw), F32), pltpu.VMEM((tm + 2 * HALO, cw), F32),
                        pltpu.VMEM((7, tm + 24, cw), F32)],
        compiler_params=_cparams(("arbitrary",), 40),
        name="mixers",
    )(proj, proj, proj, proj, proj, proj, proj, proj, *weights)


def _out_proj_kernel(npt, n, oap_ref, oas_ref, ob_ref, xp_ref, xs_ref, mod_ref, g_ref, wo_ref, wr_ref,
                     xn_ref, hp_ref, r_ref, mix_a, mix_b):
    i = pl.program_id(0)
    ka = oap_ref.shape[1]
    mm_tile = jnp.minimum(i, n - 1)
    ep_tile = jnp.maximum(i - 1, 0)

    @pl.when(i == 0)
    def _():
        mix_b[...] = jnp.zeros_like(mix_b)

    def step(mix_w, mix_r):
        oa = jnp.where(mm_tile < npt, oap_ref[...], oas_ref[...])
        mix_w[...] = jnp.dot(jnp.concatenate([oa, ob_ref[...]], axis=1), wo_ref[...],
                             preferred_element_type=F32)
        x = jnp.where(ep_tile < npt, xp_ref[...], xs_ref[...])
        xn = x + mod_ref[2:3, :] * mix_r[...]
        xn_ref[...] = xn
        h = xn * _inv_rms(xn, xn.shape[-1]) * g_ref[...]
        h = h * (1.0 + mod_ref[4:5, :]) + mod_ref[3:4, :]
        hp_ref[...] = _pack_rows(h)
        h_hi, h_lo = _split_bf16(h)
        r_ref[...] = (lax.dot_general(wr_ref[...], h_hi, _NT, preferred_element_type=F32)
                      + lax.dot_general(wr_ref[...], h_lo, _NT, preferred_element_type=F32))

    @pl.when(i % 2 == 0)
    def _():
        step(mix_a, mix_b)

    @pl.when(i % 2 == 1)
    def _():
        step(mix_b, mix_a)


def _out_proj(oap, oas, ob, xp, xs, mod_l, g, wo, wr, tm, mod_index):
    tp, d = xp.shape
    t = ob.shape[0]
    npt = tp // tm
    n = t // tm
    const = lambda i: (0, 0)
    mm = lambda i: jnp.minimum(i, n - 1)
    ep = lambda i: jnp.maximum(i - 1, 0)

    def pair(w, tile):
        return [pl.BlockSpec((tm, w), lambda i: (jnp.minimum(tile(i), npt - 1), 0)),
                pl.BlockSpec((tm, w), lambda i: (jnp.maximum(tile(i) - npt, 0), 0))]

    return pl.pallas_call(
        functools.partial(_out_proj_kernel, npt, n),
        grid=(n + 1,),
        in_specs=pair(oap.shape[1], mm) + [pl.BlockSpec((tm, ob.shape[1]), lambda i: (mm(i), 0))] + pair(d, ep) + [
                  pl.BlockSpec((None, 6, d), lambda i: (mod_index(ep(i), tm), 0, 0)),
                  pl.BlockSpec((1, d), const),
                  pl.BlockSpec(wo.shape, const, pipeline_mode=pl.Buffered(1)),
                  pl.BlockSpec(wr.shape, const, pipeline_mode=pl.Buffered(1))],
        out_specs=[pl.BlockSpec((tm, d), lambda i: (ep(i), 0)),
                   pl.BlockSpec((tm, d // 2), lambda i: (ep(i), 0)),
                   pl.BlockSpec((LANE, tm), lambda i: (0, ep(i)))],
        out_shape=[jax.ShapeDtypeStruct((t, d), F32),
                   jax.ShapeDtypeStruct((t, d // 2), U32),
                   jax.ShapeDtypeStruct((LANE, t), F32)],
        scratch_shapes=[pltpu.VMEM((tm, d), F32), pltpu.VMEM((tm, d), F32)],
        compiler_params=_cparams(("arbitrary",), 56),
        name="out_proj",
    )(oap, oas, ob, xp, xs, mod_l, g, wo, wr)


N_BUCKET_PAD = 64


def _route_kernel(rt_ref, b_ref, tri_ref, bucket_ref, rank_ref, gate_ref, cnt_ref, run):
    i = pl.program_id(0)
    tm = rt_ref.shape[1]
    ne = N_GROUPS * GROUP_SIZE

    @pl.when(i == 0)
    def _():
        run[...] = jnp.zeros_like(run)

    sc = jax.nn.sigmoid(rt_ref[0:ne, :] + rt_ref[ne:2 * ne, :])
    sel = sc + b_ref[...]
    s = [sel[r * N_GROUPS:(r + 1) * N_GROUPS, :] for r in range(GROUP_SIZE)]
    c = [sc[r * N_GROUPS:(r + 1) * N_GROUPS, :] for r in range(GROUP_SIZE)]
    hi01, lo01 = jnp.maximum(s[0], s[1]), jnp.minimum(s[0], s[1])
    hi23, lo23 = jnp.maximum(s[2], s[3]), jnp.minimum(s[2], s[3])
    g_score = jnp.maximum(hi01, hi23) + jnp.maximum(jnp.minimum(hi01, hi23), jnp.maximum(lo01, lo23))
    grp = lax.broadcasted_iota(jnp.int32, g_score.shape, 0)
    g_best = jnp.min(jnp.where(g_score == jnp.max(g_score, axis=0, keepdims=True), grp, N_GROUPS),
                     axis=0, keepdims=True)
    own = grp == g_best
    v = [jnp.sum(jnp.where(own, s[r], 0.0), axis=0, keepdims=True) for r in range(GROUP_SIZE)]
    w = [jnp.sum(jnp.where(own, c[r], 0.0), axis=0, keepdims=True) for r in range(GROUP_SIZE)]

    def first_max(vals):
        best, idx, gate = vals[0], jnp.zeros_like(g_best), w[0]
        for r in range(1, GROUP_SIZE):
            upd = vals[r] > best
            best = jnp.where(upd, vals[r], best)
            idx = jnp.where(upd, r, idx)
            gate = jnp.where(upd, w[r], gate)
        return idx, gate

    r1, w1 = first_max(v)
    r2, w2 = first_max([jnp.where(r1 == r, -jnp.inf, v[r]) for r in range(GROUP_SIZE)])
    wsum = w1 + w2
    swap = r1 > r2
    r_lo = jnp.where(swap, r2, r1)
    r_hi = jnp.where(swap, r1, r2)
    gate_ref[0:1, :] = jnp.where(swap, w2, w1) / wsum
    gate_ref[1:2, :] = jnp.where(swap, w1, w2) / wsum
    pair = r_lo * 3 - ((r_lo * (r_lo - 1)) >> 1) + (r_hi - r_lo - 1)
    bucket = g_best * N_PAIRS + pair
    bucket_ref[...] = bucket

    onehot = (lax.broadcasted_iota(jnp.int32, (N_BUCKET_PAD, tm), 0) == bucket).astype(F32)
    before = jnp.dot(onehot.astype(BF16), tri_ref[...], preferred_element_type=F32) + run[...]
    rank_ref[...] = jnp.sum(onehot * before, axis=0, keepdims=True).astype(jnp.int32)
    run[...] += jnp.sum(onehot, axis=1, keepdims=True)

    @pl.when(i == pl.num_programs(0) - 1)
    def _():
        cnt_ref[...] = run[...]


def _route(rt, router_b, n_blk, tm):
    t = rt.shape[1]
    ne = router_b.shape[0]
    b_col = router_b.astype(F32).reshape(N_GROUPS, GROUP_SIZE).T.reshape(ne, 1)
    tri = (jnp.arange(tm)[:, None] < jnp.arange(tm)[None, :]).astype(BF16)
    bucket, rank, gate, cnt = pl.pallas_call(
        _route_kernel,
        grid=(t // tm,),
        in_specs=[pl.BlockSpec((LANE, tm), lambda i: (0, i)),
                  pl.BlockSpec((ne, 1), lambda i: (0, 0)),
                  pl.BlockSpec((tm, tm), lambda i: (0, 0))],
        out_specs=[pl.BlockSpec((1, tm), lambda i: (0, i)),
                   pl.BlockSpec((1, tm), lambda i: (0, i)),
                   pl.BlockSpec((2, tm), lambda i: (0, i)),
                   pl.BlockSpec((N_BUCKET_PAD, 1), lambda i: (0, 0))],
        out_shape=[jax.ShapeDtypeStruct((1, t), jnp.int32),
                   jax.ShapeDtypeStruct((1, t), jnp.int32),
                   jax.ShapeDtypeStruct((2, t), F32),
                   jax.ShapeDtypeStruct((N_BUCKET_PAD, 1), F32)],
        scratch_shapes=[pltpu.VMEM((N_BUCKET_PAD, 1), F32)],
        compiler_params=_cparams(("arbitrary",), 32),
        name="route",
    )(rt, b_col, tri)

    n_bucket = N_GROUPS * N_PAIRS
    counts = cnt[:n_bucket, 0].astype(jnp.int32)
    padded = (counts + MOE_BM - 1) // MOE_BM * MOE_BM
    pad_end = jnp.cumsum(padded)
    pad_start = pad_end - padded
    bucket = bucket[0]
    sel = bucket[:, None] == jnp.arange(n_bucket, dtype=jnp.int32)[None, :]
    pos = jnp.sum(jnp.where(sel, pad_start[None, :], 0), axis=1).astype(jnp.int32) + rank[0]
    n_used = (pad_end[-1] // MOE_BM).astype(jnp.int32)
    blk = jnp.minimum(jnp.arange(n_blk, dtype=jnp.int32), n_used - 1)
    blk_bucket = jnp.sum((pad_end[None, :] <= (blk * MOE_BM)[:, None]).astype(jnp.int32), axis=1)
    blk_bucket = jnp.minimum(blk_bucket, n_bucket - 1)
    pair_lo = jnp.array([0, 0, 0, 1, 1, 2], jnp.int32)
    pair_hi = jnp.array([1, 2, 3, 2, 3, 3], jnp.int32)
    blk_pair = blk_bucket % N_PAIRS
    first = (blk_bucket // N_PAIRS) * GROUP_SIZE
    is_pair = blk_pair[:, None] == jnp.arange(N_PAIRS, dtype=jnp.int32)[None, :]
    blk_a = first + jnp.sum(jnp.where(is_pair, pair_lo[None, :], 0), axis=1)
    blk_b = first + jnp.sum(jnp.where(is_pair, pair_hi[None, :], 0), axis=1)
    gates = jnp.pad(gate.T, ((0, 0), (0, GATE_W - 2)))
    return pos, gates, blk, blk_a.astype(jnp.int32), blk_b.astype(jnp.int32), n_used.reshape(1)


def _dispatch_kernel(pos_ref, hp_ref, gt_ref, init_ref, xs_ref, buf, sem):
    del init_ref
    i = pl.program_id(0)
    slot = i % 2
    tm = hp_ref.shape[0]
    w = hp_ref.shape[1]
    buf[slot, :, :w] = hp_ref[...]
    buf[slot, :, w:] = pltpu.bitcast(gt_ref[...], U32)

    def start(k, carry):
        for p in range(N_DMA_PRIORITIES):
            r = k * N_DMA_PRIORITIES + p
            pltpu.make_async_copy(buf.at[slot, pl.ds(r, 1)], xs_ref.at[pl.ds(pos_ref[0, 0, r], 1)],
                                  sem.at[slot]).start(priority=p)
        return carry

    def wait_all(s):
        def wait(r, carry):
            pltpu.make_async_copy(buf.at[s, pl.ds(0, 1)], xs_ref.at[pl.ds(0, 1)], sem.at[s]).wait()
            return carry
        lax.fori_loop(0, tm, wait, 0, unroll=8)

    lax.fori_loop(0, tm // N_DMA_PRIORITIES, start, 0, unroll=4)

    @pl.when(i > 0)
    def _():
        wait_all(1 - slot)

    @pl.when(i == pl.num_programs(0) - 1)
    def _():
        wait_all(slot)


def _dispatch(hp, gates, pos, n_buf, tm):
    t, w = hp.shape
    init = jnp.zeros((n_buf, w + GATE_W), U32)
    return pl.pallas_call(
        _dispatch_kernel,
        grid=(t // tm,),
        in_specs=[pl.BlockSpec((1, 1, tm), lambda i: (i, 0, 0), memory_space=pltpu.SMEM),
                  pl.BlockSpec((tm, w), lambda i: (i, 0)),
                  pl.BlockSpec((tm, GATE_W), lambda i: (i, 0)),
                  pl.BlockSpec(memory_space=pl.ANY)],
        out_specs=pl.BlockSpec(memory_space=pl.ANY),
        out_shape=jax.ShapeDtypeStruct((n_buf, w + GATE_W), U32),
        scratch_shapes=[pltpu.VMEM((2, tm, w + GATE_W), U32), pltpu.SemaphoreType.DMA((2,))],
        input_output_aliases={3: 0},
        compiler_params=_cparams(("arbitrary",), 32),
        name="dispatch",
    )(pos.reshape(t // tm, 1, tm), hp, gates, init)


def _moe_kernel(blk_ref, ea_ref, eb_ref, nu_ref, x_ref, w1a_ref, w3a_ref, w2a_ref, w1b_ref, w3b_ref, w2b_ref,
                y_ref):
    del blk_ref, ea_ref, eb_ref
    i = pl.program_id(0)

    @pl.when(i < nu_ref[0])
    def _():
        w = x_ref.shape[1] - GATE_W
        xa, xb = _unpack_rows(x_ref[:, :w])
        x = jnp.concatenate([xa.astype(BF16), xb.astype(BF16)], axis=1)
        gates = pltpu.bitcast(x_ref[:, w:], F32)

        def hidden(w1_ref, w3_ref, g):
            a = jnp.dot(x, w1_ref[...], preferred_element_type=F32)
            b = jnp.dot(x, w3_ref[...], preferred_element_type=F32)
            return (a * jax.nn.sigmoid(a) * b * g).astype(BF16)

        y = jnp.dot(hidden(w1a_ref, w3a_ref, gates[:, 0:1]), w2a_ref[...], preferred_element_type=F32)
        y += jnp.dot(hidden(w1b_ref, w3b_ref, gates[:, 1:2]), w2b_ref[...], preferred_element_type=F32)
        y_ref[...] = _pack_rows(y)

    @pl.when(i >= nu_ref[0])
    def _():
        y_ref[...] = jnp.zeros_like(y_ref)


def _moe(xs, w1, w3, w2, blk, blk_a, blk_b, n_used):
    n_buf, wx = xs.shape
    n_blk = n_buf // MOE_BM
    _, d, de = w1.shape

    def expert(second, rows, cols):
        if second:
            return pl.BlockSpec((None, rows, cols), lambda i, blk, ea, eb, nu: (eb[i], 0, 0))
        return pl.BlockSpec((None, rows, cols), lambda i, blk, ea, eb, nu: (ea[i], 0, 0))

    grid_spec = pltpu.PrefetchScalarGridSpec(
        num_scalar_prefetch=4,
        grid=(n_blk,),
        in_specs=[pl.BlockSpec((MOE_BM, wx), lambda i, blk, ea, eb, nu: (blk[i], 0)),
                  expert(False, d, de), expert(False, d, de), expert(False, de, d),
                  expert(True, d, de), expert(True, d, de), expert(True, de, d)],
        out_specs=pl.BlockSpec((MOE_BM, d // 2), lambda i, blk, ea, eb, nu: (i, 0)),
    )
    return pl.pallas_call(
        _moe_kernel,
        grid_spec=grid_spec,
        out_shape=jax.ShapeDtypeStruct((n_buf, d // 2), U32),
        compiler_params=_cparams(("arbitrary",), 48),
        name="moe",
    )(blk, blk_a, blk_b, n_used, xs, w1, w3, w2, w1, w3, w2)


def _combine_kernel(npt, pos_ref, nxt_ref, xn_ref, mod_ref, yb_ref, op_ref, os_ref, buf, sem):
    i = pl.program_id(0)
    slot = i % 2
    tm = xn_ref.shape[0]

    def gather(idx_ref, s):
        def start(k, carry):
            for p in range(N_DMA_PRIORITIES):
                r = k * N_DMA_PRIORITIES + p
                pltpu.make_async_copy(yb_ref.at[pl.ds(idx_ref[0, 0, r], 1)], buf.at[s, pl.ds(r, 1)],
                                      sem.at[s]).start(priority=p)
            return carry
        lax.fori_loop(0, tm // N_DMA_PRIORITIES, start, 0, unroll=4)

    @pl.when(i == 0)
    def _():
        gather(pos_ref, slot)

    @pl.when(i + 1 < pl.num_programs(0))
    def _():
        gather(nxt_ref, 1 - slot)

    def wait(r, carry):
        pltpu.make_async_copy(yb_ref.at[pl.ds(0, 1)], buf.at[slot, pl.ds(0, 1)], sem.at[slot]).wait()
        return carry

    lax.fori_loop(0, tm, wait, 0, unroll=8)
    ya, yb = _unpack_rows(buf[slot])
    out = xn_ref[...] + mod_ref[5:6, :] * jnp.concatenate([ya, yb], axis=1)

    @pl.when(i < npt)
    def _():
        op_ref[...] = out

    @pl.when(i >= npt)
    def _():
        os_ref[...] = out


def _combine(xn, yb, pos, mod_l, tp, tm, mod_index):
    t, d = xn.shape
    npt = tp // tm
    n = t // tm
    pos3 = pos.reshape(n, 1, tm)
    return pl.pallas_call(
        functools.partial(_combine_kernel, npt),
        grid=(n,),
        in_specs=[pl.BlockSpec((1, 1, tm), lambda i: (i, 0, 0), memory_space=pltpu.SMEM),
                  pl.BlockSpec((1, 1, tm), lambda i: (jnp.minimum(i + 1, n - 1), 0, 0), memory_space=pltpu.SMEM),
                  pl.BlockSpec((tm, d), lambda i: (i, 0)),
                  pl.BlockSpec((None, 6, d), lambda i: (mod_index(i, tm), 0, 0)),
                  pl.BlockSpec(memory_space=pl.ANY)],
        out_specs=[pl.BlockSpec((tm, d), lambda i: (jnp.minimum(i, npt - 1), 0)),
                   pl.BlockSpec((tm, d), lambda i: (jnp.maximum(i - npt, 0), 0))],
        out_shape=[jax.ShapeDtypeStruct((tp, d), F32),
                   jax.ShapeDtypeStruct((t - tp, d), F32)],
        scratch_shapes=[pltpu.VMEM((2, tm, d // 2), U32), pltpu.SemaphoreType.DMA((2,))],
        compiler_params=_cparams(("arbitrary",), 32),
        name="combine",
    )(pos3, pos3, xn, mod_l, yb)


def _pad_cols(x, n):
    return jnp.pad(x, ((0, 0), (0, n - x.shape[1])))


def _rope_tables(dec_seq, tm):
    n_freq = MLA_ROPE // 4
    pos = jnp.arange(dec_seq, dtype=jnp.int32)
    row = (pos // GRID_W).astype(F32)
    col = (pos % GRID_W).astype(F32)
    inv = ROPE_THETA ** (-jnp.arange(n_freq, dtype=F32) / n_freq)
    ar = row[:, None] * inv[None, :]
    ac = col[:, None] * inv[None, :]
    ones = jnp.ones((dec_seq, LANE - MLA_ROPE), F32)
    cos = jnp.concatenate([jnp.cos(ar), jnp.cos(ar), jnp.cos(ac), jnp.cos(ac), ones], axis=1)
    sin = jnp.concatenate([-jnp.sin(ar), jnp.sin(ar), -jnp.sin(ac), jnp.sin(ac), 0.0 * ones], axis=1)
    ident_c = jnp.ones((tm, LANE), F32)
    ident_s = jnp.zeros((tm, LANE), F32)
    return jnp.concatenate([cos, ident_c], axis=0), jnp.concatenate([sin, ident_s], axis=0)


def _layer_params(l, q_lora, kv_lora, w_in, g_q_lora, w_uq, g_kv_lora, w_ukv, g_qk_q, g_qk_k,
                  w_dw31, b_dw31, g_conv_ln, b_conv_ln, w_pw, b_pw, g_sgu_ln, b_sgu_ln,
                  w_spatial, b_spatial, w_sc3, w_out):
    d = w_in.shape[1]
    o1 = q_lora
    o2 = o1 + kv_lora
    o3 = o2 + MLA_ROPE
    o4 = o3 + 1024
    o5 = o4 + 1024
    w = w_in[l]
    zeros = lambda n: jnp.zeros((d, n), w.dtype)
    w_in_p = jnp.concatenate(
        [w[:, o3:o4], w[:, o4:o5], w[:, o5 + 512:o5 + 1536], w[:, o5:o5 + 512],
         w[:, :o1], zeros(512 - q_lora),
         w[:, o1:o2], w[:, o2:o3], zeros(PROJ_W - KV_COL - kv_lora - MLA_ROPE)], axis=1).astype(BF16)

    uq = w_uq[l].reshape(q_lora, MLA_HEADS, MLA_QK)
    uq = jnp.pad(uq, ((0, 512 - q_lora), (0, 0), (0, HEAD_PAD - MLA_QK)))
    w_uq_p = uq.reshape(512, MLA_HEADS * HEAD_PAD).astype(BF16)
    ukv = w_ukv[l].reshape(kv_lora, MLA_HEADS, MLA_NOPE + MLA_V)
    w_ukv_p = jnp.concatenate([ukv[:, :, :MLA_NOPE].reshape(kv_lora, -1),
                               ukv[:, :, MLA_NOPE:].reshape(kv_lora, -1)], axis=1).astype(BF16)
    row = lambda v: v.reshape(1, -1).astype(F32)
    qkv_w = (w_uq_p, w_ukv_p, _pad_cols(row(g_q_lora[l]), 512), row(g_kv_lora[l]),
             _pad_cols(row(g_qk_q[l]), HEAD_PAD), _pad_cols(row(g_qk_k[l]), HEAD_PAD))

    hc = w_pw.shape[1] // GMLP_HEADS
    bsp = jnp.repeat(b_spatial[l].T.astype(F32), hc, axis=1)
    mix_w = (jnp.pad(w_dw31[l].astype(F32), ((0, 32 - CONV_K), (0, 0))), row(b_dw31[l]),
             row(g_conv_ln[l]), row(b_conv_ln[l]), w_pw[l].astype(BF16), row(b_pw[l]),
             row(g_sgu_ln[l]), row(b_sgu_ln[l]), w_spatial[l].astype(BF16), bsp,
             jnp.pad(w_sc3[l].astype(F32), ((0, 8 - SC_K), (0, 0))))
    return w_in_p, qkv_w, mix_w, w_out[l].astype(BF16)


def _tile(limit, *sizes):
    tm = limit
    while any(s % tm for s in sizes):
        tm //= 2
    return tm


def kernel(x_prompt, x_sample, cache_mla_ckv, cache_mla_krope, c, c_ctx, w_mod, b_mod, g_norm1, g_norm2, w_in, g_q_lora, w_uq, g_kv_lora, w_ukv, g_qk_q, g_qk_k, w_dw31, b_dw31, g_conv_ln, b_conv_ln, w_pw, b_pw, g_sgu_ln, b_sgu_ln, w_spatial, b_spatial, w_sc3, w_out, router_w, router_b, w1, w3, w2):
    nseq, seq, d = x_prompt.shape
    nb, dec_seq, _ = x_sample.shape
    depth = w_mod.shape[0]
    past = cache_mla_ckv.shape[2]
    q_lora = g_q_lora.shape[1]
    kv_lora = g_kv_lora.shape[1]
    n_exp = router_w.shape[1]
    tp = nseq * seq
    ts = nb * dec_seq
    t = tp + ts
    assert tp % dec_seq == 0 and seq % CHUNK == 0 and dec_seq % GRID_W == 0
    assert nb + 1 <= 8 and kv_lora == MLA_NOPE and n_exp == N_GROUPS * GROUP_SIZE

    tm_in = _tile(512, tp, dec_seq)
    tm_qkv = _tile(512, tp, dec_seq)
    tm_row = _tile(256, seq, dec_seq)
    tq = _tile(1024, dec_seq)

    def mod_index(i, tm):
        npt = tp // tm
        return jnp.where(i < npt, 0, 1 + (i - npt) // (dec_seq // tm))

    cvec = jnp.concatenate([c_ctx[None, :], c, jnp.zeros((8 - 1 - nb, d), F32)], axis=0)
    mod = _modulation(cvec, w_mod, b_mod).reshape(depth, 8, 6, d)
    cos_t, sin_t = _rope_tables(dec_seq, tm_qkv)

    rw = router_w.T.reshape(N_GROUPS, GROUP_SIZE, d).transpose(1, 0, 2).reshape(n_exp, d)
    rw_hi = rw.astype(BF16)
    rw_lo = (rw - rw_hi.astype(F32)).astype(BF16)
    wr = jnp.pad(jnp.concatenate([rw_hi, rw_lo], axis=0), ((0, LANE - 2 * n_exp), (0, 0)))
    n_blk = (t + N_GROUPS * N_PAIRS * (MOE_BM - 1) + MOE_BM - 1) // MOE_BM

    xp = x_prompt.reshape(tp, d)
    xs = x_sample.reshape(ts, d)
    ckv_states = []
    krope_states = []
    for l in range(depth):
        w_in_p, qkv_w, mix_w, w_out_b = _layer_params(
            l, q_lora, kv_lora, w_in, g_q_lora, w_uq, g_kv_lora, w_ukv, g_qk_q, g_qk_k,
            w_dw31, b_dw31, g_conv_ln, b_conv_ln, w_pw, b_pw, g_sgu_ln, b_sgu_ln,
            w_spatial, b_spatial, w_sc3, w_out)
        mod_l = mod[l]

        proj, kv = _in_proj(xp, xs, mod_l, g_norm1[l].reshape(1, d), w_in_p, tm_in, mod_index)
        q, k, v, state = _qkv(proj, kv, cos_t, sin_t, *qkv_w, tp, dec_seq, q_lora, tm_qkv)
        kvc = jnp.concatenate([cache_mla_ckv[:, l], cache_mla_krope[:, l],
                               jnp.zeros((nb, past, KV_W - kv_lora - MLA_ROPE), F32)], axis=-1)
        kc, vc = _ctx_kv(kvc.reshape(nb * past, KV_W), qkv_w[1], qkv_w[5], _tile(512, past))
        o_as, w1b, w3b, w2b = _attention_latent(q, k, v, kc, vc, w1, w3, w2, l, tp, nb, dec_seq, past, tq)
        o_ap = _attention_context(q, k, v, nseq, seq)
        o_bcd = _mixers(proj, mix_w, tp, seq, dec_seq, tm_row)

        xn, hp, r = _out_proj(o_ap, o_as, o_bcd, xp, xs, mod_l, g_norm2[l].reshape(1, d), w_out_b, wr,
                              tm_qkv, mod_index)
        pos, gates, blk, blk_a, blk_b, n_used = _route(r, router_b, n_blk, tm_qkv)
        xd = _dispatch(hp, gates, pos, n_blk * MOE_BM, tm_row)
        yb = _moe(xd, w1b, w3b, w2b, blk, blk_a, blk_b, n_used)
        xp, xs = _combine(xn, yb, pos, mod_l, tp, tm_row, mod_index)

        ckv_states.append(state[:, :kv_lora].reshape(nseq, seq, kv_lora))
        krope_states.append(state[:, kv_lora:kv_lora + MLA_ROPE].reshape(nseq, seq, MLA_ROPE))

    return (xp.reshape(nseq, seq, d), xs.reshape(nb, dec_seq, d),
            jnp.stack(ckv_states, axis=1), jnp.stack(krope_states, axis=1))
```

```python
import functools

import jax
import jax.numpy as jnp
from jax import lax
from jax.experimental import pallas as pl
from jax.experimental.pallas import tpu as pltpu

F32 = jnp.float32
BF16 = jnp.bfloat16
U32 = jnp.uint32

MLA_HEADS = 4
MLA_NOPE = 128
MLA_ROPE = 64
MLA_V = 128
MLA_QK = MLA_NOPE + MLA_ROPE
HEAD_PAD = 256
GRID_W = 64
ROPE_THETA = 10000.0
CONV_K = 31
SC_K = 3
CHUNK = 128
GMLP_HEADS = 4
N_GROUPS = 8
GROUP_SIZE = 4
N_PAIRS = 6
NORM_EPS = 1e-6

LANE = 128
HALO = 16
PROJ_W = 4352
CQ_COL = 3584
KV_COL = 4096
KV_W = 256
IN_TN = 1536
MOE_BM = 256
N_DMA_PRIORITIES = 2
IN_SUB = 512
OUT_SUB = 256
ATTN_KCHUNK = 1024
LOG2_E = 1.4426950408889634
GATE_W = 128
VMEM_CAP = 56 * 1024 * 1024


def _cparams(sem, vmem_mb):
    return pltpu.CompilerParams(dimension_semantics=sem,
                                vmem_limit_bytes=min(vmem_mb * 1024 * 1024, VMEM_CAP))


def _inv_rms(x, n):
    return lax.rsqrt(jnp.sum(x * x, axis=-1, keepdims=True) * (1.0 / n) + NORM_EPS)


def _layernorm(x, g, b):
    mu = jnp.mean(x, axis=-1, keepdims=True)
    xc = x - mu
    var = jnp.mean(xc * xc, axis=-1, keepdims=True)
    return xc * lax.rsqrt(var + NORM_EPS) * g + b


def _split_bf16(x):
    hi = x.astype(BF16)
    lo = (x - hi.astype(F32)).astype(BF16)
    return hi, lo


def _pack_rows(x):
    n = x.shape[1] // 2
    hi = pltpu.bitcast(x[:, :n].astype(BF16).astype(F32), U32)
    lo = pltpu.bitcast(x[:, n:].astype(BF16).astype(F32), U32)
    return (hi & jnp.uint32(0xFFFF0000)) | (lo >> 16)


def _unpack_rows(u):
    a = pltpu.bitcast(u & jnp.uint32(0xFFFF0000), F32)
    b = pltpu.bitcast(u << 16, F32)
    return a, b


def _mod_kernel(c_ref, w_ref, b_ref, o_ref):
    c = c_ref[...]
    a_hi, a_lo = _split_bf16(c * jax.nn.sigmoid(c))
    w_hi, w_lo = _split_bf16(w_ref[...])
    acc = jnp.dot(a_hi, w_hi, preferred_element_type=F32)
    acc += jnp.dot(a_lo, w_hi, preferred_element_type=F32)
    acc += jnp.dot(a_hi, w_lo, preferred_element_type=F32)
    o_ref[...] = acc + b_ref[...]


def _modulation(cvec, w_mod, b_mod):
    depth, d, n = w_mod.shape
    tn = 1024
    return pl.pallas_call(
        _mod_kernel,
        grid=(depth, n // tn),
        in_specs=[pl.BlockSpec((8, d), lambda l, j: (0, 0)),
                  pl.BlockSpec((None, d, tn), lambda l, j: (l, 0, j)),
                  pl.BlockSpec((None, 1, tn), lambda l, j: (l, 0, j))],
        out_specs=pl.BlockSpec((None, 8, tn), lambda l, j: (l, 0, j)),
        out_shape=jax.ShapeDtypeStruct((depth, 8, n), F32),
        compiler_params=_cparams(("arbitrary", "arbitrary"), 40),
        name="modulation",
    )(cvec, w_mod, b_mod.reshape(depth, 1, n))


def _in_proj_kernel(npt, xp_ref, xs_ref, mod_ref, g_ref, w_ref, proj_ref, kv_ref):
    i = pl.program_id(0)
    tm = proj_ref.shape[0]
    sub = min(tm, IN_SUB)
    for s in range(tm // sub):
        rows = slice(s * sub, (s + 1) * sub)
        x = jnp.where(i < npt, xp_ref[rows, :], xs_ref[rows, :])
        y = x * _inv_rms(x, x.shape[-1]) * g_ref[...]
        h = (y * (1.0 + mod_ref[1:2, :]) + mod_ref[0:1, :]).astype(BF16)
        for c0 in range(0, PROJ_W, IN_TN):
            c1 = min(c0 + IN_TN, PROJ_W)
            acc = jnp.dot(h, w_ref[:, c0:c1], preferred_element_type=F32)
            proj_ref[rows, c0:c1] = acc.astype(BF16)
            if c0 <= KV_COL < c1:
                kv_ref[rows, :] = acc[:, KV_COL - c0:KV_COL - c0 + KV_W]


def _in_proj(xp, xs, mod_l, g, w, tm, mod_index):
    tp, d = xp.shape
    t = tp + xs.shape[0]
    npt = tp // tm
    return pl.pallas_call(
        functools.partial(_in_proj_kernel, npt),
        grid=(t // tm,),
        in_specs=[pl.BlockSpec((tm, d), lambda i: (jnp.minimum(i, npt - 1), 0)),
                  pl.BlockSpec((tm, d), lambda i: (jnp.maximum(i - npt, 0), 0)),
                  pl.BlockSpec((None, 6, d), lambda i: (mod_index(i, tm), 0, 0)),
                  pl.BlockSpec((1, d), lambda i: (0, 0)),
                  pl.BlockSpec(w.shape, lambda i: (0, 0), pipeline_mode=pl.Buffered(1))],
        out_specs=[pl.BlockSpec((tm, PROJ_W), lambda i: (i, 0)),
                   pl.BlockSpec((tm, KV_W), lambda i: (i, 0))],
        out_shape=[jax.ShapeDtypeStruct((t, PROJ_W), BF16),
                   jax.ShapeDtypeStruct((t, KV_W), F32)],
        compiler_params=_cparams(("arbitrary",), 56),
        name="in_proj",
    )(xp, xs, mod_l, g, w)


def _rope_tile(t, cos, sin):
    lane = lax.broadcasted_iota(jnp.int32, t.shape, 1)
    first = (lane & 16) == 0
    swapped = jnp.where(first, pltpu.roll(t, LANE - 16, 1), pltpu.roll(t, 16, 1))
    return t * cos + swapped * sin


def _keys_values(ckv_n, krope, w_ukv_ref, gk_ref, cos, sin, k_ref, v_ref):
    kvf = jnp.dot(ckv_n.astype(BF16), w_ukv_ref[...], preferred_element_type=F32)
    kr_ss = jnp.sum(krope * krope, axis=-1, keepdims=True)
    gk = gk_ref[...]
    tail = krope * gk[:, MLA_NOPE:]
    if cos is not None:
        tail = _rope_tile(tail, cos, sin)
    for h in range(MLA_HEADS):
        kn = kvf[:, h * MLA_NOPE:(h + 1) * MLA_NOPE]
        r = lax.rsqrt((jnp.sum(kn * kn, axis=-1, keepdims=True) + kr_ss) * (1.0 / MLA_QK) + NORM_EPS)
        k_ref[:, h * HEAD_PAD:h * HEAD_PAD + MLA_NOPE] = (kn * r * gk[:, :MLA_NOPE]).astype(BF16)
        k_ref[:, h * HEAD_PAD + MLA_NOPE:(h + 1) * HEAD_PAD] = (tail * r).astype(BF16)
    v_ref[...] = kvf[:, MLA_HEADS * MLA_NOPE:].astype(BF16)


def _qkv_kernel(npt, q_lora, cq_ref, kv_ref, cos_ref, sin_ref, w_uq_ref, w_ukv_ref, gql_ref, gkv_ref,
                gq_ref, gk_ref, q_ref, k_ref, v_ref, st_ref):
    i = pl.program_id(0)
    cos = cos_ref[...]
    sin = sin_ref[...]
    cq = cq_ref[...].astype(F32)
    cqn = cq * _inv_rms(cq, q_lora) * gql_ref[...]
    qf = jnp.dot(cqn.astype(BF16), w_uq_ref[...], preferred_element_type=F32)
    gq = gq_ref[...]
    scale = MLA_QK ** -0.5 * LOG2_E
    for h in range(MLA_HEADS):
        qh = qf[:, h * HEAD_PAD:(h + 1) * HEAD_PAD]
        qh = qh * (_inv_rms(qh, MLA_QK) * scale) * gq
        q_ref[:, h * HEAD_PAD:h * HEAD_PAD + MLA_NOPE] = qh[:, :MLA_NOPE].astype(BF16)
        q_ref[:, h * HEAD_PAD + MLA_NOPE:(h + 1) * HEAD_PAD] = _rope_tile(qh[:, MLA_NOPE:], cos, sin).astype(BF16)

    kv = kv_ref[...]
    ckv = kv[:, :MLA_NOPE]
    krope = kv[:, MLA_NOPE:]
    ckv_n = ckv * _inv_rms(ckv, ckv.shape[-1]) * gkv_ref[...]
    _keys_values(ckv_n, krope, w_ukv_ref, gk_ref, cos, sin, k_ref, v_ref)

    @pl.when(i < npt)
    def _():
        st_ref[:, :MLA_NOPE] = ckv_n
        st_ref[:, MLA_NOPE:] = krope


def _qkv(proj, kv, cos_t, sin_t, w_uq, w_ukv, gql, gkv, gq, gk, tp, dec_seq, q_lora, tm):
    t = proj.shape[0]
    npt = tp // tm
    nseq = dec_seq // tm
    cq_blk = CQ_COL // 512

    def tab(i):
        return (jnp.where(i < npt, nseq, (i - npt) % nseq), 0)

    const = lambda i: (0, 0)
    return pl.pallas_call(
        functools.partial(_qkv_kernel, npt, q_lora),
        grid=(t // tm,),
        in_specs=[pl.BlockSpec((tm, 512), lambda i: (i, cq_blk)),
                  pl.BlockSpec((tm, KV_W), lambda i: (i, 0)),
                  pl.BlockSpec((tm, LANE), tab),
                  pl.BlockSpec((tm, LANE), tab),
                  pl.BlockSpec(w_uq.shape, const),
                  pl.BlockSpec(w_ukv.shape, const),
                  pl.BlockSpec(gql.shape, const),
                  pl.BlockSpec(gkv.shape, const),
                  pl.BlockSpec(gq.shape, const),
                  pl.BlockSpec(gk.shape, const)],
        out_specs=[pl.BlockSpec((tm, MLA_HEADS * HEAD_PAD), lambda i: (i, 0)),
                   pl.BlockSpec((tm, MLA_HEADS * HEAD_PAD), lambda i: (i, 0)),
                   pl.BlockSpec((tm, MLA_HEADS * MLA_V), lambda i: (i, 0)),
                   pl.BlockSpec((tm, KV_W), lambda i: (jnp.minimum(i, npt - 1), 0))],
        out_shape=[jax.ShapeDtypeStruct((t, MLA_HEADS * HEAD_PAD), BF16),
                   jax.ShapeDtypeStruct((t, MLA_HEADS * HEAD_PAD), BF16),
                   jax.ShapeDtypeStruct((t, MLA_HEADS * MLA_V), BF16),
                   jax.ShapeDtypeStruct((tp, KV_W), F32)],
        compiler_params=_cparams(("arbitrary",), 40),
        name="qkv",
    )(proj, kv, cos_t, sin_t, w_uq, w_ukv, gql, gkv, gq, gk)


def _ctx_kv_kernel(kv_ref, w_ukv_ref, gk_ref, k_ref, v_ref):
    kv = kv_ref[...]
    _keys_values(kv[:, :MLA_NOPE], kv[:, MLA_NOPE:], w_ukv_ref, gk_ref, None, None, k_ref, v_ref)


def _ctx_kv(kvc, w_ukv, gk, tm):
    r = kvc.shape[0]
    const = lambda i: (0, 0)
    return pl.pallas_call(
        _ctx_kv_kernel,
        grid=(r // tm,),
        in_specs=[pl.BlockSpec((tm, KV_W), lambda i: (i, 0)),
                  pl.BlockSpec(w_ukv.shape, const),
                  pl.BlockSpec(gk.shape, const)],
        out_specs=[pl.BlockSpec((tm, MLA_HEADS * HEAD_PAD), lambda i: (i, 0)),
                   pl.BlockSpec((tm, MLA_HEADS * MLA_V), lambda i: (i, 0))],
        out_shape=[jax.ShapeDtypeStruct((r, MLA_HEADS * HEAD_PAD), BF16),
                   jax.ShapeDtypeStruct((r, MLA_HEADS * MLA_V), BF16)],
        compiler_params=_cparams(("arbitrary",), 32),
        name="ctx_kv",
    )(kvc, w_ukv, gk)


_NT = (((1,), (1,)), ((), ()))


def _attn_kernel(has_ctx, *refs):
    if has_ctx:
        q_ref, kc_ref, vc_ref, k_ref, v_ref, w1_ref, w3_ref, w2_ref, o_ref, w1o_ref, w3o_ref, w2o_ref = refs
        w1o_ref[...] = w1_ref[...].astype(BF16)
        w3o_ref[...] = w3_ref[...].astype(BF16)
        w2o_ref[...] = w2_ref[...].astype(BF16)
    else:
        q_ref, k_ref, v_ref, o_ref = refs
    n_own = k_ref.shape[0]
    step = min(n_own, ATTN_KCHUNK)
    chunks = [(k_ref, v_ref, c * step, step) for c in range(n_own // step)]
    if has_ctx:
        chunks = [(kc_ref, vc_ref, 0, kc_ref.shape[0])] + chunks
    for h in range(q_ref.shape[1] // HEAD_PAD):
        qk = slice(h * HEAD_PAD, (h + 1) * HEAD_PAD)
        hv = slice(h * MLA_V, (h + 1) * MLA_V)
        q = q_ref[:, qk]
        m = l = acc = None
        for kr, vr, off, n in chunks:
            s = lax.dot_general(q, kr[off:off + n, qk], _NT, preferred_element_type=F32)
            mc = jnp.max(s, axis=-1, keepdims=True)
            m_new = mc if m is None else jnp.maximum(m, mc)
            p = jnp.exp2(s - m_new)
            pv = jnp.dot(p.astype(BF16), vr[off:off + n, hv], preferred_element_type=F32)
            ps = jnp.sum(p, axis=-1, keepdims=True)
            if m is None:
                l, acc = ps, pv
            else:
                alpha = jnp.exp2(m - m_new)
                l = alpha * l + ps
                acc = alpha * acc + pv
            m = m_new
        o_ref[:, hv] = (acc / l).astype(BF16)


def _attention_latent(q, k, v, kc, vc, w1, w3, w2, layer, tp, nb, dec_seq, past, tq):
    t = q.shape[0]
    nq = dec_seq // tq
    row0 = tp // tq
    kblk0 = tp // dec_seq
    steps = nb * MLA_HEADS * nq
    depth, n_exp, d, de = w1.shape
    flat = [w1.reshape(depth, n_exp * d, de), w3.reshape(depth, n_exp * d, de), w2.reshape(depth, n_exp * de, d)]
    assert all(w.shape[1] % (8 * steps) == 0 for w in flat)
    step = lambda b, h, i: (b * MLA_HEADS + h) * nq + i
    w_in = [pl.BlockSpec((None, w.shape[1] // steps, w.shape[2]), lambda b, h, i: (layer, step(b, h, i), 0))
            for w in flat]
    w_out = [pl.BlockSpec((w.shape[1] // steps, w.shape[2]), lambda b, h, i: (step(b, h, i), 0)) for w in flat]
    o, w1b, w3b, w2b = pl.pallas_call(
        functools.partial(_attn_kernel, True),
        grid=(nb, MLA_HEADS, nq),
        in_specs=[pl.BlockSpec((tq, HEAD_PAD), lambda b, h, i: (row0 + b * nq + i, h)),
                  pl.BlockSpec((past, HEAD_PAD), lambda b, h, i: (b, h)),
                  pl.BlockSpec((past, MLA_V), lambda b, h, i: (b, h)),
                  pl.BlockSpec((dec_seq, HEAD_PAD), lambda b, h, i: (kblk0 + b, h)),
                  pl.BlockSpec((dec_seq, MLA_V), lambda b, h, i: (kblk0 + b, h))] + w_in,
        out_specs=[pl.BlockSpec((tq, MLA_V), lambda b, h, i: (b * nq + i, h))] + w_out,
        out_shape=[jax.ShapeDtypeStruct((t - tp, MLA_HEADS * MLA_V), BF16)]
                  + [jax.ShapeDtypeStruct(w.shape[1:], BF16) for w in flat],
        compiler_params=_cparams(("arbitrary", "arbitrary", "arbitrary"), 56),
        name="attn_latent",
    )(q, kc, vc, k, v, *flat)
    return o, w1b.reshape(n_exp, d, de), w3b.reshape(n_exp, d, de), w2b.reshape(n_exp, de, d)


def _attention_context(q, k, v, nseq, seq):
    return pl.pallas_call(
        functools.partial(_attn_kernel, False),
        grid=(nseq,),
        in_specs=[pl.BlockSpec((seq, MLA_HEADS * HEAD_PAD), lambda b: (b, 0)),
                  pl.BlockSpec((seq, MLA_HEADS * HEAD_PAD), lambda b: (b, 0)),
                  pl.BlockSpec((seq, MLA_HEADS * MLA_V), lambda b: (b, 0))],
        out_specs=pl.BlockSpec((seq, MLA_HEADS * MLA_V), lambda b: (b, 0)),
        out_shape=jax.ShapeDtypeStruct((nseq * seq, MLA_HEADS * MLA_V), BF16),
        compiler_params=_cparams(("arbitrary",), 32),
        name="attn_context",
    )(q, k, v)


def _mixers_kernel(npt, p_tiles, s_tiles,
                   cv_ref, cvp_ref, cvn_ref, gm_ref, sc_ref, scp_ref, scn_ref, bg_ref,
                   wdw_ref, bdw_ref, gcl_ref, bcl_ref, wpw_ref, bpw_ref,
                   gsl_ref, bsl_ref, wsp_ref, bsp_ref, wsc_ref,
                   o_ref, zs, ys, zsh, cb):
    i = pl.program_id(0)
    tm = cv_ref.shape[0]
    cw = cv_ref.shape[1] // 2
    pos = jnp.where(i < npt, i % p_tiles, (i - npt) % s_tiles)
    n_tiles = jnp.where(i < npt, p_tiles, s_tiles)
    keep_prev = (pos > 0).astype(F32)
    keep_next = (pos < n_tiles - 1).astype(F32)

    def glu(ref):
        a = ref[:, :cw].astype(F32)
        g = ref[:, cw:].astype(F32)
        return a * jax.nn.sigmoid(g)

    def prod(ref):
        return ref[:, :cw].astype(F32) * ref[:, cw:].astype(F32)

    n_lane = cw // LANE
    z_prev, z_main, z_next = glu(cvp_ref) * keep_prev, glu(cv_ref), glu(cvn_ref) * keep_next
    for c in range(n_lane):
        cs = slice(c * LANE, (c + 1) * LANE)
        zs[c, 0:HALO, :] = z_prev[:, cs]
        zs[c, HALO:HALO + tm, :] = z_main[:, cs]
        zs[c, HALO + tm:, :] = z_next[:, cs]
        for r in range(1, 8):
            zsh[r - 1, c] = zs[c, pl.ds(r, tm + 24), :]
    pad = CONV_K // 2
    half = tm // 2

    def conv_chunk(idx, carry):
        c = idx // 2
        base = pl.multiple_of((idx % 2) * half, half)
        acc = jnp.zeros((half, LANE), F32)
        for r in range(8):
            for a in range(4):
                tap = 8 * a + r - (HALO - pad)
                if 0 <= tap < CONV_K:
                    rows = pl.ds(base + 8 * a, half)
                    src = zs[c, rows, :] if r == 0 else zsh[r - 1, c, rows, :]
                    acc = acc + src * wdw_ref[c, tap:tap + 1, :]
        cb[c, pl.ds(base, half), :] = acc
        return carry

    lax.fori_loop(0, 2 * n_lane, conv_chunk, 0)
    z = jnp.concatenate([cb[c] for c in range(n_lane)], axis=1) + bdw_ref[...]
    z = _layernorm(z, gcl_ref[...], bcl_ref[...])
    z = z * jax.nn.sigmoid(z)
    o_b = jnp.dot(z.astype(BF16), wpw_ref[...], preferred_element_type=F32) + bpw_ref[...]
    o_ref[:, 0:cw] = o_b.astype(BF16)

    u = gm_ref[:, :cw].astype(F32)
    vg = _layernorm(gm_ref[:, cw:].astype(F32), gsl_ref[...], bsl_ref[...]).astype(BF16)
    hc = cw // GMLP_HEADS
    for n in range(tm // CHUNK):
        rows = slice(n * CHUNK, (n + 1) * CHUNK)
        for h in range(GMLP_HEADS):
            cols = slice(h * hc, (h + 1) * hc)
            mixed = jnp.dot(wsp_ref[h], vg[rows, cols], preferred_element_type=F32) + bsp_ref[:, cols]
            o_ref[rows, cw + h * hc:cw + (h + 1) * hc] = (u[rows, cols] * mixed).astype(BF16)

    ys[0:HALO, :] = prod(scp_ref) * keep_prev
    ys[HALO:HALO + tm, :] = prod(sc_ref)
    ys[HALO + tm:, :] = prod(scn_ref) * keep_next
    acc = jnp.zeros((tm, cw), F32)
    for tap in range(SC_K):
        acc = acc + ys[pl.ds(HALO - SC_K // 2 + tap, tm), :] * wsc_ref[tap:tap + 1, :]
    o_ref[:, 2 * cw:3 * cw] = (bg_ref[...].astype(F32) * acc).astype(BF16)


def _mixers(proj, weights, tp, seq, dec_seq, tm):
    t = proj.shape[0]
    cw = 512
    npt = tp // tm
    hb = tm // HALO
    last = t // HALO - 1
    const2 = lambda i: (0, 0)

    def main(col):
        return pl.BlockSpec((tm, 2 * cw), lambda i: (i, col))

    def prev(col):
        return pl.BlockSpec((HALO, 2 * cw), lambda i: (jnp.maximum(i * hb - 1, 0), col))

    def nxt(col):
        return pl.BlockSpec((HALO, 2 * cw), lambda i: (jnp.minimum((i + 1) * hb, last), col))

    w_specs = [pl.BlockSpec(w.shape, const2 if w.ndim == 2 else (lambda i: (0, 0, 0))) for w in weights]
    return pl.pallas_call(
        functools.partial(_mixers_kernel, npt, seq // tm, dec_seq // tm),
        grid=(t // tm,),
        in_specs=[main(0), prev(0), nxt(0), main(1), main(2), prev(2), nxt(2),
                  pl.BlockSpec((tm, cw), lambda i: (i, 6))] + w_specs,
        out_specs=pl.BlockSpec((tm, 3 * cw), lambda i: (i, 0)),
        out_shape=jax.ShapeDtypeStruct((t, 3 * cw), BF16),
        scratch_shapes=[pltpu.VMEM((cw // LANE, tm + 2 * HALO, LANE), F32), pltpu.VMEM((tm + 2 * HALO, cw), F32),
                        pltpu.VMEM((7, cw // LANE, tm + 24, LANE), F32), pltpu.VMEM((cw // LANE, tm, LANE), F32)],
        compiler_params=_cparams(("arbitrary",), 40),
        name="mixers",
    )(proj, proj, proj, proj, proj, proj, proj, proj, *weights)


def _out_proj_kernel(npt, n, oap_ref, oas_ref, ob_ref, xp_ref, xs_ref, mod_ref, g_ref, wo_ref, wr_ref,
                     xn_ref, hp_ref, r_ref, mix_a, mix_b):
    i = pl.program_id(0)
    ka = oap_ref.shape[1]
    mm_tile = jnp.minimum(i, n - 1)
    ep_tile = jnp.maximum(i - 1, 0)

    @pl.when(i == 0)
    def _():
        mix_b[...] = jnp.zeros_like(mix_b)

    def step(mix_w, mix_r):
        oa = jnp.where(mm_tile < npt, oap_ref[...], oas_ref[...])
        mix_w[...] = jnp.dot(jnp.concatenate([oa, ob_ref[...]], axis=1), wo_ref[...],
                             preferred_element_type=F32)
        x = jnp.where(ep_tile < npt, xp_ref[...], xs_ref[...])
        xn = x + mod_ref[2:3, :] * mix_r[...]
        xn_ref[...] = xn
        h = xn * _inv_rms(xn, xn.shape[-1]) * g_ref[...]
        h = h * (1.0 + mod_ref[4:5, :]) + mod_ref[3:4, :]
        hp_ref[...] = _pack_rows(h)
        h_hi, h_lo = _split_bf16(h)
        r_ref[...] = (lax.dot_general(wr_ref[...], h_hi, _NT, preferred_element_type=F32)
                      + lax.dot_general(wr_ref[...], h_lo, _NT, preferred_element_type=F32))

    @pl.when(i % 2 == 0)
    def _():
        step(mix_a, mix_b)

    @pl.when(i % 2 == 1)
    def _():
        step(mix_b, mix_a)


def _out_proj(oap, oas, ob, xp, xs, mod_l, g, wo, wr, tm, mod_index):
    tp, d = xp.shape
    t = ob.shape[0]
    npt = tp // tm
    n = t // tm
    const = lambda i: (0, 0)
    mm = lambda i: jnp.minimum(i, n - 1)
    ep = lambda i: jnp.maximum(i - 1, 0)

    def pair(w, tile):
        return [pl.BlockSpec((tm, w), lambda i: (jnp.minimum(tile(i), npt - 1), 0)),
                pl.BlockSpec((tm, w), lambda i: (jnp.maximum(tile(i) - npt, 0), 0))]

    return pl.pallas_call(
        functools.partial(_out_proj_kernel, npt, n),
        grid=(n + 1,),
        in_specs=pair(oap.shape[1], mm) + [pl.BlockSpec((tm, ob.shape[1]), lambda i: (mm(i), 0))] + pair(d, ep) + [
                  pl.BlockSpec((None, 6, d), lambda i: (mod_index(ep(i), tm), 0, 0)),
                  pl.BlockSpec((1, d), const),
                  pl.BlockSpec(wo.shape, const, pipeline_mode=pl.Buffered(1)),
                  pl.BlockSpec(wr.shape, const, pipeline_mode=pl.Buffered(1))],
        out_specs=[pl.BlockSpec((tm, d), lambda i: (ep(i), 0)),
                   pl.BlockSpec((tm, d // 2), lambda i: (ep(i), 0)),
                   pl.BlockSpec((LANE, tm), lambda i: (0, ep(i)))],
        out_shape=[jax.ShapeDtypeStruct((t, d), F32),
                   jax.ShapeDtypeStruct((t, d // 2), U32),
                   jax.ShapeDtypeStruct((LANE, t), F32)],
        scratch_shapes=[pltpu.VMEM((tm, d), F32), pltpu.VMEM((tm, d), F32)],
        compiler_params=_cparams(("arbitrary",), 56),
        name="out_proj",
    )(oap, oas, ob, xp, xs, mod_l, g, wo, wr)


N_BUCKET_PAD = 64


def _route_kernel(rt_ref, b_ref, tri_ref, bucket_ref, rank_ref, gate_ref, cnt_ref, run):
    i = pl.program_id(0)
    tm = rt_ref.shape[1]
    ne = N_GROUPS * GROUP_SIZE

    @pl.when(i == 0)
    def _():
        run[...] = jnp.zeros_like(run)

    sc = jax.nn.sigmoid(rt_ref[0:ne, :] + rt_ref[ne:2 * ne, :])
    sel = sc + b_ref[...]
    s = [sel[r * N_GROUPS:(r + 1) * N_GROUPS, :] for r in range(GROUP_SIZE)]
    c = [sc[r * N_GROUPS:(r + 1) * N_GROUPS, :] for r in range(GROUP_SIZE)]
    hi01, lo01 = jnp.maximum(s[0], s[1]), jnp.minimum(s[0], s[1])
    hi23, lo23 = jnp.maximum(s[2], s[3]), jnp.minimum(s[2], s[3])
    g_score = jnp.maximum(hi01, hi23) + jnp.maximum(jnp.minimum(hi01, hi23), jnp.maximum(lo01, lo23))
    grp = lax.broadcasted_iota(jnp.int32, g_score.shape, 0)
    g_best = jnp.min(jnp.where(g_score == jnp.max(g_score, axis=0, keepdims=True), grp, N_GROUPS),
                     axis=0, keepdims=True)
    own = grp == g_best
    v = [jnp.sum(jnp.where(own, s[r], 0.0), axis=0, keepdims=True) for r in range(GROUP_SIZE)]
    w = [jnp.sum(jnp.where(own, c[r], 0.0), axis=0, keepdims=True) for r in range(GROUP_SIZE)]

    def first_max(vals):
        best, idx, gate = vals[0], jnp.zeros_like(g_best), w[0]
        for r in range(1, GROUP_SIZE):
            upd = vals[r] > best
            best = jnp.where(upd, vals[r], best)
            idx = jnp.where(upd, r, idx)
            gate = jnp.where(upd, w[r], gate)
        return idx, gate

    r1, w1 = first_max(v)
    r2, w2 = first_max([jnp.where(r1 == r, -jnp.inf, v[r]) for r in range(GROUP_SIZE)])
    wsum = w1 + w2
    swap = r1 > r2
    r_lo = jnp.where(swap, r2, r1)
    r_hi = jnp.where(swap, r1, r2)
    gate_ref[0:1, :] = jnp.where(swap, w2, w1) / wsum
    gate_ref[1:2, :] = jnp.where(swap, w1, w2) / wsum
    pair = r_lo * 3 - ((r_lo * (r_lo - 1)) >> 1) + (r_hi - r_lo - 1)
    bucket = g_best * N_PAIRS + pair
    bucket_ref[...] = bucket

    onehot = (lax.broadcasted_iota(jnp.int32, (N_BUCKET_PAD, tm), 0) == bucket).astype(F32)
    before = jnp.dot(onehot.astype(BF16), tri_ref[...], preferred_element_type=F32) + run[...]
    rank_ref[...] = jnp.sum(onehot * before, axis=0, keepdims=True).astype(jnp.int32)
    run[...] += jnp.sum(onehot, axis=1, keepdims=True)

    @pl.when(i == pl.num_programs(0) - 1)
    def _():
        cnt_ref[...] = run[...]


def _route(rt, router_b, n_blk, tm):
    t = rt.shape[1]
    ne = router_b.shape[0]
    b_col = router_b.astype(F32).reshape(N_GROUPS, GROUP_SIZE).T.reshape(ne, 1)
    tri = (jnp.arange(tm)[:, None] < jnp.arange(tm)[None, :]).astype(BF16)
    bucket, rank, gate, cnt = pl.pallas_call(
        _route_kernel,
        grid=(t // tm,),
        in_specs=[pl.BlockSpec((LANE, tm), lambda i: (0, i)),
                  pl.BlockSpec((ne, 1), lambda i: (0, 0)),
                  pl.BlockSpec((tm, tm), lambda i: (0, 0))],
        out_specs=[pl.BlockSpec((1, tm), lambda i: (0, i)),
                   pl.BlockSpec((1, tm), lambda i: (0, i)),
                   pl.BlockSpec((2, tm), lambda i: (0, i)),
                   pl.BlockSpec((N_BUCKET_PAD, 1), lambda i: (0, 0))],
        out_shape=[jax.ShapeDtypeStruct((1, t), jnp.int32),
                   jax.ShapeDtypeStruct((1, t), jnp.int32),
                   jax.ShapeDtypeStruct((2, t), F32),
                   jax.ShapeDtypeStruct((N_BUCKET_PAD, 1), F32)],
        scratch_shapes=[pltpu.VMEM((N_BUCKET_PAD, 1), F32)],
        compiler_params=_cparams(("arbitrary",), 32),
        name="route",
    )(rt, b_col, tri)

    n_bucket = N_GROUPS * N_PAIRS
    counts = cnt[:n_bucket, 0].astype(jnp.int32)
    padded = (counts + MOE_BM - 1) // MOE_BM * MOE_BM
    pad_end = jnp.cumsum(padded)
    pad_start = pad_end - padded
    bucket = bucket[0]
    sel = bucket[:, None] == jnp.arange(n_bucket, dtype=jnp.int32)[None, :]
    pos = jnp.sum(jnp.where(sel, pad_start[None, :], 0), axis=1).astype(jnp.int32) + rank[0]
    n_used = (pad_end[-1] // MOE_BM).astype(jnp.int32)
    blk = jnp.minimum(jnp.arange(n_blk, dtype=jnp.int32), n_used - 1)
    blk_bucket = jnp.sum((pad_end[None, :] <= (blk * MOE_BM)[:, None]).astype(jnp.int32), axis=1)
    blk_bucket = jnp.minimum(blk_bucket, n_bucket - 1)
    pair_lo = jnp.array([0, 0, 0, 1, 1, 2], jnp.int32)
    pair_hi = jnp.array([1, 2, 3, 2, 3, 3], jnp.int32)
    blk_pair = blk_bucket % N_PAIRS
    first = (blk_bucket // N_PAIRS) * GROUP_SIZE
    is_pair = blk_pair[:, None] == jnp.arange(N_PAIRS, dtype=jnp.int32)[None, :]
    blk_a = first + jnp.sum(jnp.where(is_pair, pair_lo[None, :], 0), axis=1)
    blk_b = first + jnp.sum(jnp.where(is_pair, pair_hi[None, :], 0), axis=1)
    is_bucket = blk_bucket[:, None] == jnp.arange(n_bucket, dtype=jnp.int32)[None, :]
    bucket_end = jnp.sum(jnp.where(is_bucket, (pad_start + counts)[None, :], 0), axis=1)
    steps = jnp.arange(n_blk, dtype=jnp.int32)
    blk_rows = jnp.where(steps < n_used, jnp.clip(bucket_end - steps * MOE_BM, 0, MOE_BM), 0).astype(jnp.int32)
    gates = jnp.pad(gate.T, ((0, 0), (0, GATE_W - 2)))
    return pos, gates, blk, blk_a.astype(jnp.int32), blk_b.astype(jnp.int32), blk_rows


def _dispatch_kernel(pos_ref, hp_ref, gt_ref, init_ref, xs_ref, buf, sem):
    del init_ref
    i = pl.program_id(0)
    slot = i % 2
    tm = hp_ref.shape[0]
    w = hp_ref.shape[1]
    buf[slot, :, :w] = hp_ref[...]
    buf[slot, :, w:] = pltpu.bitcast(gt_ref[...], U32)

    def start(k, carry):
        for p in range(N_DMA_PRIORITIES):
            r = k * N_DMA_PRIORITIES + p
            pltpu.make_async_copy(buf.at[slot, pl.ds(r, 1)], xs_ref.at[pl.ds(pos_ref[0, 0, r], 1)],
                                  sem.at[slot]).start(priority=p)
        return carry

    def wait_all(s):
        def wait(r, carry):
            pltpu.make_async_copy(buf.at[s, pl.ds(0, 1)], xs_ref.at[pl.ds(0, 1)], sem.at[s]).wait()
            return carry
        lax.fori_loop(0, tm, wait, 0, unroll=8)

    lax.fori_loop(0, tm // N_DMA_PRIORITIES, start, 0, unroll=4)

    @pl.when(i > 0)
    def _():
        wait_all(1 - slot)

    @pl.when(i == pl.num_programs(0) - 1)
    def _():
        wait_all(slot)


def _dispatch(hp, gates, pos, n_buf, tm):
    t, w = hp.shape
    init = jnp.zeros((n_buf, w + GATE_W), U32)
    return pl.pallas_call(
        _dispatch_kernel,
        grid=(t // tm,),
        in_specs=[pl.BlockSpec((1, 1, tm), lambda i: (i, 0, 0), memory_space=pltpu.SMEM),
                  pl.BlockSpec((tm, w), lambda i: (i, 0)),
                  pl.BlockSpec((tm, GATE_W), lambda i: (i, 0)),
                  pl.BlockSpec(memory_space=pl.ANY)],
        out_specs=pl.BlockSpec(memory_space=pl.ANY),
        out_shape=jax.ShapeDtypeStruct((n_buf, w + GATE_W), U32),
        scratch_shapes=[pltpu.VMEM((2, tm, w + GATE_W), U32), pltpu.SemaphoreType.DMA((2,))],
        input_output_aliases={3: 0},
        compiler_params=_cparams(("arbitrary",), 32),
        name="dispatch",
    )(pos.reshape(t // tm, 1, tm), hp, gates, init)


def _moe_kernel(blk_ref, ea_ref, eb_ref, nv_ref, x_ref, w1a_ref, w3a_ref, w2a_ref, w1b_ref, w3b_ref, w2b_ref,
                y_ref):
    del blk_ref, ea_ref, eb_ref
    nv = nv_ref[pl.program_id(0)]
    bm = x_ref.shape[0]
    w = x_ref.shape[1] - GATE_W

    def experts(m):
        xa, xb = _unpack_rows(x_ref[0:m, :w])
        x = jnp.concatenate([xa.astype(BF16), xb.astype(BF16)], axis=1)
        gates = pltpu.bitcast(x_ref[0:m, w:], F32)

        def hidden(w1_ref, w3_ref, g):
            a = jnp.dot(x, w1_ref[...], preferred_element_type=F32)
            b = jnp.dot(x, w3_ref[...], preferred_element_type=F32)
            return (a * jax.nn.sigmoid(a) * b * g).astype(BF16)

        y = jnp.dot(hidden(w1a_ref, w3a_ref, gates[:, 0:1]), w2a_ref[...], preferred_element_type=F32)
        y += jnp.dot(hidden(w1b_ref, w3b_ref, gates[:, 1:2]), w2b_ref[...], preferred_element_type=F32)
        y_ref[0:m, :] = _pack_rows(y)
        if m < bm:
            y_ref[m:, :] = jnp.zeros((bm - m, y_ref.shape[1]), y_ref.dtype)

    @pl.when(nv > bm // 2)
    def _():
        experts(bm)

    @pl.when(jnp.logical_and(nv > 0, nv <= bm // 2))
    def _():
        experts(bm // 2)

    @pl.when(nv == 0)
    def _():
        y_ref[...] = jnp.zeros_like(y_ref)


def _moe(xs, w1, w3, w2, blk, blk_a, blk_b, blk_rows):
    n_buf, wx = xs.shape
    n_blk = n_buf // MOE_BM
    _, d, de = w1.shape

    def expert(second, rows, cols):
        if second:
            return pl.BlockSpec((None, rows, cols), lambda i, blk, ea, eb, nu: (eb[i], 0, 0))
        return pl.BlockSpec((None, rows, cols), lambda i, blk, ea, eb, nu: (ea[i], 0, 0))

    grid_spec = pltpu.PrefetchScalarGridSpec(
        num_scalar_prefetch=4,
        grid=(n_blk,),
        in_specs=[pl.BlockSpec((MOE_BM, wx), lambda i, blk, ea, eb, nu: (blk[i], 0)),
                  expert(False, d, de), expert(False, d, de), expert(False, de, d),
                  expert(True, d, de), expert(True, d, de), expert(True, de, d)],
        out_specs=pl.BlockSpec((MOE_BM, d // 2), lambda i, blk, ea, eb, nu: (i, 0)),
    )
    return pl.pallas_call(
        _moe_kernel,
        grid_spec=grid_spec,
        out_shape=jax.ShapeDtypeStruct((n_buf, d // 2), U32),
        compiler_params=_cparams(("arbitrary",), 48),
        name="moe",
    )(blk, blk_a, blk_b, blk_rows, xs, w1, w3, w2, w1, w3, w2)


def _combine_kernel(npt, pos_ref, nxt_ref, xn_ref, mod_ref, yb_ref, op_ref, os_ref, buf, sem):
    i = pl.program_id(0)
    slot = i % 2
    tm = xn_ref.shape[0]

    def gather(idx_ref, s):
        def start(k, carry):
            for p in range(N_DMA_PRIORITIES):
                r = k * N_DMA_PRIORITIES + p
                pltpu.make_async_copy(yb_ref.at[pl.ds(idx_ref[0, 0, r], 1)], buf.at[s, pl.ds(r, 1)],
                                      sem.at[s]).start(priority=p)
            return carry
        lax.fori_loop(0, tm // N_DMA_PRIORITIES, start, 0, unroll=4)

    @pl.when(i == 0)
    def _():
        gather(pos_ref, slot)

    @pl.when(i + 1 < pl.num_programs(0))
    def _():
        gather(nxt_ref, 1 - slot)

    def wait(r, carry):
        pltpu.make_async_copy(yb_ref.at[pl.ds(0, 1)], buf.at[slot, pl.ds(0, 1)], sem.at[slot]).wait()
        return carry

    lax.fori_loop(0, tm, wait, 0, unroll=8)
    ya, yb = _unpack_rows(buf[slot])
    out = xn_ref[...] + mod_ref[5:6, :] * jnp.concatenate([ya, yb], axis=1)

    @pl.when(i < npt)
    def _():
        op_ref[...] = out

    @pl.when(i >= npt)
    def _():
        os_ref[...] = out


def _combine(xn, yb, pos, mod_l, tp, tm, mod_index):
    t, d = xn.shape
    npt = tp // tm
    n = t // tm
    pos3 = pos.reshape(n, 1, tm)
    return pl.pallas_call(
        functools.partial(_combine_kernel, npt),
        grid=(n,),
        in_specs=[pl.BlockSpec((1, 1, tm), lambda i: (i, 0, 0), memory_space=pltpu.SMEM),
                  pl.BlockSpec((1, 1, tm), lambda i: (jnp.minimum(i + 1, n - 1), 0, 0), memory_space=pltpu.SMEM),
                  pl.BlockSpec((tm, d), lambda i: (i, 0)),
                  pl.BlockSpec((None, 6, d), lambda i: (mod_index(i, tm), 0, 0)),
                  pl.BlockSpec(memory_space=pl.ANY)],
        out_specs=[pl.BlockSpec((tm, d), lambda i: (jnp.minimum(i, npt - 1), 0)),
                   pl.BlockSpec((tm, d), lambda i: (jnp.maximum(i - npt, 0), 0))],
        out_shape=[jax.ShapeDtypeStruct((tp, d), F32),
                   jax.ShapeDtypeStruct((t - tp, d), F32)],
        scratch_shapes=[pltpu.VMEM((2, tm, d // 2), U32), pltpu.SemaphoreType.DMA((2,))],
        compiler_params=_cparams(("arbitrary",), 32),
        name="combine",
    )(pos3, pos3, xn, mod_l, yb)


def _pad_cols(x, n):
    return jnp.pad(x, ((0, 0), (0, n - x.shape[1])))


def _rope_tables(dec_seq, tm):
    n_freq = MLA_ROPE // 4
    pos = jnp.arange(dec_seq, dtype=jnp.int32)
    row = (pos // GRID_W).astype(F32)
    col = (pos % GRID_W).astype(F32)
    inv = ROPE_THETA ** (-jnp.arange(n_freq, dtype=F32) / n_freq)
    ar = row[:, None] * inv[None, :]
    ac = col[:, None] * inv[None, :]
    ones = jnp.ones((dec_seq, LANE - MLA_ROPE), F32)
    cos = jnp.concatenate([jnp.cos(ar), jnp.cos(ar), jnp.cos(ac), jnp.cos(ac), ones], axis=1)
    sin = jnp.concatenate([-jnp.sin(ar), jnp.sin(ar), -jnp.sin(ac), jnp.sin(ac), 0.0 * ones], axis=1)
    ident_c = jnp.ones((tm, LANE), F32)
    ident_s = jnp.zeros((tm, LANE), F32)
    return jnp.concatenate([cos, ident_c], axis=0), jnp.concatenate([sin, ident_s], axis=0)


def _layer_params(l, q_lora, kv_lora, w_in, g_q_lora, w_uq, g_kv_lora, w_ukv, g_qk_q, g_qk_k,
                  w_dw31, b_dw31, g_conv_ln, b_conv_ln, w_pw, b_pw, g_sgu_ln, b_sgu_ln,
                  w_spatial, b_spatial, w_sc3, w_out):
    d = w_in.shape[1]
    o1 = q_lora
    o2 = o1 + kv_lora
    o3 = o2 + MLA_ROPE
    o4 = o3 + 1024
    o5 = o4 + 1024
    w = w_in[l]
    zeros = lambda n: jnp.zeros((d, n), w.dtype)
    w_in_p = jnp.concatenate(
        [w[:, o3:o4], w[:, o4:o5], w[:, o5 + 512:o5 + 1536], w[:, o5:o5 + 512],
         w[:, :o1], zeros(512 - q_lora),
         w[:, o1:o2], w[:, o2:o3], zeros(PROJ_W - KV_COL - kv_lora - MLA_ROPE)], axis=1).astype(BF16)

    uq = w_uq[l].reshape(q_lora, MLA_HEADS, MLA_QK)
    uq = jnp.pad(uq, ((0, 512 - q_lora), (0, 0), (0, HEAD_PAD - MLA_QK)))
    w_uq_p = uq.reshape(512, MLA_HEADS * HEAD_PAD).astype(BF16)
    ukv = w_ukv[l].reshape(kv_lora, MLA_HEADS, MLA_NOPE + MLA_V)
    w_ukv_p = jnp.concatenate([ukv[:, :, :MLA_NOPE].reshape(kv_lora, -1),
                               ukv[:, :, MLA_NOPE:].reshape(kv_lora, -1)], axis=1).astype(BF16)
    row = lambda v: v.reshape(1, -1).astype(F32)
    qkv_w = (w_uq_p, w_ukv_p, _pad_cols(row(g_q_lora[l]), 512), row(g_kv_lora[l]),
             _pad_cols(row(g_qk_q[l]), HEAD_PAD), _pad_cols(row(g_qk_k[l]), HEAD_PAD))

    hc = w_pw.shape[1] // GMLP_HEADS
    bsp = jnp.repeat(b_spatial[l].T.astype(F32), hc, axis=1)
    w_dw = jnp.pad(w_dw31[l].astype(F32), ((0, 32 - CONV_K), (0, 0)))
    w_dw = w_dw.reshape(32, -1, LANE).transpose(1, 0, 2)
    mix_w = (w_dw, row(b_dw31[l]),
             row(g_conv_ln[l]), row(b_conv_ln[l]), w_pw[l].astype(BF16), row(b_pw[l]),
             row(g_sgu_ln[l]), row(b_sgu_ln[l]), w_spatial[l].astype(BF16), bsp,
             jnp.pad(w_sc3[l].astype(F32), ((0, 8 - SC_K), (0, 0))))
    return w_in_p, qkv_w, mix_w, w_out[l].astype(BF16)


def _tile(limit, *sizes):
    tm = limit
    while any(s % tm for s in sizes):
        tm //= 2
    return tm


def kernel(x_prompt, x_sample, cache_mla_ckv, cache_mla_krope, c, c_ctx, w_mod, b_mod, g_norm1, g_norm2, w_in, g_q_lora, w_uq, g_kv_lora, w_ukv, g_qk_q, g_qk_k, w_dw31, b_dw31, g_conv_ln, b_conv_ln, w_pw, b_pw, g_sgu_ln, b_sgu_ln, w_spatial, b_spatial, w_sc3, w_out, router_w, router_b, w1, w3, w2):
    nseq, seq, d = x_prompt.shape
    nb, dec_seq, _ = x_sample.shape
    depth = w_mod.shape[0]
    past = cache_mla_ckv.shape[2]
    q_lora = g_q_lora.shape[1]
    kv_lora = g_kv_lora.shape[1]
    n_exp = router_w.shape[1]
    tp = nseq * seq
    ts = nb * dec_seq
    t = tp + ts
    assert tp % dec_seq == 0 and seq % CHUNK == 0 and dec_seq % GRID_W == 0
    assert nb + 1 <= 8 and kv_lora == MLA_NOPE and n_exp == N_GROUPS * GROUP_SIZE

    tm_in = _tile(512, tp, dec_seq)
    tm_qkv = _tile(512, tp, dec_seq)
    tm_row = _tile(256, seq, dec_seq)
    tq = _tile(1024, dec_seq)

    def mod_index(i, tm):
        npt = tp // tm
        return jnp.where(i < npt, 0, 1 + (i - npt) // (dec_seq // tm))

    cvec = jnp.concatenate([c_ctx[None, :], c, jnp.zeros((8 - 1 - nb, d), F32)], axis=0)
    mod = _modulation(cvec, w_mod, b_mod).reshape(depth, 8, 6, d)
    cos_t, sin_t = _rope_tables(dec_seq, tm_qkv)

    rw = router_w.T.reshape(N_GROUPS, GROUP_SIZE, d).transpose(1, 0, 2).reshape(n_exp, d)
    rw_hi = rw.astype(BF16)
    rw_lo = (rw - rw_hi.astype(F32)).astype(BF16)
    wr = jnp.pad(jnp.concatenate([rw_hi, rw_lo], axis=0), ((0, LANE - 2 * n_exp), (0, 0)))
    n_blk = (t + N_GROUPS * N_PAIRS * (MOE_BM - 1) + MOE_BM - 1) // MOE_BM

    xp = x_prompt.reshape(tp, d)
    xs = x_sample.reshape(ts, d)
    ckv_states = []
    krope_states = []
    for l in range(depth):
        w_in_p, qkv_w, mix_w, w_out_b = _layer_params(
            l, q_lora, kv_lora, w_in, g_q_lora, w_uq, g_kv_lora, w_ukv, g_qk_q, g_qk_k,
            w_dw31, b_dw31, g_conv_ln, b_conv_ln, w_pw, b_pw, g_sgu_ln, b_sgu_ln,
            w_spatial, b_spatial, w_sc3, w_out)
        mod_l = mod[l]

        proj, kv = _in_proj(xp, xs, mod_l, g_norm1[l].reshape(1, d), w_in_p, tm_in, mod_index)
        q, k, v, state = _qkv(proj, kv, cos_t, sin_t, *qkv_w, tp, dec_seq, q_lora, tm_qkv)
        kvc = jnp.concatenate([cache_mla_ckv[:, l], cache_mla_krope[:, l],
                               jnp.zeros((nb, past, KV_W - kv_lora - MLA_ROPE), F32)], axis=-1)
        kc, vc = _ctx_kv(kvc.reshape(nb * past, KV_W), qkv_w[1], qkv_w[5], _tile(512, past))
        o_as, w1b, w3b, w2b = _attention_latent(q, k, v, kc, vc, w1, w3, w2, l, tp, nb, dec_seq, past, tq)
        o_ap = _attention_context(q, k, v, nseq, seq)
        o_bcd = _mixers(proj, mix_w, tp, seq, dec_seq, tm_row)

        xn, hp, r = _out_proj(o_ap, o_as, o_bcd, xp, xs, mod_l, g_norm2[l].reshape(1, d), w_out_b, wr,
                              tm_qkv, mod_index)
        pos, gates, blk, blk_a, blk_b, blk_rows = _route(r, router_b, n_blk, tm_qkv)
        xd = _dispatch(hp, gates, pos, n_blk * MOE_BM, tm_row)
        yb = _moe(xd, w1b, w3b, w2b, blk, blk_a, blk_b, blk_rows)
        xp, xs = _combine(xn, yb, pos, mod_l, tp, tm_row, mod_index)

        ckv_states.append(state[:, :kv_lora].reshape(nseq, seq, kv_lora))
        krope_states.append(state[:, kv_lora:kv_lora + MLA_ROPE].reshape(nseq, seq, MLA_ROPE))

    return (xp.reshape(nseq, seq, d), xs.reshape(nb, dec_seq, d),
            jnp.stack(ckv_states, axis=1), jnp.stack(krope_states, axis=1))
```

```python
import functools

import jax
import jax.numpy as jnp
from jax import lax
from jax.experimental import pallas as pl
from jax.experimental.pallas import tpu as pltpu

F32 = jnp.float32
BF16 = jnp.bfloat16
U32 = jnp.uint32

MLA_HEADS = 4
MLA_NOPE = 128
MLA_ROPE = 64
MLA_V = 128
MLA_QK = MLA_NOPE + MLA_ROPE
HEAD_PAD = 256
GRID_W = 64
ROPE_THETA = 10000.0
CONV_K = 31
SC_K = 3
CHUNK = 128
GMLP_HEADS = 4
N_GROUPS = 8
GROUP_SIZE = 4
N_PAIRS = 6
NORM_EPS = 1e-6

LANE = 128
HALO = 16
PROJ_W = 4352
CQ_COL = 3584
KV_COL = 4096
KV_W = 256
IN_TN = 1536
MOE_BM = 256
N_DMA_PRIORITIES = 2
ATTN_KCHUNK = 1024
LOG2_E = 1.4426950408889634
GATE_W = 128
VMEM_CAP = 56 * 1024 * 1024


def _cparams(sem, vmem_mb):
    return pltpu.CompilerParams(dimension_semantics=sem,
                                vmem_limit_bytes=min(vmem_mb * 1024 * 1024, VMEM_CAP))


def _inv_rms(x, n):
    return lax.rsqrt(jnp.sum(x * x, axis=-1, keepdims=True) * (1.0 / n) + NORM_EPS)


def _layernorm(x, g, b):
    mu = jnp.mean(x, axis=-1, keepdims=True)
    xc = x - mu
    var = jnp.mean(xc * xc, axis=-1, keepdims=True)
    return xc * lax.rsqrt(var + NORM_EPS) * g + b


def _split_bf16(x):
    hi = x.astype(BF16)
    lo = (x - hi.astype(F32)).astype(BF16)
    return hi, lo


def _pack_rows(x):
    n = x.shape[1] // 2
    hi = pltpu.bitcast(x[:, :n].astype(BF16).astype(F32), U32)
    lo = pltpu.bitcast(x[:, n:].astype(BF16).astype(F32), U32)
    return (hi & jnp.uint32(0xFFFF0000)) | (lo >> 16)


def _unpack_rows(u):
    a = pltpu.bitcast(u & jnp.uint32(0xFFFF0000), F32)
    b = pltpu.bitcast(u << 16, F32)
    return a, b


def _mod_kernel(c_ref, w_ref, b_ref, o_ref):
    c = c_ref[...]
    a_hi, a_lo = _split_bf16(c * jax.nn.sigmoid(c))
    w_hi, w_lo = _split_bf16(w_ref[...])
    acc = jnp.dot(a_hi, w_hi, preferred_element_type=F32)
    acc += jnp.dot(a_lo, w_hi, preferred_element_type=F32)
    acc += jnp.dot(a_hi, w_lo, preferred_element_type=F32)
    o_ref[...] = acc + b_ref[...]


def _modulation(cvec, w_mod, b_mod):
    depth, d, n = w_mod.shape
    tn = 1024
    return pl.pallas_call(
        _mod_kernel,
        grid=(depth, n // tn),
        in_specs=[pl.BlockSpec((8, d), lambda l, j: (0, 0)),
                  pl.BlockSpec((None, d, tn), lambda l, j: (l, 0, j)),
                  pl.BlockSpec((None, 1, tn), lambda l, j: (l, 0, j))],
        out_specs=pl.BlockSpec((None, 8, tn), lambda l, j: (l, 0, j)),
        out_shape=jax.ShapeDtypeStruct((depth, 8, n), F32),
        compiler_params=_cparams(("arbitrary", "arbitrary"), 40),
        name="modulation",
    )(cvec, w_mod, b_mod.reshape(depth, 1, n))


def _norm_project(x, mod_ref, g_ref, w_ref, proj_ref, kv_ref):
    y = x * _inv_rms(x, x.shape[-1]) * g_ref[...]
    h = (y * (1.0 + mod_ref[1:2, :]) + mod_ref[0:1, :]).astype(BF16)
    for c0 in range(0, PROJ_W, IN_TN):
        c1 = min(c0 + IN_TN, PROJ_W)
        acc = jnp.dot(h, w_ref[:, c0:c1], preferred_element_type=F32)
        proj_ref[:, c0:c1] = acc.astype(BF16)
        if c0 <= KV_COL < c1:
            kv_ref[...] = acc[:, KV_COL - c0:KV_COL - c0 + KV_W]


def _in_proj_kernel(npt, xp_ref, xs_ref, mod_ref, g_ref, w_ref, proj_ref, kv_ref):
    x = jnp.where(pl.program_id(0) < npt, xp_ref[...], xs_ref[...])
    _norm_project(x, mod_ref, g_ref, w_ref, proj_ref, kv_ref)


def _gathered_rows(pos_ref, nxt_ref, yb_ref, buf, sem):
    i = pl.program_id(0)
    slot = i % 2
    tm = buf.shape[1]

    def gather(idx_ref, s):
        for r in range(tm):
            pltpu.make_async_copy(yb_ref.at[pl.ds(idx_ref[0, 0, r], 1)], buf.at[s, pl.ds(r, 1)],
                                  sem.at[s]).start(priority=r % N_DMA_PRIORITIES)

    @pl.when(i == 0)
    def _():
        gather(pos_ref, slot)

    @pl.when(i + 1 < pl.num_programs(0))
    def _():
        gather(nxt_ref, 1 - slot)

    pltpu.make_async_copy(yb_ref.at[pl.ds(0, tm)], buf.at[slot], sem.at[slot]).wait()
    ya, yb = _unpack_rows(buf[slot])
    return jnp.concatenate([ya, yb], axis=1)


def _in_proj_fused_kernel(pos_ref, nxt_ref, xn_ref, prev_mod_ref, mod_ref, g_ref, w_ref, yb_ref,
                          proj_ref, kv_ref, x_ref, buf, sem):
    x = xn_ref[...] + prev_mod_ref[5:6, :] * _gathered_rows(pos_ref, nxt_ref, yb_ref, buf, sem)
    x_ref[...] = x
    _norm_project(x, mod_ref, g_ref, w_ref, proj_ref, kv_ref)


def _in_proj_fused(xn, yb, pos, prev_mod, mod_l, g, w, tm, mod_index):
    t, d = xn.shape
    n = t // tm
    pos3 = pos.reshape(n, 1, tm)
    mod_spec = pl.BlockSpec((None, 6, d), lambda i: (mod_index(i, tm), 0, 0))
    return pl.pallas_call(
        _in_proj_fused_kernel,
        grid=(n,),
        in_specs=[pl.BlockSpec((1, 1, tm), lambda i: (i, 0, 0), memory_space=pltpu.SMEM),
                  pl.BlockSpec((1, 1, tm), lambda i: (jnp.minimum(i + 1, n - 1), 0, 0), memory_space=pltpu.SMEM),
                  pl.BlockSpec((tm, d), lambda i: (i, 0)),
                  mod_spec, mod_spec,
                  pl.BlockSpec((1, d), lambda i: (0, 0)),
                  pl.BlockSpec(w.shape, lambda i: (0, 0), pipeline_mode=pl.Buffered(1)),
                  pl.BlockSpec(memory_space=pl.ANY)],
        out_specs=[pl.BlockSpec((tm, PROJ_W), lambda i: (i, 0)),
                   pl.BlockSpec((tm, KV_W), lambda i: (i, 0)),
                   pl.BlockSpec((tm, d), lambda i: (i, 0))],
        out_shape=[jax.ShapeDtypeStruct((t, PROJ_W), BF16),
                   jax.ShapeDtypeStruct((t, KV_W), F32),
                   jax.ShapeDtypeStruct((t, d), F32)],
        scratch_shapes=[pltpu.VMEM((2, tm, d // 2), U32), pltpu.SemaphoreType.DMA((2,))],
        compiler_params=_cparams(("arbitrary",), 56),
        name="in_proj_fused",
    )(pos3, pos3, xn, prev_mod, mod_l, g, w, yb)


def _in_proj(xp, xs, mod_l, g, w, tm, mod_index):
    tp, d = xp.shape
    t = tp + xs.shape[0]
    npt = tp // tm
    return pl.pallas_call(
        functools.partial(_in_proj_kernel, npt),
        grid=(t // tm,),
        in_specs=[pl.BlockSpec((tm, d), lambda i: (jnp.minimum(i, npt - 1), 0)),
                  pl.BlockSpec((tm, d), lambda i: (jnp.maximum(i - npt, 0), 0)),
                  pl.BlockSpec((None, 6, d), lambda i: (mod_index(i, tm), 0, 0)),
                  pl.BlockSpec((1, d), lambda i: (0, 0)),
                  pl.BlockSpec(w.shape, lambda i: (0, 0), pipeline_mode=pl.Buffered(1))],
        out_specs=[pl.BlockSpec((tm, PROJ_W), lambda i: (i, 0)),
                   pl.BlockSpec((tm, KV_W), lambda i: (i, 0))],
        out_shape=[jax.ShapeDtypeStruct((t, PROJ_W), BF16),
                   jax.ShapeDtypeStruct((t, KV_W), F32)],
        compiler_params=_cparams(("arbitrary",), 56),
        name="in_proj",
    )(xp, xs, mod_l, g, w)


def _rope_tile(t, cos, sin):
    lane = lax.broadcasted_iota(jnp.int32, t.shape, 1)
    first = (lane & 16) == 0
    swapped = jnp.where(first, pltpu.roll(t, LANE - 16, 1), pltpu.roll(t, 16, 1))
    return t * cos + swapped * sin


def _keys_values(ckv_n, krope, w_ukv_ref, gk_ref, cos, sin, k_ref, v_ref):
    kvf = jnp.dot(ckv_n.astype(BF16), w_ukv_ref[...], preferred_element_type=F32)
    kr_ss = jnp.sum(krope * krope, axis=-1, keepdims=True)
    gk = gk_ref[...]
    tail = krope * gk[:, MLA_NOPE:]
    if cos is not None:
        tail = _rope_tile(tail, cos, sin)
    for h in range(MLA_HEADS):
        kn = kvf[:, h * MLA_NOPE:(h + 1) * MLA_NOPE]
        r = lax.rsqrt((jnp.sum(kn * kn, axis=-1, keepdims=True) + kr_ss) * (1.0 / MLA_QK) + NORM_EPS)
        k_ref[:, h * HEAD_PAD:h * HEAD_PAD + MLA_NOPE] = (kn * r * gk[:, :MLA_NOPE]).astype(BF16)
        k_ref[:, h * HEAD_PAD + MLA_NOPE:(h + 1) * HEAD_PAD] = (tail * r).astype(BF16)
    v_ref[...] = kvf[:, MLA_HEADS * MLA_NOPE:].astype(BF16)


def _qkv_kernel(npt, q_lora, cq_ref, kv_ref, cos_ref, sin_ref, w_uq_ref, w_ukv_ref, gql_ref, gkv_ref,
                gq_ref, gk_ref, q_ref, k_ref, v_ref, st_ref):
    i = pl.program_id(0)
    cos = cos_ref[...]
    sin = sin_ref[...]
    cq = cq_ref[...].astype(F32)
    cqn = cq * _inv_rms(cq, q_lora) * gql_ref[...]
    qf = jnp.dot(cqn.astype(BF16), w_uq_ref[...], preferred_element_type=F32)
    gq = gq_ref[...]
    scale = MLA_QK ** -0.5 * LOG2_E
    for h in range(MLA_HEADS):
        qh = qf[:, h * HEAD_PAD:(h + 1) * HEAD_PAD]
        qh = qh * (_inv_rms(qh, MLA_QK) * scale) * gq
        q_ref[:, h * HEAD_PAD:h * HEAD_PAD + MLA_NOPE] = qh[:, :MLA_NOPE].astype(BF16)
        q_ref[:, h * HEAD_PAD + MLA_NOPE:(h + 1) * HEAD_PAD] = _rope_tile(qh[:, MLA_NOPE:], cos, sin).astype(BF16)

    kv = kv_ref[...]
    ckv = kv[:, :MLA_NOPE]
    krope = kv[:, MLA_NOPE:]
    ckv_n = ckv * _inv_rms(ckv, ckv.shape[-1]) * gkv_ref[...]
    _keys_values(ckv_n, krope, w_ukv_ref, gk_ref, cos, sin, k_ref, v_ref)

    @pl.when(i < npt)
    def _():
        st_ref[:, :MLA_NOPE] = ckv_n
        st_ref[:, MLA_NOPE:] = krope


def _qkv(proj, kv, cos_t, sin_t, w_uq, w_ukv, gql, gkv, gq, gk, tp, dec_seq, q_lora, tm):
    t = proj.shape[0]
    npt = tp // tm
    nseq = dec_seq // tm
    cq_blk = CQ_COL // 512

    def tab(i):
        return (jnp.where(i < npt, nseq, (i - npt) % nseq), 0)

    const = lambda i: (0, 0)
    return pl.pallas_call(
        functools.partial(_qkv_kernel, npt, q_lora),
        grid=(t // tm,),
        in_specs=[pl.BlockSpec((tm, 512), lambda i: (i, cq_blk)),
                  pl.BlockSpec((tm, KV_W), lambda i: (i, 0)),
                  pl.BlockSpec((tm, LANE), tab),
                  pl.BlockSpec((tm, LANE), tab),
                  pl.BlockSpec(w_uq.shape, const),
                  pl.BlockSpec(w_ukv.shape, const),
                  pl.BlockSpec(gql.shape, const),
                  pl.BlockSpec(gkv.shape, const),
                  pl.BlockSpec(gq.shape, const),
                  pl.BlockSpec(gk.shape, const)],
        out_specs=[pl.BlockSpec((tm, MLA_HEADS * HEAD_PAD), lambda i: (i, 0)),
                   pl.BlockSpec((tm, MLA_HEADS * HEAD_PAD), lambda i: (i, 0)),
                   pl.BlockSpec((tm, MLA_HEADS * MLA_V), lambda i: (i, 0)),
                   pl.BlockSpec((tm, KV_W), lambda i: (jnp.minimum(i, npt - 1), 0))],
        out_shape=[jax.ShapeDtypeStruct((t, MLA_HEADS * HEAD_PAD), BF16),
                   jax.ShapeDtypeStruct((t, MLA_HEADS * HEAD_PAD), BF16),
                   jax.ShapeDtypeStruct((t, MLA_HEADS * MLA_V), BF16),
                   jax.ShapeDtypeStruct((tp, KV_W), F32)],
        compiler_params=_cparams(("arbitrary",), 40),
        name="qkv",
    )(proj, kv, cos_t, sin_t, w_uq, w_ukv, gql, gkv, gq, gk)


def _ctx_kv_kernel(kv_ref, w_ukv_ref, gk_ref, k_ref, v_ref):
    kv = kv_ref[...]
    _keys_values(kv[:, :MLA_NOPE], kv[:, MLA_NOPE:], w_ukv_ref, gk_ref, None, None, k_ref, v_ref)


def _ctx_kv(kvc, w_ukv, gk, tm):
    r = kvc.shape[0]
    const = lambda i: (0, 0)
    return pl.pallas_call(
        _ctx_kv_kernel,
        grid=(r // tm,),
        in_specs=[pl.BlockSpec((tm, KV_W), lambda i: (i, 0)),
                  pl.BlockSpec(w_ukv.shape, const),
                  pl.BlockSpec(gk.shape, const)],
        out_specs=[pl.BlockSpec((tm, MLA_HEADS * HEAD_PAD), lambda i: (i, 0)),
                   pl.BlockSpec((tm, MLA_HEADS * MLA_V), lambda i: (i, 0))],
        out_shape=[jax.ShapeDtypeStruct((r, MLA_HEADS * HEAD_PAD), BF16),
                   jax.ShapeDtypeStruct((r, MLA_HEADS * MLA_V), BF16)],
        compiler_params=_cparams(("arbitrary",), 32),
        name="ctx_kv",
    )(kvc, w_ukv, gk)


_NT = (((1,), (1,)), ((), ()))


def _attn_kernel(has_ctx, *refs):
    if has_ctx:
        q_ref, kc_ref, vc_ref, k_ref, v_ref, w1_ref, w3_ref, w2_ref, o_ref, w1o_ref, w3o_ref, w2o_ref = refs
        w1o_ref[...] = w1_ref[...].astype(BF16)
        w3o_ref[...] = w3_ref[...].astype(BF16)
        w2o_ref[...] = w2_ref[...].astype(BF16)
    else:
        q_ref, k_ref, v_ref, o_ref = refs
    n_own = k_ref.shape[0]
    step = min(n_own, ATTN_KCHUNK)
    chunks = [(k_ref, v_ref, c * step, step) for c in range(n_own // step)]
    if has_ctx:
        chunks = [(kc_ref, vc_ref, 0, kc_ref.shape[0])] + chunks
    for h in range(q_ref.shape[1] // HEAD_PAD):
        qk = slice(h * HEAD_PAD, (h + 1) * HEAD_PAD)
        hv = slice(h * MLA_V, (h + 1) * MLA_V)
        q = q_ref[:, qk]
        m = l = acc = None
        for kr, vr, off, n in chunks:
            s = lax.dot_general(q, kr[off:off + n, qk], _NT, preferred_element_type=F32)
            mc = jnp.max(s, axis=-1, keepdims=True)
            m_new = mc if m is None else jnp.maximum(m, mc)
            p = jnp.exp2(s - m_new)
            pv = jnp.dot(p.astype(BF16), vr[off:off + n, hv], preferred_element_type=F32)
            ps = jnp.sum(p, axis=-1, keepdims=True)
            if m is None:
                l, acc = ps, pv
            else:
                alpha = jnp.exp2(m - m_new)
                l = alpha * l + ps
                acc = alpha * acc + pv
            m = m_new
        o_ref[:, hv] = (acc / l).astype(BF16)


def _attention_latent(q, k, v, kc, vc, w1, w3, w2, layer, tp, nb, dec_seq, past, tq):
    t = q.shape[0]
    nq = dec_seq // tq
    row0 = tp // tq
    kblk0 = tp // dec_seq
    steps = nb * MLA_HEADS * nq
    depth, n_exp, d, de = w1.shape
    flat = [w1.reshape(depth, n_exp * d, de), w3.reshape(depth, n_exp * d, de), w2.reshape(depth, n_exp * de, d)]
    assert all(w.shape[1] % (8 * steps) == 0 for w in flat)
    step = lambda b, h, i: (b * MLA_HEADS + h) * nq + i
    w_in = [pl.BlockSpec((None, w.shape[1] // steps, w.shape[2]), lambda b, h, i: (layer, step(b, h, i), 0))
            for w in flat]
    w_out = [pl.BlockSpec((w.shape[1] // steps, w.shape[2]), lambda b, h, i: (step(b, h, i), 0)) for w in flat]
    o, w1b, w3b, w2b = pl.pallas_call(
        functools.partial(_attn_kernel, True),
        grid=(nb, MLA_HEADS, nq),
        in_specs=[pl.BlockSpec((tq, HEAD_PAD), lambda b, h, i: (row0 + b * nq + i, h)),
                  pl.BlockSpec((past, HEAD_PAD), lambda b, h, i: (b, h)),
                  pl.BlockSpec((past, MLA_V), lambda b, h, i: (b, h)),
                  pl.BlockSpec((dec_seq, HEAD_PAD), lambda b, h, i: (kblk0 + b, h)),
                  pl.BlockSpec((dec_seq, MLA_V), lambda b, h, i: (kblk0 + b, h))] + w_in,
        out_specs=[pl.BlockSpec((tq, MLA_V), lambda b, h, i: (b * nq + i, h))] + w_out,
        out_shape=[jax.ShapeDtypeStruct((t - tp, MLA_HEADS * MLA_V), BF16)]
                  + [jax.ShapeDtypeStruct(w.shape[1:], BF16) for w in flat],
        compiler_params=_cparams(("arbitrary", "arbitrary", "arbitrary"), 56),
        name="attn_latent",
    )(q, kc, vc, k, v, *flat)
    return o, w1b.reshape(n_exp, d, de), w3b.reshape(n_exp, d, de), w2b.reshape(n_exp, de, d)


def _attention_context(q, k, v, nseq, seq):
    return pl.pallas_call(
        functools.partial(_attn_kernel, False),
        grid=(nseq,),
        in_specs=[pl.BlockSpec((seq, MLA_HEADS * HEAD_PAD), lambda b: (b, 0)),
                  pl.BlockSpec((seq, MLA_HEADS * HEAD_PAD), lambda b: (b, 0)),
                  pl.BlockSpec((seq, MLA_HEADS * MLA_V), lambda b: (b, 0))],
        out_specs=pl.BlockSpec((seq, MLA_HEADS * MLA_V), lambda b: (b, 0)),
        out_shape=jax.ShapeDtypeStruct((nseq * seq, MLA_HEADS * MLA_V), BF16),
        compiler_params=_cparams(("arbitrary",), 32),
        name="attn_context",
    )(q, k, v)


def _mixers_kernel(npt, p_tiles, s_tiles,
                   cv_ref, cvp_ref, cvn_ref, gm_ref, sc_ref, scp_ref, scn_ref, bg_ref,
                   wdw_ref, bdw_ref, gcl_ref, bcl_ref, wpw_ref, bpw_ref,
                   gsl_ref, bsl_ref, wsp_ref, bsp_ref, wsc_ref,
                   o_ref, zs, ys, zsh, cb):
    i = pl.program_id(0)
    tm = cv_ref.shape[0]
    cw = cv_ref.shape[1] // 2
    pos = jnp.where(i < npt, i % p_tiles, (i - npt) % s_tiles)
    n_tiles = jnp.where(i < npt, p_tiles, s_tiles)
    keep_prev = (pos > 0).astype(F32)
    keep_next = (pos < n_tiles - 1).astype(F32)

    def glu(ref):
        a = ref[:, :cw].astype(F32)
        g = ref[:, cw:].astype(F32)
        return a * jax.nn.sigmoid(g)

    def prod(ref):
        return ref[:, :cw].astype(F32) * ref[:, cw:].astype(F32)

    n_lane = cw // LANE
    z_prev, z_main, z_next = glu(cvp_ref) * keep_prev, glu(cv_ref), glu(cvn_ref) * keep_next
    for c in range(n_lane):
        cs = slice(c * LANE, (c + 1) * LANE)
        zs[c, 0:HALO, :] = z_prev[:, cs]
        zs[c, HALO:HALO + tm, :] = z_main[:, cs]
        zs[c, HALO + tm:, :] = z_next[:, cs]
        for r in range(1, 8):
            zsh[r - 1, c] = zs[c, pl.ds(r, tm + 24), :]
    pad = CONV_K // 2
    half = tm // 2

    def conv_chunk(idx, carry):
        c = idx // 2
        base = pl.multiple_of((idx % 2) * half, half)
        acc = jnp.zeros((half, LANE), F32)
        for r in range(8):
            for a in range(4):
                tap = 8 * a + r - (HALO - pad)
                if 0 <= tap < CONV_K:
                    rows = pl.ds(base + 8 * a, half)
                    src = zs[c, rows, :] if r == 0 else zsh[r - 1, c, rows, :]
                    acc = acc + src * wdw_ref[c, tap:tap + 1, :]
        cb[c, pl.ds(base, half), :] = acc
        return carry

    lax.fori_loop(0, 2 * n_lane, conv_chunk, 0)
    z = jnp.concatenate([cb[c] for c in range(n_lane)], axis=1) + bdw_ref[...]
    z = _layernorm(z, gcl_ref[...], bcl_ref[...])
    z = z * jax.nn.sigmoid(z)
    o_b = jnp.dot(z.astype(BF16), wpw_ref[...], preferred_element_type=F32) + bpw_ref[...]
    o_ref[:, 0:cw] = o_b.astype(BF16)

    u = gm_ref[:, :cw].astype(F32)
    vg = _layernorm(gm_ref[:, cw:].astype(F32), gsl_ref[...], bsl_ref[...]).astype(BF16)
    hc = cw // GMLP_HEADS
    for n in range(tm // CHUNK):
        rows = slice(n * CHUNK, (n + 1) * CHUNK)
        for h in range(GMLP_HEADS):
            cols = slice(h * hc, (h + 1) * hc)
            mixed = jnp.dot(wsp_ref[h], vg[rows, cols], preferred_element_type=F32) + bsp_ref[:, cols]
            o_ref[rows, cw + h * hc:cw + (h + 1) * hc] = (u[rows, cols] * mixed).astype(BF16)

    ys[0:HALO, :] = prod(scp_ref) * keep_prev
    ys[HALO:HALO + tm, :] = prod(sc_ref)
    ys[HALO + tm:, :] = prod(scn_ref) * keep_next
    acc = jnp.zeros((tm, cw), F32)
    for tap in range(SC_K):
        acc = acc + ys[pl.ds(HALO - SC_K // 2 + tap, tm), :] * wsc_ref[tap:tap + 1, :]
    o_ref[:, 2 * cw:3 * cw] = (bg_ref[...].astype(F32) * acc).astype(BF16)


def _mixers(proj, weights, tp, seq, dec_seq, tm):
    t = proj.shape[0]
    cw = 512
    npt = tp // tm
    hb = tm // HALO
    last = t // HALO - 1
    const2 = lambda i: (0, 0)

    def main(col):
        return pl.BlockSpec((tm, 2 * cw), lambda i: (i, col))

    def prev(col):
        return pl.BlockSpec((HALO, 2 * cw), lambda i: (jnp.maximum(i * hb - 1, 0), col))

    def nxt(col):
        return pl.BlockSpec((HALO, 2 * cw), lambda i: (jnp.minimum((i + 1) * hb, last), col))

    w_specs = [pl.BlockSpec(w.shape, const2 if w.ndim == 2 else (lambda i: (0, 0, 0))) for w in weights]
    return pl.pallas_call(
        functools.partial(_mixers_kernel, npt, seq // tm, dec_seq // tm),
        grid=(t // tm,),
        in_specs=[main(0), prev(0), nxt(0), main(1), main(2), prev(2), nxt(2),
                  pl.BlockSpec((tm, cw), lambda i: (i, 6))] + w_specs,
        out_specs=pl.BlockSpec((tm, 3 * cw), lambda i: (i, 0)),
        out_shape=jax.ShapeDtypeStruct((t, 3 * cw), BF16),
        scratch_shapes=[pltpu.VMEM((cw // LANE, tm + 2 * HALO, LANE), F32), pltpu.VMEM((tm + 2 * HALO, cw), F32),
                        pltpu.VMEM((7, cw // LANE, tm + 24, LANE), F32), pltpu.VMEM((cw // LANE, tm, LANE), F32)],
        compiler_params=_cparams(("arbitrary",), 40),
        name="mixers",
    )(proj, proj, proj, proj, proj, proj, proj, proj, *weights)


def _out_proj_kernel(npt, n, oap_ref, oas_ref, ob_ref, xp_ref, xs_ref, mod_ref, g_ref, wo_ref, wr_ref,
                     xn_ref, hp_ref, r_ref, mix_a, mix_b):
    i = pl.program_id(0)
    ka = oap_ref.shape[1]
    mm_tile = jnp.minimum(i, n - 1)
    ep_tile = jnp.maximum(i - 1, 0)

    @pl.when(i == 0)
    def _():
        mix_b[...] = jnp.zeros_like(mix_b)

    def step(mix_w, mix_r):
        oa = jnp.where(mm_tile < npt, oap_ref[...], oas_ref[...])
        mix_w[...] = jnp.dot(jnp.concatenate([oa, ob_ref[...]], axis=1), wo_ref[...],
                             preferred_element_type=F32)
        x = jnp.where(ep_tile < npt, xp_ref[...], xs_ref[...])
        xn = x + mod_ref[2:3, :] * mix_r[...]
        xn_ref[...] = xn
        h = xn * _inv_rms(xn, xn.shape[-1]) * g_ref[...]
        h = h * (1.0 + mod_ref[4:5, :]) + mod_ref[3:4, :]
        hp_ref[...] = _pack_rows(h)
        h_hi, h_lo = _split_bf16(h)
        r_ref[...] = (lax.dot_general(wr_ref[...], h_hi, _NT, preferred_element_type=F32)
                      + lax.dot_general(wr_ref[...], h_lo, _NT, preferred_element_type=F32))

    @pl.when(i % 2 == 0)
    def _():
        step(mix_a, mix_b)

    @pl.when(i % 2 == 1)
    def _():
        step(mix_b, mix_a)


def _out_proj(oap, oas, ob, xp, xs, mod_l, g, wo, wr, tm, mod_index):
    t = ob.shape[0]
    d = xp.shape[1]
    npt = oap.shape[0] // tm
    n = t // tm
    const = lambda i: (0, 0)
    mm = lambda i: jnp.minimum(i, n - 1)
    ep = lambda i: jnp.maximum(i - 1, 0)

    def pair(w, tile, stacked=False):
        second = (lambda i: (jnp.maximum(tile(i), npt), 0)) if stacked else (
            lambda i: (jnp.maximum(tile(i) - npt, 0), 0))
        return [pl.BlockSpec((tm, w), lambda i: (jnp.minimum(tile(i), npt - 1), 0)), pl.BlockSpec((tm, w), second)]

    x_specs = pair(d, ep, stacked=xs is None)
    if xs is None:
        xs = xp
    return pl.pallas_call(
        functools.partial(_out_proj_kernel, npt, n),
        grid=(n + 1,),
        in_specs=pair(oap.shape[1], mm) + [pl.BlockSpec((tm, ob.shape[1]), lambda i: (mm(i), 0))] + x_specs + [
                  pl.BlockSpec((None, 6, d), lambda i: (mod_index(ep(i), tm), 0, 0)),
                  pl.BlockSpec((1, d), const),
                  pl.BlockSpec(wo.shape, const, pipeline_mode=pl.Buffered(1)),
                  pl.BlockSpec(wr.shape, const, pipeline_mode=pl.Buffered(1))],
        out_specs=[pl.BlockSpec((tm, d), lambda i: (ep(i), 0)),
                   pl.BlockSpec((tm, d // 2), lambda i: (ep(i), 0)),
                   pl.BlockSpec((LANE, tm), lambda i: (0, ep(i)))],
        out_shape=[jax.ShapeDtypeStruct((t, d), F32),
                   jax.ShapeDtypeStruct((t, d // 2), U32),
                   jax.ShapeDtypeStruct((LANE, t), F32)],
        scratch_shapes=[pltpu.VMEM((tm, d), F32), pltpu.VMEM((tm, d), F32)],
        compiler_params=_cparams(("arbitrary",), 56),
        name="out_proj",
    )(oap, oas, ob, xp, xs, mod_l, g, wo, wr)


N_BUCKET_PAD = 64
PAIR_FIRST = (0, 0, 0, 1, 1, 3)
PAIR_SECOND = (1, 2, 3, 3, 2, 2)


def _route_kernel(rt_ref, b_ref, tri_ref, bucket_ref, rank_ref, gate_ref, cnt_ref, run):
    i = pl.program_id(0)
    tm = rt_ref.shape[1]
    ne = N_GROUPS * GROUP_SIZE

    @pl.when(i == 0)
    def _():
        run[...] = jnp.zeros_like(run)

    sc = jax.nn.sigmoid(rt_ref[0:ne, :] + rt_ref[ne:2 * ne, :])
    sel = sc + b_ref[...]
    s = [sel[r * N_GROUPS:(r + 1) * N_GROUPS, :] for r in range(GROUP_SIZE)]
    c = [sc[r * N_GROUPS:(r + 1) * N_GROUPS, :] for r in range(GROUP_SIZE)]
    hi01, lo01 = jnp.maximum(s[0], s[1]), jnp.minimum(s[0], s[1])
    hi23, lo23 = jnp.maximum(s[2], s[3]), jnp.minimum(s[2], s[3])
    g_score = jnp.maximum(hi01, hi23) + jnp.maximum(jnp.minimum(hi01, hi23), jnp.maximum(lo01, lo23))
    grp = lax.broadcasted_iota(jnp.int32, g_score.shape, 0)
    g_best = jnp.min(jnp.where(g_score == jnp.max(g_score, axis=0, keepdims=True), grp, N_GROUPS),
                     axis=0, keepdims=True)
    own = grp == g_best
    v = [jnp.sum(jnp.where(own, s[r], 0.0), axis=0, keepdims=True) for r in range(GROUP_SIZE)]
    w = [jnp.sum(jnp.where(own, c[r], 0.0), axis=0, keepdims=True) for r in range(GROUP_SIZE)]

    def first_max(vals):
        best, idx, gate = vals[0], jnp.zeros_like(g_best), w[0]
        for r in range(1, GROUP_SIZE):
            upd = vals[r] > best
            best = jnp.where(upd, vals[r], best)
            idx = jnp.where(upd, r, idx)
            gate = jnp.where(upd, w[r], gate)
        return idx, gate

    r1, w1 = first_max(v)
    r2, w2 = first_max([jnp.where(r1 == r, -jnp.inf, v[r]) for r in range(GROUP_SIZE)])
    wsum = w1 + w2
    swap = r1 > r2
    r_lo = jnp.where(swap, r2, r1)
    r_hi = jnp.where(swap, r1, r2)
    g_lo = jnp.where(swap, w2, w1) / wsum
    g_hi = jnp.where(swap, w1, w2) / wsum
    lex = r_lo * 3 - ((r_lo * (r_lo - 1)) >> 1) + (r_hi - r_lo - 1)
    pair = jnp.where(lex == 3, 4, jnp.where(lex == 4, 3, lex))
    hi_first = pair == N_PAIRS - 1
    gate_ref[0:1, :] = jnp.where(hi_first, g_hi, g_lo)
    gate_ref[1:2, :] = jnp.where(hi_first, g_lo, g_hi)
    bucket = g_best * N_PAIRS + pair
    bucket_ref[...] = bucket

    onehot = (lax.broadcasted_iota(jnp.int32, (N_BUCKET_PAD, tm), 0) == bucket).astype(F32)
    before = jnp.dot(onehot.astype(BF16), tri_ref[...], preferred_element_type=F32) + run[...]
    rank_ref[...] = jnp.sum(onehot * before, axis=0, keepdims=True).astype(jnp.int32)
    run[...] += jnp.sum(onehot, axis=1, keepdims=True)

    @pl.when(i == pl.num_programs(0) - 1)
    def _():
        cnt_ref[...] = run[...]


def _route(rt, router_b, n_blk, tm):
    t = rt.shape[1]
    ne = router_b.shape[0]
    b_col = router_b.astype(F32).reshape(N_GROUPS, GROUP_SIZE).T.reshape(ne, 1)
    tri = (jnp.arange(tm)[:, None] < jnp.arange(tm)[None, :]).astype(BF16)
    bucket, rank, gate, cnt = pl.pallas_call(
        _route_kernel,
        grid=(t // tm,),
        in_specs=[pl.BlockSpec((LANE, tm), lambda i: (0, i)),
                  pl.BlockSpec((ne, 1), lambda i: (0, 0)),
                  pl.BlockSpec((tm, tm), lambda i: (0, 0))],
        out_specs=[pl.BlockSpec((1, tm), lambda i: (0, i)),
                   pl.BlockSpec((1, tm), lambda i: (0, i)),
                   pl.BlockSpec((2, tm), lambda i: (0, i)),
                   pl.BlockSpec((N_BUCKET_PAD, 1), lambda i: (0, 0))],
        out_shape=[jax.ShapeDtypeStruct((1, t), jnp.int32),
                   jax.ShapeDtypeStruct((1, t), jnp.int32),
                   jax.ShapeDtypeStruct((2, t), F32),
                   jax.ShapeDtypeStruct((N_BUCKET_PAD, 1), F32)],
        scratch_shapes=[pltpu.VMEM((N_BUCKET_PAD, 1), F32)],
        compiler_params=_cparams(("arbitrary",), 32),
        name="route",
    )(rt, b_col, tri)

    n_bucket = N_GROUPS * N_PAIRS
    counts = cnt[:n_bucket, 0].astype(jnp.int32)
    padded = (counts + MOE_BM - 1) // MOE_BM * MOE_BM
    pad_end = jnp.cumsum(padded)
    pad_start = pad_end - padded
    bucket = bucket[0]
    sel = bucket[:, None] == jnp.arange(n_bucket, dtype=jnp.int32)[None, :]
    pos = jnp.sum(jnp.where(sel, pad_start[None, :], 0), axis=1).astype(jnp.int32) + rank[0]
    n_used = (pad_end[-1] // MOE_BM).astype(jnp.int32)
    blk = jnp.minimum(jnp.arange(n_blk, dtype=jnp.int32), n_used - 1)
    blk_bucket = jnp.sum((pad_end[None, :] <= (blk * MOE_BM)[:, None]).astype(jnp.int32), axis=1)
    blk_bucket = jnp.minimum(blk_bucket, n_bucket - 1)
    pair_lo = jnp.array(PAIR_FIRST, jnp.int32)
    pair_hi = jnp.array(PAIR_SECOND, jnp.int32)
    blk_pair = blk_bucket % N_PAIRS
    first = (blk_bucket // N_PAIRS) * GROUP_SIZE
    is_pair = blk_pair[:, None] == jnp.arange(N_PAIRS, dtype=jnp.int32)[None, :]
    blk_a = first + jnp.sum(jnp.where(is_pair, pair_lo[None, :], 0), axis=1)
    blk_b = first + jnp.sum(jnp.where(is_pair, pair_hi[None, :], 0), axis=1)
    is_bucket = blk_bucket[:, None] == jnp.arange(n_bucket, dtype=jnp.int32)[None, :]
    per_bucket = lambda v: jnp.sum(jnp.where(is_bucket, v[None, :], 0), axis=1)
    first_blk = per_bucket(pad_start // MOE_BM)
    n_in_bucket = jnp.maximum(per_bucket(padded // MOE_BM), 1)
    src = first_blk + (blk - first_blk + n_in_bucket - 1) % n_in_bucket
    steps = jnp.arange(n_blk, dtype=jnp.int32)
    used = steps < n_used
    blk_out = jnp.where(used, src, steps).astype(jnp.int32)
    blk_rows = jnp.where(used, jnp.clip(per_bucket(pad_start + counts) - src * MOE_BM, 0, MOE_BM), 0)
    gates = jnp.pad(gate.T, ((0, 0), (0, GATE_W - 2)))
    plan = (src.astype(jnp.int32), blk_a.astype(jnp.int32), blk_b.astype(jnp.int32), blk_rows.astype(jnp.int32),
            blk_out)
    return pos, gates, plan


def _dispatch_kernel(pos_ref, hp_ref, gt_ref, init_ref, xs_ref, buf, sem):
    del init_ref
    i = pl.program_id(0)
    slot = i % 2
    tm = hp_ref.shape[0]
    w = hp_ref.shape[1]
    buf[slot, :, :w] = hp_ref[...]
    buf[slot, :, w:] = pltpu.bitcast(gt_ref[...], U32)

    def start_all():
        for r in range(tm):
            pltpu.make_async_copy(buf.at[slot, pl.ds(r, 1)], xs_ref.at[pl.ds(pos_ref[0, 0, r], 1)],
                                  sem.at[slot]).start(priority=r % N_DMA_PRIORITIES)

    def wait_all(s):
        pltpu.make_async_copy(buf.at[s], xs_ref.at[pl.ds(0, tm)], sem.at[s]).wait()

    start_all()

    @pl.when(i > 0)
    def _():
        wait_all(1 - slot)

    @pl.when(i == pl.num_programs(0) - 1)
    def _():
        wait_all(slot)


def _dispatch(hp, gates, pos, n_buf, tm):
    t, w = hp.shape
    init = jnp.zeros((n_buf, w + GATE_W), U32)
    return pl.pallas_call(
        _dispatch_kernel,
        grid=(t // tm,),
        in_specs=[pl.BlockSpec((1, 1, tm), lambda i: (i, 0, 0), memory_space=pltpu.SMEM),
                  pl.BlockSpec((tm, w), lambda i: (i, 0)),
                  pl.BlockSpec((tm, GATE_W), lambda i: (i, 0)),
                  pl.BlockSpec(memory_space=pl.ANY)],
        out_specs=pl.BlockSpec(memory_space=pl.ANY),
        out_shape=jax.ShapeDtypeStruct((n_buf, w + GATE_W), U32),
        scratch_shapes=[pltpu.VMEM((2, tm, w + GATE_W), U32), pltpu.SemaphoreType.DMA((2,))],
        input_output_aliases={3: 0},
        compiler_params=_cparams(("arbitrary",), 32),
        name="dispatch",
    )(pos.reshape(t // tm, 1, tm), hp, gates, init)


def _moe_kernel(src_ref, ea_ref, eb_ref, nv_ref, dst_ref, x_ref, w1a_ref, w3a_ref, w2a_ref, w1b_ref, w3b_ref,
                w2b_ref, y_ref):
    del src_ref, ea_ref, eb_ref, dst_ref
    nv = nv_ref[pl.program_id(0)]
    bm = x_ref.shape[0]
    w = x_ref.shape[1] - GATE_W

    def experts(m):
        xa, xb = _unpack_rows(x_ref[0:m, :w])
        x = jnp.concatenate([xa.astype(BF16), xb.astype(BF16)], axis=1)
        gates = pltpu.bitcast(x_ref[0:m, w:], F32)

        def hidden(w1_ref, w3_ref, g):
            a = jnp.dot(x, w1_ref[...], preferred_element_type=F32)
            b = jnp.dot(x, w3_ref[...], preferred_element_type=F32)
            return (a * jax.nn.sigmoid(a) * b * g).astype(BF16)

        y = jnp.dot(hidden(w1a_ref, w3a_ref, gates[:, 0:1]), w2a_ref[...], preferred_element_type=F32)
        y += jnp.dot(hidden(w1b_ref, w3b_ref, gates[:, 1:2]), w2b_ref[...], preferred_element_type=F32)
        y_ref[0:m, :] = _pack_rows(y)
        if m < bm:
            y_ref[m:, :] = jnp.zeros((bm - m, y_ref.shape[1]), y_ref.dtype)

    @pl.when(nv > bm // 2)
    def _():
        experts(bm)

    @pl.when(jnp.logical_and(nv > 0, nv <= bm // 2))
    def _():
        experts(bm // 2)

    @pl.when(nv == 0)
    def _():
        y_ref[...] = jnp.zeros_like(y_ref)


def _moe(xs, w1, w3, w2, plan):
    n_buf, wx = xs.shape
    n_blk = n_buf // MOE_BM
    _, d, de = w1.shape

    def expert(second, rows, cols):
        if second:
            return pl.BlockSpec((None, rows, cols), lambda i, src, ea, eb, nv, dst: (eb[i], 0, 0))
        return pl.BlockSpec((None, rows, cols), lambda i, src, ea, eb, nv, dst: (ea[i], 0, 0))

    grid_spec = pltpu.PrefetchScalarGridSpec(
        num_scalar_prefetch=5,
        grid=(n_blk,),
        in_specs=[pl.BlockSpec((MOE_BM, wx), lambda i, src, ea, eb, nv, dst: (src[i], 0)),
                  expert(False, d, de), expert(False, d, de), expert(False, de, d),
                  expert(True, d, de), expert(True, d, de), expert(True, de, d)],
        out_specs=pl.BlockSpec((MOE_BM, d // 2), lambda i, src, ea, eb, nv, dst: (dst[i], 0)),
    )
    return pl.pallas_call(
        _moe_kernel,
        grid_spec=grid_spec,
        out_shape=jax.ShapeDtypeStruct((n_buf, d // 2), U32),
        compiler_params=_cparams(("arbitrary",), 48),
        name="moe",
    )(*plan, xs, w1, w3, w2, w1, w3, w2)


def _combine_kernel(npt, pos_ref, nxt_ref, xn_ref, mod_ref, yb_ref, op_ref, os_ref, buf, sem):
    i = pl.program_id(0)
    out = xn_ref[...] + mod_ref[5:6, :] * _gathered_rows(pos_ref, nxt_ref, yb_ref, buf, sem)

    @pl.when(i < npt)
    def _():
        op_ref[...] = out

    @pl.when(i >= npt)
    def _():
        os_ref[...] = out


def _combine(xn, yb, pos, mod_l, tp, tm, mod_index):
    t, d = xn.shape
    npt = tp // tm
    n = t // tm
    pos3 = pos.reshape(n, 1, tm)
    return pl.pallas_call(
        functools.partial(_combine_kernel, npt),
        grid=(n,),
        in_specs=[pl.BlockSpec((1, 1, tm), lambda i: (i, 0, 0), memory_space=pltpu.SMEM),
                  pl.BlockSpec((1, 1, tm), lambda i: (jnp.minimum(i + 1, n - 1), 0, 0), memory_space=pltpu.SMEM),
                  pl.BlockSpec((tm, d), lambda i: (i, 0)),
                  pl.BlockSpec((None, 6, d), lambda i: (mod_index(i, tm), 0, 0)),
                  pl.BlockSpec(memory_space=pl.ANY)],
        out_specs=[pl.BlockSpec((tm, d), lambda i: (jnp.minimum(i, npt - 1), 0)),
                   pl.BlockSpec((tm, d), lambda i: (jnp.maximum(i - npt, 0), 0))],
        out_shape=[jax.ShapeDtypeStruct((tp, d), F32),
                   jax.ShapeDtypeStruct((t - tp, d), F32)],
        scratch_shapes=[pltpu.VMEM((2, tm, d // 2), U32), pltpu.SemaphoreType.DMA((2,))],
        compiler_params=_cparams(("arbitrary",), 32),
        name="combine",
    )(pos3, pos3, xn, mod_l, yb)


def _pad_cols(x, n):
    return jnp.pad(x, ((0, 0), (0, n - x.shape[1])))


def _rope_tables(dec_seq, tm):
    n_freq = MLA_ROPE // 4
    pos = jnp.arange(dec_seq, dtype=jnp.int32)
    row = (pos // GRID_W).astype(F32)
    col = (pos % GRID_W).astype(F32)
    inv = ROPE_THETA ** (-jnp.arange(n_freq, dtype=F32) / n_freq)
    ar = row[:, None] * inv[None, :]
    ac = col[:, None] * inv[None, :]
    ones = jnp.ones((dec_seq, LANE - MLA_ROPE), F32)
    cos = jnp.concatenate([jnp.cos(ar), jnp.cos(ar), jnp.cos(ac), jnp.cos(ac), ones], axis=1)
    sin = jnp.concatenate([-jnp.sin(ar), jnp.sin(ar), -jnp.sin(ac), jnp.sin(ac), 0.0 * ones], axis=1)
    ident_c = jnp.ones((tm, LANE), F32)
    ident_s = jnp.zeros((tm, LANE), F32)
    return jnp.concatenate([cos, ident_c], axis=0), jnp.concatenate([sin, ident_s], axis=0)


def _layer_params(l, q_lora, kv_lora, w_in, g_q_lora, w_uq, g_kv_lora, w_ukv, g_qk_q, g_qk_k,
                  w_dw31, b_dw31, g_conv_ln, b_conv_ln, w_pw, b_pw, g_sgu_ln, b_sgu_ln,
                  w_spatial, b_spatial, w_sc3, w_out):
    d = w_in.shape[1]
    o1 = q_lora
    o2 = o1 + kv_lora
    o3 = o2 + MLA_ROPE
    o4 = o3 + 1024
    o5 = o4 + 1024
    w = w_in[l]
    zeros = lambda n: jnp.zeros((d, n), w.dtype)
    w_in_p = jnp.concatenate(
        [w[:, o3:o4], w[:, o4:o5], w[:, o5 + 512:o5 + 1536], w[:, o5:o5 + 512],
         w[:, :o1], zeros(512 - q_lora),
         w[:, o1:o2], w[:, o2:o3], zeros(PROJ_W - KV_COL - kv_lora - MLA_ROPE)], axis=1).astype(BF16)

    uq = w_uq[l].reshape(q_lora, MLA_HEADS, MLA_QK)
    uq = jnp.pad(uq, ((0, 512 - q_lora), (0, 0), (0, HEAD_PAD - MLA_QK)))
    w_uq_p = uq.reshape(512, MLA_HEADS * HEAD_PAD).astype(BF16)
    ukv = w_ukv[l].reshape(kv_lora, MLA_HEADS, MLA_NOPE + MLA_V)
    w_ukv_p = jnp.concatenate([ukv[:, :, :MLA_NOPE].reshape(kv_lora, -1),
                               ukv[:, :, MLA_NOPE:].reshape(kv_lora, -1)], axis=1).astype(BF16)
    row = lambda v: v.reshape(1, -1).astype(F32)
    qkv_w = (w_uq_p, w_ukv_p, _pad_cols(row(g_q_lora[l]), 512), row(g_kv_lora[l]),
             _pad_cols(row(g_qk_q[l]), HEAD_PAD), _pad_cols(row(g_qk_k[l]), HEAD_PAD))

    hc = w_pw.shape[1] // GMLP_HEADS
    bsp = jnp.repeat(b_spatial[l].T.astype(F32), hc, axis=1)
    w_dw = jnp.pad(w_dw31[l].astype(F32), ((0, 32 - CONV_K), (0, 0)))
    w_dw = w_dw.reshape(32, -1, LANE).transpose(1, 0, 2)
    mix_w = (w_dw, row(b_dw31[l]),
             row(g_conv_ln[l]), row(b_conv_ln[l]), w_pw[l].astype(BF16), row(b_pw[l]),
             row(g_sgu_ln[l]), row(b_sgu_ln[l]), w_spatial[l].astype(BF16), bsp,
             jnp.pad(w_sc3[l].astype(F32), ((0, 8 - SC_K), (0, 0))))
    return w_in_p, qkv_w, mix_w, w_out[l].astype(BF16)


def _tile(limit, *sizes):
    tm = limit
    while any(s % tm for s in sizes):
        tm //= 2
    return tm


def kernel(x_prompt, x_sample, cache_mla_ckv, cache_mla_krope, c, c_ctx, w_mod, b_mod, g_norm1, g_norm2, w_in, g_q_lora, w_uq, g_kv_lora, w_ukv, g_qk_q, g_qk_k, w_dw31, b_dw31, g_conv_ln, b_conv_ln, w_pw, b_pw, g_sgu_ln, b_sgu_ln, w_spatial, b_spatial, w_sc3, w_out, router_w, router_b, w1, w3, w2):
    nseq, seq, d = x_prompt.shape
    nb, dec_seq, _ = x_sample.shape
    depth = w_mod.shape[0]
    past = cache_mla_ckv.shape[2]
    q_lora = g_q_lora.shape[1]
    kv_lora = g_kv_lora.shape[1]
    n_exp = router_w.shape[1]
    tp = nseq * seq
    ts = nb * dec_seq
    t = tp + ts
    assert tp % dec_seq == 0 and seq % CHUNK == 0 and dec_seq % GRID_W == 0
    assert nb + 1 <= 8 and kv_lora == MLA_NOPE and n_exp == N_GROUPS * GROUP_SIZE

    tm_in = _tile(512, tp, dec_seq)
    tm_qkv = _tile(512, tp, dec_seq)
    tm_row = _tile(256, seq, dec_seq)
    tq = _tile(1024, dec_seq)

    def mod_index(i, tm):
        npt = tp // tm
        return jnp.where(i < npt, 0, 1 + (i - npt) // (dec_seq // tm))

    cvec = jnp.concatenate([c_ctx[None, :], c, jnp.zeros((8 - 1 - nb, d), F32)], axis=0)
    mod = _modulation(cvec, w_mod, b_mod).reshape(depth, 8, 6, d)
    cos_t, sin_t = _rope_tables(dec_seq, tm_qkv)

    rw = router_w.T.reshape(N_GROUPS, GROUP_SIZE, d).transpose(1, 0, 2).reshape(n_exp, d)
    rw_hi = rw.astype(BF16)
    rw_lo = (rw - rw_hi.astype(F32)).astype(BF16)
    wr = jnp.pad(jnp.concatenate([rw_hi, rw_lo], axis=0), ((0, LANE - 2 * n_exp), (0, 0)))
    n_blk = (t + N_GROUPS * N_PAIRS * (MOE_BM - 1) + MOE_BM - 1) // MOE_BM

    xp = x_prompt.reshape(tp, d)
    xs = x_sample.reshape(ts, d)
    ckv_states = []
    krope_states = []
    for l in range(depth):
        w_in_p, qkv_w, mix_w, w_out_b = _layer_params(
            l, q_lora, kv_lora, w_in, g_q_lora, w_uq, g_kv_lora, w_ukv, g_qk_q, g_qk_k,
            w_dw31, b_dw31, g_conv_ln, b_conv_ln, w_pw, b_pw, g_sgu_ln, b_sgu_ln,
            w_spatial, b_spatial, w_sc3, w_out)
        mod_l = mod[l]

        if l == 0:
            proj, kv = _in_proj(xp, xs, mod_l, g_norm1[l].reshape(1, d), w_in_p, tm_in, mod_index)
        else:
            proj, kv, xp = _in_proj_fused(xn, yb, pos, mod[l - 1], mod_l, g_norm1[l].reshape(1, d), w_in_p,
                                          tm_in, mod_index)
            xs = None
        q, k, v, state = _qkv(proj, kv, cos_t, sin_t, *qkv_w, tp, dec_seq, q_lora, tm_qkv)
        kvc = jnp.concatenate([cache_mla_ckv[:, l], cache_mla_krope[:, l],
                               jnp.zeros((nb, past, KV_W - kv_lora - MLA_ROPE), F32)], axis=-1)
        kc, vc = _ctx_kv(kvc.reshape(nb * past, KV_W), qkv_w[1], qkv_w[5], _tile(512, past))
        o_as, w1b, w3b, w2b = _attention_latent(q, k, v, kc, vc, w1, w3, w2, l, tp, nb, dec_seq, past, tq)
        o_ap = _attention_context(q, k, v, nseq, seq)
        o_bcd = _mixers(proj, mix_w, tp, seq, dec_seq, tm_row)

        xn, hp, r = _out_proj(o_ap, o_as, o_bcd, xp, xs, mod_l, g_norm2[l].reshape(1, d), w_out_b, wr,
                              tm_qkv, mod_index)
        pos, gates, plan = _route(r, router_b, n_blk, tm_qkv)
        xd = _dispatch(hp, gates, pos, n_blk * MOE_BM, tm_row)
        yb = _moe(xd, w1b, w3b, w2b, plan)
        if l == depth - 1:
            xp, xs = _combine(xn, yb, pos, mod_l, tp, tm_row, mod_index)

        ckv_states.append(state[:, :kv_lora].reshape(nseq, seq, kv_lora))
        krope_states.append(state[:, kv_lora:kv_lora + MLA_ROPE].reshape(nseq, seq, MLA_ROPE))

    return (xp.reshape(nseq, seq, d), xs.reshape(nb, dec_seq, d),
            jnp.stack(ckv_states, axis=1), jnp.stack(krope_states, axis=1))
```

```python
import functools

import jax
import jax.numpy as jnp
from jax import lax
from jax.experimental import pallas as pl
from jax.experimental.pallas import tpu as pltpu

F32 = jnp.float32
BF16 = jnp.bfloat16
U32 = jnp.uint32

MLA_HEADS = 4
MLA_NOPE = 128
MLA_ROPE = 64
MLA_V = 128
MLA_QK = MLA_NOPE + MLA_ROPE
HEAD_PAD = 256
GRID_W = 64
ROPE_THETA = 10000.0
CONV_K = 31
SC_K = 3
CHUNK = 128
GMLP_HEADS = 4
N_GROUPS = 8
GROUP_SIZE = 4
N_PAIRS = 6
NORM_EPS = 1e-6

LANE = 128
HALO = 16
PROJ_W = 4352
CQ_COL = 3584
KV_COL = 4096
KV_W = 256
IN_TN = 1536
MOE_BM = 256
OUT_PARTS = 4
N_DMA_PRIORITIES = 2
ATTN_QSUB = 1024
ATTN_KCHUNK = 1024
LOG2_E = 1.4426950408889634
GATE_W = 128
VMEM_CAP = 56 * 1024 * 1024


def _cparams(sem, vmem_mb):
    return pltpu.CompilerParams(dimension_semantics=sem,
                                vmem_limit_bytes=min(vmem_mb * 1024 * 1024, VMEM_CAP))


def _inv_rms(x, n):
    return lax.rsqrt(jnp.sum(x * x, axis=-1, keepdims=True) * (1.0 / n) + NORM_EPS)


def _layernorm(x, g, b):
    mu = jnp.mean(x, axis=-1, keepdims=True)
    xc = x - mu
    var = jnp.mean(xc * xc, axis=-1, keepdims=True)
    return xc * lax.rsqrt(var + NORM_EPS) * g + b


def _split_bf16(x):
    hi = x.astype(BF16)
    lo = (x - hi.astype(F32)).astype(BF16)
    return hi, lo


def _pack_rows(x):
    n = x.shape[1] // 2
    hi = pltpu.bitcast(x[:, :n].astype(BF16).astype(F32), U32)
    lo = pltpu.bitcast(x[:, n:].astype(BF16).astype(F32), U32)
    return (hi & jnp.uint32(0xFFFF0000)) | (lo >> 16)


def _unpack_rows(u):
    a = pltpu.bitcast(u & jnp.uint32(0xFFFF0000), F32)
    b = pltpu.bitcast(u << 16, F32)
    return a, b


def _mod_kernel(c_ref, w_ref, b_ref, o_ref):
    c = c_ref[...]
    a_hi, a_lo = _split_bf16(c * jax.nn.sigmoid(c))
    w_hi, w_lo = _split_bf16(w_ref[...])
    acc = jnp.dot(a_hi, w_hi, preferred_element_type=F32)
    acc += jnp.dot(a_lo, w_hi, preferred_element_type=F32)
    acc += jnp.dot(a_hi, w_lo, preferred_element_type=F32)
    o_ref[...] = acc + b_ref[...]


def _modulation(cvec, w_mod, b_mod):
    depth, d, n = w_mod.shape
    tn = 1024
    return pl.pallas_call(
        _mod_kernel,
        grid=(depth, n // tn),
        in_specs=[pl.BlockSpec((8, d), lambda l, j: (0, 0)),
                  pl.BlockSpec((None, d, tn), lambda l, j: (l, 0, j)),
                  pl.BlockSpec((None, 1, tn), lambda l, j: (l, 0, j))],
        out_specs=pl.BlockSpec((None, 8, tn), lambda l, j: (l, 0, j)),
        out_shape=jax.ShapeDtypeStruct((depth, 8, n), F32),
        compiler_params=_cparams(("arbitrary", "arbitrary"), 40),
        name="modulation",
    )(cvec, w_mod, b_mod.reshape(depth, 1, n))


def _norm_project(x, mod_ref, g_ref, w_ref, proj_ref, kv_ref):
    y = x * _inv_rms(x, x.shape[-1]) * g_ref[...]
    h = (y * (1.0 + mod_ref[1:2, :]) + mod_ref[0:1, :]).astype(BF16)
    for c0 in range(0, PROJ_W, IN_TN):
        c1 = min(c0 + IN_TN, PROJ_W)
        acc = jnp.dot(h, w_ref[:, c0:c1], preferred_element_type=F32)
        proj_ref[:, c0:c1] = acc.astype(BF16)
        if c0 <= KV_COL < c1:
            kv_ref[...] = acc[:, KV_COL - c0:KV_COL - c0 + KV_W]


def _in_proj_kernel(npt, xp_ref, xs_ref, mod_ref, g_ref, w_ref, proj_ref, kv_ref):
    x = jnp.where(pl.program_id(0) < npt, xp_ref[...], xs_ref[...])
    _norm_project(x, mod_ref, g_ref, w_ref, proj_ref, kv_ref)


def _gathered_rows(pos_ref, nxt_ref, yb_ref, buf, sem):
    i = pl.program_id(0)
    slot = i % 2
    tm = buf.shape[1]

    def gather(idx_ref, s):
        for r in range(tm):
            pltpu.make_async_copy(yb_ref.at[pl.ds(idx_ref[0, 0, r], 1)], buf.at[s, pl.ds(r, 1)],
                                  sem.at[s]).start(priority=r % N_DMA_PRIORITIES)

    @pl.when(i == 0)
    def _():
        gather(pos_ref, 0)

    for s in range(2):
        @pl.when(jnp.logical_and(i + 1 < pl.num_programs(0), slot == 1 - s))
        def _():
            gather(nxt_ref, s)

    pltpu.make_async_copy(yb_ref.at[pl.ds(0, tm)], buf.at[slot], sem.at[slot]).wait()
    ya, yb = _unpack_rows(buf[slot])
    return jnp.concatenate([ya, yb], axis=1)


def _in_proj_fused_kernel(pos_ref, nxt_ref, xn_ref, prev_mod_ref, mod_ref, g_ref, w_ref, yb_ref,
                          proj_ref, kv_ref, x_ref, buf, sem):
    x = xn_ref[...] + prev_mod_ref[5:6, :] * _gathered_rows(pos_ref, nxt_ref, yb_ref, buf, sem)
    x_ref[...] = x
    _norm_project(x, mod_ref, g_ref, w_ref, proj_ref, kv_ref)


def _in_proj_fused(xn, yb, pos, prev_mod, mod_l, g, w, tm, mod_index):
    t, d = xn.shape
    n = t // tm
    pos3 = pos.reshape(n, 1, tm)
    mod_spec = pl.BlockSpec((None, 6, d), lambda i: (mod_index(i, tm), 0, 0))
    return pl.pallas_call(
        _in_proj_fused_kernel,
        grid=(n,),
        in_specs=[pl.BlockSpec((1, 1, tm), lambda i: (i, 0, 0), memory_space=pltpu.SMEM),
                  pl.BlockSpec((1, 1, tm), lambda i: (jnp.minimum(i + 1, n - 1), 0, 0), memory_space=pltpu.SMEM),
                  pl.BlockSpec((tm, d), lambda i: (i, 0)),
                  mod_spec, mod_spec,
                  pl.BlockSpec((1, d), lambda i: (0, 0)),
                  pl.BlockSpec(w.shape, lambda i: (0, 0), pipeline_mode=pl.Buffered(1)),
                  pl.BlockSpec(memory_space=pl.ANY)],
        out_specs=[pl.BlockSpec((tm, PROJ_W), lambda i: (i, 0)),
                   pl.BlockSpec((tm, KV_W), lambda i: (i, 0)),
                   pl.BlockSpec((tm, d), lambda i: (i, 0))],
        out_shape=[jax.ShapeDtypeStruct((t, PROJ_W), BF16),
                   jax.ShapeDtypeStruct((t, KV_W), F32),
                   jax.ShapeDtypeStruct((t, d), F32)],
        scratch_shapes=[pltpu.VMEM((2, tm, d // 2), U32), pltpu.SemaphoreType.DMA((2,))],
        compiler_params=_cparams(("arbitrary",), 56),
        name="in_proj_fused",
    )(pos3, pos3, xn, prev_mod, mod_l, g, w, yb)


def _in_proj(xp, xs, mod_l, g, w, tm, mod_index):
    tp, d = xp.shape
    t = tp + xs.shape[0]
    npt = tp // tm
    return pl.pallas_call(
        functools.partial(_in_proj_kernel, npt),
        grid=(t // tm,),
        in_specs=[pl.BlockSpec((tm, d), lambda i: (jnp.minimum(i, npt - 1), 0)),
                  pl.BlockSpec((tm, d), lambda i: (jnp.maximum(i - npt, 0), 0)),
                  pl.BlockSpec((None, 6, d), lambda i: (mod_index(i, tm), 0, 0)),
                  pl.BlockSpec((1, d), lambda i: (0, 0)),
                  pl.BlockSpec(w.shape, lambda i: (0, 0), pipeline_mode=pl.Buffered(1))],
        out_specs=[pl.BlockSpec((tm, PROJ_W), lambda i: (i, 0)),
                   pl.BlockSpec((tm, KV_W), lambda i: (i, 0))],
        out_shape=[jax.ShapeDtypeStruct((t, PROJ_W), BF16),
                   jax.ShapeDtypeStruct((t, KV_W), F32)],
        compiler_params=_cparams(("arbitrary",), 56),
        name="in_proj",
    )(xp, xs, mod_l, g, w)


def _rope_tile(t, cos, sin):
    lane = lax.broadcasted_iota(jnp.int32, t.shape, 1)
    first = (lane & 16) == 0
    swapped = jnp.where(first, pltpu.roll(t, LANE - 16, 1), pltpu.roll(t, 16, 1))
    return t * cos + swapped * sin


def _keys_values(ckv_n, krope, w_ukv_ref, gk_ref, cos, sin, k_ref, v_ref):
    kvf = jnp.dot(ckv_n.astype(BF16), w_ukv_ref[...], preferred_element_type=F32)
    kr_ss = jnp.sum(krope * krope, axis=-1, keepdims=True)
    gk = gk_ref[...]
    tail = krope * gk[:, MLA_NOPE:]
    if cos is not None:
        tail = _rope_tile(tail, cos, sin)
    for h in range(MLA_HEADS):
        kn = kvf[:, h * MLA_NOPE:(h + 1) * MLA_NOPE]
        r = lax.rsqrt((jnp.sum(kn * kn, axis=-1, keepdims=True) + kr_ss) * (1.0 / MLA_QK) + NORM_EPS)
        k_ref[:, h * HEAD_PAD:h * HEAD_PAD + MLA_NOPE] = (kn * r * gk[:, :MLA_NOPE]).astype(BF16)
        k_ref[:, h * HEAD_PAD + MLA_NOPE:(h + 1) * HEAD_PAD] = (tail * r).astype(BF16)
    v_ref[...] = kvf[:, MLA_HEADS * MLA_NOPE:].astype(BF16)


def _qkv_kernel(npt, q_lora, cq_ref, kv_ref, cos_ref, sin_ref, w_uq_ref, w_ukv_ref, gql_ref, gkv_ref,
                gq_ref, gk_ref, q_ref, k_ref, v_ref, st_ref):
    i = pl.program_id(0)
    cos = cos_ref[...]
    sin = sin_ref[...]
    cq = cq_ref[...].astype(F32)
    cqn = cq * _inv_rms(cq, q_lora) * gql_ref[...]
    qf = jnp.dot(cqn.astype(BF16), w_uq_ref[...], preferred_element_type=F32)
    gq = gq_ref[...]
    scale = MLA_QK ** -0.5 * LOG2_E
    for h in range(MLA_HEADS):
        qh = qf[:, h * HEAD_PAD:(h + 1) * HEAD_PAD]
        qh = qh * (_inv_rms(qh, MLA_QK) * scale) * gq
        q_ref[:, h * HEAD_PAD:h * HEAD_PAD + MLA_NOPE] = qh[:, :MLA_NOPE].astype(BF16)
        q_ref[:, h * HEAD_PAD + MLA_NOPE:(h + 1) * HEAD_PAD] = _rope_tile(qh[:, MLA_NOPE:], cos, sin).astype(BF16)

    kv = kv_ref[...]
    ckv = kv[:, :MLA_NOPE]
    krope = kv[:, MLA_NOPE:]
    ckv_n = ckv * _inv_rms(ckv, ckv.shape[-1]) * gkv_ref[...]
    _keys_values(ckv_n, krope, w_ukv_ref, gk_ref, cos, sin, k_ref, v_ref)

    @pl.when(i < npt)
    def _():
        st_ref[:, :MLA_NOPE] = ckv_n
        st_ref[:, MLA_NOPE:] = krope


def _qkv(proj, kv, cos_t, sin_t, w_uq, w_ukv, gql, gkv, gq, gk, tp, dec_seq, q_lora, tm):
    t = proj.shape[0]
    npt = tp // tm
    nseq = dec_seq // tm
    cq_blk = CQ_COL // 512

    def tab(i):
        return (jnp.where(i < npt, nseq, (i - npt) % nseq), 0)

    const = lambda i: (0, 0)
    return pl.pallas_call(
        functools.partial(_qkv_kernel, npt, q_lora),
        grid=(t // tm,),
        in_specs=[pl.BlockSpec((tm, 512), lambda i: (i, cq_blk)),
                  pl.BlockSpec((tm, KV_W), lambda i: (i, 0)),
                  pl.BlockSpec((tm, LANE), tab),
                  pl.BlockSpec((tm, LANE), tab),
                  pl.BlockSpec(w_uq.shape, const),
                  pl.BlockSpec(w_ukv.shape, const),
                  pl.BlockSpec(gql.shape, const),
                  pl.BlockSpec(gkv.shape, const),
                  pl.BlockSpec(gq.shape, const),
                  pl.BlockSpec(gk.shape, const)],
        out_specs=[pl.BlockSpec((tm, MLA_HEADS * HEAD_PAD), lambda i: (i, 0)),
                   pl.BlockSpec((tm, MLA_HEADS * HEAD_PAD), lambda i: (i, 0)),
                   pl.BlockSpec((tm, MLA_HEADS * MLA_V), lambda i: (i, 0)),
                   pl.BlockSpec((tm, KV_W), lambda i: (jnp.minimum(i, npt - 1), 0))],
        out_shape=[jax.ShapeDtypeStruct((t, MLA_HEADS * HEAD_PAD), BF16),
                   jax.ShapeDtypeStruct((t, MLA_HEADS * HEAD_PAD), BF16),
                   jax.ShapeDtypeStruct((t, MLA_HEADS * MLA_V), BF16),
                   jax.ShapeDtypeStruct((tp, KV_W), F32)],
        compiler_params=_cparams(("arbitrary",), 40),
        name="qkv",
    )(proj, kv, cos_t, sin_t, w_uq, w_ukv, gql, gkv, gq, gk)


def _ctx_kv_kernel(kv_ref, w_ukv_ref, gk_ref, k_ref, v_ref):
    kv = kv_ref[...]
    _keys_values(kv[:, :MLA_NOPE], kv[:, MLA_NOPE:], w_ukv_ref, gk_ref, None, None, k_ref, v_ref)


def _ctx_kv(kvc, w_ukv, gk, tm):
    r = kvc.shape[0]
    const = lambda i: (0, 0)
    return pl.pallas_call(
        _ctx_kv_kernel,
        grid=(r // tm,),
        in_specs=[pl.BlockSpec((tm, KV_W), lambda i: (i, 0)),
                  pl.BlockSpec(w_ukv.shape, const),
                  pl.BlockSpec(gk.shape, const)],
        out_specs=[pl.BlockSpec((tm, MLA_HEADS * HEAD_PAD), lambda i: (i, 0)),
                   pl.BlockSpec((tm, MLA_HEADS * MLA_V), lambda i: (i, 0))],
        out_shape=[jax.ShapeDtypeStruct((r, MLA_HEADS * HEAD_PAD), BF16),
                   jax.ShapeDtypeStruct((r, MLA_HEADS * MLA_V), BF16)],
        compiler_params=_cparams(("arbitrary",), 32),
        name="ctx_kv",
    )(kvc, w_ukv, gk)


_NT = (((1,), (1,)), ((), ()))


def _attn_kernel(has_ctx, *refs):
    if has_ctx:
        q_ref, kc_ref, vc_ref, k_ref, v_ref, w1_ref, w3_ref, w2_ref, o_ref, w1o_ref, w3o_ref, w2o_ref = refs
        w1o_ref[...] = w1_ref[...].astype(BF16)
        w3o_ref[...] = w3_ref[...].astype(BF16)
        w2o_ref[...] = w2_ref[...].astype(BF16)
    else:
        q_ref, k_ref, v_ref, o_ref = refs
    n_own = k_ref.shape[0]
    step = min(n_own, ATTN_KCHUNK)
    chunks = [(k_ref, v_ref, c * step, step) for c in range(n_own // step)]
    if has_ctx:
        chunks = [(kc_ref, vc_ref, 0, kc_ref.shape[0])] + chunks
    for h in range(q_ref.shape[1] // HEAD_PAD):
        qk = slice(h * HEAD_PAD, (h + 1) * HEAD_PAD)
        hv = slice(h * MLA_V, (h + 1) * MLA_V)
        tq = q_ref.shape[0]
        sub = min(tq, ATTN_QSUB)
        state = [None] * (tq // sub)
        for kr, vr, off, n in chunks:
            for t in range(tq // sub):
                rows = slice(t * sub, (t + 1) * sub)
                s = lax.dot_general(q_ref[rows, qk], kr[off:off + n, qk], _NT, preferred_element_type=F32)
                mc = jnp.max(s, axis=-1, keepdims=True)
                m_new = mc if state[t] is None else jnp.maximum(state[t][0], mc)
                p = jnp.exp2(s - m_new)
                pv = jnp.dot(p.astype(BF16), vr[off:off + n, hv], preferred_element_type=F32)
                ps = jnp.sum(p, axis=-1, keepdims=True)
                if state[t] is None:
                    state[t] = (m_new, ps, pv)
                else:
                    m, l, acc = state[t]
                    alpha = jnp.exp2(m - m_new)
                    state[t] = (m_new, alpha * l + ps, alpha * acc + pv)
        for t in range(tq // sub):
            _, l, acc = state[t]
            o_ref[t * sub:(t + 1) * sub, hv] = (acc / l).astype(BF16)


def _attention_latent(q, k, v, kc, vc, w1, w3, w2, layer, tp, nb, dec_seq, past, tq):
    t = q.shape[0]
    nq = dec_seq // tq
    row0 = tp // tq
    kblk0 = tp // dec_seq
    steps = nb * MLA_HEADS * nq
    depth, n_exp, d, de = w1.shape
    flat = [w1.reshape(depth, n_exp * d, de), w3.reshape(depth, n_exp * d, de), w2.reshape(depth, n_exp * de, d)]
    assert all(w.shape[1] % (8 * steps) == 0 for w in flat)
    step = lambda b, h, i: (b * MLA_HEADS + h) * nq + i
    w_in = [pl.BlockSpec((None, w.shape[1] // steps, w.shape[2]), lambda b, h, i: (layer, step(b, h, i), 0))
            for w in flat]
    w_out = [pl.BlockSpec((w.shape[1] // steps, w.shape[2]), lambda b, h, i: (step(b, h, i), 0)) for w in flat]
    o, w1b, w3b, w2b = pl.pallas_call(
        functools.partial(_attn_kernel, True),
        grid=(nb, MLA_HEADS, nq),
        in_specs=[pl.BlockSpec((tq, HEAD_PAD), lambda b, h, i: (row0 + b * nq + i, h)),
                  pl.BlockSpec((past, HEAD_PAD), lambda b, h, i: (b, h)),
                  pl.BlockSpec((past, MLA_V), lambda b, h, i: (b, h)),
                  pl.BlockSpec((dec_seq, HEAD_PAD), lambda b, h, i: (kblk0 + b, h)),
                  pl.BlockSpec((dec_seq, MLA_V), lambda b, h, i: (kblk0 + b, h))] + w_in,
        out_specs=[pl.BlockSpec((tq, MLA_V), lambda b, h, i: (b * nq + i, h))] + w_out,
        out_shape=[jax.ShapeDtypeStruct((t - tp, MLA_HEADS * MLA_V), BF16)]
                  + [jax.ShapeDtypeStruct(w.shape[1:], BF16) for w in flat],
        compiler_params=_cparams(("arbitrary", "arbitrary", "arbitrary"), 56),
        name="attn_latent",
    )(q, kc, vc, k, v, *flat)
    return o, w1b.reshape(n_exp, d, de), w3b.reshape(n_exp, d, de), w2b.reshape(n_exp, de, d)


def _attention_context(q, k, v, nseq, seq):
    return pl.pallas_call(
        functools.partial(_attn_kernel, False),
        grid=(nseq,),
        in_specs=[pl.BlockSpec((seq, MLA_HEADS * HEAD_PAD), lambda b: (b, 0)),
                  pl.BlockSpec((seq, MLA_HEADS * HEAD_PAD), lambda b: (b, 0)),
                  pl.BlockSpec((seq, MLA_HEADS * MLA_V), lambda b: (b, 0))],
        out_specs=pl.BlockSpec((seq, MLA_HEADS * MLA_V), lambda b: (b, 0)),
        out_shape=jax.ShapeDtypeStruct((nseq * seq, MLA_HEADS * MLA_V), BF16),
        compiler_params=_cparams(("arbitrary",), 32),
        name="attn_context",
    )(q, k, v)


def _mixers_kernel(npt, p_tiles, s_tiles,
                   cv_ref, cvp_ref, cvn_ref, gm_ref, sc_ref, scp_ref, scn_ref, bg_ref,
                   wdw_ref, bdw_ref, gcl_ref, bcl_ref, wpw_ref, bpw_ref,
                   gsl_ref, bsl_ref, wsp_ref, bsp_ref, wsc_ref,
                   o_ref, zs, ys, zsh, cb):
    i = pl.program_id(0)
    tm = cv_ref.shape[0]
    cw = cv_ref.shape[1] // 2
    pos = jnp.where(i < npt, i % p_tiles, (i - npt) % s_tiles)
    n_tiles = jnp.where(i < npt, p_tiles, s_tiles)
    keep_prev = (pos > 0).astype(F32)
    keep_next = (pos < n_tiles - 1).astype(F32)

    def glu(ref):
        a = ref[:, :cw].astype(F32)
        g = ref[:, cw:].astype(F32)
        return a * jax.nn.sigmoid(g)

    def prod(ref):
        return ref[:, :cw].astype(F32) * ref[:, cw:].astype(F32)

    n_lane = cw // LANE
    z_prev, z_main, z_next = glu(cvp_ref) * keep_prev, glu(cv_ref), glu(cvn_ref) * keep_next
    for c in range(n_lane):
        cs = slice(c * LANE, (c + 1) * LANE)
        zs[c, 0:HALO, :] = z_prev[:, cs]
        zs[c, HALO:HALO + tm, :] = z_main[:, cs]
        zs[c, HALO + tm:, :] = z_next[:, cs]
        for r in range(1, 8):
            zsh[r - 1, c] = zs[c, pl.ds(r, tm + 24), :]
    pad = CONV_K // 2
    half = tm // 2

    def conv_chunk(idx, carry):
        c = idx // 2
        base = pl.multiple_of((idx % 2) * half, half)
        acc = jnp.zeros((half, LANE), F32)
        for r in range(8):
            for a in range(4):
                tap = 8 * a + r - (HALO - pad)
                if 0 <= tap < CONV_K:
                    rows = pl.ds(base + 8 * a, half)
                    src = zs[c, rows, :] if r == 0 else zsh[r - 1, c, rows, :]
                    acc = acc + src * wdw_ref[c, tap:tap + 1, :]
        cb[c, pl.ds(base, half), :] = acc
        return carry

    lax.fori_loop(0, 2 * n_lane, conv_chunk, 0)
    z = jnp.concatenate([cb[c] for c in range(n_lane)], axis=1) + bdw_ref[...]
    z = _layernorm(z, gcl_ref[...], bcl_ref[...])
    z = z * jax.nn.sigmoid(z)
    o_b = jnp.dot(z.astype(BF16), wpw_ref[...], preferred_element_type=F32) + bpw_ref[...]
    o_ref[:, 0:cw] = o_b.astype(BF16)

    u = gm_ref[:, :cw].astype(F32)
    vg = _layernorm(gm_ref[:, cw:].astype(F32), gsl_ref[...], bsl_ref[...]).astype(BF16)
    hc = cw // GMLP_HEADS
    for n in range(tm // CHUNK):
        rows = slice(n * CHUNK, (n + 1) * CHUNK)
        for h in range(GMLP_HEADS):
            cols = slice(h * hc, (h + 1) * hc)
            mixed = jnp.dot(wsp_ref[h], vg[rows, cols], preferred_element_type=F32) + bsp_ref[:, cols]
            o_ref[rows, cw + h * hc:cw + (h + 1) * hc] = (u[rows, cols] * mixed).astype(BF16)

    ys[0:HALO, :] = prod(scp_ref) * keep_prev
    ys[HALO:HALO + tm, :] = prod(sc_ref)
    ys[HALO + tm:, :] = prod(scn_ref) * keep_next
    acc = jnp.zeros((tm, cw), F32)
    for tap in range(SC_K):
        acc = acc + ys[pl.ds(HALO - SC_K // 2 + tap, tm), :] * wsc_ref[tap:tap + 1, :]
    o_ref[:, 2 * cw:3 * cw] = (bg_ref[...].astype(F32) * acc).astype(BF16)


def _mixers(proj, weights, tp, seq, dec_seq, tm):
    t = proj.shape[0]
    cw = 512
    npt = tp // tm
    hb = tm // HALO
    last = t // HALO - 1
    const2 = lambda i: (0, 0)

    def main(col):
        return pl.BlockSpec((tm, 2 * cw), lambda i: (i, col))

    def prev(col):
        return pl.BlockSpec((HALO, 2 * cw), lambda i: (jnp.maximum(i * hb - 1, 0), col))

    def nxt(col):
        return pl.BlockSpec((HALO, 2 * cw), lambda i: (jnp.minimum((i + 1) * hb, last), col))

    w_specs = [pl.BlockSpec(w.shape, const2 if w.ndim == 2 else (lambda i: (0, 0, 0))) for w in weights]
    return pl.pallas_call(
        functools.partial(_mixers_kernel, npt, seq // tm, dec_seq // tm),
        grid=(t // tm,),
        in_specs=[main(0), prev(0), nxt(0), main(1), main(2), prev(2), nxt(2),
                  pl.BlockSpec((tm, cw), lambda i: (i, 6))] + w_specs,
        out_specs=pl.BlockSpec((tm, 3 * cw), lambda i: (i, 0)),
        out_shape=jax.ShapeDtypeStruct((t, 3 * cw), BF16),
        scratch_shapes=[pltpu.VMEM((cw // LANE, tm + 2 * HALO, LANE), F32), pltpu.VMEM((tm + 2 * HALO, cw), F32),
                        pltpu.VMEM((7, cw // LANE, tm + 24, LANE), F32), pltpu.VMEM((cw // LANE, tm, LANE), F32)],
        compiler_params=_cparams(("arbitrary",), 40),
        name="mixers",
    )(proj, proj, proj, proj, proj, proj, proj, proj, *weights)


def _out_proj_kernel(npt, n, oap_ref, oas_ref, ob_ref, xp_ref, xs_ref, mod_ref, g_ref, wo_ref, wr_ref,
                     xn_ref, hp_ref, r_ref, mix_a, mix_b):
    i = pl.program_id(0)
    ka = oap_ref.shape[1]
    mm_tile = jnp.minimum(i, n - 1)
    ep_tile = jnp.maximum(i - 1, 0)

    @pl.when(i == 0)
    def _():
        mix_b[...] = jnp.zeros_like(mix_b)

    def step(mix_w, mix_r):
        oa = jnp.where(mm_tile < npt, oap_ref[...], oas_ref[...])
        lhs = jnp.concatenate([oa, ob_ref[...]], axis=1)
        tm, d = mix_w.shape
        parts = OUT_PARTS
        for k in range(parts):
            cols = slice(k * d // parts, (k + 1) * d // parts)
            rows = slice(k * tm // parts, (k + 1) * tm // parts)
            mix_w[:, cols] = jnp.dot(lhs, wo_ref[:, cols], preferred_element_type=F32)
            x = jnp.where(ep_tile < npt, xp_ref[rows, :], xs_ref[rows, :])
            xn = x + mod_ref[2:3, :] * mix_r[rows, :]
            xn_ref[rows, :] = xn
            h = xn * _inv_rms(xn, xn.shape[-1]) * g_ref[...]
            h = h * (1.0 + mod_ref[4:5, :]) + mod_ref[3:4, :]
            hp_ref[rows, :] = _pack_rows(h)
            h_hi, h_lo = _split_bf16(h)
            r_ref[:, rows] = (lax.dot_general(wr_ref[...], h_hi, _NT, preferred_element_type=F32)
                              + lax.dot_general(wr_ref[...], h_lo, _NT, preferred_element_type=F32))

    @pl.when(i % 2 == 0)
    def _():
        step(mix_a, mix_b)

    @pl.when(i % 2 == 1)
    def _():
        step(mix_b, mix_a)


def _out_proj(oap, oas, ob, xp, xs, mod_l, g, wo, wr, tm, mod_index):
    t = ob.shape[0]
    d = xp.shape[1]
    npt = oap.shape[0] // tm
    n = t // tm
    const = lambda i: (0, 0)
    mm = lambda i: jnp.minimum(i, n - 1)
    ep = lambda i: jnp.maximum(i - 1, 0)

    def pair(w, tile, stacked=False):
        second = (lambda i: (jnp.maximum(tile(i), npt), 0)) if stacked else (
            lambda i: (jnp.maximum(tile(i) - npt, 0), 0))
        return [pl.BlockSpec((tm, w), lambda i: (jnp.minimum(tile(i), npt - 1), 0)), pl.BlockSpec((tm, w), second)]

    x_specs = pair(d, ep, stacked=xs is None)
    if xs is None:
        xs = xp
    return pl.pallas_call(
        functools.partial(_out_proj_kernel, npt, n),
        grid=(n + 1,),
        in_specs=pair(oap.shape[1], mm) + [pl.BlockSpec((tm, ob.shape[1]), lambda i: (mm(i), 0))] + x_specs + [
                  pl.BlockSpec((None, 6, d), lambda i: (mod_index(ep(i), tm), 0, 0)),
                  pl.BlockSpec((1, d), const),
                  pl.BlockSpec(wo.shape, const, pipeline_mode=pl.Buffered(1)),
                  pl.BlockSpec(wr.shape, const, pipeline_mode=pl.Buffered(1))],
        out_specs=[pl.BlockSpec((tm, d), lambda i: (ep(i), 0)),
                   pl.BlockSpec((tm, d // 2), lambda i: (ep(i), 0)),
                   pl.BlockSpec((LANE, tm), lambda i: (0, ep(i)))],
        out_shape=[jax.ShapeDtypeStruct((t, d), F32),
                   jax.ShapeDtypeStruct((t, d // 2), U32),
                   jax.ShapeDtypeStruct((LANE, t), F32)],
        scratch_shapes=[pltpu.VMEM((tm, d), F32), pltpu.VMEM((tm, d), F32)],
        compiler_params=_cparams(("arbitrary",), 56),
        name="out_proj",
    )(oap, oas, ob, xp, xs, mod_l, g, wo, wr)


N_BUCKET_PAD = 64
PAIR_FIRST = (0, 0, 0, 1, 1, 3)
PAIR_SECOND = (1, 2, 3, 3, 2, 2)


def _route_kernel(rt_ref, b_ref, tri_ref, bucket_ref, rank_ref, gate_ref, cnt_ref, run):
    i = pl.program_id(0)
    tm = rt_ref.shape[1]
    ne = N_GROUPS * GROUP_SIZE

    @pl.when(i == 0)
    def _():
        run[...] = jnp.zeros_like(run)

    sc = jax.nn.sigmoid(rt_ref[0:ne, :] + rt_ref[ne:2 * ne, :])
    sel = sc + b_ref[...]
    s = [sel[r * N_GROUPS:(r + 1) * N_GROUPS, :] for r in range(GROUP_SIZE)]
    c = [sc[r * N_GROUPS:(r + 1) * N_GROUPS, :] for r in range(GROUP_SIZE)]
    hi01, lo01 = jnp.maximum(s[0], s[1]), jnp.minimum(s[0], s[1])
    hi23, lo23 = jnp.maximum(s[2], s[3]), jnp.minimum(s[2], s[3])
    g_score = jnp.maximum(hi01, hi23) + jnp.maximum(jnp.minimum(hi01, hi23), jnp.maximum(lo01, lo23))
    grp = lax.broadcasted_iota(jnp.int32, g_score.shape, 0)
    g_best = jnp.min(jnp.where(g_score == jnp.max(g_score, axis=0, keepdims=True), grp, N_GROUPS),
                     axis=0, keepdims=True)
    own = grp == g_best
    v = [jnp.sum(jnp.where(own, s[r], 0.0), axis=0, keepdims=True) for r in range(GROUP_SIZE)]
    w = [jnp.sum(jnp.where(own, c[r], 0.0), axis=0, keepdims=True) for r in range(GROUP_SIZE)]

    def first_max(vals):
        best, idx, gate = vals[0], jnp.zeros_like(g_best), w[0]
        for r in range(1, GROUP_SIZE):
            upd = vals[r] > best
            best = jnp.where(upd, vals[r], best)
            idx = jnp.where(upd, r, idx)
            gate = jnp.where(upd, w[r], gate)
        return idx, gate

    r1, w1 = first_max(v)
    r2, w2 = first_max([jnp.where(r1 == r, -jnp.inf, v[r]) for r in range(GROUP_SIZE)])
    wsum = w1 + w2
    swap = r1 > r2
    r_lo = jnp.where(swap, r2, r1)
    r_hi = jnp.where(swap, r1, r2)
    g_lo = jnp.where(swap, w2, w1) / wsum
    g_hi = jnp.where(swap, w1, w2) / wsum
    lex = r_lo * 3 - ((r_lo * (r_lo - 1)) >> 1) + (r_hi - r_lo - 1)
    pair = jnp.where(lex == 3, 4, jnp.where(lex == 4, 3, lex))
    hi_first = pair == N_PAIRS - 1
    gate_ref[0:1, :] = jnp.where(hi_first, g_hi, g_lo)
    gate_ref[1:2, :] = jnp.where(hi_first, g_lo, g_hi)
    bucket = g_best * N_PAIRS + pair
    bucket_ref[...] = bucket

    onehot = (lax.broadcasted_iota(jnp.int32, (N_BUCKET_PAD, tm), 0) == bucket).astype(F32)
    before = jnp.dot(onehot.astype(BF16), tri_ref[...], preferred_element_type=F32) + run[...]
    rank_ref[...] = jnp.sum(onehot * before, axis=0, keepdims=True).astype(jnp.int32)
    run[...] += jnp.sum(onehot, axis=1, keepdims=True)

    @pl.when(i == pl.num_programs(0) - 1)
    def _():
        cnt_ref[...] = run[...]


def _route(rt, router_b, n_blk, tm):
    t = rt.shape[1]
    ne = router_b.shape[0]
    b_col = router_b.astype(F32).reshape(N_GROUPS, GROUP_SIZE).T.reshape(ne, 1)
    tri = (jnp.arange(tm)[:, None] < jnp.arange(tm)[None, :]).astype(BF16)
    bucket, rank, gate, cnt = pl.pallas_call(
        _route_kernel,
        grid=(t // tm,),
        in_specs=[pl.BlockSpec((LANE, tm), lambda i: (0, i)),
                  pl.BlockSpec((ne, 1), lambda i: (0, 0)),
                  pl.BlockSpec((tm, tm), lambda i: (0, 0))],
        out_specs=[pl.BlockSpec((1, tm), lambda i: (0, i)),
                   pl.BlockSpec((1, tm), lambda i: (0, i)),
                   pl.BlockSpec((2, tm), lambda i: (0, i)),
                   pl.BlockSpec((N_BUCKET_PAD, 1), lambda i: (0, 0))],
        out_shape=[jax.ShapeDtypeStruct((1, t), jnp.int32),
                   jax.ShapeDtypeStruct((1, t), jnp.int32),
                   jax.ShapeDtypeStruct((2, t), F32),
                   jax.ShapeDtypeStruct((N_BUCKET_PAD, 1), F32)],
        scratch_shapes=[pltpu.VMEM((N_BUCKET_PAD, 1), F32)],
        compiler_params=_cparams(("arbitrary",), 32),
        name="route",
    )(rt, b_col, tri)

    n_bucket = N_GROUPS * N_PAIRS
    counts = cnt[:n_bucket, 0].astype(jnp.int32)
    padded = (counts + MOE_BM - 1) // MOE_BM * MOE_BM
    pad_end = jnp.cumsum(padded)
    pad_start = pad_end - padded
    bucket = bucket[0]
    sel = bucket[:, None] == jnp.arange(n_bucket, dtype=jnp.int32)[None, :]
    pos = jnp.sum(jnp.where(sel, pad_start[None, :], 0), axis=1).astype(jnp.int32) + rank[0]
    n_used = (pad_end[-1] // MOE_BM).astype(jnp.int32)
    blk = jnp.minimum(jnp.arange(n_blk, dtype=jnp.int32), n_used - 1)
    blk_bucket = jnp.sum((pad_end[None, :] <= (blk * MOE_BM)[:, None]).astype(jnp.int32), axis=1)
    blk_bucket = jnp.minimum(blk_bucket, n_bucket - 1)
    pair_lo = jnp.array(PAIR_FIRST, jnp.int32)
    pair_hi = jnp.array(PAIR_SECOND, jnp.int32)
    blk_pair = blk_bucket % N_PAIRS
    first = (blk_bucket // N_PAIRS) * GROUP_SIZE
    is_pair = blk_pair[:, None] == jnp.arange(N_PAIRS, dtype=jnp.int32)[None, :]
    blk_a = first + jnp.sum(jnp.where(is_pair, pair_lo[None, :], 0), axis=1)
    blk_b = first + jnp.sum(jnp.where(is_pair, pair_hi[None, :], 0), axis=1)
    is_bucket = blk_bucket[:, None] == jnp.arange(n_bucket, dtype=jnp.int32)[None, :]
    per_bucket = lambda v: jnp.sum(jnp.where(is_bucket, v[None, :], 0), axis=1)
    first_blk = per_bucket(pad_start // MOE_BM)
    n_in_bucket = jnp.maximum(per_bucket(padded // MOE_BM), 1)
    src = first_blk + (blk - first_blk + n_in_bucket - 1) % n_in_bucket
    steps = jnp.arange(n_blk, dtype=jnp.int32)
    used = steps < n_used
    blk_out = jnp.where(used, src, steps).astype(jnp.int32)
    blk_rows = jnp.where(used, jnp.clip(per_bucket(pad_start + counts) - src * MOE_BM, 0, MOE_BM), 0)
    gates = jnp.pad(gate.T, ((0, 0), (0, GATE_W - 2)))
    plan = (src.astype(jnp.int32), blk_a.astype(jnp.int32), blk_b.astype(jnp.int32), blk_rows.astype(jnp.int32),
            blk_out)
    return pos, gates, plan


def _dispatch_kernel(pos_ref, hp_ref, gt_ref, init_ref, xs_ref, buf, sem):
    del init_ref
    i = pl.program_id(0)
    slot = i % 2
    tm = hp_ref.shape[0]
    w = hp_ref.shape[1]
    buf[slot, :, :w] = hp_ref[...]
    buf[slot, :, w:] = pltpu.bitcast(gt_ref[...], U32)

    def wait_all(s):
        pltpu.make_async_copy(buf.at[s], xs_ref.at[pl.ds(0, tm)], sem.at[s]).wait()

    for s in range(2):
        @pl.when(slot == s)
        def _():
            for r in range(tm):
                pltpu.make_async_copy(buf.at[s, pl.ds(r, 1)], xs_ref.at[pl.ds(pos_ref[0, 0, r], 1)],
                                      sem.at[s]).start(priority=r % N_DMA_PRIORITIES)

    @pl.when(i > 0)
    def _():
        wait_all(1 - slot)

    @pl.when(i == pl.num_programs(0) - 1)
    def _():
        wait_all(slot)


def _dispatch(hp, gates, pos, n_buf, tm):
    t, w = hp.shape
    init = jnp.zeros((n_buf, w + GATE_W), U32)
    return pl.pallas_call(
        _dispatch_kernel,
        grid=(t // tm,),
        in_specs=[pl.BlockSpec((1, 1, tm), lambda i: (i, 0, 0), memory_space=pltpu.SMEM),
                  pl.BlockSpec((tm, w), lambda i: (i, 0)),
                  pl.BlockSpec((tm, GATE_W), lambda i: (i, 0)),
                  pl.BlockSpec(memory_space=pl.ANY)],
        out_specs=pl.BlockSpec(memory_space=pl.ANY),
        out_shape=jax.ShapeDtypeStruct((n_buf, w + GATE_W), U32),
        scratch_shapes=[pltpu.VMEM((2, tm, w + GATE_W), U32), pltpu.SemaphoreType.DMA((2,))],
        input_output_aliases={3: 0},
        compiler_params=_cparams(("arbitrary",), 32),
        name="dispatch",
    )(pos.reshape(t // tm, 1, tm), hp, gates, init)


def _moe_kernel(src_ref, ea_ref, eb_ref, nv_ref, dst_ref, x_ref, w1a_ref, w3a_ref, w2a_ref, w1b_ref, w3b_ref,
                w2b_ref, y_ref):
    del src_ref, ea_ref, eb_ref, dst_ref
    nv = nv_ref[pl.program_id(0)]
    bm = x_ref.shape[0]
    w = x_ref.shape[1] - GATE_W

    def experts(m):
        xa, xb = _unpack_rows(x_ref[0:m, :w])
        x = jnp.concatenate([xa.astype(BF16), xb.astype(BF16)], axis=1)
        gates = pltpu.bitcast(x_ref[0:m, w:], F32)

        def hidden(w1_ref, w3_ref, g):
            a = jnp.dot(x, w1_ref[...], preferred_element_type=F32)
            b = jnp.dot(x, w3_ref[...], preferred_element_type=F32)
            return (a * jax.nn.sigmoid(a) * b * g).astype(BF16)

        y = jnp.dot(hidden(w1a_ref, w3a_ref, gates[:, 0:1]), w2a_ref[...], preferred_element_type=F32)
        y += jnp.dot(hidden(w1b_ref, w3b_ref, gates[:, 1:2]), w2b_ref[...], preferred_element_type=F32)
        y_ref[0:m, :] = _pack_rows(y)
        if m < bm:
            y_ref[m:, :] = jnp.zeros((bm - m, y_ref.shape[1]), y_ref.dtype)

    @pl.when(nv > bm // 2)
    def _():
        experts(bm)

    @pl.when(jnp.logical_and(nv > 0, nv <= bm // 2))
    def _():
        experts(bm // 2)

    @pl.when(nv == 0)
    def _():
        y_ref[...] = jnp.zeros_like(y_ref)


def _moe(xs, w1, w3, w2, plan):
    n_buf, wx = xs.shape
    n_blk = n_buf // MOE_BM
    _, d, de = w1.shape

    def expert(second, rows, cols):
        if second:
            return pl.BlockSpec((None, rows, cols), lambda i, src, ea, eb, nv, dst: (eb[i], 0, 0))
        return pl.BlockSpec((None, rows, cols), lambda i, src, ea, eb, nv, dst: (ea[i], 0, 0))

    grid_spec = pltpu.PrefetchScalarGridSpec(
        num_scalar_prefetch=5,
        grid=(n_blk,),
        in_specs=[pl.BlockSpec((MOE_BM, wx), lambda i, src, ea, eb, nv, dst: (src[i], 0)),
                  expert(False, d, de), expert(False, d, de), expert(False, de, d),
                  expert(True, d, de), expert(True, d, de), expert(True, de, d)],
        out_specs=pl.BlockSpec((MOE_BM, d // 2), lambda i, src, ea, eb, nv, dst: (dst[i], 0)),
    )
    return pl.pallas_call(
        _moe_kernel,
        grid_spec=grid_spec,
        out_shape=jax.ShapeDtypeStruct((n_buf, d // 2), U32),
        compiler_params=_cparams(("arbitrary",), 48),
        name="moe",
    )(*plan, xs, w1, w3, w2, w1, w3, w2)


def _combine_kernel(npt, pos_ref, nxt_ref, xn_ref, mod_ref, yb_ref, op_ref, os_ref, buf, sem):
    i = pl.program_id(0)
    out = xn_ref[...] + mod_ref[5:6, :] * _gathered_rows(pos_ref, nxt_ref, yb_ref, buf, sem)

    @pl.when(i < npt)
    def _():
        op_ref[...] = out

    @pl.when(i >= npt)
    def _():
        os_ref[...] = out


def _combine(xn, yb, pos, mod_l, tp, tm, mod_index):
    t, d = xn.shape
    npt = tp // tm
    n = t // tm
    pos3 = pos.reshape(n, 1, tm)
    return pl.pallas_call(
        functools.partial(_combine_kernel, npt),
        grid=(n,),
        in_specs=[pl.BlockSpec((1, 1, tm), lambda i: (i, 0, 0), memory_space=pltpu.SMEM),
                  pl.BlockSpec((1, 1, tm), lambda i: (jnp.minimum(i + 1, n - 1), 0, 0), memory_space=pltpu.SMEM),
                  pl.BlockSpec((tm, d), lambda i: (i, 0)),
                  pl.BlockSpec((None, 6, d), lambda i: (mod_index(i, tm), 0, 0)),
                  pl.BlockSpec(memory_space=pl.ANY)],
        out_specs=[pl.BlockSpec((tm, d), lambda i: (jnp.minimum(i, npt - 1), 0)),
                   pl.BlockSpec((tm, d), lambda i: (jnp.maximum(i - npt, 0), 0))],
        out_shape=[jax.ShapeDtypeStruct((tp, d), F32),
                   jax.ShapeDtypeStruct((t - tp, d), F32)],
        scratch_shapes=[pltpu.VMEM((2, tm, d // 2), U32), pltpu.SemaphoreType.DMA((2,))],
        compiler_params=_cparams(("arbitrary",), 32),
        name="combine",
    )(pos3, pos3, xn, mod_l, yb)


def _pad_cols(x, n):
    return jnp.pad(x, ((0, 0), (0, n - x.shape[1])))


def _rope_tables(dec_seq, tm):
    n_freq = MLA_ROPE // 4
    pos = jnp.arange(dec_seq, dtype=jnp.int32)
    row = (pos // GRID_W).astype(F32)
    col = (pos % GRID_W).astype(F32)
    inv = ROPE_THETA ** (-jnp.arange(n_freq, dtype=F32) / n_freq)
    ar = row[:, None] * inv[None, :]
    ac = col[:, None] * inv[None, :]
    ones = jnp.ones((dec_seq, LANE - MLA_ROPE), F32)
    cos = jnp.concatenate([jnp.cos(ar), jnp.cos(ar), jnp.cos(ac), jnp.cos(ac), ones], axis=1)
    sin = jnp.concatenate([-jnp.sin(ar), jnp.sin(ar), -jnp.sin(ac), jnp.sin(ac), 0.0 * ones], axis=1)
    ident_c = jnp.ones((tm, LANE), F32)
    ident_s = jnp.zeros((tm, LANE), F32)
    return jnp.concatenate([cos, ident_c], axis=0), jnp.concatenate([sin, ident_s], axis=0)


def _layer_params(l, q_lora, kv_lora, w_in, g_q_lora, w_uq, g_kv_lora, w_ukv, g_qk_q, g_qk_k,
                  w_dw31, b_dw31, g_conv_ln, b_conv_ln, w_pw, b_pw, g_sgu_ln, b_sgu_ln,
                  w_spatial, b_spatial, w_sc3, w_out):
    d = w_in.shape[1]
    o1 = q_lora
    o2 = o1 + kv_lora
    o3 = o2 + MLA_ROPE
    o4 = o3 + 1024
    o5 = o4 + 1024
    w = w_in[l]
    zeros = lambda n: jnp.zeros((d, n), w.dtype)
    w_in_p = jnp.concatenate(
        [w[:, o3:o4], w[:, o4:o5], w[:, o5 + 512:o5 + 1536], w[:, o5:o5 + 512],
         w[:, :o1], zeros(512 - q_lora),
         w[:, o1:o2], w[:, o2:o3], zeros(PROJ_W - KV_COL - kv_lora - MLA_ROPE)], axis=1).astype(BF16)

    uq = w_uq[l].reshape(q_lora, MLA_HEADS, MLA_QK)
    uq = jnp.pad(uq, ((0, 512 - q_lora), (0, 0), (0, HEAD_PAD - MLA_QK)))
    w_uq_p = uq.reshape(512, MLA_HEADS * HEAD_PAD).astype(BF16)
    ukv = w_ukv[l].reshape(kv_lora, MLA_HEADS, MLA_NOPE + MLA_V)
    w_ukv_p = jnp.concatenate([ukv[:, :, :MLA_NOPE].reshape(kv_lora, -1),
                               ukv[:, :, MLA_NOPE:].reshape(kv_lora, -1)], axis=1).astype(BF16)
    row = lambda v: v.reshape(1, -1).astype(F32)
    qkv_w = (w_uq_p, w_ukv_p, _pad_cols(row(g_q_lora[l]), 512), row(g_kv_lora[l]),
             _pad_cols(row(g_qk_q[l]), HEAD_PAD), _pad_cols(row(g_qk_k[l]), HEAD_PAD))

    hc = w_pw.shape[1] // GMLP_HEADS
    bsp = jnp.repeat(b_spatial[l].T.astype(F32), hc, axis=1)
    w_dw = jnp.pad(w_dw31[l].astype(F32), ((0, 32 - CONV_K), (0, 0)))
    w_dw = w_dw.reshape(32, -1, LANE).transpose(1, 0, 2)
    mix_w = (w_dw, row(b_dw31[l]),
             row(g_conv_ln[l]), row(b_conv_ln[l]), w_pw[l].astype(BF16), row(b_pw[l]),
             row(g_sgu_ln[l]), row(b_sgu_ln[l]), w_spatial[l].astype(BF16), bsp,
             jnp.pad(w_sc3[l].astype(F32), ((0, 8 - SC_K), (0, 0))))
    return w_in_p, qkv_w, mix_w, w_out[l].astype(BF16)


def _tile(limit, *sizes):
    tm = limit
    while any(s % tm for s in sizes):
        tm //= 2
    return tm


def kernel(x_prompt, x_sample, cache_mla_ckv, cache_mla_krope, c, c_ctx, w_mod, b_mod, g_norm1, g_norm2, w_in, g_q_lora, w_uq, g_kv_lora, w_ukv, g_qk_q, g_qk_k, w_dw31, b_dw31, g_conv_ln, b_conv_ln, w_pw, b_pw, g_sgu_ln, b_sgu_ln, w_spatial, b_spatial, w_sc3, w_out, router_w, router_b, w1, w3, w2):
    nseq, seq, d = x_prompt.shape
    nb, dec_seq, _ = x_sample.shape
    depth = w_mod.shape[0]
    past = cache_mla_ckv.shape[2]
    q_lora = g_q_lora.shape[1]
    kv_lora = g_kv_lora.shape[1]
    n_exp = router_w.shape[1]
    tp = nseq * seq
    ts = nb * dec_seq
    t = tp + ts
    assert tp % dec_seq == 0 and seq % CHUNK == 0 and dec_seq % GRID_W == 0
    assert nb + 1 <= 8 and kv_lora == MLA_NOPE and n_exp == N_GROUPS * GROUP_SIZE

    tm_in = _tile(512, tp, dec_seq)
    tm_qkv = _tile(512, tp, dec_seq)
    tm_row = _tile(256, seq, dec_seq)
    tq = _tile(1024, dec_seq)

    def mod_index(i, tm):
        npt = tp // tm
        return jnp.where(i < npt, 0, 1 + (i - npt) // (dec_seq // tm))

    cvec = jnp.concatenate([c_ctx[None, :], c, jnp.zeros((8 - 1 - nb, d), F32)], axis=0)
    mod = _modulation(cvec, w_mod, b_mod).reshape(depth, 8, 6, d)
    cos_t, sin_t = _rope_tables(dec_seq, tm_qkv)

    rw = router_w.T.reshape(N_GROUPS, GROUP_SIZE, d).transpose(1, 0, 2).reshape(n_exp, d)
    rw_hi = rw.astype(BF16)
    rw_lo = (rw - rw_hi.astype(F32)).astype(BF16)
    wr = jnp.pad(jnp.concatenate([rw_hi, rw_lo], axis=0), ((0, LANE - 2 * n_exp), (0, 0)))
    n_blk = (t + N_GROUPS * N_PAIRS * (MOE_BM - 1) + MOE_BM - 1) // MOE_BM

    xp = x_prompt.reshape(tp, d)
    xs = x_sample.reshape(ts, d)
    ckv_states = []
    krope_states = []
    for l in range(depth):
        w_in_p, qkv_w, mix_w, w_out_b = _layer_params(
            l, q_lora, kv_lora, w_in, g_q_lora, w_uq, g_kv_lora, w_ukv, g_qk_q, g_qk_k,
            w_dw31, b_dw31, g_conv_ln, b_conv_ln, w_pw, b_pw, g_sgu_ln, b_sgu_ln,
            w_spatial, b_spatial, w_sc3, w_out)
        mod_l = mod[l]

        if l == 0:
            proj, kv = _in_proj(xp, xs, mod_l, g_norm1[l].reshape(1, d), w_in_p, tm_in, mod_index)
        else:
            proj, kv, xp = _in_proj_fused(xn, yb, pos, mod[l - 1], mod_l, g_norm1[l].reshape(1, d), w_in_p,
                                          tm_in, mod_index)
            xs = None
        q, k, v, state = _qkv(proj, kv, cos_t, sin_t, *qkv_w, tp, dec_seq, q_lora, tm_qkv)
        kvc = jnp.concatenate([cache_mla_ckv[:, l], cache_mla_krope[:, l],
                               jnp.zeros((nb, past, KV_W - kv_lora - MLA_ROPE), F32)], axis=-1)
        kc, vc = _ctx_kv(kvc.reshape(nb * past, KV_W), qkv_w[1], qkv_w[5], _tile(512, past))
        o_as, w1b, w3b, w2b = _attention_latent(q, k, v, kc, vc, w1, w3, w2, l, tp, nb, dec_seq, past, tq)
        o_ap = _attention_context(q, k, v, nseq, seq)
        o_bcd = _mixers(proj, mix_w, tp, seq, dec_seq, tm_row)

        xn, hp, r = _out_proj(o_ap, o_as, o_bcd, xp, xs, mod_l, g_norm2[l].reshape(1, d), w_out_b, wr,
                              tm_qkv, mod_index)
        pos, gates, plan = _route(r, router_b, n_blk, tm_qkv)
        xd = _dispatch(hp, gates, pos, n_blk * MOE_BM, tm_row)
        yb = _moe(xd, w1b, w3b, w2b, plan)
        if l == depth - 1:
            xp, xs = _combine(xn, yb, pos, mod_l, tp, tm_row, mod_index)

        ckv_states.append(state[:, :kv_lora].reshape(nseq, seq, kv_lora))
        krope_states.append(state[:, kv_lora:kv_lora + MLA_ROPE].reshape(nseq, seq, MLA_ROPE))

    return (xp.reshape(nseq, seq, d), xs.reshape(nb, dec_seq, d),
            jnp.stack(ckv_states, axis=1), jnp.stack(krope_states, axis=1))
```

```python
import functools

import jax
import jax.numpy as jnp
from jax import lax
from jax.experimental import pallas as pl
from jax.experimental.pallas import tpu as pltpu

F32 = jnp.float32
BF16 = jnp.bfloat16
U32 = jnp.uint32

MLA_HEADS = 4
MLA_NOPE = 128
MLA_ROPE = 64
MLA_V = 128
MLA_QK = MLA_NOPE + MLA_ROPE
HEAD_PAD = 256
GRID_W = 64
ROPE_THETA = 10000.0
CONV_K = 31
SC_K = 3
CHUNK = 128
GMLP_HEADS = 4
N_GROUPS = 8
GROUP_SIZE = 4
N_PAIRS = 6
NORM_EPS = 1e-6

LANE = 128
HALO = 16
PROJ_W = 4352
CQ_COL = 3584
KV_COL = 4096
KV_W = 256
IN_TN = 1536
MOE_BM = 256
OUT_PARTS = 4
N_DMA_PRIORITIES = 2
ATTN_QSUB = 1024
ATTN_KCHUNK = 1024
LOG2_E = 1.4426950408889634
GATE_W = 128
X_ROWS = 9
Y_ROWS = 8
VMEM_CAP = 56 * 1024 * 1024


def _cparams(sem, vmem_mb):
    return pltpu.CompilerParams(dimension_semantics=sem,
                                vmem_limit_bytes=min(vmem_mb * 1024 * 1024, VMEM_CAP))


def _inv_rms(x, n):
    return lax.rsqrt(jnp.sum(x * x, axis=-1, keepdims=True) * (1.0 / n) + NORM_EPS)


def _layernorm(x, g, b):
    mu = jnp.mean(x, axis=-1, keepdims=True)
    xc = x - mu
    var = jnp.mean(xc * xc, axis=-1, keepdims=True)
    return xc * lax.rsqrt(var + NORM_EPS) * g + b


def _split_bf16(x):
    hi = x.astype(BF16)
    lo = (x - hi.astype(F32)).astype(BF16)
    return hi, lo


def _pack_rows(x):
    n = x.shape[1] // 2
    hi = pltpu.bitcast(x[:, :n].astype(BF16).astype(F32), U32)
    lo = pltpu.bitcast(x[:, n:].astype(BF16).astype(F32), U32)
    return (hi & jnp.uint32(0xFFFF0000)) | (lo >> 16)


def _unpack_rows(u):
    a = pltpu.bitcast(u & jnp.uint32(0xFFFF0000), F32)
    b = pltpu.bitcast(u << 16, F32)
    return a, b


def _mod_kernel(c_ref, w_ref, b_ref, o_ref):
    c = c_ref[...]
    a_hi, a_lo = _split_bf16(c * jax.nn.sigmoid(c))
    w_hi, w_lo = _split_bf16(w_ref[...])
    acc = jnp.dot(a_hi, w_hi, preferred_element_type=F32)
    acc += jnp.dot(a_lo, w_hi, preferred_element_type=F32)
    acc += jnp.dot(a_hi, w_lo, preferred_element_type=F32)
    o_ref[...] = acc + b_ref[...]


def _modulation(cvec, w_mod, b_mod):
    depth, d, n = w_mod.shape
    tn = 1024
    return pl.pallas_call(
        _mod_kernel,
        grid=(depth, n // tn),
        in_specs=[pl.BlockSpec((8, d), lambda l, j: (0, 0)),
                  pl.BlockSpec((None, d, tn), lambda l, j: (l, 0, j)),
                  pl.BlockSpec((None, 1, tn), lambda l, j: (l, 0, j))],
        out_specs=pl.BlockSpec((None, 8, tn), lambda l, j: (l, 0, j)),
        out_shape=jax.ShapeDtypeStruct((depth, 8, n), F32),
        compiler_params=_cparams(("arbitrary", "arbitrary"), 40),
        name="modulation",
    )(cvec, w_mod, b_mod.reshape(depth, 1, n))


def _norm_project(x, mod_ref, g_ref, w_ref, proj_ref, kv_ref):
    y = x * _inv_rms(x, x.shape[-1]) * g_ref[...]
    h = (y * (1.0 + mod_ref[1:2, :]) + mod_ref[0:1, :]).astype(BF16)
    for c0 in range(0, PROJ_W, IN_TN):
        c1 = min(c0 + IN_TN, PROJ_W)
        acc = jnp.dot(h, w_ref[:, c0:c1], preferred_element_type=F32)
        proj_ref[:, c0:c1] = acc.astype(BF16)
        if c0 <= KV_COL < c1:
            kv_ref[...] = acc[:, KV_COL - c0:KV_COL - c0 + KV_W]


def _in_proj_kernel(npt, xp_ref, xs_ref, mod_ref, g_ref, w_ref, proj_ref, kv_ref):
    x = jnp.where(pl.program_id(0) < npt, xp_ref[...], xs_ref[...])
    _norm_project(x, mod_ref, g_ref, w_ref, proj_ref, kv_ref)


def _gathered_rows(pos_ref, nxt_ref, yb_ref, buf, sem):
    i = pl.program_id(0)
    slot = i % 2
    tm = buf.shape[1] // Y_ROWS

    def gather(idx_ref, s):
        for r in range(tm):
            src = pl.multiple_of(idx_ref[0, 0, r] * Y_ROWS, Y_ROWS)
            pltpu.make_async_copy(yb_ref.at[pl.ds(src, Y_ROWS)], buf.at[s, pl.ds(r * Y_ROWS, Y_ROWS)],
                                  sem.at[s]).start(priority=r % N_DMA_PRIORITIES)

    @pl.when(i == 0)
    def _():
        gather(pos_ref, 0)

    for s in range(2):
        @pl.when(jnp.logical_and(i + 1 < pl.num_programs(0), slot == 1 - s))
        def _():
            gather(nxt_ref, s)

    pltpu.make_async_copy(yb_ref.at[pl.ds(0, tm * Y_ROWS)], buf.at[slot], sem.at[slot]).wait()
    halves = [_unpack_rows(buf[slot, pl.ds(j, tm, stride=Y_ROWS), :]) for j in range(Y_ROWS)]
    return jnp.concatenate([a for a, _ in halves] + [b for _, b in halves], axis=1)


def _in_proj_fused_kernel(pos_ref, nxt_ref, xn_ref, prev_mod_ref, mod_ref, g_ref, w_ref, yb_ref,
                          proj_ref, kv_ref, x_ref, buf, sem):
    x = xn_ref[...] + prev_mod_ref[5:6, :] * _gathered_rows(pos_ref, nxt_ref, yb_ref, buf, sem)
    x_ref[...] = x
    _norm_project(x, mod_ref, g_ref, w_ref, proj_ref, kv_ref)


def _in_proj_fused(xn, yb, pos, prev_mod, mod_l, g, w, tm, mod_index):
    t, d = xn.shape
    n = t // tm
    pos3 = pos.reshape(n, 1, tm)
    mod_spec = pl.BlockSpec((None, 6, d), lambda i: (mod_index(i, tm), 0, 0))
    return pl.pallas_call(
        _in_proj_fused_kernel,
        grid=(n,),
        in_specs=[pl.BlockSpec((1, 1, tm), lambda i: (i, 0, 0), memory_space=pltpu.SMEM),
                  pl.BlockSpec((1, 1, tm), lambda i: (jnp.minimum(i + 1, n - 1), 0, 0), memory_space=pltpu.SMEM),
                  pl.BlockSpec((tm, d), lambda i: (i, 0)),
                  mod_spec, mod_spec,
                  pl.BlockSpec((1, d), lambda i: (0, 0)),
                  pl.BlockSpec(w.shape, lambda i: (0, 0), pipeline_mode=pl.Buffered(1)),
                  pl.BlockSpec(memory_space=pl.ANY)],
        out_specs=[pl.BlockSpec((tm, PROJ_W), lambda i: (i, 0)),
                   pl.BlockSpec((tm, KV_W), lambda i: (i, 0)),
                   pl.BlockSpec((tm, d), lambda i: (i, 0))],
        out_shape=[jax.ShapeDtypeStruct((t, PROJ_W), BF16),
                   jax.ShapeDtypeStruct((t, KV_W), F32),
                   jax.ShapeDtypeStruct((t, d), F32)],
        scratch_shapes=[pltpu.VMEM((2, tm * Y_ROWS, LANE), U32), pltpu.SemaphoreType.DMA((2,))],
        compiler_params=_cparams(("arbitrary",), 56),
        name="in_proj_fused",
    )(pos3, pos3, xn, prev_mod, mod_l, g, w, yb)


def _in_proj(xp, xs, mod_l, g, w, tm, mod_index):
    tp, d = xp.shape
    t = tp + xs.shape[0]
    npt = tp // tm
    return pl.pallas_call(
        functools.partial(_in_proj_kernel, npt),
        grid=(t // tm,),
        in_specs=[pl.BlockSpec((tm, d), lambda i: (jnp.minimum(i, npt - 1), 0)),
                  pl.BlockSpec((tm, d), lambda i: (jnp.maximum(i - npt, 0), 0)),
                  pl.BlockSpec((None, 6, d), lambda i: (mod_index(i, tm), 0, 0)),
                  pl.BlockSpec((1, d), lambda i: (0, 0)),
                  pl.BlockSpec(w.shape, lambda i: (0, 0), pipeline_mode=pl.Buffered(1))],
        out_specs=[pl.BlockSpec((tm, PROJ_W), lambda i: (i, 0)),
                   pl.BlockSpec((tm, KV_W), lambda i: (i, 0))],
        out_shape=[jax.ShapeDtypeStruct((t, PROJ_W), BF16),
                   jax.ShapeDtypeStruct((t, KV_W), F32)],
        compiler_params=_cparams(("arbitrary",), 56),
        name="in_proj",
    )(xp, xs, mod_l, g, w)


def _rope_tile(t, cos, sin):
    lane = lax.broadcasted_iota(jnp.int32, t.shape, 1)
    first = (lane & 16) == 0
    swapped = jnp.where(first, pltpu.roll(t, LANE - 16, 1), pltpu.roll(t, 16, 1))
    return t * cos + swapped * sin


def _keys_values(ckv_n, krope, w_ukv_ref, gk_ref, cos, sin, k_ref, v_ref):
    kvf = jnp.dot(ckv_n.astype(BF16), w_ukv_ref[...], preferred_element_type=F32)
    kr_ss = jnp.sum(krope * krope, axis=-1, keepdims=True)
    gk = gk_ref[...]
    tail = krope * gk[:, MLA_NOPE:]
    if cos is not None:
        tail = _rope_tile(tail, cos, sin)
    for h in range(MLA_HEADS):
        kn = kvf[:, h * MLA_NOPE:(h + 1) * MLA_NOPE]
        r = lax.rsqrt((jnp.sum(kn * kn, axis=-1, keepdims=True) + kr_ss) * (1.0 / MLA_QK) + NORM_EPS)
        k_ref[:, h * HEAD_PAD:h * HEAD_PAD + MLA_NOPE] = (kn * r * gk[:, :MLA_NOPE]).astype(BF16)
        k_ref[:, h * HEAD_PAD + MLA_NOPE:(h + 1) * HEAD_PAD] = (tail * r).astype(BF16)
    v_ref[...] = kvf[:, MLA_HEADS * MLA_NOPE:].astype(BF16)


def _qkv_kernel(npt, q_lora, cq_ref, kv_ref, cos_ref, sin_ref, w_uq_ref, w_ukv_ref, gql_ref, gkv_ref,
                gq_ref, gk_ref, q_ref, k_ref, v_ref, st_ref):
    i = pl.program_id(0)
    cos = cos_ref[...]
    sin = sin_ref[...]
    cq = cq_ref[...].astype(F32)
    cqn = cq * _inv_rms(cq, q_lora) * gql_ref[...]
    qf = jnp.dot(cqn.astype(BF16), w_uq_ref[...], preferred_element_type=F32)
    gq = gq_ref[...]
    scale = MLA_QK ** -0.5 * LOG2_E
    for h in range(MLA_HEADS):
        qh = qf[:, h * HEAD_PAD:(h + 1) * HEAD_PAD]
        qh = qh * (_inv_rms(qh, MLA_QK) * scale) * gq
        q_ref[:, h * HEAD_PAD:h * HEAD_PAD + MLA_NOPE] = qh[:, :MLA_NOPE].astype(BF16)
        q_ref[:, h * HEAD_PAD + MLA_NOPE:(h + 1) * HEAD_PAD] = _rope_tile(qh[:, MLA_NOPE:], cos, sin).astype(BF16)

    kv = kv_ref[...]
    ckv = kv[:, :MLA_NOPE]
    krope = kv[:, MLA_NOPE:]
    ckv_n = ckv * _inv_rms(ckv, ckv.shape[-1]) * gkv_ref[...]
    _keys_values(ckv_n, krope, w_ukv_ref, gk_ref, cos, sin, k_ref, v_ref)

    @pl.when(i < npt)
    def _():
        st_ref[:, :MLA_NOPE] = ckv_n
        st_ref[:, MLA_NOPE:] = krope


def _qkv(proj, kv, cos_t, sin_t, w_uq, w_ukv, gql, gkv, gq, gk, tp, dec_seq, q_lora, tm):
    t = proj.shape[0]
    npt = tp // tm
    nseq = dec_seq // tm
    cq_blk = CQ_COL // 512

    def tab(i):
        return (jnp.where(i < npt, nseq, (i - npt) % nseq), 0)

    const = lambda i: (0, 0)
    return pl.pallas_call(
        functools.partial(_qkv_kernel, npt, q_lora),
        grid=(t // tm,),
        in_specs=[pl.BlockSpec((tm, 512), lambda i: (i, cq_blk)),
                  pl.BlockSpec((tm, KV_W), lambda i: (i, 0)),
                  pl.BlockSpec((tm, LANE), tab),
                  pl.BlockSpec((tm, LANE), tab),
                  pl.BlockSpec(w_uq.shape, const),
                  pl.BlockSpec(w_ukv.shape, const),
                  pl.BlockSpec(gql.shape, const),
                  pl.BlockSpec(gkv.shape, const),
                  pl.BlockSpec(gq.shape, const),
                  pl.BlockSpec(gk.shape, const)],
        out_specs=[pl.BlockSpec((tm, MLA_HEADS * HEAD_PAD), lambda i: (i, 0)),
                   pl.BlockSpec((tm, MLA_HEADS * HEAD_PAD), lambda i: (i, 0)),
                   pl.BlockSpec((tm, MLA_HEADS * MLA_V), lambda i: (i, 0)),
                   pl.BlockSpec((tm, KV_W), lambda i: (jnp.minimum(i, npt - 1), 0))],
        out_shape=[jax.ShapeDtypeStruct((t, MLA_HEADS * HEAD_PAD), BF16),
                   jax.ShapeDtypeStruct((t, MLA_HEADS * HEAD_PAD), BF16),
                   jax.ShapeDtypeStruct((t, MLA_HEADS * MLA_V), BF16),
                   jax.ShapeDtypeStruct((tp, KV_W), F32)],
        compiler_params=_cparams(("arbitrary",), 40),
        name="qkv",
    )(proj, kv, cos_t, sin_t, w_uq, w_ukv, gql, gkv, gq, gk)


def _ctx_kv_kernel(kv_ref, w_ukv_ref, gk_ref, k_ref, v_ref):
    kv = kv_ref[...]
    _keys_values(kv[:, :MLA_NOPE], kv[:, MLA_NOPE:], w_ukv_ref, gk_ref, None, None, k_ref, v_ref)


def _ctx_kv(kvc, w_ukv, gk, tm):
    r = kvc.shape[0]
    const = lambda i: (0, 0)
    return pl.pallas_call(
        _ctx_kv_kernel,
        grid=(r // tm,),
        in_specs=[pl.BlockSpec((tm, KV_W), lambda i: (i, 0)),
                  pl.BlockSpec(w_ukv.shape, const),
                  pl.BlockSpec(gk.shape, const)],
        out_specs=[pl.BlockSpec((tm, MLA_HEADS * HEAD_PAD), lambda i: (i, 0)),
                   pl.BlockSpec((tm, MLA_HEADS * MLA_V), lambda i: (i, 0))],
        out_shape=[jax.ShapeDtypeStruct((r, MLA_HEADS * HEAD_PAD), BF16),
                   jax.ShapeDtypeStruct((r, MLA_HEADS * MLA_V), BF16)],
        compiler_params=_cparams(("arbitrary",), 32),
        name="ctx_kv",
    )(kvc, w_ukv, gk)


_NT = (((1,), (1,)), ((), ()))


def _attn_kernel(has_ctx, *refs):
    if has_ctx:
        q_ref, kc_ref, vc_ref, k_ref, v_ref, w1_ref, w3_ref, w2_ref, o_ref, w1o_ref, w3o_ref, w2o_ref = refs
        w1o_ref[...] = w1_ref[...].astype(BF16)
        w3o_ref[...] = w3_ref[...].astype(BF16)
        w2o_ref[...] = w2_ref[...].astype(BF16)
    else:
        q_ref, k_ref, v_ref, o_ref = refs
    n_own = k_ref.shape[0]
    step = min(n_own, ATTN_KCHUNK)
    chunks = [(k_ref, v_ref, c * step, step) for c in range(n_own // step)]
    if has_ctx:
        chunks = [(kc_ref, vc_ref, 0, kc_ref.shape[0])] + chunks
    for h in range(q_ref.shape[1] // HEAD_PAD):
        qk = slice(h * HEAD_PAD, (h + 1) * HEAD_PAD)
        hv = slice(h * MLA_V, (h + 1) * MLA_V)
        tq = q_ref.shape[0]
        sub = min(tq, ATTN_QSUB)
        state = [None] * (tq // sub)
        for kr, vr, off, n in chunks:
            for t in range(tq // sub):
                rows = slice(t * sub, (t + 1) * sub)
                s = lax.dot_general(q_ref[rows, qk], kr[off:off + n, qk], _NT, preferred_element_type=F32)
                mc = jnp.max(s, axis=-1, keepdims=True)
                m_new = mc if state[t] is None else jnp.maximum(state[t][0], mc)
                p = jnp.exp2(s - m_new)
                pv = jnp.dot(p.astype(BF16), vr[off:off + n, hv], preferred_element_type=F32)
                ps = jnp.sum(p, axis=-1, keepdims=True)
                if state[t] is None:
                    state[t] = (m_new, ps, pv)
                else:
                    m, l, acc = state[t]
                    alpha = jnp.exp2(m - m_new)
                    state[t] = (m_new, alpha * l + ps, alpha * acc + pv)
        for t in range(tq // sub):
            _, l, acc = state[t]
            o_ref[t * sub:(t + 1) * sub, hv] = (acc / l).astype(BF16)


def _attention_latent(q, k, v, kc, vc, w1, w3, w2, layer, tp, nb, dec_seq, past, tq):
    t = q.shape[0]
    nq = dec_seq // tq
    row0 = tp // tq
    kblk0 = tp // dec_seq
    steps = nb * MLA_HEADS * nq
    depth, n_exp, d, de = w1.shape
    flat = [w1.reshape(depth, n_exp * d, de), w3.reshape(depth, n_exp * d, de), w2.reshape(depth, n_exp * de, d)]
    assert all(w.shape[1] % (8 * steps) == 0 for w in flat)
    step = lambda b, h, i: (b * MLA_HEADS + h) * nq + i
    w_in = [pl.BlockSpec((None, w.shape[1] // steps, w.shape[2]), lambda b, h, i: (layer, step(b, h, i), 0))
            for w in flat]
    w_out = [pl.BlockSpec((w.shape[1] // steps, w.shape[2]), lambda b, h, i: (step(b, h, i), 0)) for w in flat]
    o, w1b, w3b, w2b = pl.pallas_call(
        functools.partial(_attn_kernel, True),
        grid=(nb, MLA_HEADS, nq),
        in_specs=[pl.BlockSpec((tq, HEAD_PAD), lambda b, h, i: (row0 + b * nq + i, h)),
                  pl.BlockSpec((past, HEAD_PAD), lambda b, h, i: (b, h)),
                  pl.BlockSpec((past, MLA_V), lambda b, h, i: (b, h)),
                  pl.BlockSpec((dec_seq, HEAD_PAD), lambda b, h, i: (kblk0 + b, h)),
                  pl.BlockSpec((dec_seq, MLA_V), lambda b, h, i: (kblk0 + b, h))] + w_in,
        out_specs=[pl.BlockSpec((tq, MLA_V), lambda b, h, i: (b * nq + i, h))] + w_out,
        out_shape=[jax.ShapeDtypeStruct((t - tp, MLA_HEADS * MLA_V), BF16)]
                  + [jax.ShapeDtypeStruct(w.shape[1:], BF16) for w in flat],
        compiler_params=_cparams(("arbitrary", "arbitrary", "arbitrary"), 56),
        name="attn_latent",
    )(q, kc, vc, k, v, *flat)
    return o, w1b.reshape(n_exp, d, de), w3b.reshape(n_exp, d, de), w2b.reshape(n_exp, de, d)


def _attention_context(q, k, v, nseq, seq):
    return pl.pallas_call(
        functools.partial(_attn_kernel, False),
        grid=(nseq,),
        in_specs=[pl.BlockSpec((seq, MLA_HEADS * HEAD_PAD), lambda b: (b, 0)),
                  pl.BlockSpec((seq, MLA_HEADS * HEAD_PAD), lambda b: (b, 0)),
                  pl.BlockSpec((seq, MLA_HEADS * MLA_V), lambda b: (b, 0))],
        out_specs=pl.BlockSpec((seq, MLA_HEADS * MLA_V), lambda b: (b, 0)),
        out_shape=jax.ShapeDtypeStruct((nseq * seq, MLA_HEADS * MLA_V), BF16),
        compiler_params=_cparams(("arbitrary",), 32),
        name="attn_context",
    )(q, k, v)


def _mixers_kernel(npt, p_tiles, s_tiles,
                   cv_ref, cvp_ref, cvn_ref, gm_ref, sc_ref, scp_ref, scn_ref, bg_ref,
                   wdw_ref, bdw_ref, gcl_ref, bcl_ref, wpw_ref, bpw_ref,
                   gsl_ref, bsl_ref, wsp_ref, bsp_ref, wsc_ref,
                   o_ref, zs, ys, zsh, cb):
    i = pl.program_id(0)
    tm = cv_ref.shape[0]
    cw = cv_ref.shape[1] // 2
    pos = jnp.where(i < npt, i % p_tiles, (i - npt) % s_tiles)
    n_tiles = jnp.where(i < npt, p_tiles, s_tiles)
    keep_prev = (pos > 0).astype(F32)
    keep_next = (pos < n_tiles - 1).astype(F32)

    def glu(ref):
        a = ref[:, :cw].astype(F32)
        g = ref[:, cw:].astype(F32)
        return a * jax.nn.sigmoid(g)

    def prod(ref):
        return ref[:, :cw].astype(F32) * ref[:, cw:].astype(F32)

    n_lane = cw // LANE
    z_prev, z_main, z_next = glu(cvp_ref) * keep_prev, glu(cv_ref), glu(cvn_ref) * keep_next
    for c in range(n_lane):
        cs = slice(c * LANE, (c + 1) * LANE)
        zs[c, 0:HALO, :] = z_prev[:, cs]
        zs[c, HALO:HALO + tm, :] = z_main[:, cs]
        zs[c, HALO + tm:, :] = z_next[:, cs]
        for r in range(1, 8):
            zsh[r - 1, c] = zs[c, pl.ds(r, tm + 24), :]
    pad = CONV_K // 2
    half = tm // 2

    def conv_chunk(idx, carry):
        c = idx // 2
        base = pl.multiple_of((idx % 2) * half, half)
        acc = jnp.zeros((half, LANE), F32)
        for r in range(8):
            for a in range(4):
                tap = 8 * a + r - (HALO - pad)
                if 0 <= tap < CONV_K:
                    rows = pl.ds(base + 8 * a, half)
                    src = zs[c, rows, :] if r == 0 else zsh[r - 1, c, rows, :]
                    acc = acc + src * wdw_ref[c, tap:tap + 1, :]
        cb[c, pl.ds(base, half), :] = acc
        return carry

    lax.fori_loop(0, 2 * n_lane, conv_chunk, 0)
    z = jnp.concatenate([cb[c] for c in range(n_lane)], axis=1) + bdw_ref[...]
    z = _layernorm(z, gcl_ref[...], bcl_ref[...])
    z = z * jax.nn.sigmoid(z)
    o_b = jnp.dot(z.astype(BF16), wpw_ref[...], preferred_element_type=F32) + bpw_ref[...]
    o_ref[:, 0:cw] = o_b.astype(BF16)

    u = gm_ref[:, :cw].astype(F32)
    vg = _layernorm(gm_ref[:, cw:].astype(F32), gsl_ref[...], bsl_ref[...]).astype(BF16)
    hc = cw // GMLP_HEADS
    for n in range(tm // CHUNK):
        rows = slice(n * CHUNK, (n + 1) * CHUNK)
        for h in range(GMLP_HEADS):
            cols = slice(h * hc, (h + 1) * hc)
            mixed = jnp.dot(wsp_ref[h], vg[rows, cols], preferred_element_type=F32) + bsp_ref[:, cols]
            o_ref[rows, cw + h * hc:cw + (h + 1) * hc] = (u[rows, cols] * mixed).astype(BF16)

    ys[0:HALO, :] = prod(scp_ref) * keep_prev
    ys[HALO:HALO + tm, :] = prod(sc_ref)
    ys[HALO + tm:, :] = prod(scn_ref) * keep_next
    acc = jnp.zeros((tm, cw), F32)
    for tap in range(SC_K):
        acc = acc + ys[pl.ds(HALO - SC_K // 2 + tap, tm), :] * wsc_ref[tap:tap + 1, :]
    o_ref[:, 2 * cw:3 * cw] = (bg_ref[...].astype(F32) * acc).astype(BF16)


def _mixers(proj, weights, tp, seq, dec_seq, tm):
    t = proj.shape[0]
    cw = 512
    npt = tp // tm
    hb = tm // HALO
    last = t // HALO - 1
    const2 = lambda i: (0, 0)

    def main(col):
        return pl.BlockSpec((tm, 2 * cw), lambda i: (i, col))

    def prev(col):
        return pl.BlockSpec((HALO, 2 * cw), lambda i: (jnp.maximum(i * hb - 1, 0), col))

    def nxt(col):
        return pl.BlockSpec((HALO, 2 * cw), lambda i: (jnp.minimum((i + 1) * hb, last), col))

    w_specs = [pl.BlockSpec(w.shape, const2 if w.ndim == 2 else (lambda i: (0, 0, 0))) for w in weights]
    return pl.pallas_call(
        functools.partial(_mixers_kernel, npt, seq // tm, dec_seq // tm),
        grid=(t // tm,),
        in_specs=[main(0), prev(0), nxt(0), main(1), main(2), prev(2), nxt(2),
                  pl.BlockSpec((tm, cw), lambda i: (i, 6))] + w_specs,
        out_specs=pl.BlockSpec((tm, 3 * cw), lambda i: (i, 0)),
        out_shape=jax.ShapeDtypeStruct((t, 3 * cw), BF16),
        scratch_shapes=[pltpu.VMEM((cw // LANE, tm + 2 * HALO, LANE), F32), pltpu.VMEM((tm + 2 * HALO, cw), F32),
                        pltpu.VMEM((7, cw // LANE, tm + 24, LANE), F32), pltpu.VMEM((cw // LANE, tm, LANE), F32)],
        compiler_params=_cparams(("arbitrary",), 40),
        name="mixers",
    )(proj, proj, proj, proj, proj, proj, proj, proj, *weights)


def _out_proj_kernel(npt, n, oap_ref, oas_ref, ob_ref, xp_ref, xs_ref, mod_ref, g_ref, wo_ref, wr_ref,
                     xn_ref, hp_ref, r_ref, mix_a, mix_b):
    i = pl.program_id(0)
    ka = oap_ref.shape[1]
    mm_tile = jnp.minimum(i, n - 1)
    ep_tile = jnp.maximum(i - 1, 0)

    @pl.when(i == 0)
    def _():
        mix_b[...] = jnp.zeros_like(mix_b)

    def step(mix_w, mix_r):
        oa = jnp.where(mm_tile < npt, oap_ref[...], oas_ref[...])
        lhs = jnp.concatenate([oa, ob_ref[...]], axis=1)
        tm, d = mix_w.shape
        parts = OUT_PARTS
        for k in range(parts):
            cols = slice(k * d // parts, (k + 1) * d // parts)
            rows = slice(k * tm // parts, (k + 1) * tm // parts)
            mix_w[:, cols] = jnp.dot(lhs, wo_ref[:, cols], preferred_element_type=F32)
            x = jnp.where(ep_tile < npt, xp_ref[rows, :], xs_ref[rows, :])
            xn = x + mod_ref[2:3, :] * mix_r[rows, :]
            xn_ref[rows, :] = xn
            h = xn * _inv_rms(xn, xn.shape[-1]) * g_ref[...]
            h = h * (1.0 + mod_ref[4:5, :]) + mod_ref[3:4, :]
            hp_ref[rows, :] = _pack_rows(h)
            h_hi, h_lo = _split_bf16(h)
            r_ref[:, rows] = (lax.dot_general(wr_ref[...], h_hi, _NT, preferred_element_type=F32)
                              + lax.dot_general(wr_ref[...], h_lo, _NT, preferred_element_type=F32))

    @pl.when(i % 2 == 0)
    def _():
        step(mix_a, mix_b)

    @pl.when(i % 2 == 1)
    def _():
        step(mix_b, mix_a)


def _out_proj(oap, oas, ob, xp, xs, mod_l, g, wo, wr, tm, mod_index):
    t = ob.shape[0]
    d = xp.shape[1]
    npt = oap.shape[0] // tm
    n = t // tm
    const = lambda i: (0, 0)
    mm = lambda i: jnp.minimum(i, n - 1)
    ep = lambda i: jnp.maximum(i - 1, 0)

    def pair(w, tile, stacked=False):
        second = (lambda i: (jnp.maximum(tile(i), npt), 0)) if stacked else (
            lambda i: (jnp.maximum(tile(i) - npt, 0), 0))
        return [pl.BlockSpec((tm, w), lambda i: (jnp.minimum(tile(i), npt - 1), 0)), pl.BlockSpec((tm, w), second)]

    x_specs = pair(d, ep, stacked=xs is None)
    if xs is None:
        xs = xp
    return pl.pallas_call(
        functools.partial(_out_proj_kernel, npt, n),
        grid=(n + 1,),
        in_specs=pair(oap.shape[1], mm) + [pl.BlockSpec((tm, ob.shape[1]), lambda i: (mm(i), 0))] + x_specs + [
                  pl.BlockSpec((None, 6, d), lambda i: (mod_index(ep(i), tm), 0, 0)),
                  pl.BlockSpec((1, d), const),
                  pl.BlockSpec(wo.shape, const, pipeline_mode=pl.Buffered(1)),
                  pl.BlockSpec(wr.shape, const, pipeline_mode=pl.Buffered(1))],
        out_specs=[pl.BlockSpec((tm, d), lambda i: (ep(i), 0)),
                   pl.BlockSpec((tm, d // 2), lambda i: (ep(i), 0)),
                   pl.BlockSpec((LANE, tm), lambda i: (0, ep(i)))],
        out_shape=[jax.ShapeDtypeStruct((t, d), F32),
                   jax.ShapeDtypeStruct((t, d // 2), U32),
                   jax.ShapeDtypeStruct((LANE, t), F32)],
        scratch_shapes=[pltpu.VMEM((tm, d), F32), pltpu.VMEM((tm, d), F32)],
        compiler_params=_cparams(("arbitrary",), 56),
        name="out_proj",
    )(oap, oas, ob, xp, xs, mod_l, g, wo, wr)


N_BUCKET_PAD = 64
PAIR_FIRST = (0, 0, 0, 1, 1, 3)
PAIR_SECOND = (1, 2, 3, 3, 2, 2)


def _route_kernel(rt_ref, b_ref, tri_ref, bucket_ref, rank_ref, gate_ref, cnt_ref, run):
    i = pl.program_id(0)
    tm = rt_ref.shape[1]
    ne = N_GROUPS * GROUP_SIZE

    @pl.when(i == 0)
    def _():
        run[...] = jnp.zeros_like(run)

    sc = jax.nn.sigmoid(rt_ref[0:ne, :] + rt_ref[ne:2 * ne, :])
    sel = sc + b_ref[...]
    s = [sel[r * N_GROUPS:(r + 1) * N_GROUPS, :] for r in range(GROUP_SIZE)]
    c = [sc[r * N_GROUPS:(r + 1) * N_GROUPS, :] for r in range(GROUP_SIZE)]
    hi01, lo01 = jnp.maximum(s[0], s[1]), jnp.minimum(s[0], s[1])
    hi23, lo23 = jnp.maximum(s[2], s[3]), jnp.minimum(s[2], s[3])
    g_score = jnp.maximum(hi01, hi23) + jnp.maximum(jnp.minimum(hi01, hi23), jnp.maximum(lo01, lo23))
    grp = lax.broadcasted_iota(jnp.int32, g_score.shape, 0)
    g_best = jnp.min(jnp.where(g_score == jnp.max(g_score, axis=0, keepdims=True), grp, N_GROUPS),
                     axis=0, keepdims=True)
    own = grp == g_best
    v = [jnp.sum(jnp.where(own, s[r], 0.0), axis=0, keepdims=True) for r in range(GROUP_SIZE)]
    w = [jnp.sum(jnp.where(own, c[r], 0.0), axis=0, keepdims=True) for r in range(GROUP_SIZE)]

    def first_max(vals):
        best, idx, gate = vals[0], jnp.zeros_like(g_best), w[0]
        for r in range(1, GROUP_SIZE):
            upd = vals[r] > best
            best = jnp.where(upd, vals[r], best)
            idx = jnp.where(upd, r, idx)
            gate = jnp.where(upd, w[r], gate)
        return idx, gate

    r1, w1 = first_max(v)
    r2, w2 = first_max([jnp.where(r1 == r, -jnp.inf, v[r]) for r in range(GROUP_SIZE)])
    wsum = w1 + w2
    swap = r1 > r2
    r_lo = jnp.where(swap, r2, r1)
    r_hi = jnp.where(swap, r1, r2)
    g_lo = jnp.where(swap, w2, w1) / wsum
    g_hi = jnp.where(swap, w1, w2) / wsum
    lex = r_lo * 3 - ((r_lo * (r_lo - 1)) >> 1) + (r_hi - r_lo - 1)
    pair = jnp.where(lex == 3, 4, jnp.where(lex == 4, 3, lex))
    hi_first = pair == N_PAIRS - 1
    gate_ref[0:1, :] = jnp.where(hi_first, g_hi, g_lo)
    gate_ref[1:2, :] = jnp.where(hi_first, g_lo, g_hi)
    bucket = g_best * N_PAIRS + pair
    bucket_ref[...] = bucket

    onehot = (lax.broadcasted_iota(jnp.int32, (N_BUCKET_PAD, tm), 0) == bucket).astype(F32)
    before = jnp.dot(onehot.astype(BF16), tri_ref[...], preferred_element_type=F32) + run[...]
    rank_ref[...] = jnp.sum(onehot * before, axis=0, keepdims=True).astype(jnp.int32)
    run[...] += jnp.sum(onehot, axis=1, keepdims=True)

    @pl.when(i == pl.num_programs(0) - 1)
    def _():
        cnt_ref[...] = run[...]


def _route(rt, router_b, n_blk, tm):
    t = rt.shape[1]
    ne = router_b.shape[0]
    b_col = router_b.astype(F32).reshape(N_GROUPS, GROUP_SIZE).T.reshape(ne, 1)
    tri = (jnp.arange(tm)[:, None] < jnp.arange(tm)[None, :]).astype(BF16)
    bucket, rank, gate, cnt = pl.pallas_call(
        _route_kernel,
        grid=(t // tm,),
        in_specs=[pl.BlockSpec((LANE, tm), lambda i: (0, i)),
                  pl.BlockSpec((ne, 1), lambda i: (0, 0)),
                  pl.BlockSpec((tm, tm), lambda i: (0, 0))],
        out_specs=[pl.BlockSpec((1, tm), lambda i: (0, i)),
                   pl.BlockSpec((1, tm), lambda i: (0, i)),
                   pl.BlockSpec((2, tm), lambda i: (0, i)),
                   pl.BlockSpec((N_BUCKET_PAD, 1), lambda i: (0, 0))],
        out_shape=[jax.ShapeDtypeStruct((1, t), jnp.int32),
                   jax.ShapeDtypeStruct((1, t), jnp.int32),
                   jax.ShapeDtypeStruct((2, t), F32),
                   jax.ShapeDtypeStruct((N_BUCKET_PAD, 1), F32)],
        scratch_shapes=[pltpu.VMEM((N_BUCKET_PAD, 1), F32)],
        compiler_params=_cparams(("arbitrary",), 32),
        name="route",
    )(rt, b_col, tri)

    n_bucket = N_GROUPS * N_PAIRS
    counts = cnt[:n_bucket, 0].astype(jnp.int32)
    padded = (counts + MOE_BM - 1) // MOE_BM * MOE_BM
    pad_end = jnp.cumsum(padded)
    pad_start = pad_end - padded
    bucket = bucket[0]
    sel = bucket[:, None] == jnp.arange(n_bucket, dtype=jnp.int32)[None, :]
    pos = jnp.sum(jnp.where(sel, pad_start[None, :], 0), axis=1).astype(jnp.int32) + rank[0]
    n_used = (pad_end[-1] // MOE_BM).astype(jnp.int32)
    blk = jnp.minimum(jnp.arange(n_blk, dtype=jnp.int32), n_used - 1)
    blk_bucket = jnp.sum((pad_end[None, :] <= (blk * MOE_BM)[:, None]).astype(jnp.int32), axis=1)
    blk_bucket = jnp.minimum(blk_bucket, n_bucket - 1)
    pair_lo = jnp.array(PAIR_FIRST, jnp.int32)
    pair_hi = jnp.array(PAIR_SECOND, jnp.int32)
    blk_pair = blk_bucket % N_PAIRS
    first = (blk_bucket // N_PAIRS) * GROUP_SIZE
    is_pair = blk_pair[:, None] == jnp.arange(N_PAIRS, dtype=jnp.int32)[None, :]
    blk_a = first + jnp.sum(jnp.where(is_pair, pair_lo[None, :], 0), axis=1)
    blk_b = first + jnp.sum(jnp.where(is_pair, pair_hi[None, :], 0), axis=1)
    is_bucket = blk_bucket[:, None] == jnp.arange(n_bucket, dtype=jnp.int32)[None, :]
    per_bucket = lambda v: jnp.sum(jnp.where(is_bucket, v[None, :], 0), axis=1)
    first_blk = per_bucket(pad_start // MOE_BM)
    n_in_bucket = jnp.maximum(per_bucket(padded // MOE_BM), 1)
    src = first_blk + (blk - first_blk + n_in_bucket - 1) % n_in_bucket
    steps = jnp.arange(n_blk, dtype=jnp.int32)
    used = steps < n_used
    blk_out = jnp.where(used, src, steps).astype(jnp.int32)
    blk_rows = jnp.where(used, jnp.clip(per_bucket(pad_start + counts) - src * MOE_BM, 0, MOE_BM), 0)
    gates = jnp.pad(gate.T, ((0, 0), (0, GATE_W - 2)))
    plan = (src.astype(jnp.int32), blk_a.astype(jnp.int32), blk_b.astype(jnp.int32), blk_rows.astype(jnp.int32),
            blk_out)
    rows_in_blk = jnp.clip(per_bucket(pad_start + counts) - blk * MOE_BM, 0, MOE_BM)
    fill = jnp.logical_or(jnp.logical_not(used), rows_in_blk < MOE_BM).astype(jnp.int32)
    return pos, gates, plan, fill


def _dispatch_kernel(fill_ref, pos_ref, hp_ref, gt_ref, xs_ref, buf, zeros, sem, zsem):
    i = pl.program_id(0)
    slot = i % 2
    tm = hp_ref.shape[0]
    n_data = hp_ref.shape[1] // LANE
    blk_rows = zeros.shape[0]

    def wait_all(s):
        pltpu.make_async_copy(buf.at[s], xs_ref.at[pl.ds(0, tm * X_ROWS)], sem.at[s]).wait()

    @pl.when(i == 0)
    def _():
        zeros[...] = jnp.zeros_like(zeros)

        def zero_block(b):
            return pltpu.make_async_copy(zeros, xs_ref.at[pl.ds(pl.multiple_of(b * blk_rows, blk_rows), blk_rows)],
                                         zsem)

        def start(b, carry):
            @pl.when(fill_ref[b] != 0)
            def _():
                zero_block(b).start()
            return carry

        def wait(b, carry):
            @pl.when(fill_ref[b] != 0)
            def _():
                zero_block(b).wait()
            return carry

        lax.fori_loop(0, fill_ref.shape[0], start, 0)
        lax.fori_loop(0, fill_ref.shape[0], wait, 0)

    for s in range(2):
        @pl.when(slot == s)
        def _():
            for j in range(n_data):
                buf[s, pl.ds(j, tm, stride=X_ROWS), :] = hp_ref[:, j * LANE:(j + 1) * LANE]
            buf[s, pl.ds(n_data, tm, stride=X_ROWS), :] = pltpu.bitcast(gt_ref[...], U32)
            for r in range(tm):
                pltpu.make_async_copy(buf.at[s, pl.ds(r * X_ROWS, X_ROWS)],
                                      xs_ref.at[pl.ds(pos_ref[0, 0, r] * X_ROWS, X_ROWS)],
                                      sem.at[s]).start(priority=r % N_DMA_PRIORITIES)

    @pl.when(i > 0)
    def _():
        wait_all(1 - slot)

    @pl.when(i == pl.num_programs(0) - 1)
    def _():
        wait_all(slot)


def _dispatch(hp, gates, pos, fill, tm):
    t, w = hp.shape
    assert w == (X_ROWS - 1) * LANE and GATE_W == LANE
    n_blk = fill.shape[0]
    grid_spec = pltpu.PrefetchScalarGridSpec(
        num_scalar_prefetch=1,
        grid=(t // tm,),
        in_specs=[pl.BlockSpec((1, 1, tm), lambda i, fill: (i, 0, 0), memory_space=pltpu.SMEM),
                  pl.BlockSpec((tm, w), lambda i, fill: (i, 0)),
                  pl.BlockSpec((tm, GATE_W), lambda i, fill: (i, 0))],
        out_specs=pl.BlockSpec(memory_space=pl.ANY),
        scratch_shapes=[pltpu.VMEM((2, tm * X_ROWS, LANE), U32), pltpu.VMEM((MOE_BM * X_ROWS, LANE), U32),
                        pltpu.SemaphoreType.DMA((2,)), pltpu.SemaphoreType.DMA(())],
    )
    return pl.pallas_call(
        _dispatch_kernel,
        grid_spec=grid_spec,
        out_shape=jax.ShapeDtypeStruct((n_blk * MOE_BM * X_ROWS, LANE), U32),
        compiler_params=_cparams(("arbitrary",), 32),
        name="dispatch",
    )(fill, pos.reshape(t // tm, 1, tm), hp, gates)


def _moe_kernel(src_ref, ea_ref, eb_ref, nv_ref, dst_ref, x_ref, w1a_ref, w3a_ref, w2a_ref, w1b_ref, w3b_ref,
                w2b_ref, y_ref):
    del src_ref, ea_ref, eb_ref, dst_ref
    nv = nv_ref[pl.program_id(0)]
    bm = x_ref.shape[0] // X_ROWS

    def experts(m):
        halves = [_unpack_rows(x_ref[pl.ds(j, m, stride=X_ROWS), :]) for j in range(X_ROWS - 1)]
        x = jnp.concatenate([a.astype(BF16) for a, _ in halves] + [b.astype(BF16) for _, b in halves], axis=1)
        gates = pltpu.bitcast(x_ref[pl.ds(X_ROWS - 1, m, stride=X_ROWS), :], F32)

        def hidden(w1_ref, w3_ref, g):
            a = jnp.dot(x, w1_ref[...], preferred_element_type=F32)
            b = jnp.dot(x, w3_ref[...], preferred_element_type=F32)
            return (a * jax.nn.sigmoid(a) * b * g).astype(BF16)

        y = jnp.dot(hidden(w1a_ref, w3a_ref, gates[:, 0:1]), w2a_ref[...], preferred_element_type=F32)
        y += jnp.dot(hidden(w1b_ref, w3b_ref, gates[:, 1:2]), w2b_ref[...], preferred_element_type=F32)
        yp = _pack_rows(y)
        for j in range(Y_ROWS):
            y_ref[pl.ds(j, m, stride=Y_ROWS), :] = yp[:, j * LANE:(j + 1) * LANE]
        if m < bm:
            y_ref[m * Y_ROWS:, :] = jnp.zeros(((bm - m) * Y_ROWS, LANE), y_ref.dtype)

    @pl.when(nv > bm // 2)
    def _():
        experts(bm)

    @pl.when(jnp.logical_and(nv > 0, nv <= bm // 2))
    def _():
        experts(bm // 2)

    @pl.when(nv == 0)
    def _():
        y_ref[...] = jnp.zeros_like(y_ref)


def _moe(xs, w1, w3, w2, plan):
    n_blk = xs.shape[0] // (MOE_BM * X_ROWS)
    _, d, de = w1.shape
    assert d == 2 * Y_ROWS * LANE

    def expert(second, rows, cols):
        if second:
            return pl.BlockSpec((None, rows, cols), lambda i, src, ea, eb, nv, dst: (eb[i], 0, 0))
        return pl.BlockSpec((None, rows, cols), lambda i, src, ea, eb, nv, dst: (ea[i], 0, 0))

    grid_spec = pltpu.PrefetchScalarGridSpec(
        num_scalar_prefetch=5,
        grid=(n_blk,),
        in_specs=[pl.BlockSpec((MOE_BM * X_ROWS, LANE), lambda i, src, ea, eb, nv, dst: (src[i], 0)),
                  expert(False, d, de), expert(False, d, de), expert(False, de, d),
                  expert(True, d, de), expert(True, d, de), expert(True, de, d)],
        out_specs=pl.BlockSpec((MOE_BM * Y_ROWS, LANE), lambda i, src, ea, eb, nv, dst: (dst[i], 0)),
    )
    return pl.pallas_call(
        _moe_kernel,
        grid_spec=grid_spec,
        out_shape=jax.ShapeDtypeStruct((n_blk * MOE_BM * Y_ROWS, LANE), U32),
        compiler_params=_cparams(("arbitrary",), 48),
        name="moe",
    )(*plan, xs, w1, w3, w2, w1, w3, w2)


def _combine_kernel(npt, pos_ref, nxt_ref, xn_ref, mod_ref, yb_ref, op_ref, os_ref, buf, sem):
    i = pl.program_id(0)
    out = xn_ref[...] + mod_ref[5:6, :] * _gathered_rows(pos_ref, nxt_ref, yb_ref, buf, sem)

    @pl.when(i < npt)
    def _():
        op_ref[...] = out

    @pl.when(i >= npt)
    def _():
        os_ref[...] = out


def _combine(xn, yb, pos, mod_l, tp, tm, mod_index):
    t, d = xn.shape
    npt = tp // tm
    n = t // tm
    pos3 = pos.reshape(n, 1, tm)
    return pl.pallas_call(
        functools.partial(_combine_kernel, npt),
        grid=(n,),
        in_specs=[pl.BlockSpec((1, 1, tm), lambda i: (i, 0, 0), memory_space=pltpu.SMEM),
                  pl.BlockSpec((1, 1, tm), lambda i: (jnp.minimum(i + 1, n - 1), 0, 0), memory_space=pltpu.SMEM),
                  pl.BlockSpec((tm, d), lambda i: (i, 0)),
                  pl.BlockSpec((None, 6, d), lambda i: (mod_index(i, tm), 0, 0)),
                  pl.BlockSpec(memory_space=pl.ANY)],
        out_specs=[pl.BlockSpec((tm, d), lambda i: (jnp.minimum(i, npt - 1), 0)),
                   pl.BlockSpec((tm, d), lambda i: (jnp.maximum(i - npt, 0), 0))],
        out_shape=[jax.ShapeDtypeStruct((tp, d), F32),
                   jax.ShapeDtypeStruct((t - tp, d), F32)],
        scratch_shapes=[pltpu.VMEM((2, tm * Y_ROWS, LANE), U32), pltpu.SemaphoreType.DMA((2,))],
        compiler_params=_cparams(("arbitrary",), 32),
        name="combine",
    )(pos3, pos3, xn, mod_l, yb)


def _pad_cols(x, n):
    return jnp.pad(x, ((0, 0), (0, n - x.shape[1])))


def _rope_tables(dec_seq, tm):
    n_freq = MLA_ROPE // 4
    pos = jnp.arange(dec_seq, dtype=jnp.int32)
    row = (pos // GRID_W).astype(F32)
    col = (pos % GRID_W).astype(F32)
    inv = ROPE_THETA ** (-jnp.arange(n_freq, dtype=F32) / n_freq)
    ar = row[:, None] * inv[None, :]
    ac = col[:, None] * inv[None, :]
    ones = jnp.ones((dec_seq, LANE - MLA_ROPE), F32)
    cos = jnp.concatenate([jnp.cos(ar), jnp.cos(ar), jnp.cos(ac), jnp.cos(ac), ones], axis=1)
    sin = jnp.concatenate([-jnp.sin(ar), jnp.sin(ar), -jnp.sin(ac), jnp.sin(ac), 0.0 * ones], axis=1)
    ident_c = jnp.ones((tm, LANE), F32)
    ident_s = jnp.zeros((tm, LANE), F32)
    return jnp.concatenate([cos, ident_c], axis=0), jnp.concatenate([sin, ident_s], axis=0)


def _layer_params(l, q_lora, kv_lora, w_in, g_q_lora, w_uq, g_kv_lora, w_ukv, g_qk_q, g_qk_k,
                  w_dw31, b_dw31, g_conv_ln, b_conv_ln, w_pw, b_pw, g_sgu_ln, b_sgu_ln,
                  w_spatial, b_spatial, w_sc3, w_out):
    d = w_in.shape[1]
    o1 = q_lora
    o2 = o1 + kv_lora
    o3 = o2 + MLA_ROPE
    o4 = o3 + 1024
    o5 = o4 + 1024
    w = w_in[l]
    zeros = lambda n: jnp.zeros((d, n), w.dtype)
    w_in_p = jnp.concatenate(
        [w[:, o3:o4], w[:, o4:o5], w[:, o5 + 512:o5 + 1536], w[:, o5:o5 + 512],
         w[:, :o1], zeros(512 - q_lora),
         w[:, o1:o2], w[:, o2:o3], zeros(PROJ_W - KV_COL - kv_lora - MLA_ROPE)], axis=1).astype(BF16)

    uq = w_uq[l].reshape(q_lora, MLA_HEADS, MLA_QK)
    uq = jnp.pad(uq, ((0, 512 - q_lora), (0, 0), (0, HEAD_PAD - MLA_QK)))
    w_uq_p = uq.reshape(512, MLA_HEADS * HEAD_PAD).astype(BF16)
    ukv = w_ukv[l].reshape(kv_lora, MLA_HEADS, MLA_NOPE + MLA_V)
    w_ukv_p = jnp.concatenate([ukv[:, :, :MLA_NOPE].reshape(kv_lora, -1),
                               ukv[:, :, MLA_NOPE:].reshape(kv_lora, -1)], axis=1).astype(BF16)
    row = lambda v: v.reshape(1, -1).astype(F32)
    qkv_w = (w_uq_p, w_ukv_p, _pad_cols(row(g_q_lora[l]), 512), row(g_kv_lora[l]),
             _pad_cols(row(g_qk_q[l]), HEAD_PAD), _pad_cols(row(g_qk_k[l]), HEAD_PAD))

    hc = w_pw.shape[1] // GMLP_HEADS
    bsp = jnp.repeat(b_spatial[l].T.astype(F32), hc, axis=1)
    w_dw = jnp.pad(w_dw31[l].astype(F32), ((0, 32 - CONV_K), (0, 0)))
    w_dw = w_dw.reshape(32, -1, LANE).transpose(1, 0, 2)
    mix_w = (w_dw, row(b_dw31[l]),
             row(g_conv_ln[l]), row(b_conv_ln[l]), w_pw[l].astype(BF16), row(b_pw[l]),
             row(g_sgu_ln[l]), row(b_sgu_ln[l]), w_spatial[l].astype(BF16), bsp,
             jnp.pad(w_sc3[l].astype(F32), ((0, 8 - SC_K), (0, 0))))
    return w_in_p, qkv_w, mix_w, w_out[l].astype(BF16)


def _tile(limit, *sizes):
    tm = limit
    while any(s % tm for s in sizes):
        tm //= 2
    return tm


def kernel(x_prompt, x_sample, cache_mla_ckv, cache_mla_krope, c, c_ctx, w_mod, b_mod, g_norm1, g_norm2, w_in, g_q_lora, w_uq, g_kv_lora, w_ukv, g_qk_q, g_qk_k, w_dw31, b_dw31, g_conv_ln, b_conv_ln, w_pw, b_pw, g_sgu_ln, b_sgu_ln, w_spatial, b_spatial, w_sc3, w_out, router_w, router_b, w1, w3, w2):
    nseq, seq, d = x_prompt.shape
    nb, dec_seq, _ = x_sample.shape
    depth = w_mod.shape[0]
    past = cache_mla_ckv.shape[2]
    q_lora = g_q_lora.shape[1]
    kv_lora = g_kv_lora.shape[1]
    n_exp = router_w.shape[1]
    tp = nseq * seq
    ts = nb * dec_seq
    t = tp + ts
    assert tp % dec_seq == 0 and seq % CHUNK == 0 and dec_seq % GRID_W == 0
    assert nb + 1 <= 8 and kv_lora == MLA_NOPE and n_exp == N_GROUPS * GROUP_SIZE

    tm_in = _tile(512, tp, dec_seq)
    tm_qkv = _tile(512, tp, dec_seq)
    tm_row = _tile(256, seq, dec_seq)
    tq = _tile(1024, dec_seq)

    def mod_index(i, tm):
        npt = tp // tm
        return jnp.where(i < npt, 0, 1 + (i - npt) // (dec_seq // tm))

    cvec = jnp.concatenate([c_ctx[None, :], c, jnp.zeros((8 - 1 - nb, d), F32)], axis=0)
    mod = _modulation(cvec, w_mod, b_mod).reshape(depth, 8, 6, d)
    cos_t, sin_t = _rope_tables(dec_seq, tm_qkv)

    rw = router_w.T.reshape(N_GROUPS, GROUP_SIZE, d).transpose(1, 0, 2).reshape(n_exp, d)
    rw_hi = rw.astype(BF16)
    rw_lo = (rw - rw_hi.astype(F32)).astype(BF16)
    wr = jnp.pad(jnp.concatenate([rw_hi, rw_lo], axis=0), ((0, LANE - 2 * n_exp), (0, 0)))
    n_blk = (t + N_GROUPS * N_PAIRS * (MOE_BM - 1) + MOE_BM - 1) // MOE_BM

    xp = x_prompt.reshape(tp, d)
    xs = x_sample.reshape(ts, d)
    ckv_states = []
    krope_states = []
    for l in range(depth):
        w_in_p, qkv_w, mix_w, w_out_b = _layer_params(
            l, q_lora, kv_lora, w_in, g_q_lora, w_uq, g_kv_lora, w_ukv, g_qk_q, g_qk_k,
            w_dw31, b_dw31, g_conv_ln, b_conv_ln, w_pw, b_pw, g_sgu_ln, b_sgu_ln,
            w_spatial, b_spatial, w_sc3, w_out)
        mod_l = mod[l]

        if l == 0:
            proj, kv = _in_proj(xp, xs, mod_l, g_norm1[l].reshape(1, d), w_in_p, tm_in, mod_index)
        else:
            proj, kv, xp = _in_proj_fused(xn, yb, pos, mod[l - 1], mod_l, g_norm1[l].reshape(1, d), w_in_p,
                                          tm_in, mod_index)
            xs = None
        q, k, v, state = _qkv(proj, kv, cos_t, sin_t, *qkv_w, tp, dec_seq, q_lora, tm_qkv)
        kvc = jnp.concatenate([cache_mla_ckv[:, l], cache_mla_krope[:, l],
                               jnp.zeros((nb, past, KV_W - kv_lora - MLA_ROPE), F32)], axis=-1)
        kc, vc = _ctx_kv(kvc.reshape(nb * past, KV_W), qkv_w[1], qkv_w[5], _tile(512, past))
        o_as, w1b, w3b, w2b = _attention_latent(q, k, v, kc, vc, w1, w3, w2, l, tp, nb, dec_seq, past, tq)
        o_ap = _attention_context(q, k, v, nseq, seq)
        o_bcd = _mixers(proj, mix_w, tp, seq, dec_seq, tm_row)

        xn, hp, r = _out_proj(o_ap, o_as, o_bcd, xp, xs, mod_l, g_norm2[l].reshape(1, d), w_out_b, wr,
                              tm_qkv, mod_index)
        pos, gates, plan, fill = _route(r, router_b, n_blk, tm_qkv)
        xd = _dispatch(hp, gates, pos, fill, tm_row)
        yb = _moe(xd, w1b, w3b, w2b, plan)
        if l == depth - 1:
            xp, xs = _combine(xn, yb, pos, mod_l, tp, tm_row, mod_index)

        ckv_states.append(state[:, :kv_lora].reshape(nseq, seq, kv_lora))
        krope_states.append(state[:, kv_lora:kv_lora + MLA_ROPE].reshape(nseq, seq, MLA_ROPE))

    return (xp.reshape(nseq, seq, d), xs.reshape(nb, dec_seq, d),
            jnp.stack(ckv_states, axis=1), jnp.stack(krope_states, axis=1))
```

```python
import functools

import jax
import jax.numpy as jnp
from jax import lax
from jax.experimental import pallas as pl
from jax.experimental.pallas import tpu as pltpu

F32 = jnp.float32
BF16 = jnp.bfloat16
U32 = jnp.uint32

MLA_HEADS = 4
MLA_NOPE = 128
MLA_ROPE = 64
MLA_V = 128
MLA_QK = MLA_NOPE + MLA_ROPE
HEAD_PAD = 256
GRID_W = 64
ROPE_THETA = 10000.0
CONV_K = 31
SC_K = 3
CHUNK = 128
GMLP_HEADS = 4
N_GROUPS = 8
GROUP_SIZE = 4
N_PAIRS = 6
NORM_EPS = 1e-6

LANE = 128
HALO = 16
PROJ_W = 4352
CQ_COL = 3584
KV_COL = 4096
KV_W = 256
IN_TN = 1536
MOE_BM = 256
OUT_PARTS = 4
N_DMA_PRIORITIES = 2
ATTN_QSUB = 1024
ATTN_KCHUNK = 1024
LOG2_E = 1.4426950408889634
GATE_W = 128
X_ROWS = 9
Y_ROWS = 8
VMEM_CAP = 56 * 1024 * 1024


def _cparams(sem, vmem_mb):
    return pltpu.CompilerParams(dimension_semantics=sem,
                                vmem_limit_bytes=min(vmem_mb * 1024 * 1024, VMEM_CAP))


def _inv_rms(x, n):
    return lax.rsqrt(jnp.sum(x * x, axis=-1, keepdims=True) * (1.0 / n) + NORM_EPS)


def _layernorm(x, g, b):
    mu = jnp.mean(x, axis=-1, keepdims=True)
    xc = x - mu
    var = jnp.mean(xc * xc, axis=-1, keepdims=True)
    return xc * lax.rsqrt(var + NORM_EPS) * g + b


def _split_bf16(x):
    hi = x.astype(BF16)
    lo = (x - hi.astype(F32)).astype(BF16)
    return hi, lo


def _pack_rows(x):
    n = x.shape[1] // 2
    hi = pltpu.bitcast(x[:, :n].astype(BF16).astype(F32), U32)
    lo = pltpu.bitcast(x[:, n:].astype(BF16).astype(F32), U32)
    return (hi & jnp.uint32(0xFFFF0000)) | (lo >> 16)


def _unpack_rows(u):
    a = pltpu.bitcast(u & jnp.uint32(0xFFFF0000), F32)
    b = pltpu.bitcast(u << 16, F32)
    return a, b


def _mod_kernel(c_ref, w_ref, b_ref, o_ref):
    c = c_ref[...]
    a_hi, a_lo = _split_bf16(c * jax.nn.sigmoid(c))
    w_hi, w_lo = _split_bf16(w_ref[...])
    acc = jnp.dot(a_hi, w_hi, preferred_element_type=F32)
    acc += jnp.dot(a_lo, w_hi, preferred_element_type=F32)
    acc += jnp.dot(a_hi, w_lo, preferred_element_type=F32)
    o_ref[...] = acc + b_ref[...]


def _modulation(cvec, w_mod, b_mod):
    depth, d, n = w_mod.shape
    tn = 1024
    return pl.pallas_call(
        _mod_kernel,
        grid=(depth, n // tn),
        in_specs=[pl.BlockSpec((8, d), lambda l, j: (0, 0)),
                  pl.BlockSpec((None, d, tn), lambda l, j: (l, 0, j)),
                  pl.BlockSpec((None, 1, tn), lambda l, j: (l, 0, j))],
        out_specs=pl.BlockSpec((None, 8, tn), lambda l, j: (l, 0, j)),
        out_shape=jax.ShapeDtypeStruct((depth, 8, n), F32),
        compiler_params=_cparams(("arbitrary", "arbitrary"), 40),
        name="modulation",
    )(cvec, w_mod, b_mod.reshape(depth, 1, n))


def _norm_project(x, mod_ref, g_ref, w_ref, proj_ref, kv_ref):
    y = x * _inv_rms(x, x.shape[-1]) * g_ref[...]
    h = (y * (1.0 + mod_ref[1:2, :]) + mod_ref[0:1, :]).astype(BF16)
    for c0 in range(0, PROJ_W, IN_TN):
        c1 = min(c0 + IN_TN, PROJ_W)
        acc = jnp.dot(h, w_ref[:, c0:c1], preferred_element_type=F32)
        proj_ref[:, c0:c1] = acc.astype(BF16)
        if c0 <= KV_COL < c1:
            kv_ref[...] = acc[:, KV_COL - c0:KV_COL - c0 + KV_W]


def _in_proj_kernel(npt, xp_ref, xs_ref, mod_ref, g_ref, w_ref, proj_ref, kv_ref):
    x = jnp.where(pl.program_id(0) < npt, xp_ref[...], xs_ref[...])
    _norm_project(x, mod_ref, g_ref, w_ref, proj_ref, kv_ref)


def _gathered_rows(pos_ref, nxt_ref, yb_ref, buf, sem):
    i = pl.program_id(0)
    slot = i % 2
    tm = buf.shape[1] // Y_ROWS

    def gather(idx_ref, s):
        for r in range(tm):
            src = pl.multiple_of(idx_ref[0, 0, r] * Y_ROWS, Y_ROWS)
            pltpu.make_async_copy(yb_ref.at[pl.ds(src, Y_ROWS)], buf.at[s, pl.ds(r * Y_ROWS, Y_ROWS)],
                                  sem.at[s]).start(priority=r % N_DMA_PRIORITIES)

    @pl.when(i == 0)
    def _():
        gather(pos_ref, 0)

    for s in range(2):
        @pl.when(jnp.logical_and(i + 1 < pl.num_programs(0), slot == 1 - s))
        def _():
            gather(nxt_ref, s)

    pltpu.make_async_copy(yb_ref.at[pl.ds(0, tm * Y_ROWS)], buf.at[slot], sem.at[slot]).wait()
    halves = [_unpack_rows(buf[slot, pl.ds(j, tm, stride=Y_ROWS), :]) for j in range(Y_ROWS)]
    return jnp.concatenate([a for a, _ in halves] + [b for _, b in halves], axis=1)


def _in_proj_fused_kernel(pos_ref, nxt_ref, xn_ref, prev_mod_ref, mod_ref, g_ref, w_ref, yb_ref,
                          proj_ref, kv_ref, x_ref, buf, sem):
    x = xn_ref[...] + prev_mod_ref[5:6, :] * _gathered_rows(pos_ref, nxt_ref, yb_ref, buf, sem)
    x_ref[...] = x
    _norm_project(x, mod_ref, g_ref, w_ref, proj_ref, kv_ref)


def _in_proj_fused(xn, yb, pos, prev_mod, mod_l, g, w, tm, mod_index):
    t, d = xn.shape
    n = t // tm
    pos3 = pos.reshape(n, 1, tm)
    mod_spec = pl.BlockSpec((None, 6, d), lambda i: (mod_index(i, tm), 0, 0))
    return pl.pallas_call(
        _in_proj_fused_kernel,
        grid=(n,),
        in_specs=[pl.BlockSpec((1, 1, tm), lambda i: (i, 0, 0), memory_space=pltpu.SMEM),
                  pl.BlockSpec((1, 1, tm), lambda i: (jnp.minimum(i + 1, n - 1), 0, 0), memory_space=pltpu.SMEM),
                  pl.BlockSpec((tm, d), lambda i: (i, 0)),
                  mod_spec, mod_spec,
                  pl.BlockSpec((1, d), lambda i: (0, 0)),
                  pl.BlockSpec(w.shape, lambda i: (0, 0), pipeline_mode=pl.Buffered(1)),
                  pl.BlockSpec(memory_space=pl.ANY)],
        out_specs=[pl.BlockSpec((tm, PROJ_W), lambda i: (i, 0)),
                   pl.BlockSpec((tm, KV_W), lambda i: (i, 0)),
                   pl.BlockSpec((tm, d), lambda i: (i, 0))],
        out_shape=[jax.ShapeDtypeStruct((t, PROJ_W), BF16),
                   jax.ShapeDtypeStruct((t, KV_W), F32),
                   jax.ShapeDtypeStruct((t, d), F32)],
        scratch_shapes=[pltpu.VMEM((2, tm * Y_ROWS, LANE), U32), pltpu.SemaphoreType.DMA((2,))],
        compiler_params=_cparams(("arbitrary",), 56),
        name="in_proj_fused",
    )(pos3, pos3, xn, prev_mod, mod_l, g, w, yb)


def _in_proj(xp, xs, mod_l, g, w, tm, mod_index):
    tp, d = xp.shape
    t = tp + xs.shape[0]
    npt = tp // tm
    return pl.pallas_call(
        functools.partial(_in_proj_kernel, npt),
        grid=(t // tm,),
        in_specs=[pl.BlockSpec((tm, d), lambda i: (jnp.minimum(i, npt - 1), 0)),
                  pl.BlockSpec((tm, d), lambda i: (jnp.maximum(i - npt, 0), 0)),
                  pl.BlockSpec((None, 6, d), lambda i: (mod_index(i, tm), 0, 0)),
                  pl.BlockSpec((1, d), lambda i: (0, 0)),
                  pl.BlockSpec(w.shape, lambda i: (0, 0), pipeline_mode=pl.Buffered(1))],
        out_specs=[pl.BlockSpec((tm, PROJ_W), lambda i: (i, 0)),
                   pl.BlockSpec((tm, KV_W), lambda i: (i, 0))],
        out_shape=[jax.ShapeDtypeStruct((t, PROJ_W), BF16),
                   jax.ShapeDtypeStruct((t, KV_W), F32)],
        compiler_params=_cparams(("arbitrary",), 56),
        name="in_proj",
    )(xp, xs, mod_l, g, w)


def _rope_tile(t, cos, sin):
    lane = lax.broadcasted_iota(jnp.int32, t.shape, 1)
    first = (lane & 16) == 0
    swapped = jnp.where(first, pltpu.roll(t, LANE - 16, 1), pltpu.roll(t, 16, 1))
    return t * cos + swapped * sin


def _row_ss(x):
    sq = (x * x).astype(BF16)
    return jnp.dot(sq, jnp.ones((x.shape[1], LANE), BF16), preferred_element_type=F32)


def _keys_values(ckv_n, krope, w_ukv_ref, gk_ref, cos, sin, k_ref, v_ref):
    kvf = jnp.dot(ckv_n.astype(BF16), w_ukv_ref[...], preferred_element_type=F32)
    kr_ss = _row_ss(krope)
    gk = gk_ref[...]
    tail = krope * gk[:, MLA_NOPE:]
    if cos is not None:
        tail = _rope_tile(tail, cos, sin)
    for h in range(MLA_HEADS):
        kn = kvf[:, h * MLA_NOPE:(h + 1) * MLA_NOPE]
        r = lax.rsqrt((_row_ss(kn) + kr_ss) * (1.0 / MLA_QK) + NORM_EPS)
        k_ref[:, h * HEAD_PAD:h * HEAD_PAD + MLA_NOPE] = (kn * r * gk[:, :MLA_NOPE]).astype(BF16)
        k_ref[:, h * HEAD_PAD + MLA_NOPE:(h + 1) * HEAD_PAD] = (tail * r).astype(BF16)
    v_ref[...] = kvf[:, MLA_HEADS * MLA_NOPE:].astype(BF16)


def _qkv_kernel(npt, q_lora, cq_ref, kv_ref, cos_ref, sin_ref, w_uq_ref, w_ukv_ref, gql_ref, gkv_ref,
                gq_ref, gk_ref, q_ref, k_ref, v_ref, st_ref):
    i = pl.program_id(0)
    cos = cos_ref[...]
    sin = sin_ref[...]
    cq = cq_ref[...].astype(F32)
    r_cq = lax.rsqrt(_row_ss(cq) * (1.0 / q_lora) + NORM_EPS)
    cqn = cq * jnp.concatenate([r_cq] * (cq.shape[1] // LANE), axis=1) * gql_ref[...]
    qf = jnp.dot(cqn.astype(BF16), w_uq_ref[...], preferred_element_type=F32)
    gq = gq_ref[...]
    scale = MLA_QK ** -0.5 * LOG2_E
    for h in range(MLA_HEADS):
        qh = qf[:, h * HEAD_PAD:(h + 1) * HEAD_PAD]
        r = lax.rsqrt(_row_ss(qh) * (1.0 / MLA_QK) + NORM_EPS) * scale
        head = qh[:, :MLA_NOPE] * r * gq[:, :MLA_NOPE]
        tail = qh[:, MLA_NOPE:] * r * gq[:, MLA_NOPE:]
        q_ref[:, h * HEAD_PAD:h * HEAD_PAD + MLA_NOPE] = head.astype(BF16)
        q_ref[:, h * HEAD_PAD + MLA_NOPE:(h + 1) * HEAD_PAD] = _rope_tile(tail, cos, sin).astype(BF16)

    kv = kv_ref[...]
    ckv = kv[:, :MLA_NOPE]
    krope = kv[:, MLA_NOPE:]
    ckv_n = ckv * _inv_rms(ckv, ckv.shape[-1]) * gkv_ref[...]
    _keys_values(ckv_n, krope, w_ukv_ref, gk_ref, cos, sin, k_ref, v_ref)

    @pl.when(i < npt)
    def _():
        st_ref[:, :MLA_NOPE] = ckv_n
        st_ref[:, MLA_NOPE:] = krope


def _qkv(proj, kv, cos_t, sin_t, w_uq, w_ukv, gql, gkv, gq, gk, tp, dec_seq, q_lora, tm):
    t = proj.shape[0]
    npt = tp // tm
    nseq = dec_seq // tm
    cq_blk = CQ_COL // 512

    def tab(i):
        return (jnp.where(i < npt, nseq, (i - npt) % nseq), 0)

    const = lambda i: (0, 0)
    return pl.pallas_call(
        functools.partial(_qkv_kernel, npt, q_lora),
        grid=(t // tm,),
        in_specs=[pl.BlockSpec((tm, 512), lambda i: (i, cq_blk)),
                  pl.BlockSpec((tm, KV_W), lambda i: (i, 0)),
                  pl.BlockSpec((tm, LANE), tab),
                  pl.BlockSpec((tm, LANE), tab),
                  pl.BlockSpec(w_uq.shape, const),
                  pl.BlockSpec(w_ukv.shape, const),
                  pl.BlockSpec(gql.shape, const),
                  pl.BlockSpec(gkv.shape, const),
                  pl.BlockSpec(gq.shape, const),
                  pl.BlockSpec(gk.shape, const)],
        out_specs=[pl.BlockSpec((tm, MLA_HEADS * HEAD_PAD), lambda i: (i, 0)),
                   pl.BlockSpec((tm, MLA_HEADS * HEAD_PAD), lambda i: (i, 0)),
                   pl.BlockSpec((tm, MLA_HEADS * MLA_V), lambda i: (i, 0)),
                   pl.BlockSpec((tm, KV_W), lambda i: (jnp.minimum(i, npt - 1), 0))],
        out_shape=[jax.ShapeDtypeStruct((t, MLA_HEADS * HEAD_PAD), BF16),
                   jax.ShapeDtypeStruct((t, MLA_HEADS * HEAD_PAD), BF16),
                   jax.ShapeDtypeStruct((t, MLA_HEADS * MLA_V), BF16),
                   jax.ShapeDtypeStruct((tp, KV_W), F32)],
        compiler_params=_cparams(("arbitrary",), 40),
        name="qkv",
    )(proj, kv, cos_t, sin_t, w_uq, w_ukv, gql, gkv, gq, gk)


def _ctx_kv_kernel(kv_ref, w_ukv_ref, gk_ref, k_ref, v_ref):
    kv = kv_ref[...]
    _keys_values(kv[:, :MLA_NOPE], kv[:, MLA_NOPE:], w_ukv_ref, gk_ref, None, None, k_ref, v_ref)


def _ctx_kv(kvc, w_ukv, gk, tm):
    r = kvc.shape[0]
    const = lambda i: (0, 0)
    return pl.pallas_call(
        _ctx_kv_kernel,
        grid=(r // tm,),
        in_specs=[pl.BlockSpec((tm, KV_W), lambda i: (i, 0)),
                  pl.BlockSpec(w_ukv.shape, const),
                  pl.BlockSpec(gk.shape, const)],
        out_specs=[pl.BlockSpec((tm, MLA_HEADS * HEAD_PAD), lambda i: (i, 0)),
                   pl.BlockSpec((tm, MLA_HEADS * MLA_V), lambda i: (i, 0))],
        out_shape=[jax.ShapeDtypeStruct((r, MLA_HEADS * HEAD_PAD), BF16),
                   jax.ShapeDtypeStruct((r, MLA_HEADS * MLA_V), BF16)],
        compiler_params=_cparams(("arbitrary",), 32),
        name="ctx_kv",
    )(kvc, w_ukv, gk)


_NT = (((1,), (1,)), ((), ()))


def _attn_kernel(has_ctx, *refs):
    if has_ctx:
        q_ref, kc_ref, vc_ref, k_ref, v_ref, w1_ref, w3_ref, w2_ref, o_ref, w1o_ref, w3o_ref, w2o_ref = refs
        w1o_ref[...] = w1_ref[...].astype(BF16)
        w3o_ref[...] = w3_ref[...].astype(BF16)
        w2o_ref[...] = w2_ref[...].astype(BF16)
    else:
        q_ref, k_ref, v_ref, o_ref = refs
    n_own = k_ref.shape[0]
    step = min(n_own, ATTN_KCHUNK)
    chunks = [(k_ref, v_ref, c * step, step) for c in range(n_own // step)]
    if has_ctx:
        chunks = [(kc_ref, vc_ref, 0, kc_ref.shape[0])] + chunks
    for h in range(q_ref.shape[1] // HEAD_PAD):
        qk = slice(h * HEAD_PAD, (h + 1) * HEAD_PAD)
        hv = slice(h * MLA_V, (h + 1) * MLA_V)
        tq = q_ref.shape[0]
        sub = min(tq, ATTN_QSUB)
        state = [None] * (tq // sub)
        for kr, vr, off, n in chunks:
            for t in range(tq // sub):
                rows = slice(t * sub, (t + 1) * sub)
                s = lax.dot_general(q_ref[rows, qk], kr[off:off + n, qk], _NT, preferred_element_type=F32)
                mc = jnp.max(s, axis=-1, keepdims=True)
                m_new = mc if state[t] is None else jnp.maximum(state[t][0], mc)
                p = jnp.exp2(s - m_new)
                pv = jnp.dot(p.astype(BF16), vr[off:off + n, hv], preferred_element_type=F32)
                ps = jnp.sum(p, axis=-1, keepdims=True)
                if state[t] is None:
                    state[t] = (m_new, ps, pv)
                else:
                    m, l, acc = state[t]
                    alpha = jnp.exp2(m - m_new)
                    state[t] = (m_new, alpha * l + ps, alpha * acc + pv)
        for t in range(tq // sub):
            _, l, acc = state[t]
            o_ref[t * sub:(t + 1) * sub, hv] = (acc / l).astype(BF16)


def _attention_latent(q, k, v, kc, vc, w1, w3, w2, layer, tp, nb, dec_seq, past, tq):
    t = q.shape[0]
    nq = dec_seq // tq
    row0 = tp // tq
    kblk0 = tp // dec_seq
    steps = nb * MLA_HEADS * nq
    depth, n_exp, d, de = w1.shape
    flat = [w1.reshape(depth, n_exp * d, de), w3.reshape(depth, n_exp * d, de), w2.reshape(depth, n_exp * de, d)]
    assert all(w.shape[1] % (8 * steps) == 0 for w in flat)
    step = lambda b, h, i: (b * MLA_HEADS + h) * nq + i
    w_in = [pl.BlockSpec((None, w.shape[1] // steps, w.shape[2]), lambda b, h, i: (layer, step(b, h, i), 0))
            for w in flat]
    w_out = [pl.BlockSpec((w.shape[1] // steps, w.shape[2]), lambda b, h, i: (step(b, h, i), 0)) for w in flat]
    o, w1b, w3b, w2b = pl.pallas_call(
        functools.partial(_attn_kernel, True),
        grid=(nb, MLA_HEADS, nq),
        in_specs=[pl.BlockSpec((tq, HEAD_PAD), lambda b, h, i: (row0 + b * nq + i, h)),
                  pl.BlockSpec((past, HEAD_PAD), lambda b, h, i: (b, h)),
                  pl.BlockSpec((past, MLA_V), lambda b, h, i: (b, h)),
                  pl.BlockSpec((dec_seq, HEAD_PAD), lambda b, h, i: (kblk0 + b, h)),
                  pl.BlockSpec((dec_seq, MLA_V), lambda b, h, i: (kblk0 + b, h))] + w_in,
        out_specs=[pl.BlockSpec((tq, MLA_V), lambda b, h, i: (b * nq + i, h))] + w_out,
        out_shape=[jax.ShapeDtypeStruct((t - tp, MLA_HEADS * MLA_V), BF16)]
                  + [jax.ShapeDtypeStruct(w.shape[1:], BF16) for w in flat],
        compiler_params=_cparams(("arbitrary", "arbitrary", "arbitrary"), 56),
        name="attn_latent",
    )(q, kc, vc, k, v, *flat)
    return o, w1b.reshape(n_exp, d, de), w3b.reshape(n_exp, d, de), w2b.reshape(n_exp, de, d)


def _attention_context(q, k, v, nseq, seq):
    return pl.pallas_call(
        functools.partial(_attn_kernel, False),
        grid=(nseq,),
        in_specs=[pl.BlockSpec((seq, MLA_HEADS * HEAD_PAD), lambda b: (b, 0)),
                  pl.BlockSpec((seq, MLA_HEADS * HEAD_PAD), lambda b: (b, 0)),
                  pl.BlockSpec((seq, MLA_HEADS * MLA_V), lambda b: (b, 0))],
        out_specs=pl.BlockSpec((seq, MLA_HEADS * MLA_V), lambda b: (b, 0)),
        out_shape=jax.ShapeDtypeStruct((nseq * seq, MLA_HEADS * MLA_V), BF16),
        compiler_params=_cparams(("arbitrary",), 32),
        name="attn_context",
    )(q, k, v)


def _mixers_kernel(npt, p_tiles, s_tiles,
                   cv_ref, cvp_ref, cvn_ref, gm_ref, sc_ref, scp_ref, scn_ref, bg_ref,
                   wdw_ref, bdw_ref, gcl_ref, bcl_ref, wpw_ref, bpw_ref,
                   gsl_ref, bsl_ref, wsp_ref, bsp_ref, wsc_ref,
                   o_ref, zs, ys, zsh, cb):
    i = pl.program_id(0)
    tm = cv_ref.shape[0]
    cw = cv_ref.shape[1] // 2
    pos = jnp.where(i < npt, i % p_tiles, (i - npt) % s_tiles)
    n_tiles = jnp.where(i < npt, p_tiles, s_tiles)
    keep_prev = (pos > 0).astype(F32)
    keep_next = (pos < n_tiles - 1).astype(F32)

    def glu(ref):
        a = ref[:, :cw].astype(F32)
        g = ref[:, cw:].astype(F32)
        return a * jax.nn.sigmoid(g)

    def prod(ref):
        return ref[:, :cw].astype(F32) * ref[:, cw:].astype(F32)

    n_lane = cw // LANE
    z_prev, z_main, z_next = glu(cvp_ref) * keep_prev, glu(cv_ref), glu(cvn_ref) * keep_next
    for c in range(n_lane):
        cs = slice(c * LANE, (c + 1) * LANE)
        zs[c, 0:HALO, :] = z_prev[:, cs]
        zs[c, HALO:HALO + tm, :] = z_main[:, cs]
        zs[c, HALO + tm:, :] = z_next[:, cs]
        for r in range(1, 8):
            zsh[r - 1, c] = zs[c, pl.ds(r, tm + 24), :]
    pad = CONV_K // 2
    half = tm // 2

    def conv_chunk(idx, carry):
        c = idx // 2
        base = pl.multiple_of((idx % 2) * half, half)
        acc = jnp.zeros((half, LANE), F32)
        for r in range(8):
            for a in range(4):
                tap = 8 * a + r - (HALO - pad)
                if 0 <= tap < CONV_K:
                    rows = pl.ds(base + 8 * a, half)
                    src = zs[c, rows, :] if r == 0 else zsh[r - 1, c, rows, :]
                    acc = acc + src * wdw_ref[c, tap:tap + 1, :]
        cb[c, pl.ds(base, half), :] = acc
        return carry

    lax.fori_loop(0, 2 * n_lane, conv_chunk, 0)
    z = jnp.concatenate([cb[c] for c in range(n_lane)], axis=1) + bdw_ref[...]
    z = _layernorm(z, gcl_ref[...], bcl_ref[...])
    z = z * jax.nn.sigmoid(z)
    o_b = jnp.dot(z.astype(BF16), wpw_ref[...], preferred_element_type=F32) + bpw_ref[...]
    o_ref[:, 0:cw] = o_b.astype(BF16)

    u = gm_ref[:, :cw].astype(F32)
    vg = _layernorm(gm_ref[:, cw:].astype(F32), gsl_ref[...], bsl_ref[...]).astype(BF16)
    hc = cw // GMLP_HEADS
    for n in range(tm // CHUNK):
        rows = slice(n * CHUNK, (n + 1) * CHUNK)
        for h in range(GMLP_HEADS):
            cols = slice(h * hc, (h + 1) * hc)
            mixed = jnp.dot(wsp_ref[h], vg[rows, cols], preferred_element_type=F32) + bsp_ref[:, cols]
            o_ref[rows, cw + h * hc:cw + (h + 1) * hc] = (u[rows, cols] * mixed).astype(BF16)

    ys[0:HALO, :] = prod(scp_ref) * keep_prev
    ys[HALO:HALO + tm, :] = prod(sc_ref)
    ys[HALO + tm:, :] = prod(scn_ref) * keep_next
    acc = jnp.zeros((tm, cw), F32)
    for tap in range(SC_K):
        acc = acc + ys[pl.ds(HALO - SC_K // 2 + tap, tm), :] * wsc_ref[tap:tap + 1, :]
    o_ref[:, 2 * cw:3 * cw] = (bg_ref[...].astype(F32) * acc).astype(BF16)


def _mixers(proj, weights, tp, seq, dec_seq, tm):
    t = proj.shape[0]
    cw = 512
    npt = tp // tm
    hb = tm // HALO
    last = t // HALO - 1
    const2 = lambda i: (0, 0)

    def main(col):
        return pl.BlockSpec((tm, 2 * cw), lambda i: (i, col))

    def prev(col):
        return pl.BlockSpec((HALO, 2 * cw), lambda i: (jnp.maximum(i * hb - 1, 0), col))

    def nxt(col):
        return pl.BlockSpec((HALO, 2 * cw), lambda i: (jnp.minimum((i + 1) * hb, last), col))

    w_specs = [pl.BlockSpec(w.shape, const2 if w.ndim == 2 else (lambda i: (0, 0, 0))) for w in weights]
    return pl.pallas_call(
        functools.partial(_mixers_kernel, npt, seq // tm, dec_seq // tm),
        grid=(t // tm,),
        in_specs=[main(0), prev(0), nxt(0), main(1), main(2), prev(2), nxt(2),
                  pl.BlockSpec((tm, cw), lambda i: (i, 6))] + w_specs,
        out_specs=pl.BlockSpec((tm, 3 * cw), lambda i: (i, 0)),
        out_shape=jax.ShapeDtypeStruct((t, 3 * cw), BF16),
        scratch_shapes=[pltpu.VMEM((cw // LANE, tm + 2 * HALO, LANE), F32), pltpu.VMEM((tm + 2 * HALO, cw), F32),
                        pltpu.VMEM((7, cw // LANE, tm + 24, LANE), F32), pltpu.VMEM((cw // LANE, tm, LANE), F32)],
        compiler_params=_cparams(("arbitrary",), 40),
        name="mixers",
    )(proj, proj, proj, proj, proj, proj, proj, proj, *weights)


def _out_proj_kernel(npt, n, oap_ref, oas_ref, ob_ref, xp_ref, xs_ref, mod_ref, g_ref, wo_ref, wr_ref,
                     xn_ref, hp_ref, r_ref, mix_a, mix_b):
    i = pl.program_id(0)
    ka = oap_ref.shape[1]
    mm_tile = jnp.minimum(i, n - 1)
    ep_tile = jnp.maximum(i - 1, 0)

    @pl.when(i == 0)
    def _():
        mix_b[...] = jnp.zeros_like(mix_b)

    def step(mix_w, mix_r):
        oa = jnp.where(mm_tile < npt, oap_ref[...], oas_ref[...])
        lhs = jnp.concatenate([oa, ob_ref[...]], axis=1)
        tm, d = mix_w.shape
        parts = OUT_PARTS
        for k in range(parts):
            cols = slice(k * d // parts, (k + 1) * d // parts)
            rows = slice(k * tm // parts, (k + 1) * tm // parts)
            mix_w[:, cols] = jnp.dot(lhs, wo_ref[:, cols], preferred_element_type=F32)
            x = jnp.where(ep_tile < npt, xp_ref[rows, :], xs_ref[rows, :])
            xn = x + mod_ref[2:3, :] * mix_r[rows, :]
            xn_ref[rows, :] = xn
            h = xn * _inv_rms(xn, xn.shape[-1]) * g_ref[...]
            h = h * (1.0 + mod_ref[4:5, :]) + mod_ref[3:4, :]
            hp_ref[rows, :] = _pack_rows(h)
            h_hi, h_lo = _split_bf16(h)
            r_ref[:, rows] = (lax.dot_general(wr_ref[...], h_hi, _NT, preferred_element_type=F32)
                              + lax.dot_general(wr_ref[...], h_lo, _NT, preferred_element_type=F32))

    @pl.when(i % 2 == 0)
    def _():
        step(mix_a, mix_b)

    @pl.when(i % 2 == 1)
    def _():
        step(mix_b, mix_a)


def _out_proj(oap, oas, ob, xp, xs, mod_l, g, wo, wr, tm, mod_index):
    t = ob.shape[0]
    d = xp.shape[1]
    npt = oap.shape[0] // tm
    n = t // tm
    const = lambda i: (0, 0)
    mm = lambda i: jnp.minimum(i, n - 1)
    ep = lambda i: jnp.maximum(i - 1, 0)

    def pair(w, tile, stacked=False):
        second = (lambda i: (jnp.maximum(tile(i), npt), 0)) if stacked else (
            lambda i: (jnp.maximum(tile(i) - npt, 0), 0))
        return [pl.BlockSpec((tm, w), lambda i: (jnp.minimum(tile(i), npt - 1), 0)), pl.BlockSpec((tm, w), second)]

    x_specs = pair(d, ep, stacked=xs is None)
    if xs is None:
        xs = xp
    return pl.pallas_call(
        functools.partial(_out_proj_kernel, npt, n),
        grid=(n + 1,),
        in_specs=pair(oap.shape[1], mm) + [pl.BlockSpec((tm, ob.shape[1]), lambda i: (mm(i), 0))] + x_specs + [
                  pl.BlockSpec((None, 6, d), lambda i: (mod_index(ep(i), tm), 0, 0)),
                  pl.BlockSpec((1, d), const),
                  pl.BlockSpec(wo.shape, const, pipeline_mode=pl.Buffered(1)),
                  pl.BlockSpec(wr.shape, const, pipeline_mode=pl.Buffered(1))],
        out_specs=[pl.BlockSpec((tm, d), lambda i: (ep(i), 0)),
                   pl.BlockSpec((tm, d // 2), lambda i: (ep(i), 0)),
                   pl.BlockSpec((LANE, tm), lambda i: (0, ep(i)))],
        out_shape=[jax.ShapeDtypeStruct((t, d), F32),
                   jax.ShapeDtypeStruct((t, d // 2), U32),
                   jax.ShapeDtypeStruct((LANE, t), F32)],
        scratch_shapes=[pltpu.VMEM((tm, d), F32), pltpu.VMEM((tm, d), F32)],
        compiler_params=_cparams(("arbitrary",), 56),
        name="out_proj",
    )(oap, oas, ob, xp, xs, mod_l, g, wo, wr)


N_BUCKET_PAD = 64
PAIR_FIRST = (0, 0, 0, 1, 1, 3)
PAIR_SECOND = (1, 2, 3, 3, 2, 2)


def _route_kernel(rt_ref, b_ref, tri_ref, bucket_ref, rank_ref, gate_ref, cnt_ref, run):
    i = pl.program_id(0)
    tm = rt_ref.shape[1]
    ne = N_GROUPS * GROUP_SIZE

    @pl.when(i == 0)
    def _():
        run[...] = jnp.zeros_like(run)

    sc = jax.nn.sigmoid(rt_ref[0:ne, :] + rt_ref[ne:2 * ne, :])
    sel = sc + b_ref[...]
    s = [sel[r * N_GROUPS:(r + 1) * N_GROUPS, :] for r in range(GROUP_SIZE)]
    c = [sc[r * N_GROUPS:(r + 1) * N_GROUPS, :] for r in range(GROUP_SIZE)]
    hi01, lo01 = jnp.maximum(s[0], s[1]), jnp.minimum(s[0], s[1])
    hi23, lo23 = jnp.maximum(s[2], s[3]), jnp.minimum(s[2], s[3])
    g_score = jnp.maximum(hi01, hi23) + jnp.maximum(jnp.minimum(hi01, hi23), jnp.maximum(lo01, lo23))
    grp = lax.broadcasted_iota(jnp.int32, g_score.shape, 0)
    g_best = jnp.min(jnp.where(g_score == jnp.max(g_score, axis=0, keepdims=True), grp, N_GROUPS),
                     axis=0, keepdims=True)
    own = grp == g_best
    v = [jnp.sum(jnp.where(own, s[r], 0.0), axis=0, keepdims=True) for r in range(GROUP_SIZE)]
    w = [jnp.sum(jnp.where(own, c[r], 0.0), axis=0, keepdims=True) for r in range(GROUP_SIZE)]

    def first_max(vals):
        best, idx, gate = vals[0], jnp.zeros_like(g_best), w[0]
        for r in range(1, GROUP_SIZE):
            upd = vals[r] > best
            best = jnp.where(upd, vals[r], best)
            idx = jnp.where(upd, r, idx)
            gate = jnp.where(upd, w[r], gate)
        return idx, gate

    r1, w1 = first_max(v)
    r2, w2 = first_max([jnp.where(r1 == r, -jnp.inf, v[r]) for r in range(GROUP_SIZE)])
    wsum = w1 + w2
    swap = r1 > r2
    r_lo = jnp.where(swap, r2, r1)
    r_hi = jnp.where(swap, r1, r2)
    g_lo = jnp.where(swap, w2, w1) / wsum
    g_hi = jnp.where(swap, w1, w2) / wsum
    lex = r_lo * 3 - ((r_lo * (r_lo - 1)) >> 1) + (r_hi - r_lo - 1)
    pair = jnp.where(lex == 3, 4, jnp.where(lex == 4, 3, lex))
    hi_first = pair == N_PAIRS - 1
    gate_ref[0:1, :] = jnp.where(hi_first, g_hi, g_lo)
    gate_ref[1:2, :] = jnp.where(hi_first, g_lo, g_hi)
    bucket = g_best * N_PAIRS + pair
    bucket_ref[...] = bucket

    onehot = (lax.broadcasted_iota(jnp.int32, (N_BUCKET_PAD, tm), 0) == bucket).astype(F32)
    before = jnp.dot(onehot.astype(BF16), tri_ref[...], preferred_element_type=F32) + run[...]
    rank_ref[...] = jnp.sum(onehot * before, axis=0, keepdims=True).astype(jnp.int32)
    run[...] += jnp.sum(onehot, axis=1, keepdims=True)

    @pl.when(i == pl.num_programs(0) - 1)
    def _():
        cnt_ref[...] = run[...]


def _route(rt, router_b, n_blk, tm):
    t = rt.shape[1]
    ne = router_b.shape[0]
    b_col = router_b.astype(F32).reshape(N_GROUPS, GROUP_SIZE).T.reshape(ne, 1)
    tri = (jnp.arange(tm)[:, None] < jnp.arange(tm)[None, :]).astype(BF16)
    bucket, rank, gate, cnt = pl.pallas_call(
        _route_kernel,
        grid=(t // tm,),
        in_specs=[pl.BlockSpec((LANE, tm), lambda i: (0, i)),
                  pl.BlockSpec((ne, 1), lambda i: (0, 0)),
                  pl.BlockSpec((tm, tm), lambda i: (0, 0))],
        out_specs=[pl.BlockSpec((1, tm), lambda i: (0, i)),
                   pl.BlockSpec((1, tm), lambda i: (0, i)),
                   pl.BlockSpec((2, tm), lambda i: (0, i)),
                   pl.BlockSpec((N_BUCKET_PAD, 1), lambda i: (0, 0))],
        out_shape=[jax.ShapeDtypeStruct((1, t), jnp.int32),
                   jax.ShapeDtypeStruct((1, t), jnp.int32),
                   jax.ShapeDtypeStruct((2, t), F32),
                   jax.ShapeDtypeStruct((N_BUCKET_PAD, 1), F32)],
        scratch_shapes=[pltpu.VMEM((N_BUCKET_PAD, 1), F32)],
        compiler_params=_cparams(("arbitrary",), 32),
        name="route",
    )(rt, b_col, tri)

    n_bucket = N_GROUPS * N_PAIRS
    counts = cnt[:n_bucket, 0].astype(jnp.int32)
    padded = (counts + MOE_BM - 1) // MOE_BM * MOE_BM
    pad_end = jnp.cumsum(padded)
    pad_start = pad_end - padded
    bucket = bucket[0]
    sel = bucket[:, None] == jnp.arange(n_bucket, dtype=jnp.int32)[None, :]
    pos = jnp.sum(jnp.where(sel, pad_start[None, :], 0), axis=1).astype(jnp.int32) + rank[0]
    n_used = (pad_end[-1] // MOE_BM).astype(jnp.int32)
    blk = jnp.minimum(jnp.arange(n_blk, dtype=jnp.int32), n_used - 1)
    blk_bucket = jnp.sum((pad_end[None, :] <= (blk * MOE_BM)[:, None]).astype(jnp.int32), axis=1)
    blk_bucket = jnp.minimum(blk_bucket, n_bucket - 1)
    pair_lo = jnp.array(PAIR_FIRST, jnp.int32)
    pair_hi = jnp.array(PAIR_SECOND, jnp.int32)
    blk_pair = blk_bucket % N_PAIRS
    first = (blk_bucket // N_PAIRS) * GROUP_SIZE
    is_pair = blk_pair[:, None] == jnp.arange(N_PAIRS, dtype=jnp.int32)[None, :]
    blk_a = first + jnp.sum(jnp.where(is_pair, pair_lo[None, :], 0), axis=1)
    blk_b = first + jnp.sum(jnp.where(is_pair, pair_hi[None, :], 0), axis=1)
    is_bucket = blk_bucket[:, None] == jnp.arange(n_bucket, dtype=jnp.int32)[None, :]
    per_bucket = lambda v: jnp.sum(jnp.where(is_bucket, v[None, :], 0), axis=1)
    first_blk = per_bucket(pad_start // MOE_BM)
    n_in_bucket = jnp.maximum(per_bucket(padded // MOE_BM), 1)
    src = first_blk + (blk - first_blk + n_in_bucket - 1) % n_in_bucket
    steps = jnp.arange(n_blk, dtype=jnp.int32)
    used = steps < n_used
    blk_out = jnp.where(used, src, steps).astype(jnp.int32)
    blk_rows = jnp.where(used, jnp.clip(per_bucket(pad_start + counts) - src * MOE_BM, 0, MOE_BM), 0)
    gates = jnp.pad(gate.T, ((0, 0), (0, GATE_W - 2)))
    plan = (src.astype(jnp.int32), blk_a.astype(jnp.int32), blk_b.astype(jnp.int32), blk_rows.astype(jnp.int32),
            blk_out)
    rows_in_blk = jnp.clip(per_bucket(pad_start + counts) - blk * MOE_BM, 0, MOE_BM)
    fill = jnp.logical_or(jnp.logical_not(used), rows_in_blk < MOE_BM).astype(jnp.int32)
    return pos, gates, plan, fill


def _dispatch_kernel(fill_ref, pos_ref, hp_ref, gt_ref, xs_ref, buf, zeros, sem, zsem):
    i = pl.program_id(0)
    slot = i % 2
    tm = hp_ref.shape[0]
    n_data = hp_ref.shape[1] // LANE
    blk_rows = zeros.shape[0]

    def wait_all(s):
        pltpu.make_async_copy(buf.at[s], xs_ref.at[pl.ds(0, tm * X_ROWS)], sem.at[s]).wait()

    @pl.when(i == 0)
    def _():
        zeros[...] = jnp.zeros_like(zeros)

        def zero_block(b):
            return pltpu.make_async_copy(zeros, xs_ref.at[pl.ds(pl.multiple_of(b * blk_rows, blk_rows), blk_rows)],
                                         zsem)

        def start(b, carry):
            @pl.when(fill_ref[b] != 0)
            def _():
                zero_block(b).start()
            return carry

        def wait(b, carry):
            @pl.when(fill_ref[b] != 0)
            def _():
                zero_block(b).wait()
            return carry

        lax.fori_loop(0, fill_ref.shape[0], start, 0)
        lax.fori_loop(0, fill_ref.shape[0], wait, 0)

    for s in range(2):
        @pl.when(slot == s)
        def _():
            for j in range(n_data):
                buf[s, pl.ds(j, tm, stride=X_ROWS), :] = hp_ref[:, j * LANE:(j + 1) * LANE]
            buf[s, pl.ds(n_data, tm, stride=X_ROWS), :] = pltpu.bitcast(gt_ref[...], U32)
            for r in range(tm):
                pltpu.make_async_copy(buf.at[s, pl.ds(r * X_ROWS, X_ROWS)],
                                      xs_ref.at[pl.ds(pos_ref[0, 0, r] * X_ROWS, X_ROWS)],
                                      sem.at[s]).start(priority=r % N_DMA_PRIORITIES)

    @pl.when(i > 0)
    def _():
        wait_all(1 - slot)

    @pl.when(i == pl.num_programs(0) - 1)
    def _():
        wait_all(slot)


def _dispatch(hp, gates, pos, fill, tm):
    t, w = hp.shape
    assert w == (X_ROWS - 1) * LANE and GATE_W == LANE
    n_blk = fill.shape[0]
    grid_spec = pltpu.PrefetchScalarGridSpec(
        num_scalar_prefetch=1,
        grid=(t // tm,),
        in_specs=[pl.BlockSpec((1, 1, tm), lambda i, fill: (i, 0, 0), memory_space=pltpu.SMEM),
                  pl.BlockSpec((tm, w), lambda i, fill: (i, 0)),
                  pl.BlockSpec((tm, GATE_W), lambda i, fill: (i, 0))],
        out_specs=pl.BlockSpec(memory_space=pl.ANY),
        scratch_shapes=[pltpu.VMEM((2, tm * X_ROWS, LANE), U32), pltpu.VMEM((MOE_BM * X_ROWS, LANE), U32),
                        pltpu.SemaphoreType.DMA((2,)), pltpu.SemaphoreType.DMA(())],
    )
    return pl.pallas_call(
        _dispatch_kernel,
        grid_spec=grid_spec,
        out_shape=jax.ShapeDtypeStruct((n_blk * MOE_BM * X_ROWS, LANE), U32),
        compiler_params=_cparams(("arbitrary",), 32),
        name="dispatch",
    )(fill, pos.reshape(t // tm, 1, tm), hp, gates)


def _moe_kernel(src_ref, ea_ref, eb_ref, nv_ref, dst_ref, x_ref, w1a_ref, w3a_ref, w2a_ref, w1b_ref, w3b_ref,
                w2b_ref, y_ref):
    del src_ref, ea_ref, eb_ref, dst_ref
    nv = nv_ref[pl.program_id(0)]
    bm = x_ref.shape[0] // X_ROWS

    def experts(m):
        halves = [_unpack_rows(x_ref[pl.ds(j, m, stride=X_ROWS), :]) for j in range(X_ROWS - 1)]
        x = jnp.concatenate([a.astype(BF16) for a, _ in halves] + [b.astype(BF16) for _, b in halves], axis=1)
        gates = pltpu.bitcast(x_ref[pl.ds(X_ROWS - 1, m, stride=X_ROWS), :], F32)

        def hidden(w1_ref, w3_ref, g):
            a = jnp.dot(x, w1_ref[...], preferred_element_type=F32)
            b = jnp.dot(x, w3_ref[...], preferred_element_type=F32)
            return (a * jax.nn.sigmoid(a) * b * g).astype(BF16)

        y = jnp.dot(hidden(w1a_ref, w3a_ref, gates[:, 0:1]), w2a_ref[...], preferred_element_type=F32)
        y += jnp.dot(hidden(w1b_ref, w3b_ref, gates[:, 1:2]), w2b_ref[...], preferred_element_type=F32)
        yp = _pack_rows(y)
        for j in range(Y_ROWS):
            y_ref[pl.ds(j, m, stride=Y_ROWS), :] = yp[:, j * LANE:(j + 1) * LANE]
        if m < bm:
            y_ref[m * Y_ROWS:, :] = jnp.zeros(((bm - m) * Y_ROWS, LANE), y_ref.dtype)

    @pl.when(nv > bm // 2)
    def _():
        experts(bm)

    @pl.when(jnp.logical_and(nv > 0, nv <= bm // 2))
    def _():
        experts(bm // 2)

    @pl.when(nv == 0)
    def _():
        y_ref[...] = jnp.zeros_like(y_ref)


def _moe(xs, w1, w3, w2, plan):
    n_blk = xs.shape[0] // (MOE_BM * X_ROWS)
    _, d, de = w1.shape
    assert d == 2 * Y_ROWS * LANE

    def expert(second, rows, cols):
        if second:
            return pl.BlockSpec((None, rows, cols), lambda i, src, ea, eb, nv, dst: (eb[i], 0, 0))
        return pl.BlockSpec((None, rows, cols), lambda i, src, ea, eb, nv, dst: (ea[i], 0, 0))

    grid_spec = pltpu.PrefetchScalarGridSpec(
        num_scalar_prefetch=5,
        grid=(n_blk,),
        in_specs=[pl.BlockSpec((MOE_BM * X_ROWS, LANE), lambda i, src, ea, eb, nv, dst: (src[i], 0)),
                  expert(False, d, de), expert(False, d, de), expert(False, de, d),
                  expert(True, d, de), expert(True, d, de), expert(True, de, d)],
        out_specs=pl.BlockSpec((MOE_BM * Y_ROWS, LANE), lambda i, src, ea, eb, nv, dst: (dst[i], 0)),
    )
    return pl.pallas_call(
        _moe_kernel,
        grid_spec=grid_spec,
        out_shape=jax.ShapeDtypeStruct((n_blk * MOE_BM * Y_ROWS, LANE), U32),
        compiler_params=_cparams(("arbitrary",), 48),
        name="moe",
    )(*plan, xs, w1, w3, w2, w1, w3, w2)


def _combine_kernel(npt, pos_ref, nxt_ref, xn_ref, mod_ref, yb_ref, op_ref, os_ref, buf, sem):
    i = pl.program_id(0)
    out = xn_ref[...] + mod_ref[5:6, :] * _gathered_rows(pos_ref, nxt_ref, yb_ref, buf, sem)

    @pl.when(i < npt)
    def _():
        op_ref[...] = out

    @pl.when(i >= npt)
    def _():
        os_ref[...] = out


def _combine(xn, yb, pos, mod_l, tp, tm, mod_index):
    t, d = xn.shape
    npt = tp // tm
    n = t // tm
    pos3 = pos.reshape(n, 1, tm)
    return pl.pallas_call(
        functools.partial(_combine_kernel, npt),
        grid=(n,),
        in_specs=[pl.BlockSpec((1, 1, tm), lambda i: (i, 0, 0), memory_space=pltpu.SMEM),
                  pl.BlockSpec((1, 1, tm), lambda i: (jnp.minimum(i + 1, n - 1), 0, 0), memory_space=pltpu.SMEM),
                  pl.BlockSpec((tm, d), lambda i: (i, 0)),
                  pl.BlockSpec((None, 6, d), lambda i: (mod_index(i, tm), 0, 0)),
                  pl.BlockSpec(memory_space=pl.ANY)],
        out_specs=[pl.BlockSpec((tm, d), lambda i: (jnp.minimum(i, npt - 1), 0)),
                   pl.BlockSpec((tm, d), lambda i: (jnp.maximum(i - npt, 0), 0))],
        out_shape=[jax.ShapeDtypeStruct((tp, d), F32),
                   jax.ShapeDtypeStruct((t - tp, d), F32)],
        scratch_shapes=[pltpu.VMEM((2, tm * Y_ROWS, LANE), U32), pltpu.SemaphoreType.DMA((2,))],
        compiler_params=_cparams(("arbitrary",), 32),
        name="combine",
    )(pos3, pos3, xn, mod_l, yb)


def _pad_cols(x, n):
    return jnp.pad(x, ((0, 0), (0, n - x.shape[1])))


def _rope_tables(dec_seq, tm):
    n_freq = MLA_ROPE // 4
    pos = jnp.arange(dec_seq, dtype=jnp.int32)
    row = (pos // GRID_W).astype(F32)
    col = (pos % GRID_W).astype(F32)
    inv = ROPE_THETA ** (-jnp.arange(n_freq, dtype=F32) / n_freq)
    ar = row[:, None] * inv[None, :]
    ac = col[:, None] * inv[None, :]
    ones = jnp.ones((dec_seq, LANE - MLA_ROPE), F32)
    cos = jnp.concatenate([jnp.cos(ar), jnp.cos(ar), jnp.cos(ac), jnp.cos(ac), ones], axis=1)
    sin = jnp.concatenate([-jnp.sin(ar), jnp.sin(ar), -jnp.sin(ac), jnp.sin(ac), 0.0 * ones], axis=1)
    ident_c = jnp.ones((tm, LANE), F32)
    ident_s = jnp.zeros((tm, LANE), F32)
    return jnp.concatenate([cos, ident_c], axis=0), jnp.concatenate([sin, ident_s], axis=0)


def _layer_params(l, q_lora, kv_lora, w_in, g_q_lora, w_uq, g_kv_lora, w_ukv, g_qk_q, g_qk_k,
                  w_dw31, b_dw31, g_conv_ln, b_conv_ln, w_pw, b_pw, g_sgu_ln, b_sgu_ln,
                  w_spatial, b_spatial, w_sc3, w_out):
    d = w_in.shape[1]
    o1 = q_lora
    o2 = o1 + kv_lora
    o3 = o2 + MLA_ROPE
    o4 = o3 + 1024
    o5 = o4 + 1024
    w = w_in[l]
    zeros = lambda n: jnp.zeros((d, n), w.dtype)
    w_in_p = jnp.concatenate(
        [w[:, o3:o4], w[:, o4:o5], w[:, o5 + 512:o5 + 1536], w[:, o5:o5 + 512],
         w[:, :o1], zeros(512 - q_lora),
         w[:, o1:o2], w[:, o2:o3], zeros(PROJ_W - KV_COL - kv_lora - MLA_ROPE)], axis=1).astype(BF16)

    uq = w_uq[l].reshape(q_lora, MLA_HEADS, MLA_QK)
    uq = jnp.pad(uq, ((0, 512 - q_lora), (0, 0), (0, HEAD_PAD - MLA_QK)))
    w_uq_p = uq.reshape(512, MLA_HEADS * HEAD_PAD).astype(BF16)
    ukv = w_ukv[l].reshape(kv_lora, MLA_HEADS, MLA_NOPE + MLA_V)
    w_ukv_p = jnp.concatenate([ukv[:, :, :MLA_NOPE].reshape(kv_lora, -1),
                               ukv[:, :, MLA_NOPE:].reshape(kv_lora, -1)], axis=1).astype(BF16)
    row = lambda v: v.reshape(1, -1).astype(F32)
    qkv_w = (w_uq_p, w_ukv_p, _pad_cols(row(g_q_lora[l]), 512), row(g_kv_lora[l]),
             _pad_cols(row(g_qk_q[l]), HEAD_PAD), _pad_cols(row(g_qk_k[l]), HEAD_PAD))

    hc = w_pw.shape[1] // GMLP_HEADS
    bsp = jnp.repeat(b_spatial[l].T.astype(F32), hc, axis=1)
    w_dw = jnp.pad(w_dw31[l].astype(F32), ((0, 32 - CONV_K), (0, 0)))
    w_dw = w_dw.reshape(32, -1, LANE).transpose(1, 0, 2)
    mix_w = (w_dw, row(b_dw31[l]),
             row(g_conv_ln[l]), row(b_conv_ln[l]), w_pw[l].astype(BF16), row(b_pw[l]),
             row(g_sgu_ln[l]), row(b_sgu_ln[l]), w_spatial[l].astype(BF16), bsp,
             jnp.pad(w_sc3[l].astype(F32), ((0, 8 - SC_K), (0, 0))))
    return w_in_p, qkv_w, mix_w, w_out[l].astype(BF16)


def _tile(limit, *sizes):
    tm = limit
    while any(s % tm for s in sizes):
        tm //= 2
    return tm


def kernel(x_prompt, x_sample, cache_mla_ckv, cache_mla_krope, c, c_ctx, w_mod, b_mod, g_norm1, g_norm2, w_in, g_q_lora, w_uq, g_kv_lora, w_ukv, g_qk_q, g_qk_k, w_dw31, b_dw31, g_conv_ln, b_conv_ln, w_pw, b_pw, g_sgu_ln, b_sgu_ln, w_spatial, b_spatial, w_sc3, w_out, router_w, router_b, w1, w3, w2):
    nseq, seq, d = x_prompt.shape
    nb, dec_seq, _ = x_sample.shape
    depth = w_mod.shape[0]
    past = cache_mla_ckv.shape[2]
    q_lora = g_q_lora.shape[1]
    kv_lora = g_kv_lora.shape[1]
    n_exp = router_w.shape[1]
    tp = nseq * seq
    ts = nb * dec_seq
    t = tp + ts
    assert tp % dec_seq == 0 and seq % CHUNK == 0 and dec_seq % GRID_W == 0
    assert nb + 1 <= 8 and kv_lora == MLA_NOPE and n_exp == N_GROUPS * GROUP_SIZE

    tm_in = _tile(512, tp, dec_seq)
    tm_qkv = _tile(512, tp, dec_seq)
    tm_row = _tile(256, seq, dec_seq)
    tq = _tile(1024, dec_seq)

    def mod_index(i, tm):
        npt = tp // tm
        return jnp.where(i < npt, 0, 1 + (i - npt) // (dec_seq // tm))

    cvec = jnp.concatenate([c_ctx[None, :], c, jnp.zeros((8 - 1 - nb, d), F32)], axis=0)
    mod = _modulation(cvec, w_mod, b_mod).reshape(depth, 8, 6, d)
    cos_t, sin_t = _rope_tables(dec_seq, tm_qkv)

    rw = router_w.T.reshape(N_GROUPS, GROUP_SIZE, d).transpose(1, 0, 2).reshape(n_exp, d)
    rw_hi = rw.astype(BF16)
    rw_lo = (rw - rw_hi.astype(F32)).astype(BF16)
    wr = jnp.pad(jnp.concatenate([rw_hi, rw_lo], axis=0), ((0, LANE - 2 * n_exp), (0, 0)))
    n_blk = (t + N_GROUPS * N_PAIRS * (MOE_BM - 1) + MOE_BM - 1) // MOE_BM

    xp = x_prompt.reshape(tp, d)
    xs = x_sample.reshape(ts, d)
    ckv_states = []
    krope_states = []
    for l in range(depth):
        w_in_p, qkv_w, mix_w, w_out_b = _layer_params(
            l, q_lora, kv_lora, w_in, g_q_lora, w_uq, g_kv_lora, w_ukv, g_qk_q, g_qk_k,
            w_dw31, b_dw31, g_conv_ln, b_conv_ln, w_pw, b_pw, g_sgu_ln, b_sgu_ln,
            w_spatial, b_spatial, w_sc3, w_out)
        mod_l = mod[l]

        if l == 0:
            proj, kv = _in_proj(xp, xs, mod_l, g_norm1[l].reshape(1, d), w_in_p, tm_in, mod_index)
        else:
            proj, kv, xp = _in_proj_fused(xn, yb, pos, mod[l - 1], mod_l, g_norm1[l].reshape(1, d), w_in_p,
                                          tm_in, mod_index)
            xs = None
        q, k, v, state = _qkv(proj, kv, cos_t, sin_t, *qkv_w, tp, dec_seq, q_lora, tm_qkv)
        kvc = jnp.concatenate([cache_mla_ckv[:, l], cache_mla_krope[:, l],
                               jnp.zeros((nb, past, KV_W - kv_lora - MLA_ROPE), F32)], axis=-1)
        kc, vc = _ctx_kv(kvc.reshape(nb * past, KV_W), qkv_w[1], qkv_w[5], _tile(512, past))
        o_as, w1b, w3b, w2b = _attention_latent(q, k, v, kc, vc, w1, w3, w2, l, tp, nb, dec_seq, past, tq)
        o_ap = _attention_context(q, k, v, nseq, seq)
        o_bcd = _mixers(proj, mix_w, tp, seq, dec_seq, tm_row)

        xn, hp, r = _out_proj(o_ap, o_as, o_bcd, xp, xs, mod_l, g_norm2[l].reshape(1, d), w_out_b, wr,
                              tm_qkv, mod_index)
        pos, gates, plan, fill = _route(r, router_b, n_blk, tm_qkv)
        xd = _dispatch(hp, gates, pos, fill, tm_row)
        yb = _moe(xd, w1b, w3b, w2b, plan)
        if l == depth - 1:
            xp, xs = _combine(xn, yb, pos, mod_l, tp, tm_row, mod_index)

        ckv_states.append(state[:, :kv_lora].reshape(nseq, seq, kv_lora))
        krope_states.append(state[:, kv_lora:kv_lora + MLA_ROPE].reshape(nseq, seq, MLA_ROPE))

    return (xp.reshape(nseq, seq, d), xs.reshape(nb, dec_seq, d),
            jnp.stack(ckv_states, axis=1), jnp.stack(krope_states, axis=1))
```

```python
import functools

import jax
import jax.numpy as jnp
from jax import lax
from jax.experimental import pallas as pl
from jax.experimental.pallas import tpu as pltpu

F32 = jnp.float32
BF16 = jnp.bfloat16
U32 = jnp.uint32

MLA_HEADS = 4
MLA_NOPE = 128
MLA_ROPE = 64
MLA_V = 128
MLA_QK = MLA_NOPE + MLA_ROPE
HEAD_PAD = 256
GRID_W = 64
ROPE_THETA = 10000.0
CONV_K = 31
SC_K = 3
CHUNK = 128
GMLP_HEADS = 4
N_GROUPS = 8
GROUP_SIZE = 4
N_PAIRS = 6
NORM_EPS = 1e-6

LANE = 128
HALO = 16
PROJ_W = 4352
CQ_COL = 3584
KV_COL = 4096
KV_W = 256
IN_TN = 1536
MOE_BM = 256
OUT_PARTS = 4
N_DMA_PRIORITIES = 2
ATTN_QSUB = 1024
ATTN_KCHUNK = 1024
LOG2_E = 1.4426950408889634
GATE_W = 128
X_ROWS = 9
Y_ROWS = 8
VMEM_CAP = 56 * 1024 * 1024


def _cparams(sem, vmem_mb):
    return pltpu.CompilerParams(dimension_semantics=sem,
                                vmem_limit_bytes=min(vmem_mb * 1024 * 1024, VMEM_CAP))


def _inv_rms(x, n):
    return lax.rsqrt(jnp.sum(x * x, axis=-1, keepdims=True) * (1.0 / n) + NORM_EPS)


def _layernorm(x, g, b):
    mu = jnp.mean(x, axis=-1, keepdims=True)
    xc = x - mu
    var = jnp.mean(xc * xc, axis=-1, keepdims=True)
    return xc * lax.rsqrt(var + NORM_EPS) * g + b


def _split_bf16(x):
    hi = x.astype(BF16)
    lo = (x - hi.astype(F32)).astype(BF16)
    return hi, lo


def _pack_rows(x):
    n = x.shape[1] // 2
    hi = pltpu.bitcast(x[:, :n].astype(BF16).astype(F32), U32)
    lo = pltpu.bitcast(x[:, n:].astype(BF16).astype(F32), U32)
    return (hi & jnp.uint32(0xFFFF0000)) | (lo >> 16)


def _unpack_rows(u):
    a = pltpu.bitcast(u & jnp.uint32(0xFFFF0000), F32)
    b = pltpu.bitcast(u << 16, F32)
    return a, b


def _mod_kernel(c_ref, w_ref, b_ref, o_ref):
    c = c_ref[...]
    a_hi, a_lo = _split_bf16(c * jax.nn.sigmoid(c))
    w_hi, w_lo = _split_bf16(w_ref[...])
    acc = jnp.dot(a_hi, w_hi, preferred_element_type=F32)
    acc += jnp.dot(a_lo, w_hi, preferred_element_type=F32)
    acc += jnp.dot(a_hi, w_lo, preferred_element_type=F32)
    o_ref[...] = acc + b_ref[...]


def _modulation(cvec, w_mod, b_mod):
    depth, d, n = w_mod.shape
    tn = 1024
    return pl.pallas_call(
        _mod_kernel,
        grid=(depth, n // tn),
        in_specs=[pl.BlockSpec((8, d), lambda l, j: (0, 0)),
                  pl.BlockSpec((None, d, tn), lambda l, j: (l, 0, j)),
                  pl.BlockSpec((None, 1, tn), lambda l, j: (l, 0, j))],
        out_specs=pl.BlockSpec((None, 8, tn), lambda l, j: (l, 0, j)),
        out_shape=jax.ShapeDtypeStruct((depth, 8, n), F32),
        compiler_params=_cparams(("arbitrary", "arbitrary"), 40),
        name="modulation",
    )(cvec, w_mod, b_mod.reshape(depth, 1, n))


def _norm_project(x, mod_ref, g_ref, w_ref, proj_ref, kv_ref):
    y = x * _inv_rms(x, x.shape[-1]) * g_ref[...]
    h = (y * (1.0 + mod_ref[1:2, :]) + mod_ref[0:1, :]).astype(BF16)
    for c0 in range(0, PROJ_W, IN_TN):
        c1 = min(c0 + IN_TN, PROJ_W)
        acc = jnp.dot(h, w_ref[:, c0:c1], preferred_element_type=F32)
        proj_ref[:, c0:c1] = acc.astype(BF16)
        if c0 <= KV_COL < c1:
            kv_ref[...] = acc[:, KV_COL - c0:KV_COL - c0 + KV_W]


def _in_proj_kernel(npt, xp_ref, xs_ref, mod_ref, g_ref, w_ref, proj_ref, kv_ref):
    x = jnp.where(pl.program_id(0) < npt, xp_ref[...], xs_ref[...])
    _norm_project(x, mod_ref, g_ref, w_ref, proj_ref, kv_ref)


def _gathered_rows(pos_ref, nxt_ref, yb_ref, buf, sem):
    i = pl.program_id(0)
    slot = i % 2
    tm = buf.shape[1] // Y_ROWS

    def gather(idx_ref, s):
        for r in range(tm):
            src = pl.multiple_of(idx_ref[0, 0, r] * Y_ROWS, Y_ROWS)
            pltpu.make_async_copy(yb_ref.at[pl.ds(src, Y_ROWS)], buf.at[s, pl.ds(r * Y_ROWS, Y_ROWS)],
                                  sem.at[s]).start(priority=r % N_DMA_PRIORITIES)

    @pl.when(i == 0)
    def _():
        gather(pos_ref, 0)

    for s in range(2):
        @pl.when(jnp.logical_and(i + 1 < pl.num_programs(0), slot == 1 - s))
        def _():
            gather(nxt_ref, s)

    pltpu.make_async_copy(yb_ref.at[pl.ds(0, tm * Y_ROWS)], buf.at[slot], sem.at[slot]).wait()
    halves = [_unpack_rows(buf[slot, pl.ds(j, tm, stride=Y_ROWS), :]) for j in range(Y_ROWS)]
    return jnp.concatenate([a for a, _ in halves] + [b for _, b in halves], axis=1)


def _in_proj_fused_kernel(pos_ref, nxt_ref, xn_ref, prev_mod_ref, mod_ref, g_ref, w_ref, yb_ref,
                          proj_ref, kv_ref, x_ref, buf, sem):
    x = xn_ref[...] + prev_mod_ref[5:6, :] * _gathered_rows(pos_ref, nxt_ref, yb_ref, buf, sem)
    x_ref[...] = x
    _norm_project(x, mod_ref, g_ref, w_ref, proj_ref, kv_ref)


def _in_proj_fused(xn, yb, pos, prev_mod, mod_l, g, w, tm, mod_index):
    t, d = xn.shape
    n = t // tm
    pos3 = pos.reshape(n, 1, tm)
    mod_spec = pl.BlockSpec((None, 6, d), lambda i: (mod_index(i, tm), 0, 0))
    return pl.pallas_call(
        _in_proj_fused_kernel,
        grid=(n,),
        in_specs=[pl.BlockSpec((1, 1, tm), lambda i: (i, 0, 0), memory_space=pltpu.SMEM),
                  pl.BlockSpec((1, 1, tm), lambda i: (jnp.minimum(i + 1, n - 1), 0, 0), memory_space=pltpu.SMEM),
                  pl.BlockSpec((tm, d), lambda i: (i, 0)),
                  mod_spec, mod_spec,
                  pl.BlockSpec((1, d), lambda i: (0, 0)),
                  pl.BlockSpec(w.shape, lambda i: (0, 0), pipeline_mode=pl.Buffered(1)),
                  pl.BlockSpec(memory_space=pl.ANY)],
        out_specs=[pl.BlockSpec((tm, PROJ_W), lambda i: (i, 0)),
                   pl.BlockSpec((tm, KV_W), lambda i: (i, 0)),
                   pl.BlockSpec((tm, d), lambda i: (i, 0))],
        out_shape=[jax.ShapeDtypeStruct((t, PROJ_W), BF16),
                   jax.ShapeDtypeStruct((t, KV_W), F32),
                   jax.ShapeDtypeStruct((t, d), F32)],
        scratch_shapes=[pltpu.VMEM((2, tm * Y_ROWS, LANE), U32), pltpu.SemaphoreType.DMA((2,))],
        compiler_params=_cparams(("arbitrary",), 56),
        name="in_proj_fused",
    )(pos3, pos3, xn, prev_mod, mod_l, g, w, yb)


def _in_proj(xp, xs, mod_l, g, w, tm, mod_index):
    tp, d = xp.shape
    t = tp + xs.shape[0]
    npt = tp // tm
    return pl.pallas_call(
        functools.partial(_in_proj_kernel, npt),
        grid=(t // tm,),
        in_specs=[pl.BlockSpec((tm, d), lambda i: (jnp.minimum(i, npt - 1), 0)),
                  pl.BlockSpec((tm, d), lambda i: (jnp.maximum(i - npt, 0), 0)),
                  pl.BlockSpec((None, 6, d), lambda i: (mod_index(i, tm), 0, 0)),
                  pl.BlockSpec((1, d), lambda i: (0, 0)),
                  pl.BlockSpec(w.shape, lambda i: (0, 0), pipeline_mode=pl.Buffered(1))],
        out_specs=[pl.BlockSpec((tm, PROJ_W), lambda i: (i, 0)),
                   pl.BlockSpec((tm, KV_W), lambda i: (i, 0))],
        out_shape=[jax.ShapeDtypeStruct((t, PROJ_W), BF16),
                   jax.ShapeDtypeStruct((t, KV_W), F32)],
        compiler_params=_cparams(("arbitrary",), 56),
        name="in_proj",
    )(xp, xs, mod_l, g, w)


def _rope_tile(t, cos, sin):
    lane = lax.broadcasted_iota(jnp.int32, t.shape, 1)
    first = (lane & 16) == 0
    swapped = jnp.where(first, pltpu.roll(t, LANE - 16, 1), pltpu.roll(t, 16, 1))
    return t * cos + swapped * sin


def _row_ss(x):
    sq = (x * x).astype(BF16)
    return jnp.dot(sq, jnp.ones((x.shape[1], LANE), BF16), preferred_element_type=F32)


def _keys_values(ckv_n, krope, w_ukv_ref, gk_ref, cos, sin, k_ref, v_ref):
    kvf = jnp.dot(ckv_n.astype(BF16), w_ukv_ref[...], preferred_element_type=F32)
    kr_ss = _row_ss(krope)
    gk = gk_ref[...]
    tail = krope * gk[:, MLA_NOPE:]
    if cos is not None:
        tail = _rope_tile(tail, cos, sin)
    for h in range(MLA_HEADS):
        kn = kvf[:, h * MLA_NOPE:(h + 1) * MLA_NOPE]
        r = lax.rsqrt((_row_ss(kn) + kr_ss) * (1.0 / MLA_QK) + NORM_EPS)
        k_ref[:, h * HEAD_PAD:h * HEAD_PAD + MLA_NOPE] = (kn * r * gk[:, :MLA_NOPE]).astype(BF16)
        k_ref[:, h * HEAD_PAD + MLA_NOPE:(h + 1) * HEAD_PAD] = (tail * r).astype(BF16)
    v_ref[...] = kvf[:, MLA_HEADS * MLA_NOPE:].astype(BF16)


def _qkv_kernel(npt, q_lora, cq_ref, kv_ref, cos_ref, sin_ref, w_uq_ref, w_ukv_ref, gql_ref, gkv_ref,
                gq_ref, gk_ref, q_ref, k_ref, v_ref, st_ref):
    i = pl.program_id(0)
    cos = cos_ref[...]
    sin = sin_ref[...]
    cq = cq_ref[...].astype(F32)
    r_cq = lax.rsqrt(_row_ss(cq) * (1.0 / q_lora) + NORM_EPS)
    cqn = cq * jnp.concatenate([r_cq] * (cq.shape[1] // LANE), axis=1) * gql_ref[...]
    qf = jnp.dot(cqn.astype(BF16), w_uq_ref[...], preferred_element_type=F32)
    gq = gq_ref[...]
    scale = MLA_QK ** -0.5 * LOG2_E
    for h in range(MLA_HEADS):
        qh = qf[:, h * HEAD_PAD:(h + 1) * HEAD_PAD]
        r = lax.rsqrt(_row_ss(qh) * (1.0 / MLA_QK) + NORM_EPS) * scale
        head = qh[:, :MLA_NOPE] * r * gq[:, :MLA_NOPE]
        tail = qh[:, MLA_NOPE:] * r * gq[:, MLA_NOPE:]
        q_ref[:, h * HEAD_PAD:h * HEAD_PAD + MLA_NOPE] = head.astype(BF16)
        q_ref[:, h * HEAD_PAD + MLA_NOPE:(h + 1) * HEAD_PAD] = _rope_tile(tail, cos, sin).astype(BF16)

    kv = kv_ref[...]
    ckv = kv[:, :MLA_NOPE]
    krope = kv[:, MLA_NOPE:]
    ckv_n = ckv * _inv_rms(ckv, ckv.shape[-1]) * gkv_ref[...]
    _keys_values(ckv_n, krope, w_ukv_ref, gk_ref, cos, sin, k_ref, v_ref)

    @pl.when(i < npt)
    def _():
        st_ref[:, :MLA_NOPE] = ckv_n
        st_ref[:, MLA_NOPE:] = krope


def _qkv(proj, kv, cos_t, sin_t, w_uq, w_ukv, gql, gkv, gq, gk, tp, dec_seq, q_lora, tm):
    t = proj.shape[0]
    npt = tp // tm
    nseq = dec_seq // tm
    cq_blk = CQ_COL // 512

    def tab(i):
        return (jnp.where(i < npt, nseq, (i - npt) % nseq), 0)

    const = lambda i: (0, 0)
    return pl.pallas_call(
        functools.partial(_qkv_kernel, npt, q_lora),
        grid=(t // tm,),
        in_specs=[pl.BlockSpec((tm, 512), lambda i: (i, cq_blk)),
                  pl.BlockSpec((tm, KV_W), lambda i: (i, 0)),
                  pl.BlockSpec((tm, LANE), tab),
                  pl.BlockSpec((tm, LANE), tab),
                  pl.BlockSpec(w_uq.shape, const),
                  pl.BlockSpec(w_ukv.shape, const),
                  pl.BlockSpec(gql.shape, const),
                  pl.BlockSpec(gkv.shape, const),
                  pl.BlockSpec(gq.shape, const),
                  pl.BlockSpec(gk.shape, const)],
        out_specs=[pl.BlockSpec((tm, MLA_HEADS * HEAD_PAD), lambda i: (i, 0)),
                   pl.BlockSpec((tm, MLA_HEADS * HEAD_PAD), lambda i: (i, 0)),
                   pl.BlockSpec((tm, MLA_HEADS * MLA_V), lambda i: (i, 0)),
                   pl.BlockSpec((tm, KV_W), lambda i: (jnp.minimum(i, npt - 1), 0))],
        out_shape=[jax.ShapeDtypeStruct((t, MLA_HEADS * HEAD_PAD), BF16),
                   jax.ShapeDtypeStruct((t, MLA_HEADS * HEAD_PAD), BF16),
                   jax.ShapeDtypeStruct((t, MLA_HEADS * MLA_V), BF16),
                   jax.ShapeDtypeStruct((tp, KV_W), F32)],
        compiler_params=_cparams(("arbitrary",), 40),
        name="qkv",
    )(proj, kv, cos_t, sin_t, w_uq, w_ukv, gql, gkv, gq, gk)


def _ctx_kv_kernel(kv_ref, w_ukv_ref, gk_ref, k_ref, v_ref):
    kv = kv_ref[...]
    _keys_values(kv[:, :MLA_NOPE], kv[:, MLA_NOPE:], w_ukv_ref, gk_ref, None, None, k_ref, v_ref)


def _ctx_kv(kvc, w_ukv, gk, tm):
    r = kvc.shape[0]
    const = lambda i: (0, 0)
    return pl.pallas_call(
        _ctx_kv_kernel,
        grid=(r // tm,),
        in_specs=[pl.BlockSpec((tm, KV_W), lambda i: (i, 0)),
                  pl.BlockSpec(w_ukv.shape, const),
                  pl.BlockSpec(gk.shape, const)],
        out_specs=[pl.BlockSpec((tm, MLA_HEADS * HEAD_PAD), lambda i: (i, 0)),
                   pl.BlockSpec((tm, MLA_HEADS * MLA_V), lambda i: (i, 0))],
        out_shape=[jax.ShapeDtypeStruct((r, MLA_HEADS * HEAD_PAD), BF16),
                   jax.ShapeDtypeStruct((r, MLA_HEADS * MLA_V), BF16)],
        compiler_params=_cparams(("arbitrary",), 32),
        name="ctx_kv",
    )(kvc, w_ukv, gk)


_NT = (((1,), (1,)), ((), ()))


def _attn_kernel(has_ctx, *refs):
    if has_ctx:
        q_ref, kc_ref, vc_ref, k_ref, v_ref, w1_ref, w3_ref, w2_ref, o_ref, w1o_ref, w3o_ref, w2o_ref = refs
        w1o_ref[...] = w1_ref[...].astype(BF16)
        w3o_ref[...] = w3_ref[...].astype(BF16)
        w2o_ref[...] = w2_ref[...].astype(BF16)
    else:
        q_ref, k_ref, v_ref, o_ref = refs
    n_own = k_ref.shape[0]
    step = min(n_own, ATTN_KCHUNK)
    chunks = [(k_ref, v_ref, c * step, step) for c in range(n_own // step)]
    if has_ctx:
        chunks = [(kc_ref, vc_ref, 0, kc_ref.shape[0])] + chunks
    for h in range(q_ref.shape[1] // HEAD_PAD):
        qk = slice(h * HEAD_PAD, (h + 1) * HEAD_PAD)
        hv = slice(h * MLA_V, (h + 1) * MLA_V)
        tq = q_ref.shape[0]
        sub = min(tq, ATTN_QSUB)
        state = [None] * (tq // sub)
        for kr, vr, off, n in chunks:
            for t in range(tq // sub):
                rows = slice(t * sub, (t + 1) * sub)
                s = lax.dot_general(q_ref[rows, qk], kr[off:off + n, qk], _NT, preferred_element_type=F32)
                mc = jnp.max(s, axis=-1, keepdims=True)
                m_new = mc if state[t] is None else jnp.maximum(state[t][0], mc)
                p = jnp.exp2(s - m_new).astype(BF16)
                v1 = jnp.concatenate([vr[off:off + n, hv], jnp.ones((n, MLA_V), BF16)], axis=1)
                pv = jnp.dot(p, v1, preferred_element_type=F32)
                if state[t] is None:
                    state[t] = (m_new, pv)
                else:
                    m, acc = state[t]
                    state[t] = (m_new, jnp.exp2(m - m_new) * acc + pv)
        for t in range(tq // sub):
            _, acc = state[t]
            o_ref[t * sub:(t + 1) * sub, hv] = (acc[:, :MLA_V] / acc[:, MLA_V:]).astype(BF16)


def _attention_latent(q, k, v, kc, vc, w1, w3, w2, layer, tp, nb, dec_seq, past, tq):
    t = q.shape[0]
    nq = dec_seq // tq
    row0 = tp // tq
    kblk0 = tp // dec_seq
    steps = nb * MLA_HEADS * nq
    depth, n_exp, d, de = w1.shape
    flat = [w1.reshape(depth, n_exp * d, de), w3.reshape(depth, n_exp * d, de), w2.reshape(depth, n_exp * de, d)]
    assert all(w.shape[1] % (8 * steps) == 0 for w in flat)
    step = lambda b, h, i: (b * MLA_HEADS + h) * nq + i
    w_in = [pl.BlockSpec((None, w.shape[1] // steps, w.shape[2]), lambda b, h, i: (layer, step(b, h, i), 0))
            for w in flat]
    w_out = [pl.BlockSpec((w.shape[1] // steps, w.shape[2]), lambda b, h, i: (step(b, h, i), 0)) for w in flat]
    o, w1b, w3b, w2b = pl.pallas_call(
        functools.partial(_attn_kernel, True),
        grid=(nb, MLA_HEADS, nq),
        in_specs=[pl.BlockSpec((tq, HEAD_PAD), lambda b, h, i: (row0 + b * nq + i, h)),
                  pl.BlockSpec((past, HEAD_PAD), lambda b, h, i: (b, h)),
                  pl.BlockSpec((past, MLA_V), lambda b, h, i: (b, h)),
                  pl.BlockSpec((dec_seq, HEAD_PAD), lambda b, h, i: (kblk0 + b, h)),
                  pl.BlockSpec((dec_seq, MLA_V), lambda b, h, i: (kblk0 + b, h))] + w_in,
        out_specs=[pl.BlockSpec((tq, MLA_V), lambda b, h, i: (b * nq + i, h))] + w_out,
        out_shape=[jax.ShapeDtypeStruct((t - tp, MLA_HEADS * MLA_V), BF16)]
                  + [jax.ShapeDtypeStruct(w.shape[1:], BF16) for w in flat],
        compiler_params=_cparams(("arbitrary", "arbitrary", "arbitrary"), 56),
        name="attn_latent",
    )(q, kc, vc, k, v, *flat)
    return o, w1b.reshape(n_exp, d, de), w3b.reshape(n_exp, d, de), w2b.reshape(n_exp, de, d)


def _attention_context(q, k, v, nseq, seq):
    return pl.pallas_call(
        functools.partial(_attn_kernel, False),
        grid=(nseq,),
        in_specs=[pl.BlockSpec((seq, MLA_HEADS * HEAD_PAD), lambda b: (b, 0)),
                  pl.BlockSpec((seq, MLA_HEADS * HEAD_PAD), lambda b: (b, 0)),
                  pl.BlockSpec((seq, MLA_HEADS * MLA_V), lambda b: (b, 0))],
        out_specs=pl.BlockSpec((seq, MLA_HEADS * MLA_V), lambda b: (b, 0)),
        out_shape=jax.ShapeDtypeStruct((nseq * seq, MLA_HEADS * MLA_V), BF16),
        compiler_params=_cparams(("arbitrary",), 32),
        name="attn_context",
    )(q, k, v)


def _mixers_kernel(npt, p_tiles, s_tiles,
                   cv_ref, cvp_ref, cvn_ref, gm_ref, sc_ref, scp_ref, scn_ref, bg_ref,
                   wdw_ref, bdw_ref, gcl_ref, bcl_ref, wpw_ref, bpw_ref,
                   gsl_ref, bsl_ref, wsp_ref, bsp_ref, wsc_ref,
                   o_ref, zs, ys, zsh, cb):
    i = pl.program_id(0)
    tm = cv_ref.shape[0]
    cw = cv_ref.shape[1] // 2
    pos = jnp.where(i < npt, i % p_tiles, (i - npt) % s_tiles)
    n_tiles = jnp.where(i < npt, p_tiles, s_tiles)
    keep_prev = (pos > 0).astype(F32)
    keep_next = (pos < n_tiles - 1).astype(F32)

    def glu(ref):
        a = ref[:, :cw].astype(F32)
        g = ref[:, cw:].astype(F32)
        return a * jax.nn.sigmoid(g)

    def prod(ref):
        return ref[:, :cw].astype(F32) * ref[:, cw:].astype(F32)

    n_lane = cw // LANE
    z_prev, z_main, z_next = glu(cvp_ref) * keep_prev, glu(cv_ref), glu(cvn_ref) * keep_next
    for c in range(n_lane):
        cs = slice(c * LANE, (c + 1) * LANE)
        zs[c, 0:HALO, :] = z_prev[:, cs]
        zs[c, HALO:HALO + tm, :] = z_main[:, cs]
        zs[c, HALO + tm:, :] = z_next[:, cs]
        for r in range(1, 8):
            zsh[r - 1, c] = zs[c, pl.ds(r, tm + 24), :]
    pad = CONV_K // 2
    half = tm // 2

    def conv_chunk(idx, carry):
        c = idx // 2
        base = pl.multiple_of((idx % 2) * half, half)
        acc = jnp.zeros((half, LANE), F32)
        for r in range(8):
            for a in range(4):
                tap = 8 * a + r - (HALO - pad)
                if 0 <= tap < CONV_K:
                    rows = pl.ds(base + 8 * a, half)
                    src = zs[c, rows, :] if r == 0 else zsh[r - 1, c, rows, :]
                    acc = acc + src * wdw_ref[c, tap:tap + 1, :]
        cb[c, pl.ds(base, half), :] = acc
        return carry

    lax.fori_loop(0, 2 * n_lane, conv_chunk, 0)
    z = jnp.concatenate([cb[c] for c in range(n_lane)], axis=1) + bdw_ref[...]
    z = _layernorm(z, gcl_ref[...], bcl_ref[...])
    z = z * jax.nn.sigmoid(z)
    o_b = jnp.dot(z.astype(BF16), wpw_ref[...], preferred_element_type=F32) + bpw_ref[...]
    o_ref[:, 0:cw] = o_b.astype(BF16)

    u = gm_ref[:, :cw].astype(F32)
    vg = _layernorm(gm_ref[:, cw:].astype(F32), gsl_ref[...], bsl_ref[...]).astype(BF16)
    hc = cw // GMLP_HEADS
    for n in range(tm // CHUNK):
        rows = slice(n * CHUNK, (n + 1) * CHUNK)
        for h in range(GMLP_HEADS):
            cols = slice(h * hc, (h + 1) * hc)
            mixed = jnp.dot(wsp_ref[h], vg[rows, cols], preferred_element_type=F32) + bsp_ref[:, cols]
            o_ref[rows, cw + h * hc:cw + (h + 1) * hc] = (u[rows, cols] * mixed).astype(BF16)

    ys[0:HALO, :] = prod(scp_ref) * keep_prev
    ys[HALO:HALO + tm, :] = prod(sc_ref)
    ys[HALO + tm:, :] = prod(scn_ref) * keep_next
    acc = jnp.zeros((tm, cw), F32)
    for tap in range(SC_K):
        acc = acc + ys[pl.ds(HALO - SC_K // 2 + tap, tm), :] * wsc_ref[tap:tap + 1, :]
    o_ref[:, 2 * cw:3 * cw] = (bg_ref[...].astype(F32) * acc).astype(BF16)


def _mixers(proj, weights, tp, seq, dec_seq, tm):
    t = proj.shape[0]
    cw = 512
    npt = tp // tm
    hb = tm // HALO
    last = t // HALO - 1
    const2 = lambda i: (0, 0)

    def main(col):
        return pl.BlockSpec((tm, 2 * cw), lambda i: (i, col))

    def prev(col):
        return pl.BlockSpec((HALO, 2 * cw), lambda i: (jnp.maximum(i * hb - 1, 0), col))

    def nxt(col):
        return pl.BlockSpec((HALO, 2 * cw), lambda i: (jnp.minimum((i + 1) * hb, last), col))

    w_specs = [pl.BlockSpec(w.shape, const2 if w.ndim == 2 else (lambda i: (0, 0, 0))) for w in weights]
    return pl.pallas_call(
        functools.partial(_mixers_kernel, npt, seq // tm, dec_seq // tm),
        grid=(t // tm,),
        in_specs=[main(0), prev(0), nxt(0), main(1), main(2), prev(2), nxt(2),
                  pl.BlockSpec((tm, cw), lambda i: (i, 6))] + w_specs,
        out_specs=pl.BlockSpec((tm, 3 * cw), lambda i: (i, 0)),
        out_shape=jax.ShapeDtypeStruct((t, 3 * cw), BF16),
        scratch_shapes=[pltpu.VMEM((cw // LANE, tm + 2 * HALO, LANE), F32), pltpu.VMEM((tm + 2 * HALO, cw), F32),
                        pltpu.VMEM((7, cw // LANE, tm + 24, LANE), F32), pltpu.VMEM((cw // LANE, tm, LANE), F32)],
        compiler_params=_cparams(("arbitrary",), 40),
        name="mixers",
    )(proj, proj, proj, proj, proj, proj, proj, proj, *weights)


def _out_proj_kernel(npt, n, oap_ref, oas_ref, ob_ref, xp_ref, xs_ref, mod_ref, g_ref, wo_ref, wr_ref,
                     xn_ref, hp_ref, r_ref, mix_a, mix_b):
    i = pl.program_id(0)
    ka = oap_ref.shape[1]
    mm_tile = jnp.minimum(i, n - 1)
    ep_tile = jnp.maximum(i - 1, 0)

    @pl.when(i == 0)
    def _():
        mix_b[...] = jnp.zeros_like(mix_b)

    def step(mix_w, mix_r):
        oa = jnp.where(mm_tile < npt, oap_ref[...], oas_ref[...])
        lhs = jnp.concatenate([oa, ob_ref[...]], axis=1)
        tm, d = mix_w.shape
        parts = OUT_PARTS
        for k in range(parts):
            cols = slice(k * d // parts, (k + 1) * d // parts)
            rows = slice(k * tm // parts, (k + 1) * tm // parts)
            mix_w[:, cols] = jnp.dot(lhs, wo_ref[:, cols], preferred_element_type=F32)
            x = jnp.where(ep_tile < npt, xp_ref[rows, :], xs_ref[rows, :])
            xn = x + mod_ref[2:3, :] * mix_r[rows, :]
            xn_ref[rows, :] = xn
            h = xn * _inv_rms(xn, xn.shape[-1]) * g_ref[...]
            h = h * (1.0 + mod_ref[4:5, :]) + mod_ref[3:4, :]
            hp_ref[rows, :] = _pack_rows(h)
            h_hi, h_lo = _split_bf16(h)
            r_ref[:, rows] = (lax.dot_general(wr_ref[...], h_hi, _NT, preferred_element_type=F32)
                              + lax.dot_general(wr_ref[...], h_lo, _NT, preferred_element_type=F32))

    @pl.when(i % 2 == 0)
    def _():
        step(mix_a, mix_b)

    @pl.when(i % 2 == 1)
    def _():
        step(mix_b, mix_a)


def _out_proj(oap, oas, ob, xp, xs, mod_l, g, wo, wr, tm, mod_index):
    t = ob.shape[0]
    d = xp.shape[1]
    npt = oap.shape[0] // tm
    n = t // tm
    const = lambda i: (0, 0)
    mm = lambda i: jnp.minimum(i, n - 1)
    ep = lambda i: jnp.maximum(i - 1, 0)

    def pair(w, tile, stacked=False):
        second = (lambda i: (jnp.maximum(tile(i), npt), 0)) if stacked else (
            lambda i: (jnp.maximum(tile(i) - npt, 0), 0))
        return [pl.BlockSpec((tm, w), lambda i: (jnp.minimum(tile(i), npt - 1), 0)), pl.BlockSpec((tm, w), second)]

    x_specs = pair(d, ep, stacked=xs is None)
    if xs is None:
        xs = xp
    return pl.pallas_call(
        functools.partial(_out_proj_kernel, npt, n),
        grid=(n + 1,),
        in_specs=pair(oap.shape[1], mm) + [pl.BlockSpec((tm, ob.shape[1]), lambda i: (mm(i), 0))] + x_specs + [
                  pl.BlockSpec((None, 6, d), lambda i: (mod_index(ep(i), tm), 0, 0)),
                  pl.BlockSpec((1, d), const),
                  pl.BlockSpec(wo.shape, const, pipeline_mode=pl.Buffered(1)),
                  pl.BlockSpec(wr.shape, const, pipeline_mode=pl.Buffered(1))],
        out_specs=[pl.BlockSpec((tm, d), lambda i: (ep(i), 0)),
                   pl.BlockSpec((tm, d // 2), lambda i: (ep(i), 0)),
                   pl.BlockSpec((LANE, tm), lambda i: (0, ep(i)))],
        out_shape=[jax.ShapeDtypeStruct((t, d), F32),
                   jax.ShapeDtypeStruct((t, d // 2), U32),
                   jax.ShapeDtypeStruct((LANE, t), F32)],
        scratch_shapes=[pltpu.VMEM((tm, d), F32), pltpu.VMEM((tm, d), F32)],
        compiler_params=_cparams(("arbitrary",), 56),
        name="out_proj",
    )(oap, oas, ob, xp, xs, mod_l, g, wo, wr)


N_BUCKET_PAD = 64
PAIR_FIRST = (0, 0, 0, 1, 1, 3)
PAIR_SECOND = (1, 2, 3, 3, 2, 2)


def _route_kernel(rt_ref, b_ref, tri_ref, bucket_ref, rank_ref, gate_ref, cnt_ref, run):
    i = pl.program_id(0)
    tm = rt_ref.shape[1]
    ne = N_GROUPS * GROUP_SIZE

    @pl.when(i == 0)
    def _():
        run[...] = jnp.zeros_like(run)

    sc = jax.nn.sigmoid(rt_ref[0:ne, :] + rt_ref[ne:2 * ne, :])
    sel = sc + b_ref[...]
    s = [sel[r * N_GROUPS:(r + 1) * N_GROUPS, :] for r in range(GROUP_SIZE)]
    c = [sc[r * N_GROUPS:(r + 1) * N_GROUPS, :] for r in range(GROUP_SIZE)]
    hi01, lo01 = jnp.maximum(s[0], s[1]), jnp.minimum(s[0], s[1])
    hi23, lo23 = jnp.maximum(s[2], s[3]), jnp.minimum(s[2], s[3])
    g_score = jnp.maximum(hi01, hi23) + jnp.maximum(jnp.minimum(hi01, hi23), jnp.maximum(lo01, lo23))
    grp = lax.broadcasted_iota(jnp.int32, g_score.shape, 0)
    g_best = jnp.min(jnp.where(g_score == jnp.max(g_score, axis=0, keepdims=True), grp, N_GROUPS),
                     axis=0, keepdims=True)
    own = grp == g_best
    v = [jnp.sum(jnp.where(own, s[r], 0.0), axis=0, keepdims=True) for r in range(GROUP_SIZE)]
    w = [jnp.sum(jnp.where(own, c[r], 0.0), axis=0, keepdims=True) for r in range(GROUP_SIZE)]

    def first_max(vals):
        best, idx, gate = vals[0], jnp.zeros_like(g_best), w[0]
        for r in range(1, GROUP_SIZE):
            upd = vals[r] > best
            best = jnp.where(upd, vals[r], best)
            idx = jnp.where(upd, r, idx)
            gate = jnp.where(upd, w[r], gate)
        return idx, gate

    r1, w1 = first_max(v)
    r2, w2 = first_max([jnp.where(r1 == r, -jnp.inf, v[r]) for r in range(GROUP_SIZE)])
    wsum = w1 + w2
    swap = r1 > r2
    r_lo = jnp.where(swap, r2, r1)
    r_hi = jnp.where(swap, r1, r2)
    g_lo = jnp.where(swap, w2, w1) / wsum
    g_hi = jnp.where(swap, w1, w2) / wsum
    lex = r_lo * 3 - ((r_lo * (r_lo - 1)) >> 1) + (r_hi - r_lo - 1)
    pair = jnp.where(lex == 3, 4, jnp.where(lex == 4, 3, lex))
    hi_first = pair == N_PAIRS - 1
    gate_ref[0:1, :] = jnp.where(hi_first, g_hi, g_lo)
    gate_ref[1:2, :] = jnp.where(hi_first, g_lo, g_hi)
    bucket = g_best * N_PAIRS + pair
    bucket_ref[...] = bucket

    onehot = (lax.broadcasted_iota(jnp.int32, (N_BUCKET_PAD, tm), 0) == bucket).astype(F32)
    before = jnp.dot(onehot.astype(BF16), tri_ref[...], preferred_element_type=F32) + run[...]
    rank_ref[...] = jnp.sum(onehot * before, axis=0, keepdims=True).astype(jnp.int32)
    run[...] += jnp.sum(onehot, axis=1, keepdims=True)

    @pl.when(i == pl.num_programs(0) - 1)
    def _():
        cnt_ref[...] = run[...]


def _route(rt, router_b, n_blk, tm):
    t = rt.shape[1]
    ne = router_b.shape[0]
    b_col = router_b.astype(F32).reshape(N_GROUPS, GROUP_SIZE).T.reshape(ne, 1)
    tri = (jnp.arange(tm)[:, None] < jnp.arange(tm)[None, :]).astype(BF16)
    bucket, rank, gate, cnt = pl.pallas_call(
        _route_kernel,
        grid=(t // tm,),
        in_specs=[pl.BlockSpec((LANE, tm), lambda i: (0, i)),
                  pl.BlockSpec((ne, 1), lambda i: (0, 0)),
                  pl.BlockSpec((tm, tm), lambda i: (0, 0))],
        out_specs=[pl.BlockSpec((1, tm), lambda i: (0, i)),
                   pl.BlockSpec((1, tm), lambda i: (0, i)),
                   pl.BlockSpec((2, tm), lambda i: (0, i)),
                   pl.BlockSpec((N_BUCKET_PAD, 1), lambda i: (0, 0))],
        out_shape=[jax.ShapeDtypeStruct((1, t), jnp.int32),
                   jax.ShapeDtypeStruct((1, t), jnp.int32),
                   jax.ShapeDtypeStruct((2, t), F32),
                   jax.ShapeDtypeStruct((N_BUCKET_PAD, 1), F32)],
        scratch_shapes=[pltpu.VMEM((N_BUCKET_PAD, 1), F32)],
        compiler_params=_cparams(("arbitrary",), 32),
        name="route",
    )(rt, b_col, tri)

    n_bucket = N_GROUPS * N_PAIRS
    counts = cnt[:n_bucket, 0].astype(jnp.int32)
    padded = (counts + MOE_BM - 1) // MOE_BM * MOE_BM
    pad_end = jnp.cumsum(padded)
    pad_start = pad_end - padded
    bucket = bucket[0]
    sel = bucket[:, None] == jnp.arange(n_bucket, dtype=jnp.int32)[None, :]
    pos = jnp.sum(jnp.where(sel, pad_start[None, :], 0), axis=1).astype(jnp.int32) + rank[0]
    n_used = (pad_end[-1] // MOE_BM).astype(jnp.int32)
    blk = jnp.minimum(jnp.arange(n_blk, dtype=jnp.int32), n_used - 1)
    blk_bucket = jnp.sum((pad_end[None, :] <= (blk * MOE_BM)[:, None]).astype(jnp.int32), axis=1)
    blk_bucket = jnp.minimum(blk_bucket, n_bucket - 1)
    pair_lo = jnp.array(PAIR_FIRST, jnp.int32)
    pair_hi = jnp.array(PAIR_SECOND, jnp.int32)
    blk_pair = blk_bucket % N_PAIRS
    first = (blk_bucket // N_PAIRS) * GROUP_SIZE
    is_pair = blk_pair[:, None] == jnp.arange(N_PAIRS, dtype=jnp.int32)[None, :]
    blk_a = first + jnp.sum(jnp.where(is_pair, pair_lo[None, :], 0), axis=1)
    blk_b = first + jnp.sum(jnp.where(is_pair, pair_hi[None, :], 0), axis=1)
    is_bucket = blk_bucket[:, None] == jnp.arange(n_bucket, dtype=jnp.int32)[None, :]
    per_bucket = lambda v: jnp.sum(jnp.where(is_bucket, v[None, :], 0), axis=1)
    first_blk = per_bucket(pad_start // MOE_BM)
    n_in_bucket = jnp.maximum(per_bucket(padded // MOE_BM), 1)
    src = first_blk + (blk - first_blk + n_in_bucket - 1) % n_in_bucket
    steps = jnp.arange(n_blk, dtype=jnp.int32)
    used = steps < n_used
    blk_out = jnp.where(used, src, steps).astype(jnp.int32)
    blk_rows = jnp.where(used, jnp.clip(per_bucket(pad_start + counts) - src * MOE_BM, 0, MOE_BM), 0)
    gates = jnp.pad(gate.T, ((0, 0), (0, GATE_W - 2)))
    plan = (src.astype(jnp.int32), blk_a.astype(jnp.int32), blk_b.astype(jnp.int32), blk_rows.astype(jnp.int32),
            blk_out)
    rows_in_blk = jnp.clip(per_bucket(pad_start + counts) - blk * MOE_BM, 0, MOE_BM)
    fill = jnp.logical_or(jnp.logical_not(used), rows_in_blk < MOE_BM).astype(jnp.int32)
    return pos, gates, plan, fill


def _dispatch_kernel(fill_ref, pos_ref, hp_ref, gt_ref, xs_ref, buf, zeros, sem, zsem):
    i = pl.program_id(0)
    slot = i % 2
    tm = hp_ref.shape[0]
    n_data = hp_ref.shape[1] // LANE
    blk_rows = zeros.shape[0]

    def wait_all(s):
        pltpu.make_async_copy(buf.at[s], xs_ref.at[pl.ds(0, tm * X_ROWS)], sem.at[s]).wait()

    @pl.when(i == 0)
    def _():
        zeros[...] = jnp.zeros_like(zeros)

        def zero_block(b):
            return pltpu.make_async_copy(zeros, xs_ref.at[pl.ds(pl.multiple_of(b * blk_rows, blk_rows), blk_rows)],
                                         zsem)

        def start(b, carry):
            @pl.when(fill_ref[b] != 0)
            def _():
                zero_block(b).start()
            return carry

        def wait(b, carry):
            @pl.when(fill_ref[b] != 0)
            def _():
                zero_block(b).wait()
            return carry

        lax.fori_loop(0, fill_ref.shape[0], start, 0)
        lax.fori_loop(0, fill_ref.shape[0], wait, 0)

    for s in range(2):
        @pl.when(slot == s)
        def _():
            for j in range(n_data):
                buf[s, pl.ds(j, tm, stride=X_ROWS), :] = hp_ref[:, j * LANE:(j + 1) * LANE]
            buf[s, pl.ds(n_data, tm, stride=X_ROWS), :] = pltpu.bitcast(gt_ref[...], U32)
            for r in range(tm):
                pltpu.make_async_copy(buf.at[s, pl.ds(r * X_ROWS, X_ROWS)],
                                      xs_ref.at[pl.ds(pos_ref[0, 0, r] * X_ROWS, X_ROWS)],
                                      sem.at[s]).start(priority=r % N_DMA_PRIORITIES)

    @pl.when(i > 0)
    def _():
        wait_all(1 - slot)

    @pl.when(i == pl.num_programs(0) - 1)
    def _():
        wait_all(slot)


def _dispatch(hp, gates, pos, fill, tm):
    t, w = hp.shape
    assert w == (X_ROWS - 1) * LANE and GATE_W == LANE
    n_blk = fill.shape[0]
    grid_spec = pltpu.PrefetchScalarGridSpec(
        num_scalar_prefetch=1,
        grid=(t // tm,),
        in_specs=[pl.BlockSpec((1, 1, tm), lambda i, fill: (i, 0, 0), memory_space=pltpu.SMEM),
                  pl.BlockSpec((tm, w), lambda i, fill: (i, 0)),
                  pl.BlockSpec((tm, GATE_W), lambda i, fill: (i, 0))],
        out_specs=pl.BlockSpec(memory_space=pl.ANY),
        scratch_shapes=[pltpu.VMEM((2, tm * X_ROWS, LANE), U32), pltpu.VMEM((MOE_BM * X_ROWS, LANE), U32),
                        pltpu.SemaphoreType.DMA((2,)), pltpu.SemaphoreType.DMA(())],
    )
    return pl.pallas_call(
        _dispatch_kernel,
        grid_spec=grid_spec,
        out_shape=jax.ShapeDtypeStruct((n_blk * MOE_BM * X_ROWS, LANE), U32),
        compiler_params=_cparams(("arbitrary",), 32),
        name="dispatch",
    )(fill, pos.reshape(t // tm, 1, tm), hp, gates)


def _moe_kernel(src_ref, ea_ref, eb_ref, nv_ref, dst_ref, x_ref, w1a_ref, w3a_ref, w2a_ref, w1b_ref, w3b_ref,
                w2b_ref, y_ref):
    del src_ref, ea_ref, eb_ref, dst_ref
    nv = nv_ref[pl.program_id(0)]
    bm = x_ref.shape[0] // X_ROWS

    def experts(m):
        halves = [_unpack_rows(x_ref[pl.ds(j, m, stride=X_ROWS), :]) for j in range(X_ROWS - 1)]
        x = jnp.concatenate([a.astype(BF16) for a, _ in halves] + [b.astype(BF16) for _, b in halves], axis=1)
        gates = pltpu.bitcast(x_ref[pl.ds(X_ROWS - 1, m, stride=X_ROWS), :], F32)

        def hidden(w1_ref, w3_ref, g):
            a = jnp.dot(x, w1_ref[...], preferred_element_type=F32)
            b = jnp.dot(x, w3_ref[...], preferred_element_type=F32)
            return (a * jax.nn.sigmoid(a) * b * g).astype(BF16)

        y = jnp.dot(hidden(w1a_ref, w3a_ref, gates[:, 0:1]), w2a_ref[...], preferred_element_type=F32)
        y += jnp.dot(hidden(w1b_ref, w3b_ref, gates[:, 1:2]), w2b_ref[...], preferred_element_type=F32)
        yp = _pack_rows(y)
        for j in range(Y_ROWS):
            y_ref[pl.ds(j, m, stride=Y_ROWS), :] = yp[:, j * LANE:(j + 1) * LANE]
        if m < bm:
            y_ref[m * Y_ROWS:, :] = jnp.zeros(((bm - m) * Y_ROWS, LANE), y_ref.dtype)

    @pl.when(nv > bm // 2)
    def _():
        experts(bm)

    @pl.when(jnp.logical_and(nv > 0, nv <= bm // 2))
    def _():
        experts(bm // 2)

    @pl.when(nv == 0)
    def _():
        y_ref[...] = jnp.zeros_like(y_ref)


def _moe(xs, w1, w3, w2, plan):
    n_blk = xs.shape[0] // (MOE_BM * X_ROWS)
    _, d, de = w1.shape
    assert d == 2 * Y_ROWS * LANE

    def expert(second, rows, cols):
        if second:
            return pl.BlockSpec((None, rows, cols), lambda i, src, ea, eb, nv, dst: (eb[i], 0, 0))
        return pl.BlockSpec((None, rows, cols), lambda i, src, ea, eb, nv, dst: (ea[i], 0, 0))

    grid_spec = pltpu.PrefetchScalarGridSpec(
        num_scalar_prefetch=5,
        grid=(n_blk,),
        in_specs=[pl.BlockSpec((MOE_BM * X_ROWS, LANE), lambda i, src, ea, eb, nv, dst: (src[i], 0)),
                  expert(False, d, de), expert(False, d, de), expert(False, de, d),
                  expert(True, d, de), expert(True, d, de), expert(True, de, d)],
        out_specs=pl.BlockSpec((MOE_BM * Y_ROWS, LANE), lambda i, src, ea, eb, nv, dst: (dst[i], 0)),
    )
    return pl.pallas_call(
        _moe_kernel,
        grid_spec=grid_spec,
        out_shape=jax.ShapeDtypeStruct((n_blk * MOE_BM * Y_ROWS, LANE), U32),
        compiler_params=_cparams(("arbitrary",), 48),
        name="moe",
    )(*plan, xs, w1, w3, w2, w1, w3, w2)


def _combine_kernel(npt, pos_ref, nxt_ref, xn_ref, mod_ref, yb_ref, op_ref, os_ref, buf, sem):
    i = pl.program_id(0)
    out = xn_ref[...] + mod_ref[5:6, :] * _gathered_rows(pos_ref, nxt_ref, yb_ref, buf, sem)

    @pl.when(i < npt)
    def _():
        op_ref[...] = out

    @pl.when(i >= npt)
    def _():
        os_ref[...] = out


def _combine(xn, yb, pos, mod_l, tp, tm, mod_index):
    t, d = xn.shape
    npt = tp // tm
    n = t // tm
    pos3 = pos.reshape(n, 1, tm)
    return pl.pallas_call(
        functools.partial(_combine_kernel, npt),
        grid=(n,),
        in_specs=[pl.BlockSpec((1, 1, tm), lambda i: (i, 0, 0), memory_space=pltpu.SMEM),
                  pl.BlockSpec((1, 1, tm), lambda i: (jnp.minimum(i + 1, n - 1), 0, 0), memory_space=pltpu.SMEM),
                  pl.BlockSpec((tm, d), lambda i: (i, 0)),
                  pl.BlockSpec((None, 6, d), lambda i: (mod_index(i, tm), 0, 0)),
                  pl.BlockSpec(memory_space=pl.ANY)],
        out_specs=[pl.BlockSpec((tm, d), lambda i: (jnp.minimum(i, npt - 1), 0)),
                   pl.BlockSpec((tm, d), lambda i: (jnp.maximum(i - npt, 0), 0))],
        out_shape=[jax.ShapeDtypeStruct((tp, d), F32),
                   jax.ShapeDtypeStruct((t - tp, d), F32)],
        scratch_shapes=[pltpu.VMEM((2, tm * Y_ROWS, LANE), U32), pltpu.SemaphoreType.DMA((2,))],
        compiler_params=_cparams(("arbitrary",), 32),
        name="combine",
    )(pos3, pos3, xn, mod_l, yb)


def _pad_cols(x, n):
    return jnp.pad(x, ((0, 0), (0, n - x.shape[1])))


def _rope_tables(dec_seq, tm):
    n_freq = MLA_ROPE // 4
    pos = jnp.arange(dec_seq, dtype=jnp.int32)
    row = (pos // GRID_W).astype(F32)
    col = (pos % GRID_W).astype(F32)
    inv = ROPE_THETA ** (-jnp.arange(n_freq, dtype=F32) / n_freq)
    ar = row[:, None] * inv[None, :]
    ac = col[:, None] * inv[None, :]
    ones = jnp.ones((dec_seq, LANE - MLA_ROPE), F32)
    cos = jnp.concatenate([jnp.cos(ar), jnp.cos(ar), jnp.cos(ac), jnp.cos(ac), ones], axis=1)
    sin = jnp.concatenate([-jnp.sin(ar), jnp.sin(ar), -jnp.sin(ac), jnp.sin(ac), 0.0 * ones], axis=1)
    ident_c = jnp.ones((tm, LANE), F32)
    ident_s = jnp.zeros((tm, LANE), F32)
    return jnp.concatenate([cos, ident_c], axis=0), jnp.concatenate([sin, ident_s], axis=0)


def _layer_params(l, q_lora, kv_lora, w_in, g_q_lora, w_uq, g_kv_lora, w_ukv, g_qk_q, g_qk_k,
                  w_dw31, b_dw31, g_conv_ln, b_conv_ln, w_pw, b_pw, g_sgu_ln, b_sgu_ln,
                  w_spatial, b_spatial, w_sc3, w_out):
    d = w_in.shape[1]
    o1 = q_lora
    o2 = o1 + kv_lora
    o3 = o2 + MLA_ROPE
    o4 = o3 + 1024
    o5 = o4 + 1024
    w = w_in[l]
    zeros = lambda n: jnp.zeros((d, n), w.dtype)
    w_in_p = jnp.concatenate(
        [w[:, o3:o4], w[:, o4:o5], w[:, o5 + 512:o5 + 1536], w[:, o5:o5 + 512],
         w[:, :o1], zeros(512 - q_lora),
         w[:, o1:o2], w[:, o2:o3], zeros(PROJ_W - KV_COL - kv_lora - MLA_ROPE)], axis=1).astype(BF16)

    uq = w_uq[l].reshape(q_lora, MLA_HEADS, MLA_QK)
    uq = jnp.pad(uq, ((0, 512 - q_lora), (0, 0), (0, HEAD_PAD - MLA_QK)))
    w_uq_p = uq.reshape(512, MLA_HEADS * HEAD_PAD).astype(BF16)
    ukv = w_ukv[l].reshape(kv_lora, MLA_HEADS, MLA_NOPE + MLA_V)
    w_ukv_p = jnp.concatenate([ukv[:, :, :MLA_NOPE].reshape(kv_lora, -1),
                               ukv[:, :, MLA_NOPE:].reshape(kv_lora, -1)], axis=1).astype(BF16)
    row = lambda v: v.reshape(1, -1).astype(F32)
    qkv_w = (w_uq_p, w_ukv_p, _pad_cols(row(g_q_lora[l]), 512), row(g_kv_lora[l]),
             _pad_cols(row(g_qk_q[l]), HEAD_PAD), _pad_cols(row(g_qk_k[l]), HEAD_PAD))

    hc = w_pw.shape[1] // GMLP_HEADS
    bsp = jnp.repeat(b_spatial[l].T.astype(F32), hc, axis=1)
    w_dw = jnp.pad(w_dw31[l].astype(F32), ((0, 32 - CONV_K), (0, 0)))
    w_dw = w_dw.reshape(32, -1, LANE).transpose(1, 0, 2)
    mix_w = (w_dw, row(b_dw31[l]),
             row(g_conv_ln[l]), row(b_conv_ln[l]), w_pw[l].astype(BF16), row(b_pw[l]),
             row(g_sgu_ln[l]), row(b_sgu_ln[l]), w_spatial[l].astype(BF16), bsp,
             jnp.pad(w_sc3[l].astype(F32), ((0, 8 - SC_K), (0, 0))))
    return w_in_p, qkv_w, mix_w, w_out[l].astype(BF16)


def _tile(limit, *sizes):
    tm = limit
    while any(s % tm for s in sizes):
        tm //= 2
    return tm


def kernel(x_prompt, x_sample, cache_mla_ckv, cache_mla_krope, c, c_ctx, w_mod, b_mod, g_norm1, g_norm2, w_in, g_q_lora, w_uq, g_kv_lora, w_ukv, g_qk_q, g_qk_k, w_dw31, b_dw31, g_conv_ln, b_conv_ln, w_pw, b_pw, g_sgu_ln, b_sgu_ln, w_spatial, b_spatial, w_sc3, w_out, router_w, router_b, w1, w3, w2):
    nseq, seq, d = x_prompt.shape
    nb, dec_seq, _ = x_sample.shape
    depth = w_mod.shape[0]
    past = cache_mla_ckv.shape[2]
    q_lora = g_q_lora.shape[1]
    kv_lora = g_kv_lora.shape[1]
    n_exp = router_w.shape[1]
    tp = nseq * seq
    ts = nb * dec_seq
    t = tp + ts
    assert tp % dec_seq == 0 and seq % CHUNK == 0 and dec_seq % GRID_W == 0
    assert nb + 1 <= 8 and kv_lora == MLA_NOPE and n_exp == N_GROUPS * GROUP_SIZE

    tm_in = _tile(512, tp, dec_seq)
    tm_qkv = _tile(512, tp, dec_seq)
    tm_row = _tile(256, seq, dec_seq)
    tq = _tile(1024, dec_seq)

    def mod_index(i, tm):
        npt = tp // tm
        return jnp.where(i < npt, 0, 1 + (i - npt) // (dec_seq // tm))

    cvec = jnp.concatenate([c_ctx[None, :], c, jnp.zeros((8 - 1 - nb, d), F32)], axis=0)
    mod = _modulation(cvec, w_mod, b_mod).reshape(depth, 8, 6, d)
    cos_t, sin_t = _rope_tables(dec_seq, tm_qkv)

    rw = router_w.T.reshape(N_GROUPS, GROUP_SIZE, d).transpose(1, 0, 2).reshape(n_exp, d)
    rw_hi = rw.astype(BF16)
    rw_lo = (rw - rw_hi.astype(F32)).astype(BF16)
    wr = jnp.pad(jnp.concatenate([rw_hi, rw_lo], axis=0), ((0, LANE - 2 * n_exp), (0, 0)))
    n_blk = (t + N_GROUPS * N_PAIRS * (MOE_BM - 1) + MOE_BM - 1) // MOE_BM

    xp = x_prompt.reshape(tp, d)
    xs = x_sample.reshape(ts, d)
    ckv_states = []
    krope_states = []
    for l in range(depth):
        w_in_p, qkv_w, mix_w, w_out_b = _layer_params(
            l, q_lora, kv_lora, w_in, g_q_lora, w_uq, g_kv_lora, w_ukv, g_qk_q, g_qk_k,
            w_dw31, b_dw31, g_conv_ln, b_conv_ln, w_pw, b_pw, g_sgu_ln, b_sgu_ln,
            w_spatial, b_spatial, w_sc3, w_out)
        mod_l = mod[l]

        if l == 0:
            proj, kv = _in_proj(xp, xs, mod_l, g_norm1[l].reshape(1, d), w_in_p, tm_in, mod_index)
        else:
            proj, kv, xp = _in_proj_fused(xn, yb, pos, mod[l - 1], mod_l, g_norm1[l].reshape(1, d), w_in_p,
                                          tm_in, mod_index)
            xs = None
        q, k, v, state = _qkv(proj, kv, cos_t, sin_t, *qkv_w, tp, dec_seq, q_lora, tm_qkv)
        kvc = jnp.concatenate([cache_mla_ckv[:, l], cache_mla_krope[:, l],
                               jnp.zeros((nb, past, KV_W - kv_lora - MLA_ROPE), F32)], axis=-1)
        kc, vc = _ctx_kv(kvc.reshape(nb * past, KV_W), qkv_w[1], qkv_w[5], _tile(512, past))
        o_as, w1b, w3b, w2b = _attention_latent(q, k, v, kc, vc, w1, w3, w2, l, tp, nb, dec_seq, past, tq)
        o_ap = _attention_context(q, k, v, nseq, seq)
        o_bcd = _mixers(proj, mix_w, tp, seq, dec_seq, tm_row)

        xn, hp, r = _out_proj(o_ap, o_as, o_bcd, xp, xs, mod_l, g_norm2[l].reshape(1, d), w_out_b, wr,
                              tm_qkv, mod_index)
        pos, gates, plan, fill = _route(r, router_b, n_blk, tm_qkv)
        xd = _dispatch(hp, gates, pos, fill, tm_row)
        yb = _moe(xd, w1b, w3b, w2b, plan)
        if l == depth - 1:
            xp, xs = _combine(xn, yb, pos, mod_l, tp, tm_row, mod_index)

        ckv_states.append(state[:, :kv_lora].reshape(nseq, seq, kv_lora))
        krope_states.append(state[:, kv_lora:kv_lora + MLA_ROPE].reshape(nseq, seq, MLA_ROPE))

    return (xp.reshape(nseq, seq, d), xs.reshape(nb, dec_seq, d),
            jnp.stack(ckv_states, axis=1), jnp.stack(krope_states, axis=1))
```

```python
import functools

import jax
import jax.numpy as jnp
from jax import lax
from jax.experimental import pallas as pl
from jax.experimental.pallas import tpu as pltpu

F32 = jnp.float32
BF16 = jnp.bfloat16
U32 = jnp.uint32

MLA_HEADS = 4
MLA_NOPE = 128
MLA_ROPE = 64
MLA_V = 128
MLA_QK = MLA_NOPE + MLA_ROPE
HEAD_PAD = 256
GRID_W = 64
ROPE_THETA = 10000.0
CONV_K = 31
SC_K = 3
CHUNK = 128
GMLP_HEADS = 4
N_GROUPS = 8
GROUP_SIZE = 4
N_PAIRS = 6
NORM_EPS = 1e-6

LANE = 128
HALO = 16
PROJ_W = 4352
CQ_COL = 3584
KV_COL = 4096
KV_W = 256
IN_TN = 1536
MOE_BM = 256
OUT_PARTS = 4
N_DMA_PRIORITIES = 2
ATTN_QSUB = 1024
ATTN_KCHUNK = 1024
LOG2_E = 1.4426950408889634
GATE_W = 128
X_ROWS = 9
Y_ROWS = 8
VMEM_CAP = 56 * 1024 * 1024


def _cparams(sem, vmem_mb):
    return pltpu.CompilerParams(dimension_semantics=sem,
                                vmem_limit_bytes=min(vmem_mb * 1024 * 1024, VMEM_CAP))


def _inv_rms(x, n):
    return lax.rsqrt(jnp.sum(x * x, axis=-1, keepdims=True) * (1.0 / n) + NORM_EPS)


def _layernorm(x, g, b):
    mu = jnp.mean(x, axis=-1, keepdims=True)
    xc = x - mu
    var = jnp.mean(xc * xc, axis=-1, keepdims=True)
    return xc * lax.rsqrt(var + NORM_EPS) * g + b


def _split_bf16(x):
    hi = x.astype(BF16)
    lo = (x - hi.astype(F32)).astype(BF16)
    return hi, lo


def _pack_rows(x):
    n = x.shape[1] // 2
    hi = pltpu.bitcast(x[:, :n].astype(BF16).astype(F32), U32)
    lo = pltpu.bitcast(x[:, n:].astype(BF16).astype(F32), U32)
    return (hi & jnp.uint32(0xFFFF0000)) | (lo >> 16)


def _unpack_rows(u):
    a = pltpu.bitcast(u & jnp.uint32(0xFFFF0000), F32)
    b = pltpu.bitcast(u << 16, F32)
    return a, b


def _mod_kernel(c_ref, w_ref, b_ref, o_ref):
    c = c_ref[...]
    a_hi, a_lo = _split_bf16(c * jax.nn.sigmoid(c))
    w_hi, w_lo = _split_bf16(w_ref[...])
    acc = jnp.dot(a_hi, w_hi, preferred_element_type=F32)
    acc += jnp.dot(a_lo, w_hi, preferred_element_type=F32)
    acc += jnp.dot(a_hi, w_lo, preferred_element_type=F32)
    o_ref[...] = acc + b_ref[...]


def _modulation(cvec, w_mod, b_mod):
    depth, d, n = w_mod.shape
    tn = 1024
    return pl.pallas_call(
        _mod_kernel,
        grid=(depth, n // tn),
        in_specs=[pl.BlockSpec((8, d), lambda l, j: (0, 0)),
                  pl.BlockSpec((None, d, tn), lambda l, j: (l, 0, j)),
                  pl.BlockSpec((None, 1, tn), lambda l, j: (l, 0, j))],
        out_specs=pl.BlockSpec((None, 8, tn), lambda l, j: (l, 0, j)),
        out_shape=jax.ShapeDtypeStruct((depth, 8, n), F32),
        compiler_params=_cparams(("arbitrary", "arbitrary"), 40),
        name="modulation",
    )(cvec, w_mod, b_mod.reshape(depth, 1, n))


def _norm_project(x, mod_ref, g_ref, w_ref, proj_ref, kv_ref):
    y = x * _inv_rms(x, x.shape[-1]) * g_ref[...]
    h = (y * (1.0 + mod_ref[1:2, :]) + mod_ref[0:1, :]).astype(BF16)
    for c0 in range(0, PROJ_W, IN_TN):
        c1 = min(c0 + IN_TN, PROJ_W)
        acc = jnp.dot(h, w_ref[:, c0:c1], preferred_element_type=F32)
        proj_ref[:, c0:c1] = acc.astype(BF16)
        if c0 <= KV_COL < c1:
            kv_ref[...] = acc[:, KV_COL - c0:KV_COL - c0 + KV_W]


def _in_proj_kernel(npt, xp_ref, xs_ref, mod_ref, g_ref, w_ref, proj_ref, kv_ref):
    x = jnp.where(pl.program_id(0) < npt, xp_ref[...], xs_ref[...])
    _norm_project(x, mod_ref, g_ref, w_ref, proj_ref, kv_ref)


def _gathered_rows(pos_ref, nxt_ref, yb_ref, buf, sem):
    i = pl.program_id(0)
    slot = i % 2
    tm = buf.shape[1] // Y_ROWS

    def gather(idx_ref, s):
        for r in range(tm):
            src = pl.multiple_of(idx_ref[0, 0, r] * Y_ROWS, Y_ROWS)
            pltpu.make_async_copy(yb_ref.at[pl.ds(src, Y_ROWS)], buf.at[s, pl.ds(r * Y_ROWS, Y_ROWS)],
                                  sem.at[s]).start(priority=r % N_DMA_PRIORITIES)

    @pl.when(i == 0)
    def _():
        gather(pos_ref, 0)

    for s in range(2):
        @pl.when(jnp.logical_and(i + 1 < pl.num_programs(0), slot == 1 - s))
        def _():
            gather(nxt_ref, s)

    pltpu.make_async_copy(yb_ref.at[pl.ds(0, tm * Y_ROWS)], buf.at[slot], sem.at[slot]).wait()
    halves = [_unpack_rows(buf[slot, pl.ds(j, tm, stride=Y_ROWS), :]) for j in range(Y_ROWS)]
    return jnp.concatenate([a for a, _ in halves] + [b for _, b in halves], axis=1)


def _in_proj_fused_kernel(pos_ref, nxt_ref, xn_ref, prev_mod_ref, mod_ref, g_ref, w_ref, yb_ref,
                          proj_ref, kv_ref, x_ref, buf, sem):
    x = xn_ref[...] + prev_mod_ref[5:6, :] * _gathered_rows(pos_ref, nxt_ref, yb_ref, buf, sem)
    x_ref[...] = x
    _norm_project(x, mod_ref, g_ref, w_ref, proj_ref, kv_ref)


def _in_proj_fused(xn, yb, pos, prev_mod, mod_l, g, w, tm, mod_index):
    t, d = xn.shape
    n = t // tm
    pos3 = pos.reshape(n, 1, tm)
    mod_spec = pl.BlockSpec((None, 6, d), lambda i: (mod_index(i, tm), 0, 0))
    return pl.pallas_call(
        _in_proj_fused_kernel,
        grid=(n,),
        in_specs=[pl.BlockSpec((1, 1, tm), lambda i: (i, 0, 0), memory_space=pltpu.SMEM),
                  pl.BlockSpec((1, 1, tm), lambda i: (jnp.minimum(i + 1, n - 1), 0, 0), memory_space=pltpu.SMEM),
                  pl.BlockSpec((tm, d), lambda i: (i, 0)),
                  mod_spec, mod_spec,
                  pl.BlockSpec((1, d), lambda i: (0, 0)),
                  pl.BlockSpec(w.shape, lambda i: (0, 0), pipeline_mode=pl.Buffered(1)),
                  pl.BlockSpec(memory_space=pl.ANY)],
        out_specs=[pl.BlockSpec((tm, PROJ_W), lambda i: (i, 0)),
                   pl.BlockSpec((tm, KV_W), lambda i: (i, 0)),
                   pl.BlockSpec((tm, d), lambda i: (i, 0))],
        out_shape=[jax.ShapeDtypeStruct((t, PROJ_W), BF16),
                   jax.ShapeDtypeStruct((t, KV_W), F32),
                   jax.ShapeDtypeStruct((t, d), F32)],
        scratch_shapes=[pltpu.VMEM((2, tm * Y_ROWS, LANE), U32), pltpu.SemaphoreType.DMA((2,))],
        compiler_params=_cparams(("arbitrary",), 56),
        name="in_proj_fused",
    )(pos3, pos3, xn, prev_mod, mod_l, g, w, yb)


def _in_proj(xp, xs, mod_l, g, w, tm, mod_index):
    tp, d = xp.shape
    t = tp + xs.shape[0]
    npt = tp // tm
    return pl.pallas_call(
        functools.partial(_in_proj_kernel, npt),
        grid=(t // tm,),
        in_specs=[pl.BlockSpec((tm, d), lambda i: (jnp.minimum(i, npt - 1), 0)),
                  pl.BlockSpec((tm, d), lambda i: (jnp.maximum(i - npt, 0), 0)),
                  pl.BlockSpec((None, 6, d), lambda i: (mod_index(i, tm), 0, 0)),
                  pl.BlockSpec((1, d), lambda i: (0, 0)),
                  pl.BlockSpec(w.shape, lambda i: (0, 0), pipeline_mode=pl.Buffered(1))],
        out_specs=[pl.BlockSpec((tm, PROJ_W), lambda i: (i, 0)),
                   pl.BlockSpec((tm, KV_W), lambda i: (i, 0))],
        out_shape=[jax.ShapeDtypeStruct((t, PROJ_W), BF16),
                   jax.ShapeDtypeStruct((t, KV_W), F32)],
        compiler_params=_cparams(("arbitrary",), 56),
        name="in_proj",
    )(xp, xs, mod_l, g, w)


def _rope_tile(t, cos, sin):
    lane = lax.broadcasted_iota(jnp.int32, t.shape, 1)
    first = (lane & 16) == 0
    swapped = jnp.where(first, pltpu.roll(t, LANE - 16, 1), pltpu.roll(t, 16, 1))
    return t * cos + swapped * sin


def _row_ss(x):
    sq = (x * x).astype(BF16)
    return jnp.dot(sq, jnp.ones((x.shape[1], LANE), BF16), preferred_element_type=F32)


def _keys_values(ckv_n, krope, w_ukv_ref, gk_ref, cos, sin, k_ref, v_ref):
    kvf = jnp.dot(ckv_n.astype(BF16), w_ukv_ref[...], preferred_element_type=F32)
    kr_ss = _row_ss(krope)
    gk = gk_ref[...]
    tail = krope * gk[:, MLA_NOPE:]
    if cos is not None:
        tail = _rope_tile(tail, cos, sin)
    for h in range(MLA_HEADS):
        kn = kvf[:, h * MLA_NOPE:(h + 1) * MLA_NOPE]
        r = lax.rsqrt((_row_ss(kn) + kr_ss) * (1.0 / MLA_QK) + NORM_EPS)
        k_ref[:, h * HEAD_PAD:h * HEAD_PAD + MLA_NOPE] = (kn * r * gk[:, :MLA_NOPE]).astype(BF16)
        k_ref[:, h * HEAD_PAD + MLA_NOPE:(h + 1) * HEAD_PAD] = (tail * r).astype(BF16)
    v_ref[...] = kvf[:, MLA_HEADS * MLA_NOPE:].astype(BF16)


def _qkv_kernel(npt, q_lora, cq_ref, kv_ref, cos_ref, sin_ref, w_uq_ref, w_ukv_ref, gql_ref, gkv_ref,
                gq_ref, gk_ref, q_ref, k_ref, v_ref, st_ref):
    i = pl.program_id(0)
    cos = cos_ref[...]
    sin = sin_ref[...]
    cq = cq_ref[...].astype(F32)
    r_cq = lax.rsqrt(_row_ss(cq) * (1.0 / q_lora) + NORM_EPS)
    cqn = cq * jnp.concatenate([r_cq] * (cq.shape[1] // LANE), axis=1) * gql_ref[...]
    qf = jnp.dot(cqn.astype(BF16), w_uq_ref[...], preferred_element_type=F32)
    gq = gq_ref[...]
    scale = MLA_QK ** -0.5 * LOG2_E
    for h in range(MLA_HEADS):
        qh = qf[:, h * HEAD_PAD:(h + 1) * HEAD_PAD]
        r = lax.rsqrt(_row_ss(qh) * (1.0 / MLA_QK) + NORM_EPS) * scale
        head = qh[:, :MLA_NOPE] * r * gq[:, :MLA_NOPE]
        tail = qh[:, MLA_NOPE:] * r * gq[:, MLA_NOPE:]
        q_ref[:, h * HEAD_PAD:h * HEAD_PAD + MLA_NOPE] = head.astype(BF16)
        q_ref[:, h * HEAD_PAD + MLA_NOPE:(h + 1) * HEAD_PAD] = _rope_tile(tail, cos, sin).astype(BF16)

    kv = kv_ref[...]
    ckv = kv[:, :MLA_NOPE]
    krope = kv[:, MLA_NOPE:]
    ckv_n = ckv * _inv_rms(ckv, ckv.shape[-1]) * gkv_ref[...]
    _keys_values(ckv_n, krope, w_ukv_ref, gk_ref, cos, sin, k_ref, v_ref)

    @pl.when(i < npt)
    def _():
        st_ref[:, :MLA_NOPE] = ckv_n
        st_ref[:, MLA_NOPE:] = krope


def _qkv(proj, kv, cos_t, sin_t, w_uq, w_ukv, gql, gkv, gq, gk, tp, dec_seq, q_lora, tm):
    t = proj.shape[0]
    npt = tp // tm
    nseq = dec_seq // tm
    cq_blk = CQ_COL // 512

    def tab(i):
        return (jnp.where(i < npt, nseq, (i - npt) % nseq), 0)

    const = lambda i: (0, 0)
    return pl.pallas_call(
        functools.partial(_qkv_kernel, npt, q_lora),
        grid=(t // tm,),
        in_specs=[pl.BlockSpec((tm, 512), lambda i: (i, cq_blk)),
                  pl.BlockSpec((tm, KV_W), lambda i: (i, 0)),
                  pl.BlockSpec((tm, LANE), tab),
                  pl.BlockSpec((tm, LANE), tab),
                  pl.BlockSpec(w_uq.shape, const),
                  pl.BlockSpec(w_ukv.shape, const),
                  pl.BlockSpec(gql.shape, const),
                  pl.BlockSpec(gkv.shape, const),
                  pl.BlockSpec(gq.shape, const),
                  pl.BlockSpec(gk.shape, const)],
        out_specs=[pl.BlockSpec((tm, MLA_HEADS * HEAD_PAD), lambda i: (i, 0)),
                   pl.BlockSpec((tm, MLA_HEADS * HEAD_PAD), lambda i: (i, 0)),
                   pl.BlockSpec((tm, MLA_HEADS * MLA_V), lambda i: (i, 0)),
                   pl.BlockSpec((tm, KV_W), lambda i: (jnp.minimum(i, npt - 1), 0))],
        out_shape=[jax.ShapeDtypeStruct((t, MLA_HEADS * HEAD_PAD), BF16),
                   jax.ShapeDtypeStruct((t, MLA_HEADS * HEAD_PAD), BF16),
                   jax.ShapeDtypeStruct((t, MLA_HEADS * MLA_V), BF16),
                   jax.ShapeDtypeStruct((tp, KV_W), F32)],
        compiler_params=_cparams(("arbitrary",), 40),
        name="qkv",
    )(proj, kv, cos_t, sin_t, w_uq, w_ukv, gql, gkv, gq, gk)


def _ctx_kv_kernel(kv_ref, w_ukv_ref, gk_ref, k_ref, v_ref):
    kv = kv_ref[...]
    _keys_values(kv[:, :MLA_NOPE], kv[:, MLA_NOPE:], w_ukv_ref, gk_ref, None, None, k_ref, v_ref)


def _ctx_kv(kvc, w_ukv, gk, tm):
    r = kvc.shape[0]
    const = lambda i: (0, 0)
    return pl.pallas_call(
        _ctx_kv_kernel,
        grid=(r // tm,),
        in_specs=[pl.BlockSpec((tm, KV_W), lambda i: (i, 0)),
                  pl.BlockSpec(w_ukv.shape, const),
                  pl.BlockSpec(gk.shape, const)],
        out_specs=[pl.BlockSpec((tm, MLA_HEADS * HEAD_PAD), lambda i: (i, 0)),
                   pl.BlockSpec((tm, MLA_HEADS * MLA_V), lambda i: (i, 0))],
        out_shape=[jax.ShapeDtypeStruct((r, MLA_HEADS * HEAD_PAD), BF16),
                   jax.ShapeDtypeStruct((r, MLA_HEADS * MLA_V), BF16)],
        compiler_params=_cparams(("arbitrary",), 32),
        name="ctx_kv",
    )(kvc, w_ukv, gk)


_NT = (((1,), (1,)), ((), ()))


def _attn_kernel(has_ctx, *refs):
    if has_ctx:
        q_ref, kc_ref, vc_ref, k_ref, v_ref, w1_ref, w3_ref, w2_ref, o_ref, w1o_ref, w3o_ref, w2o_ref = refs
        w1o_ref[...] = w1_ref[...].astype(BF16)
        w3o_ref[...] = w3_ref[...].astype(BF16)
        w2o_ref[...] = w2_ref[...].astype(BF16)
    else:
        q_ref, k_ref, v_ref, o_ref = refs
    n_own = k_ref.shape[0]
    step = min(n_own, ATTN_KCHUNK)
    chunks = [(k_ref, v_ref, c * step, step) for c in range(n_own // step)]
    if has_ctx:
        chunks = chunks + [(kc_ref, vc_ref, 0, kc_ref.shape[0])]
    for h in range(q_ref.shape[1] // HEAD_PAD):
        qk = slice(h * HEAD_PAD, (h + 1) * HEAD_PAD)
        hv = slice(h * MLA_V, (h + 1) * MLA_V)
        tq = q_ref.shape[0]
        sub = min(tq, ATTN_QSUB)
        state = [None] * (tq // sub)
        for kr, vr, off, n in chunks:
            for t in range(tq // sub):
                rows = slice(t * sub, (t + 1) * sub)
                s = lax.dot_general(q_ref[rows, qk], kr[off:off + n, qk], _NT, preferred_element_type=F32)
                mc = jnp.max(s, axis=-1, keepdims=True)
                m_new = mc if state[t] is None else jnp.maximum(state[t][0], mc)
                p = jnp.exp2(s - m_new).astype(BF16)
                v1 = jnp.concatenate([vr[off:off + n, hv], jnp.ones((n, MLA_V), BF16)], axis=1)
                pv = jnp.dot(p, v1, preferred_element_type=F32)
                if state[t] is None:
                    state[t] = (m_new, pv)
                else:
                    m, acc = state[t]
                    state[t] = (m_new, jnp.exp2(m - m_new) * acc + pv)
        for t in range(tq // sub):
            _, acc = state[t]
            o_ref[t * sub:(t + 1) * sub, hv] = (acc[:, :MLA_V] / acc[:, MLA_V:]).astype(BF16)


def _attention_latent(q, k, v, kc, vc, w1, w3, w2, layer, tp, nb, dec_seq, past, tq):
    t = q.shape[0]
    nq = dec_seq // tq
    row0 = tp // tq
    kblk0 = tp // dec_seq
    steps = nb * MLA_HEADS * nq
    depth, n_exp, d, de = w1.shape
    flat = [w1.reshape(depth, n_exp * d, de), w3.reshape(depth, n_exp * d, de), w2.reshape(depth, n_exp * de, d)]
    assert all(w.shape[1] % (8 * steps) == 0 for w in flat)
    step = lambda b, h, i: (b * MLA_HEADS + h) * nq + i
    w_in = [pl.BlockSpec((None, w.shape[1] // steps, w.shape[2]), lambda b, h, i: (layer, step(b, h, i), 0))
            for w in flat]
    w_out = [pl.BlockSpec((w.shape[1] // steps, w.shape[2]), lambda b, h, i: (step(b, h, i), 0)) for w in flat]
    o, w1b, w3b, w2b = pl.pallas_call(
        functools.partial(_attn_kernel, True),
        grid=(nb, MLA_HEADS, nq),
        in_specs=[pl.BlockSpec((tq, HEAD_PAD), lambda b, h, i: (row0 + b * nq + i, h)),
                  pl.BlockSpec((past, HEAD_PAD), lambda b, h, i: (b, h)),
                  pl.BlockSpec((past, MLA_V), lambda b, h, i: (b, h)),
                  pl.BlockSpec((dec_seq, HEAD_PAD), lambda b, h, i: (kblk0 + b, h)),
                  pl.BlockSpec((dec_seq, MLA_V), lambda b, h, i: (kblk0 + b, h))] + w_in,
        out_specs=[pl.BlockSpec((tq, MLA_V), lambda b, h, i: (b * nq + i, h))] + w_out,
        out_shape=[jax.ShapeDtypeStruct((t - tp, MLA_HEADS * MLA_V), BF16)]
                  + [jax.ShapeDtypeStruct(w.shape[1:], BF16) for w in flat],
        compiler_params=_cparams(("arbitrary", "arbitrary", "arbitrary"), 56),
        name="attn_latent",
    )(q, kc, vc, k, v, *flat)
    return o, w1b.reshape(n_exp, d, de), w3b.reshape(n_exp, d, de), w2b.reshape(n_exp, de, d)


def _attention_context(q, k, v, nseq, seq):
    return pl.pallas_call(
        functools.partial(_attn_kernel, False),
        grid=(nseq,),
        in_specs=[pl.BlockSpec((seq, MLA_HEADS * HEAD_PAD), lambda b: (b, 0)),
                  pl.BlockSpec((seq, MLA_HEADS * HEAD_PAD), lambda b: (b, 0)),
                  pl.BlockSpec((seq, MLA_HEADS * MLA_V), lambda b: (b, 0))],
        out_specs=pl.BlockSpec((seq, MLA_HEADS * MLA_V), lambda b: (b, 0)),
        out_shape=jax.ShapeDtypeStruct((nseq * seq, MLA_HEADS * MLA_V), BF16),
        compiler_params=_cparams(("arbitrary",), 32),
        name="attn_context",
    )(q, k, v)


def _mixers_kernel(npt, p_tiles, s_tiles,
                   cv_ref, cvp_ref, cvn_ref, gm_ref, sc_ref, scp_ref, scn_ref, bg_ref,
                   wdw_ref, bdw_ref, gcl_ref, bcl_ref, wpw_ref, bpw_ref,
                   gsl_ref, bsl_ref, wsp_ref, bsp_ref, wsc_ref,
                   o_ref, zs, ys, zsh, cb):
    i = pl.program_id(0)
    tm = cv_ref.shape[0]
    cw = cv_ref.shape[1] // 2
    pos = jnp.where(i < npt, i % p_tiles, (i - npt) % s_tiles)
    n_tiles = jnp.where(i < npt, p_tiles, s_tiles)
    keep_prev = (pos > 0).astype(F32)
    keep_next = (pos < n_tiles - 1).astype(F32)

    def glu(ref):
        a = ref[:, :cw].astype(F32)
        g = ref[:, cw:].astype(F32)
        return a * jax.nn.sigmoid(g)

    def prod(ref):
        return ref[:, :cw].astype(F32) * ref[:, cw:].astype(F32)

    ys[0:HALO, :] = prod(scp_ref) * keep_prev
    ys[HALO:HALO + tm, :] = prod(sc_ref)
    ys[HALO + tm:, :] = prod(scn_ref) * keep_next
    acc = jnp.zeros((tm, cw), F32)
    for tap in range(SC_K):
        acc = acc + ys[pl.ds(HALO - SC_K // 2 + tap, tm), :] * wsc_ref[tap:tap + 1, :]
    o_ref[:, 2 * cw:3 * cw] = (bg_ref[...].astype(F32) * acc).astype(BF16)

    n_lane = cw // LANE
    z_prev, z_main, z_next = glu(cvp_ref) * keep_prev, glu(cv_ref), glu(cvn_ref) * keep_next
    for c in range(n_lane):
        cs = slice(c * LANE, (c + 1) * LANE)
        zs[c, 0:HALO, :] = z_prev[:, cs]
        zs[c, HALO:HALO + tm, :] = z_main[:, cs]
        zs[c, HALO + tm:, :] = z_next[:, cs]
        for r in range(1, 8):
            zsh[r - 1, c] = zs[c, pl.ds(r, tm + 24), :]
    pad = CONV_K // 2
    half = tm // 2

    def conv_chunk(idx, carry):
        c = idx // 2
        base = pl.multiple_of((idx % 2) * half, half)
        acc = jnp.zeros((half, LANE), F32)
        for r in range(8):
            for a in range(4):
                tap = 8 * a + r - (HALO - pad)
                if 0 <= tap < CONV_K:
                    rows = pl.ds(base + 8 * a, half)
                    src = zs[c, rows, :] if r == 0 else zsh[r - 1, c, rows, :]
                    acc = acc + src * wdw_ref[c, tap:tap + 1, :]
        cb[c, pl.ds(base, half), :] = acc
        return carry

    lax.fori_loop(0, 2 * n_lane, conv_chunk, 0)
    z = jnp.concatenate([cb[c] for c in range(n_lane)], axis=1) + bdw_ref[...]
    z = _layernorm(z, gcl_ref[...], bcl_ref[...])
    z = z * jax.nn.sigmoid(z)
    o_b = jnp.dot(z.astype(BF16), wpw_ref[...], preferred_element_type=F32) + bpw_ref[...]
    o_ref[:, 0:cw] = o_b.astype(BF16)

    u = gm_ref[:, :cw].astype(F32)
    vg = _layernorm(gm_ref[:, cw:].astype(F32), gsl_ref[...], bsl_ref[...]).astype(BF16)
    hc = cw // GMLP_HEADS
    for n in range(tm // CHUNK):
        rows = slice(n * CHUNK, (n + 1) * CHUNK)
        for h in range(GMLP_HEADS):
            cols = slice(h * hc, (h + 1) * hc)
            mixed = jnp.dot(wsp_ref[h], vg[rows, cols], preferred_element_type=F32) + bsp_ref[:, cols]
            o_ref[rows, cw + h * hc:cw + (h + 1) * hc] = (u[rows, cols] * mixed).astype(BF16)


def _mixers(proj, weights, tp, seq, dec_seq, tm):
    t = proj.shape[0]
    cw = 512
    npt = tp // tm
    hb = tm // HALO
    last = t // HALO - 1
    const2 = lambda i: (0, 0)

    def main(col):
        return pl.BlockSpec((tm, 2 * cw), lambda i: (i, col))

    def prev(col):
        return pl.BlockSpec((HALO, 2 * cw), lambda i: (jnp.maximum(i * hb - 1, 0), col))

    def nxt(col):
        return pl.BlockSpec((HALO, 2 * cw), lambda i: (jnp.minimum((i + 1) * hb, last), col))

    w_specs = [pl.BlockSpec(w.shape, const2 if w.ndim == 2 else (lambda i: (0, 0, 0))) for w in weights]
    return pl.pallas_call(
        functools.partial(_mixers_kernel, npt, seq // tm, dec_seq // tm),
        grid=(t // tm,),
        in_specs=[main(0), prev(0), nxt(0), main(1), main(2), prev(2), nxt(2),
                  pl.BlockSpec((tm, cw), lambda i: (i, 6))] + w_specs,
        out_specs=pl.BlockSpec((tm, 3 * cw), lambda i: (i, 0)),
        out_shape=jax.ShapeDtypeStruct((t, 3 * cw), BF16),
        scratch_shapes=[pltpu.VMEM((cw // LANE, tm + 2 * HALO, LANE), F32), pltpu.VMEM((tm + 2 * HALO, cw), F32),
                        pltpu.VMEM((7, cw // LANE, tm + 24, LANE), F32), pltpu.VMEM((cw // LANE, tm, LANE), F32)],
        compiler_params=_cparams(("arbitrary",), 40),
        name="mixers",
    )(proj, proj, proj, proj, proj, proj, proj, proj, *weights)


def _out_proj_kernel(npt, n, oap_ref, oas_ref, ob_ref, xp_ref, xs_ref, mod_ref, g_ref, wo_ref, wr_ref,
                     xn_ref, hp_ref, r_ref, mix_a, mix_b):
    i = pl.program_id(0)
    ka = oap_ref.shape[1]
    mm_tile = jnp.minimum(i, n - 1)
    ep_tile = jnp.maximum(i - 1, 0)

    @pl.when(i == 0)
    def _():
        mix_b[...] = jnp.zeros_like(mix_b)

    def step(mix_w, mix_r):
        oa = jnp.where(mm_tile < npt, oap_ref[...], oas_ref[...])
        lhs = jnp.concatenate([oa, ob_ref[...]], axis=1)
        tm, d = mix_w.shape
        parts = OUT_PARTS
        for k in range(parts):
            cols = slice(k * d // parts, (k + 1) * d // parts)
            rows = slice(k * tm // parts, (k + 1) * tm // parts)
            mix_w[:, cols] = jnp.dot(lhs, wo_ref[:, cols], preferred_element_type=F32)
            x = jnp.where(ep_tile < npt, xp_ref[rows, :], xs_ref[rows, :])
            xn = x + mod_ref[2:3, :] * mix_r[rows, :]
            xn_ref[rows, :] = xn
            h = xn * _inv_rms(xn, xn.shape[-1]) * g_ref[...]
            h = h * (1.0 + mod_ref[4:5, :]) + mod_ref[3:4, :]
            hp_ref[rows, :] = _pack_rows(h)
            h_hi, h_lo = _split_bf16(h)
            r_ref[:, rows] = (lax.dot_general(wr_ref[...], h_hi, _NT, preferred_element_type=F32)
                              + lax.dot_general(wr_ref[...], h_lo, _NT, preferred_element_type=F32))

    @pl.when(i % 2 == 0)
    def _():
        step(mix_a, mix_b)

    @pl.when(i % 2 == 1)
    def _():
        step(mix_b, mix_a)


def _out_proj(oap, oas, ob, xp, xs, mod_l, g, wo, wr, tm, mod_index):
    t = ob.shape[0]
    d = xp.shape[1]
    npt = oap.shape[0] // tm
    n = t // tm
    const = lambda i: (0, 0)
    mm = lambda i: jnp.minimum(i, n - 1)
    ep = lambda i: jnp.maximum(i - 1, 0)

    def pair(w, tile, stacked=False):
        second = (lambda i: (jnp.maximum(tile(i), npt), 0)) if stacked else (
            lambda i: (jnp.maximum(tile(i) - npt, 0), 0))
        return [pl.BlockSpec((tm, w), lambda i: (jnp.minimum(tile(i), npt - 1), 0)), pl.BlockSpec((tm, w), second)]

    x_specs = pair(d, ep, stacked=xs is None)
    if xs is None:
        xs = xp
    return pl.pallas_call(
        functools.partial(_out_proj_kernel, npt, n),
        grid=(n + 1,),
        in_specs=pair(oap.shape[1], mm) + [pl.BlockSpec((tm, ob.shape[1]), lambda i: (mm(i), 0))] + x_specs + [
                  pl.BlockSpec((None, 6, d), lambda i: (mod_index(ep(i), tm), 0, 0)),
                  pl.BlockSpec((1, d), const),
                  pl.BlockSpec(wo.shape, const, pipeline_mode=pl.Buffered(1)),
                  pl.BlockSpec(wr.shape, const, pipeline_mode=pl.Buffered(1))],
        out_specs=[pl.BlockSpec((tm, d), lambda i: (ep(i), 0)),
                   pl.BlockSpec((tm, d // 2), lambda i: (ep(i), 0)),
                   pl.BlockSpec((LANE, tm), lambda i: (0, ep(i)))],
        out_shape=[jax.ShapeDtypeStruct((t, d), F32),
                   jax.ShapeDtypeStruct((t, d // 2), U32),
                   jax.ShapeDtypeStruct((LANE, t), F32)],
        scratch_shapes=[pltpu.VMEM((tm, d), F32), pltpu.VMEM((tm, d), F32)],
        compiler_params=_cparams(("arbitrary",), 56),
        name="out_proj",
    )(oap, oas, ob, xp, xs, mod_l, g, wo, wr)


N_BUCKET_PAD = 64
PAIR_FIRST = (0, 0, 0, 1, 1, 3)
PAIR_SECOND = (1, 2, 3, 3, 2, 2)


def _route_kernel(rt_ref, b_ref, tri_ref, bucket_ref, rank_ref, gate_ref, cnt_ref, run):
    i = pl.program_id(0)
    tm = rt_ref.shape[1]
    ne = N_GROUPS * GROUP_SIZE

    @pl.when(i == 0)
    def _():
        run[...] = jnp.zeros_like(run)

    sc = jax.nn.sigmoid(rt_ref[0:ne, :] + rt_ref[ne:2 * ne, :])
    sel = sc + b_ref[...]
    s = [sel[r * N_GROUPS:(r + 1) * N_GROUPS, :] for r in range(GROUP_SIZE)]
    c = [sc[r * N_GROUPS:(r + 1) * N_GROUPS, :] for r in range(GROUP_SIZE)]
    hi01, lo01 = jnp.maximum(s[0], s[1]), jnp.minimum(s[0], s[1])
    hi23, lo23 = jnp.maximum(s[2], s[3]), jnp.minimum(s[2], s[3])
    g_score = jnp.maximum(hi01, hi23) + jnp.maximum(jnp.minimum(hi01, hi23), jnp.maximum(lo01, lo23))
    grp = lax.broadcasted_iota(jnp.int32, g_score.shape, 0)
    g_best = jnp.min(jnp.where(g_score == jnp.max(g_score, axis=0, keepdims=True), grp, N_GROUPS),
                     axis=0, keepdims=True)
    own = grp == g_best
    v = [jnp.sum(jnp.where(own, s[r], 0.0), axis=0, keepdims=True) for r in range(GROUP_SIZE)]
    w = [jnp.sum(jnp.where(own, c[r], 0.0), axis=0, keepdims=True) for r in range(GROUP_SIZE)]

    def first_max(vals):
        best, idx, gate = vals[0], jnp.zeros_like(g_best), w[0]
        for r in range(1, GROUP_SIZE):
            upd = vals[r] > best
            best = jnp.where(upd, vals[r], best)
            idx = jnp.where(upd, r, idx)
            gate = jnp.where(upd, w[r], gate)
        return idx, gate

    r1, w1 = first_max(v)
    r2, w2 = first_max([jnp.where(r1 == r, -jnp.inf, v[r]) for r in range(GROUP_SIZE)])
    wsum = w1 + w2
    swap = r1 > r2
    r_lo = jnp.where(swap, r2, r1)
    r_hi = jnp.where(swap, r1, r2)
    g_lo = jnp.where(swap, w2, w1) / wsum
    g_hi = jnp.where(swap, w1, w2) / wsum
    lex = r_lo * 3 - ((r_lo * (r_lo - 1)) >> 1) + (r_hi - r_lo - 1)
    pair = jnp.where(lex == 3, 4, jnp.where(lex == 4, 3, lex))
    hi_first = pair == N_PAIRS - 1
    gate_ref[0:1, :] = jnp.where(hi_first, g_hi, g_lo)
    gate_ref[1:2, :] = jnp.where(hi_first, g_lo, g_hi)
    bucket = g_best * N_PAIRS + pair
    bucket_ref[...] = bucket

    onehot = (lax.broadcasted_iota(jnp.int32, (N_BUCKET_PAD, tm), 0) == bucket).astype(F32)
    before = jnp.dot(onehot.astype(BF16), tri_ref[...], preferred_element_type=F32) + run[...]
    rank_ref[...] = jnp.sum(onehot * before, axis=0, keepdims=True).astype(jnp.int32)
    run[...] += jnp.sum(onehot, axis=1, keepdims=True)

    @pl.when(i == pl.num_programs(0) - 1)
    def _():
        cnt_ref[...] = run[...]


def _route(rt, router_b, n_blk, tm):
    t = rt.shape[1]
    ne = router_b.shape[0]
    b_col = router_b.astype(F32).reshape(N_GROUPS, GROUP_SIZE).T.reshape(ne, 1)
    tri = (jnp.arange(tm)[:, None] < jnp.arange(tm)[None, :]).astype(BF16)
    bucket, rank, gate, cnt = pl.pallas_call(
        _route_kernel,
        grid=(t // tm,),
        in_specs=[pl.BlockSpec((LANE, tm), lambda i: (0, i)),
                  pl.BlockSpec((ne, 1), lambda i: (0, 0)),
                  pl.BlockSpec((tm, tm), lambda i: (0, 0))],
        out_specs=[pl.BlockSpec((1, tm), lambda i: (0, i)),
                   pl.BlockSpec((1, tm), lambda i: (0, i)),
                   pl.BlockSpec((2, tm), lambda i: (0, i)),
                   pl.BlockSpec((N_BUCKET_PAD, 1), lambda i: (0, 0))],
        out_shape=[jax.ShapeDtypeStruct((1, t), jnp.int32),
                   jax.ShapeDtypeStruct((1, t), jnp.int32),
                   jax.ShapeDtypeStruct((2, t), F32),
                   jax.ShapeDtypeStruct((N_BUCKET_PAD, 1), F32)],
        scratch_shapes=[pltpu.VMEM((N_BUCKET_PAD, 1), F32)],
        compiler_params=_cparams(("arbitrary",), 32),
        name="route",
    )(rt, b_col, tri)

    n_bucket = N_GROUPS * N_PAIRS
    counts = cnt[:n_bucket, 0].astype(jnp.int32)
    padded = (counts + MOE_BM - 1) // MOE_BM * MOE_BM
    pad_end = jnp.cumsum(padded)
    pad_start = pad_end - padded
    bucket = bucket[0]
    sel = bucket[:, None] == jnp.arange(n_bucket, dtype=jnp.int32)[None, :]
    pos = jnp.sum(jnp.where(sel, pad_start[None, :], 0), axis=1).astype(jnp.int32) + rank[0]
    n_used = (pad_end[-1] // MOE_BM).astype(jnp.int32)
    blk = jnp.minimum(jnp.arange(n_blk, dtype=jnp.int32), n_used - 1)
    blk_bucket = jnp.sum((pad_end[None, :] <= (blk * MOE_BM)[:, None]).astype(jnp.int32), axis=1)
    blk_bucket = jnp.minimum(blk_bucket, n_bucket - 1)
    pair_lo = jnp.array(PAIR_FIRST, jnp.int32)
    pair_hi = jnp.array(PAIR_SECOND, jnp.int32)
    blk_pair = blk_bucket % N_PAIRS
    first = (blk_bucket // N_PAIRS) * GROUP_SIZE
    is_pair = blk_pair[:, None] == jnp.arange(N_PAIRS, dtype=jnp.int32)[None, :]
    blk_a = first + jnp.sum(jnp.where(is_pair, pair_lo[None, :], 0), axis=1)
    blk_b = first + jnp.sum(jnp.where(is_pair, pair_hi[None, :], 0), axis=1)
    is_bucket = blk_bucket[:, None] == jnp.arange(n_bucket, dtype=jnp.int32)[None, :]
    per_bucket = lambda v: jnp.sum(jnp.where(is_bucket, v[None, :], 0), axis=1)
    first_blk = per_bucket(pad_start // MOE_BM)
    n_in_bucket = jnp.maximum(per_bucket(padded // MOE_BM), 1)
    src = first_blk + (blk - first_blk + n_in_bucket - 1) % n_in_bucket
    steps = jnp.arange(n_blk, dtype=jnp.int32)
    used = steps < n_used
    blk_out = jnp.where(used, src, steps).astype(jnp.int32)
    blk_rows = jnp.where(used, jnp.clip(per_bucket(pad_start + counts) - src * MOE_BM, 0, MOE_BM), 0)
    gates = jnp.pad(gate.T, ((0, 0), (0, GATE_W - 2)))
    plan = (src.astype(jnp.int32), blk_a.astype(jnp.int32), blk_b.astype(jnp.int32), blk_rows.astype(jnp.int32),
            blk_out)
    rows_in_blk = jnp.clip(per_bucket(pad_start + counts) - blk * MOE_BM, 0, MOE_BM)
    fill = jnp.logical_or(jnp.logical_not(used), rows_in_blk < MOE_BM).astype(jnp.int32)
    return pos, gates, plan, fill


def _dispatch_kernel(fill_ref, pos_ref, hp_ref, gt_ref, xs_ref, buf, zeros, sem, zsem):
    i = pl.program_id(0)
    slot = i % 2
    tm = hp_ref.shape[0]
    n_data = hp_ref.shape[1] // LANE
    blk_rows = zeros.shape[0]

    def wait_all(s):
        pltpu.make_async_copy(buf.at[s], xs_ref.at[pl.ds(0, tm * X_ROWS)], sem.at[s]).wait()

    @pl.when(i == 0)
    def _():
        zeros[...] = jnp.zeros_like(zeros)

        def zero_block(b):
            return pltpu.make_async_copy(zeros, xs_ref.at[pl.ds(pl.multiple_of(b * blk_rows, blk_rows), blk_rows)],
                                         zsem)

        def start(b, carry):
            @pl.when(fill_ref[b] != 0)
            def _():
                zero_block(b).start()
            return carry

        def wait(b, carry):
            @pl.when(fill_ref[b] != 0)
            def _():
                zero_block(b).wait()
            return carry

        lax.fori_loop(0, fill_ref.shape[0], start, 0)
        lax.fori_loop(0, fill_ref.shape[0], wait, 0)

    for s in range(2):
        @pl.when(slot == s)
        def _():
            for j in range(n_data):
                buf[s, pl.ds(j, tm, stride=X_ROWS), :] = hp_ref[:, j * LANE:(j + 1) * LANE]
            buf[s, pl.ds(n_data, tm, stride=X_ROWS), :] = pltpu.bitcast(gt_ref[...], U32)
            for r in range(tm):
                pltpu.make_async_copy(buf.at[s, pl.ds(r * X_ROWS, X_ROWS)],
                                      xs_ref.at[pl.ds(pos_ref[0, 0, r] * X_ROWS, X_ROWS)],
                                      sem.at[s]).start(priority=r % N_DMA_PRIORITIES)

    @pl.when(i > 0)
    def _():
        wait_all(1 - slot)

    @pl.when(i == pl.num_programs(0) - 1)
    def _():
        wait_all(slot)


def _dispatch(hp, gates, pos, fill, tm):
    t, w = hp.shape
    assert w == (X_ROWS - 1) * LANE and GATE_W == LANE
    n_blk = fill.shape[0]
    grid_spec = pltpu.PrefetchScalarGridSpec(
        num_scalar_prefetch=1,
        grid=(t // tm,),
        in_specs=[pl.BlockSpec((1, 1, tm), lambda i, fill: (i, 0, 0), memory_space=pltpu.SMEM),
                  pl.BlockSpec((tm, w), lambda i, fill: (i, 0)),
                  pl.BlockSpec((tm, GATE_W), lambda i, fill: (i, 0))],
        out_specs=pl.BlockSpec(memory_space=pl.ANY),
        scratch_shapes=[pltpu.VMEM((2, tm * X_ROWS, LANE), U32), pltpu.VMEM((MOE_BM * X_ROWS, LANE), U32),
                        pltpu.SemaphoreType.DMA((2,)), pltpu.SemaphoreType.DMA(())],
    )
    return pl.pallas_call(
        _dispatch_kernel,
        grid_spec=grid_spec,
        out_shape=jax.ShapeDtypeStruct((n_blk * MOE_BM * X_ROWS, LANE), U32),
        compiler_params=_cparams(("arbitrary",), 32),
        name="dispatch",
    )(fill, pos.reshape(t // tm, 1, tm), hp, gates)


def _moe_kernel(src_ref, ea_ref, eb_ref, nv_ref, dst_ref, x_ref, w1a_ref, w3a_ref, w2a_ref, w1b_ref, w3b_ref,
                w2b_ref, y_ref):
    del src_ref, ea_ref, eb_ref, dst_ref
    nv = nv_ref[pl.program_id(0)]
    bm = x_ref.shape[0] // X_ROWS

    def experts(m):
        halves = [_unpack_rows(x_ref[pl.ds(j, m, stride=X_ROWS), :]) for j in range(X_ROWS - 1)]
        x = jnp.concatenate([a.astype(BF16) for a, _ in halves] + [b.astype(BF16) for _, b in halves], axis=1)
        gates = pltpu.bitcast(x_ref[pl.ds(X_ROWS - 1, m, stride=X_ROWS), :], F32)

        def hidden(w1_ref, w3_ref, g):
            a = jnp.dot(x, w1_ref[...], preferred_element_type=F32)
            b = jnp.dot(x, w3_ref[...], preferred_element_type=F32)
            return (a * jax.nn.sigmoid(a) * b * g).astype(BF16)

        y = jnp.dot(hidden(w1a_ref, w3a_ref, gates[:, 0:1]), w2a_ref[...], preferred_element_type=F32)
        y += jnp.dot(hidden(w1b_ref, w3b_ref, gates[:, 1:2]), w2b_ref[...], preferred_element_type=F32)
        yp = _pack_rows(y)
        for j in range(Y_ROWS):
            y_ref[pl.ds(j, m, stride=Y_ROWS), :] = yp[:, j * LANE:(j + 1) * LANE]
        if m < bm:
            y_ref[m * Y_ROWS:, :] = jnp.zeros(((bm - m) * Y_ROWS, LANE), y_ref.dtype)

    @pl.when(nv > bm // 2)
    def _():
        experts(bm)

    @pl.when(jnp.logical_and(nv > 0, nv <= bm // 2))
    def _():
        experts(bm // 2)

    @pl.when(nv == 0)
    def _():
        y_ref[...] = jnp.zeros_like(y_ref)


def _moe(xs, w1, w3, w2, plan):
    n_blk = xs.shape[0] // (MOE_BM * X_ROWS)
    _, d, de = w1.shape
    assert d == 2 * Y_ROWS * LANE

    def expert(second, rows, cols):
        if second:
            return pl.BlockSpec((None, rows, cols), lambda i, src, ea, eb, nv, dst: (eb[i], 0, 0))
        return pl.BlockSpec((None, rows, cols), lambda i, src, ea, eb, nv, dst: (ea[i], 0, 0))

    grid_spec = pltpu.PrefetchScalarGridSpec(
        num_scalar_prefetch=5,
        grid=(n_blk,),
        in_specs=[pl.BlockSpec((MOE_BM * X_ROWS, LANE), lambda i, src, ea, eb, nv, dst: (src[i], 0)),
                  expert(False, d, de), expert(False, d, de), expert(False, de, d),
                  expert(True, d, de), expert(True, d, de), expert(True, de, d)],
        out_specs=pl.BlockSpec((MOE_BM * Y_ROWS, LANE), lambda i, src, ea, eb, nv, dst: (dst[i], 0)),
    )
    return pl.pallas_call(
        _moe_kernel,
        grid_spec=grid_spec,
        out_shape=jax.ShapeDtypeStruct((n_blk * MOE_BM * Y_ROWS, LANE), U32),
        compiler_params=_cparams(("arbitrary",), 48),
        name="moe",
    )(*plan, xs, w1, w3, w2, w1, w3, w2)


def _combine_kernel(npt, pos_ref, nxt_ref, xn_ref, mod_ref, yb_ref, op_ref, os_ref, buf, sem):
    i = pl.program_id(0)
    out = xn_ref[...] + mod_ref[5:6, :] * _gathered_rows(pos_ref, nxt_ref, yb_ref, buf, sem)

    @pl.when(i < npt)
    def _():
        op_ref[...] = out

    @pl.when(i >= npt)
    def _():
        os_ref[...] = out


def _combine(xn, yb, pos, mod_l, tp, tm, mod_index):
    t, d = xn.shape
    npt = tp // tm
    n = t // tm
    pos3 = pos.reshape(n, 1, tm)
    return pl.pallas_call(
        functools.partial(_combine_kernel, npt),
        grid=(n,),
        in_specs=[pl.BlockSpec((1, 1, tm), lambda i: (i, 0, 0), memory_space=pltpu.SMEM),
                  pl.BlockSpec((1, 1, tm), lambda i: (jnp.minimum(i + 1, n - 1), 0, 0), memory_space=pltpu.SMEM),
                  pl.BlockSpec((tm, d), lambda i: (i, 0)),
                  pl.BlockSpec((None, 6, d), lambda i: (mod_index(i, tm), 0, 0)),
                  pl.BlockSpec(memory_space=pl.ANY)],
        out_specs=[pl.BlockSpec((tm, d), lambda i: (jnp.minimum(i, npt - 1), 0)),
                   pl.BlockSpec((tm, d), lambda i: (jnp.maximum(i - npt, 0), 0))],
        out_shape=[jax.ShapeDtypeStruct((tp, d), F32),
                   jax.ShapeDtypeStruct((t - tp, d), F32)],
        scratch_shapes=[pltpu.VMEM((2, tm * Y_ROWS, LANE), U32), pltpu.SemaphoreType.DMA((2,))],
        compiler_params=_cparams(("arbitrary",), 32),
        name="combine",
    )(pos3, pos3, xn, mod_l, yb)


def _pad_cols(x, n):
    return jnp.pad(x, ((0, 0), (0, n - x.shape[1])))


def _rope_tables(dec_seq, tm):
    n_freq = MLA_ROPE // 4
    pos = jnp.arange(dec_seq, dtype=jnp.int32)
    row = (pos // GRID_W).astype(F32)
    col = (pos % GRID_W).astype(F32)
    inv = ROPE_THETA ** (-jnp.arange(n_freq, dtype=F32) / n_freq)
    ar = row[:, None] * inv[None, :]
    ac = col[:, None] * inv[None, :]
    ones = jnp.ones((dec_seq, LANE - MLA_ROPE), F32)
    cos = jnp.concatenate([jnp.cos(ar), jnp.cos(ar), jnp.cos(ac), jnp.cos(ac), ones], axis=1)
    sin = jnp.concatenate([-jnp.sin(ar), jnp.sin(ar), -jnp.sin(ac), jnp.sin(ac), 0.0 * ones], axis=1)
    ident_c = jnp.ones((tm, LANE), F32)
    ident_s = jnp.zeros((tm, LANE), F32)
    return jnp.concatenate([cos, ident_c], axis=0), jnp.concatenate([sin, ident_s], axis=0)


def _layer_params(l, q_lora, kv_lora, w_in, g_q_lora, w_uq, g_kv_lora, w_ukv, g_qk_q, g_qk_k,
                  w_dw31, b_dw31, g_conv_ln, b_conv_ln, w_pw, b_pw, g_sgu_ln, b_sgu_ln,
                  w_spatial, b_spatial, w_sc3, w_out):
    d = w_in.shape[1]
    o1 = q_lora
    o2 = o1 + kv_lora
    o3 = o2 + MLA_ROPE
    o4 = o3 + 1024
    o5 = o4 + 1024
    w = w_in[l]
    zeros = lambda n: jnp.zeros((d, n), w.dtype)
    w_in_p = jnp.concatenate(
        [w[:, o3:o4], w[:, o4:o5], w[:, o5 + 512:o5 + 1536], w[:, o5:o5 + 512],
         w[:, :o1], zeros(512 - q_lora),
         w[:, o1:o2], w[:, o2:o3], zeros(PROJ_W - KV_COL - kv_lora - MLA_ROPE)], axis=1).astype(BF16)

    uq = w_uq[l].reshape(q_lora, MLA_HEADS, MLA_QK)
    uq = jnp.pad(uq, ((0, 512 - q_lora), (0, 0), (0, HEAD_PAD - MLA_QK)))
    w_uq_p = uq.reshape(512, MLA_HEADS * HEAD_PAD).astype(BF16)
    ukv = w_ukv[l].reshape(kv_lora, MLA_HEADS, MLA_NOPE + MLA_V)
    w_ukv_p = jnp.concatenate([ukv[:, :, :MLA_NOPE].reshape(kv_lora, -1),
                               ukv[:, :, MLA_NOPE:].reshape(kv_lora, -1)], axis=1).astype(BF16)
    row = lambda v: v.reshape(1, -1).astype(F32)
    qkv_w = (w_uq_p, w_ukv_p, _pad_cols(row(g_q_lora[l]), 512), row(g_kv_lora[l]),
             _pad_cols(row(g_qk_q[l]), HEAD_PAD), _pad_cols(row(g_qk_k[l]), HEAD_PAD))

    hc = w_pw.shape[1] // GMLP_HEADS
    bsp = jnp.repeat(b_spatial[l].T.astype(F32), hc, axis=1)
    w_dw = jnp.pad(w_dw31[l].astype(F32), ((0, 32 - CONV_K), (0, 0)))
    w_dw = w_dw.reshape(32, -1, LANE).transpose(1, 0, 2)
    mix_w = (w_dw, row(b_dw31[l]),
             row(g_conv_ln[l]), row(b_conv_ln[l]), w_pw[l].astype(BF16), row(b_pw[l]),
             row(g_sgu_ln[l]), row(b_sgu_ln[l]), w_spatial[l].astype(BF16), bsp,
             jnp.pad(w_sc3[l].astype(F32), ((0, 8 - SC_K), (0, 0))))
    return w_in_p, qkv_w, mix_w, w_out[l].astype(BF16)


def _tile(limit, *sizes):
    tm = limit
    while any(s % tm for s in sizes):
        tm //= 2
    return tm


def kernel(x_prompt, x_sample, cache_mla_ckv, cache_mla_krope, c, c_ctx, w_mod, b_mod, g_norm1, g_norm2, w_in, g_q_lora, w_uq, g_kv_lora, w_ukv, g_qk_q, g_qk_k, w_dw31, b_dw31, g_conv_ln, b_conv_ln, w_pw, b_pw, g_sgu_ln, b_sgu_ln, w_spatial, b_spatial, w_sc3, w_out, router_w, router_b, w1, w3, w2):
    nseq, seq, d = x_prompt.shape
    nb, dec_seq, _ = x_sample.shape
    depth = w_mod.shape[0]
    past = cache_mla_ckv.shape[2]
    q_lora = g_q_lora.shape[1]
    kv_lora = g_kv_lora.shape[1]
    n_exp = router_w.shape[1]
    tp = nseq * seq
    ts = nb * dec_seq
    t = tp + ts
    assert tp % dec_seq == 0 and seq % CHUNK == 0 and dec_seq % GRID_W == 0
    assert nb + 1 <= 8 and kv_lora == MLA_NOPE and n_exp == N_GROUPS * GROUP_SIZE

    tm_in = _tile(512, tp, dec_seq)
    tm_qkv = _tile(512, tp, dec_seq)
    tm_row = _tile(256, seq, dec_seq)
    tq = _tile(1024, dec_seq)

    def mod_index(i, tm):
        npt = tp // tm
        return jnp.where(i < npt, 0, 1 + (i - npt) // (dec_seq // tm))

    cvec = jnp.concatenate([c_ctx[None, :], c, jnp.zeros((8 - 1 - nb, d), F32)], axis=0)
    mod = _modulation(cvec, w_mod, b_mod).reshape(depth, 8, 6, d)
    cos_t, sin_t = _rope_tables(dec_seq, tm_qkv)

    rw = router_w.T.reshape(N_GROUPS, GROUP_SIZE, d).transpose(1, 0, 2).reshape(n_exp, d)
    rw_hi = rw.astype(BF16)
    rw_lo = (rw - rw_hi.astype(F32)).astype(BF16)
    wr = jnp.pad(jnp.concatenate([rw_hi, rw_lo], axis=0), ((0, LANE - 2 * n_exp), (0, 0)))
    n_blk = (t + N_GROUPS * N_PAIRS * (MOE_BM - 1) + MOE_BM - 1) // MOE_BM

    xp = x_prompt.reshape(tp, d)
    xs = x_sample.reshape(ts, d)
    ckv_states = []
    krope_states = []
    for l in range(depth):
        w_in_p, qkv_w, mix_w, w_out_b = _layer_params(
            l, q_lora, kv_lora, w_in, g_q_lora, w_uq, g_kv_lora, w_ukv, g_qk_q, g_qk_k,
            w_dw31, b_dw31, g_conv_ln, b_conv_ln, w_pw, b_pw, g_sgu_ln, b_sgu_ln,
            w_spatial, b_spatial, w_sc3, w_out)
        mod_l = mod[l]

        if l == 0:
            proj, kv = _in_proj(xp, xs, mod_l, g_norm1[l].reshape(1, d), w_in_p, tm_in, mod_index)
        else:
            proj, kv, xp = _in_proj_fused(xn, yb, pos, mod[l - 1], mod_l, g_norm1[l].reshape(1, d), w_in_p,
                                          tm_in, mod_index)
            xs = None
        q, k, v, state = _qkv(proj, kv, cos_t, sin_t, *qkv_w, tp, dec_seq, q_lora, tm_qkv)
        kvc = jnp.concatenate([cache_mla_ckv[:, l], cache_mla_krope[:, l],
                               jnp.zeros((nb, past, KV_W - kv_lora - MLA_ROPE), F32)], axis=-1)
        kc, vc = _ctx_kv(kvc.reshape(nb * past, KV_W), qkv_w[1], qkv_w[5], _tile(512, past))
        o_as, w1b, w3b, w2b = _attention_latent(q, k, v, kc, vc, w1, w3, w2, l, tp, nb, dec_seq, past, tq)
        o_ap = _attention_context(q, k, v, nseq, seq)
        o_bcd = _mixers(proj, mix_w, tp, seq, dec_seq, tm_row)

        xn, hp, r = _out_proj(o_ap, o_as, o_bcd, xp, xs, mod_l, g_norm2[l].reshape(1, d), w_out_b, wr,
                              tm_qkv, mod_index)
        pos, gates, plan, fill = _route(r, router_b, n_blk, tm_qkv)
        xd = _dispatch(hp, gates, pos, fill, tm_row)
        yb = _moe(xd, w1b, w3b, w2b, plan)
        if l == depth - 1:
            xp, xs = _combine(xn, yb, pos, mod_l, tp, tm_row, mod_index)

        ckv_states.append(state[:, :kv_lora].reshape(nseq, seq, kv_lora))
        krope_states.append(state[:, kv_lora:kv_lora + MLA_ROPE].reshape(nseq, seq, MLA_ROPE))

    return (xp.reshape(nseq, seq, d), xs.reshape(nb, dec_seq, d),
            jnp.stack(ckv_states, axis=1), jnp.stack(krope_states, axis=1))
```

```python
import functools

import jax
import jax.numpy as jnp
from jax import lax
from jax.experimental import pallas as pl
from jax.experimental.pallas import tpu as pltpu

F32 = jnp.float32
BF16 = jnp.bfloat16
U32 = jnp.uint32

MLA_HEADS = 4
MLA_NOPE = 128
MLA_ROPE = 64
MLA_V = 128
MLA_QK = MLA_NOPE + MLA_ROPE
HEAD_PAD = 256
GRID_W = 64
ROPE_THETA = 10000.0
CONV_K = 31
SC_K = 3
CHUNK = 128
GMLP_HEADS = 4
N_GROUPS = 8
GROUP_SIZE = 4
N_PAIRS = 6
NORM_EPS = 1e-6

LANE = 128
HALO = 16
PROJ_W = 4352
CQ_COL = 3584
KV_COL = 4096
KV_W = 256
IN_TN = 1536
MOE_BM = 256
OUT_PARTS = 4
N_DMA_PRIORITIES = 2
ATTN_QSUB = 1024
ATTN_KCHUNK = 1024
LOG2_E = 1.4426950408889634
GATE_W = 128
X_ROWS = 9
Y_ROWS = 8
VMEM_CAP = 56 * 1024 * 1024


def _cparams(sem, vmem_mb):
    return pltpu.CompilerParams(dimension_semantics=sem,
                                vmem_limit_bytes=min(vmem_mb * 1024 * 1024, VMEM_CAP))


def _inv_rms(x, n):
    return lax.rsqrt(jnp.sum(x * x, axis=-1, keepdims=True) * (1.0 / n) + NORM_EPS)


def _layernorm(x, g, b):
    mu = jnp.mean(x, axis=-1, keepdims=True)
    xc = x - mu
    var = jnp.mean(xc * xc, axis=-1, keepdims=True)
    return xc * lax.rsqrt(var + NORM_EPS) * g + b


def _split_bf16(x):
    hi = x.astype(BF16)
    lo = (x - hi.astype(F32)).astype(BF16)
    return hi, lo


def _pack_rows(x):
    n = x.shape[1] // 2
    hi = pltpu.bitcast(x[:, :n].astype(BF16).astype(F32), U32)
    lo = pltpu.bitcast(x[:, n:].astype(BF16).astype(F32), U32)
    return (hi & jnp.uint32(0xFFFF0000)) | (lo >> 16)


def _unpack_rows(u):
    a = pltpu.bitcast(u & jnp.uint32(0xFFFF0000), F32)
    b = pltpu.bitcast(u << 16, F32)
    return a, b


def _mod_kernel(c_ref, w_ref, b_ref, o_ref):
    c = c_ref[...]
    a_hi, a_lo = _split_bf16(c * jax.nn.sigmoid(c))
    w_hi, w_lo = _split_bf16(w_ref[...])
    acc = jnp.dot(a_hi, w_hi, preferred_element_type=F32)
    acc += jnp.dot(a_lo, w_hi, preferred_element_type=F32)
    acc += jnp.dot(a_hi, w_lo, preferred_element_type=F32)
    o_ref[...] = acc + b_ref[...]


def _modulation(cvec, w_mod, b_mod):
    depth, d, n = w_mod.shape
    tn = 1024
    return pl.pallas_call(
        _mod_kernel,
        grid=(depth, n // tn),
        in_specs=[pl.BlockSpec((8, d), lambda l, j: (0, 0)),
                  pl.BlockSpec((None, d, tn), lambda l, j: (l, 0, j)),
                  pl.BlockSpec((None, 1, tn), lambda l, j: (l, 0, j))],
        out_specs=pl.BlockSpec((None, 8, tn), lambda l, j: (l, 0, j)),
        out_shape=jax.ShapeDtypeStruct((depth, 8, n), F32),
        compiler_params=_cparams(("arbitrary", "arbitrary"), 40),
        name="modulation",
    )(cvec, w_mod, b_mod.reshape(depth, 1, n))


def _norm_project(x, mod_ref, g_ref, w_ref, proj_ref, kv_ref):
    y = x * _inv_rms(x, x.shape[-1]) * g_ref[...]
    h = (y * (1.0 + mod_ref[1:2, :]) + mod_ref[0:1, :]).astype(BF16)
    for c0 in range(0, PROJ_W, IN_TN):
        c1 = min(c0 + IN_TN, PROJ_W)
        acc = jnp.dot(h, w_ref[:, c0:c1], preferred_element_type=F32)
        proj_ref[:, c0:c1] = acc.astype(BF16)
        if c0 <= KV_COL < c1:
            kv_ref[...] = acc[:, KV_COL - c0:KV_COL - c0 + KV_W]


def _in_proj_kernel(npt, xp_ref, xs_ref, mod_ref, g_ref, w_ref, proj_ref, kv_ref):
    x = jnp.where(pl.program_id(0) < npt, xp_ref[...], xs_ref[...])
    _norm_project(x, mod_ref, g_ref, w_ref, proj_ref, kv_ref)


def _gathered_rows(pos_ref, nxt_ref, yb_ref, buf, sem):
    i = pl.program_id(0)
    slot = i % 2
    tm = buf.shape[1] // Y_ROWS

    def gather(idx_ref, s):
        for r in range(tm):
            src = pl.multiple_of(idx_ref[0, 0, r] * Y_ROWS, Y_ROWS)
            pltpu.make_async_copy(yb_ref.at[pl.ds(src, Y_ROWS)], buf.at[s, pl.ds(r * Y_ROWS, Y_ROWS)],
                                  sem.at[s]).start(priority=r % N_DMA_PRIORITIES)

    @pl.when(i == 0)
    def _():
        gather(pos_ref, 0)

    for s in range(2):
        @pl.when(jnp.logical_and(i + 1 < pl.num_programs(0), slot == 1 - s))
        def _():
            gather(nxt_ref, s)

    pltpu.make_async_copy(yb_ref.at[pl.ds(0, tm * Y_ROWS)], buf.at[slot], sem.at[slot]).wait()
    halves = [_unpack_rows(buf[slot, pl.ds(j, tm, stride=Y_ROWS), :]) for j in range(Y_ROWS)]
    return jnp.concatenate([a for a, _ in halves] + [b for _, b in halves], axis=1)


def _in_proj_fused_kernel(pos_ref, nxt_ref, xn_ref, prev_mod_ref, mod_ref, g_ref, w_ref, yb_ref,
                          proj_ref, kv_ref, x_ref, buf, sem):
    x = xn_ref[...] + prev_mod_ref[5:6, :] * _gathered_rows(pos_ref, nxt_ref, yb_ref, buf, sem)
    x_ref[...] = x
    _norm_project(x, mod_ref, g_ref, w_ref, proj_ref, kv_ref)


def _in_proj_fused(xn, yb, pos, prev_mod, mod_l, g, w, tm, mod_index):
    t, d = xn.shape
    n = t // tm
    pos3 = pos.reshape(n, 1, tm)
    mod_spec = pl.BlockSpec((None, 6, d), lambda i: (mod_index(i, tm), 0, 0))
    return pl.pallas_call(
        _in_proj_fused_kernel,
        grid=(n,),
        in_specs=[pl.BlockSpec((1, 1, tm), lambda i: (i, 0, 0), memory_space=pltpu.SMEM),
                  pl.BlockSpec((1, 1, tm), lambda i: (jnp.minimum(i + 1, n - 1), 0, 0), memory_space=pltpu.SMEM),
                  pl.BlockSpec((tm, d), lambda i: (i, 0)),
                  mod_spec, mod_spec,
                  pl.BlockSpec((1, d), lambda i: (0, 0)),
                  pl.BlockSpec(w.shape, lambda i: (0, 0), pipeline_mode=pl.Buffered(1)),
                  pl.BlockSpec(memory_space=pl.ANY)],
        out_specs=[pl.BlockSpec((tm, PROJ_W), lambda i: (i, 0)),
                   pl.BlockSpec((tm, KV_W), lambda i: (i, 0)),
                   pl.BlockSpec((tm, d), lambda i: (i, 0))],
        out_shape=[jax.ShapeDtypeStruct((t, PROJ_W), BF16),
                   jax.ShapeDtypeStruct((t, KV_W), F32),
                   jax.ShapeDtypeStruct((t, d), F32)],
        scratch_shapes=[pltpu.VMEM((2, tm * Y_ROWS, LANE), U32), pltpu.SemaphoreType.DMA((2,))],
        compiler_params=_cparams(("arbitrary",), 56),
        name="in_proj_fused",
    )(pos3, pos3, xn, prev_mod, mod_l, g, w, yb)


def _in_proj(xp, xs, mod_l, g, w, tm, mod_index):
    tp, d = xp.shape
    t = tp + xs.shape[0]
    npt = tp // tm
    return pl.pallas_call(
        functools.partial(_in_proj_kernel, npt),
        grid=(t // tm,),
        in_specs=[pl.BlockSpec((tm, d), lambda i: (jnp.minimum(i, npt - 1), 0)),
                  pl.BlockSpec((tm, d), lambda i: (jnp.maximum(i - npt, 0), 0)),
                  pl.BlockSpec((None, 6, d), lambda i: (mod_index(i, tm), 0, 0)),
                  pl.BlockSpec((1, d), lambda i: (0, 0)),
                  pl.BlockSpec(w.shape, lambda i: (0, 0), pipeline_mode=pl.Buffered(1))],
        out_specs=[pl.BlockSpec((tm, PROJ_W), lambda i: (i, 0)),
                   pl.BlockSpec((tm, KV_W), lambda i: (i, 0))],
        out_shape=[jax.ShapeDtypeStruct((t, PROJ_W), BF16),
                   jax.ShapeDtypeStruct((t, KV_W), F32)],
        compiler_params=_cparams(("arbitrary",), 56),
        name="in_proj",
    )(xp, xs, mod_l, g, w)


def _rope_tile(t, cos, sin):
    lane = lax.broadcasted_iota(jnp.int32, t.shape, 1)
    first = (lane & 16) == 0
    swapped = jnp.where(first, pltpu.roll(t, LANE - 16, 1), pltpu.roll(t, 16, 1))
    return t * cos + swapped * sin


def _row_ss(x):
    sq = (x * x).astype(BF16)
    return jnp.dot(sq, jnp.ones((x.shape[1], LANE), BF16), preferred_element_type=F32)


def _keys_values(ckv_n, krope, w_ukv_ref, gk_ref, cos, sin, k_ref, v_ref):
    kvf = jnp.dot(ckv_n.astype(BF16), w_ukv_ref[...], preferred_element_type=F32)
    kr_ss = _row_ss(krope)
    gk = gk_ref[...]
    tail = krope * gk[:, MLA_NOPE:]
    if cos is not None:
        tail = _rope_tile(tail, cos, sin)
    for h in range(MLA_HEADS):
        kn = kvf[:, h * MLA_NOPE:(h + 1) * MLA_NOPE]
        r = lax.rsqrt((_row_ss(kn) + kr_ss) * (1.0 / MLA_QK) + NORM_EPS)
        k_ref[:, h * HEAD_PAD:h * HEAD_PAD + MLA_NOPE] = (kn * r * gk[:, :MLA_NOPE]).astype(BF16)
        k_ref[:, h * HEAD_PAD + MLA_NOPE:(h + 1) * HEAD_PAD] = (tail * r).astype(BF16)
    v_ref[...] = kvf[:, MLA_HEADS * MLA_NOPE:].astype(BF16)


def _qkv_kernel(npt, q_lora, cq_ref, kv_ref, cos_ref, sin_ref, w_uq_ref, w_ukv_ref, gql_ref, gkv_ref,
                gq_ref, gk_ref, q_ref, k_ref, v_ref, st_ref):
    i = pl.program_id(0)
    cos = cos_ref[...]
    sin = sin_ref[...]
    cq = cq_ref[...].astype(F32)
    r_cq = lax.rsqrt(_row_ss(cq) * (1.0 / q_lora) + NORM_EPS)
    cqn = cq * jnp.concatenate([r_cq] * (cq.shape[1] // LANE), axis=1) * gql_ref[...]
    qf = jnp.dot(cqn.astype(BF16), w_uq_ref[...], preferred_element_type=F32)
    gq = gq_ref[...]
    scale = MLA_QK ** -0.5 * LOG2_E
    for h in range(MLA_HEADS):
        qh = qf[:, h * HEAD_PAD:(h + 1) * HEAD_PAD]
        r = lax.rsqrt(_row_ss(qh) * (1.0 / MLA_QK) + NORM_EPS) * scale
        head = qh[:, :MLA_NOPE] * r * gq[:, :MLA_NOPE]
        tail = qh[:, MLA_NOPE:] * r * gq[:, MLA_NOPE:]
        q_ref[:, h * HEAD_PAD:h * HEAD_PAD + MLA_NOPE] = head.astype(BF16)
        q_ref[:, h * HEAD_PAD + MLA_NOPE:(h + 1) * HEAD_PAD] = _rope_tile(tail, cos, sin).astype(BF16)

    kv = kv_ref[...]
    ckv = kv[:, :MLA_NOPE]
    krope = kv[:, MLA_NOPE:]
    ckv_n = ckv * _inv_rms(ckv, ckv.shape[-1]) * gkv_ref[...]
    _keys_values(ckv_n, krope, w_ukv_ref, gk_ref, cos, sin, k_ref, v_ref)

    @pl.when(i < npt)
    def _():
        st_ref[:, :MLA_NOPE] = ckv_n
        st_ref[:, MLA_NOPE:] = krope


def _qkv(proj, kv, cos_t, sin_t, w_uq, w_ukv, gql, gkv, gq, gk, tp, dec_seq, q_lora, tm):
    t = proj.shape[0]
    npt = tp // tm
    nseq = dec_seq // tm
    cq_blk = CQ_COL // 512

    def tab(i):
        return (jnp.where(i < npt, nseq, (i - npt) % nseq), 0)

    const = lambda i: (0, 0)
    return pl.pallas_call(
        functools.partial(_qkv_kernel, npt, q_lora),
        grid=(t // tm,),
        in_specs=[pl.BlockSpec((tm, 512), lambda i: (i, cq_blk)),
                  pl.BlockSpec((tm, KV_W), lambda i: (i, 0)),
                  pl.BlockSpec((tm, LANE), tab),
                  pl.BlockSpec((tm, LANE), tab),
                  pl.BlockSpec(w_uq.shape, const),
                  pl.BlockSpec(w_ukv.shape, const),
                  pl.BlockSpec(gql.shape, const),
                  pl.BlockSpec(gkv.shape, const),
                  pl.BlockSpec(gq.shape, const),
                  pl.BlockSpec(gk.shape, const)],
        out_specs=[pl.BlockSpec((tm, MLA_HEADS * HEAD_PAD), lambda i: (i, 0)),
                   pl.BlockSpec((tm, MLA_HEADS * HEAD_PAD), lambda i: (i, 0)),
                   pl.BlockSpec((tm, MLA_HEADS * MLA_V), lambda i: (i, 0)),
                   pl.BlockSpec((tm, KV_W), lambda i: (jnp.minimum(i, npt - 1), 0))],
        out_shape=[jax.ShapeDtypeStruct((t, MLA_HEADS * HEAD_PAD), BF16),
                   jax.ShapeDtypeStruct((t, MLA_HEADS * HEAD_PAD), BF16),
                   jax.ShapeDtypeStruct((t, MLA_HEADS * MLA_V), BF16),
                   jax.ShapeDtypeStruct((tp, KV_W), F32)],
        compiler_params=_cparams(("arbitrary",), 40),
        name="qkv",
    )(proj, kv, cos_t, sin_t, w_uq, w_ukv, gql, gkv, gq, gk)


def _ctx_kv_kernel(kv_ref, w_ukv_ref, gk_ref, k_ref, v_ref):
    kv = kv_ref[...]
    _keys_values(kv[:, :MLA_NOPE], kv[:, MLA_NOPE:], w_ukv_ref, gk_ref, None, None, k_ref, v_ref)


def _ctx_kv(kvc, w_ukv, gk, tm):
    r = kvc.shape[0]
    const = lambda i: (0, 0)
    return pl.pallas_call(
        _ctx_kv_kernel,
        grid=(r // tm,),
        in_specs=[pl.BlockSpec((tm, KV_W), lambda i: (i, 0)),
                  pl.BlockSpec(w_ukv.shape, const),
                  pl.BlockSpec(gk.shape, const)],
        out_specs=[pl.BlockSpec((tm, MLA_HEADS * HEAD_PAD), lambda i: (i, 0)),
                   pl.BlockSpec((tm, MLA_HEADS * MLA_V), lambda i: (i, 0))],
        out_shape=[jax.ShapeDtypeStruct((r, MLA_HEADS * HEAD_PAD), BF16),
                   jax.ShapeDtypeStruct((r, MLA_HEADS * MLA_V), BF16)],
        compiler_params=_cparams(("arbitrary",), 32),
        name="ctx_kv",
    )(kvc, w_ukv, gk)


_NT = (((1,), (1,)), ((), ()))


def _attn_kernel(has_ctx, *refs):
    if has_ctx:
        q_ref, kc_ref, vc_ref, k_ref, v_ref, w1_ref, w3_ref, w2_ref, o_ref, w1o_ref, w3o_ref, w2o_ref = refs
        w1o_ref[...] = w1_ref[...].astype(BF16)
        w3o_ref[...] = w3_ref[...].astype(BF16)
        w2o_ref[...] = w2_ref[...].astype(BF16)
    else:
        q_ref, k_ref, v_ref, o_ref = refs
    n_own = k_ref.shape[0]
    step = min(n_own, ATTN_KCHUNK)
    chunks = [(k_ref, v_ref, c * step, step) for c in range(n_own // step)]
    if has_ctx:
        chunks = chunks + [(kc_ref, vc_ref, 0, kc_ref.shape[0])]
    for h in range(q_ref.shape[1] // HEAD_PAD):
        qk = slice(h * HEAD_PAD, (h + 1) * HEAD_PAD)
        hv = slice(h * MLA_V, (h + 1) * MLA_V)
        tq = q_ref.shape[0]
        sub = min(tq, ATTN_QSUB)
        state = [None] * (tq // sub)
        for kr, vr, off, n in chunks:
            for t in range(tq // sub):
                rows = slice(t * sub, (t + 1) * sub)
                s = lax.dot_general(q_ref[rows, qk], kr[off:off + n, qk], _NT, preferred_element_type=F32)
                mc = jnp.max(s, axis=-1, keepdims=True)
                m_new = mc if state[t] is None else jnp.maximum(state[t][0], mc)
                p = jnp.exp2(s - m_new).astype(BF16)
                v1 = jnp.concatenate([vr[off:off + n, hv], jnp.ones((n, MLA_V), BF16)], axis=1)
                pv = jnp.dot(p, v1, preferred_element_type=F32)
                if state[t] is None:
                    state[t] = (m_new, pv)
                else:
                    m, acc = state[t]
                    state[t] = (m_new, jnp.exp2(m - m_new) * acc + pv)
        for t in range(tq // sub):
            _, acc = state[t]
            o_ref[t * sub:(t + 1) * sub, hv] = (acc[:, :MLA_V] / acc[:, MLA_V:]).astype(BF16)


def _attention_latent(q, k, v, kc, vc, w1, w3, w2, layer, tp, nb, dec_seq, past, tq):
    t = q.shape[0]
    nq = dec_seq // tq
    row0 = tp // tq
    kblk0 = tp // dec_seq
    steps = nb * MLA_HEADS * nq
    depth, n_exp, d, de = w1.shape
    flat = [w1.reshape(depth, n_exp * d, de), w3.reshape(depth, n_exp * d, de), w2.reshape(depth, n_exp * de, d)]
    assert all(w.shape[1] % (8 * steps) == 0 for w in flat)
    step = lambda b, h, i: (b * MLA_HEADS + h) * nq + i
    w_in = [pl.BlockSpec((None, w.shape[1] // steps, w.shape[2]), lambda b, h, i: (layer, step(b, h, i), 0))
            for w in flat]
    w_out = [pl.BlockSpec((w.shape[1] // steps, w.shape[2]), lambda b, h, i: (step(b, h, i), 0)) for w in flat]
    o, w1b, w3b, w2b = pl.pallas_call(
        functools.partial(_attn_kernel, True),
        grid=(nb, MLA_HEADS, nq),
        in_specs=[pl.BlockSpec((tq, HEAD_PAD), lambda b, h, i: (row0 + b * nq + i, h)),
                  pl.BlockSpec((past, HEAD_PAD), lambda b, h, i: (b, h)),
                  pl.BlockSpec((past, MLA_V), lambda b, h, i: (b, h)),
                  pl.BlockSpec((dec_seq, HEAD_PAD), lambda b, h, i: (kblk0 + b, h)),
                  pl.BlockSpec((dec_seq, MLA_V), lambda b, h, i: (kblk0 + b, h))] + w_in,
        out_specs=[pl.BlockSpec((tq, MLA_V), lambda b, h, i: (b * nq + i, h))] + w_out,
        out_shape=[jax.ShapeDtypeStruct((t - tp, MLA_HEADS * MLA_V), BF16)]
                  + [jax.ShapeDtypeStruct(w.shape[1:], BF16) for w in flat],
        compiler_params=_cparams(("arbitrary", "arbitrary", "arbitrary"), 56),
        name="attn_latent",
    )(q, kc, vc, k, v, *flat)
    return o, w1b.reshape(n_exp, d, de), w3b.reshape(n_exp, d, de), w2b.reshape(n_exp, de, d)


def _attention_context(q, k, v, nseq, seq):
    return pl.pallas_call(
        functools.partial(_attn_kernel, False),
        grid=(nseq,),
        in_specs=[pl.BlockSpec((seq, MLA_HEADS * HEAD_PAD), lambda b: (b, 0)),
                  pl.BlockSpec((seq, MLA_HEADS * HEAD_PAD), lambda b: (b, 0)),
                  pl.BlockSpec((seq, MLA_HEADS * MLA_V), lambda b: (b, 0))],
        out_specs=pl.BlockSpec((seq, MLA_HEADS * MLA_V), lambda b: (b, 0)),
        out_shape=jax.ShapeDtypeStruct((nseq * seq, MLA_HEADS * MLA_V), BF16),
        compiler_params=_cparams(("arbitrary",), 32),
        name="attn_context",
    )(q, k, v)


def _mixers_kernel(npt, p_tiles, s_tiles,
                   cv_ref, cvp_ref, cvn_ref, gm_ref, sc_ref, scp_ref, scn_ref, bg_ref,
                   wdw_ref, bdw_ref, gcl_ref, bcl_ref, wpw_ref, bpw_ref,
                   gsl_ref, bsl_ref, wsp_ref, bsp_ref, wsc_ref,
                   o_ref, zs, ys, zsh, cb):
    i = pl.program_id(0)
    tm = cv_ref.shape[0]
    cw = cv_ref.shape[1] // 2
    pos = jnp.where(i < npt, i % p_tiles, (i - npt) % s_tiles)
    n_tiles = jnp.where(i < npt, p_tiles, s_tiles)
    keep_prev = (pos > 0).astype(F32)
    keep_next = (pos < n_tiles - 1).astype(F32)

    def glu(ref):
        a = ref[:, :cw].astype(F32)
        g = ref[:, cw:].astype(F32)
        return a * jax.nn.sigmoid(g)

    def prod(ref):
        return ref[:, :cw].astype(F32) * ref[:, cw:].astype(F32)

    ys[0:HALO, :] = prod(scp_ref) * keep_prev
    ys[HALO:HALO + tm, :] = prod(sc_ref)
    ys[HALO + tm:, :] = prod(scn_ref) * keep_next
    acc = jnp.zeros((tm, cw), F32)
    for tap in range(SC_K):
        acc = acc + ys[pl.ds(HALO - SC_K // 2 + tap, tm), :] * wsc_ref[tap:tap + 1, :]
    o_ref[:, 2 * cw:3 * cw] = (bg_ref[...].astype(F32) * acc).astype(BF16)

    n_lane = cw // LANE
    z_prev, z_main, z_next = glu(cvp_ref) * keep_prev, glu(cv_ref), glu(cvn_ref) * keep_next
    for c in range(n_lane):
        cs = slice(c * LANE, (c + 1) * LANE)
        zs[c, 0:HALO, :] = z_prev[:, cs]
        zs[c, HALO:HALO + tm, :] = z_main[:, cs]
        zs[c, HALO + tm:, :] = z_next[:, cs]
        for r in range(1, 8):
            zsh[r - 1, c] = zs[c, pl.ds(r, tm + 24), :]
    pad = CONV_K // 2
    half = tm // 2

    def conv_chunk(idx, carry):
        c = idx // 2
        base = pl.multiple_of((idx % 2) * half, half)
        acc = jnp.zeros((half, LANE), F32)
        for r in range(8):
            for a in range(4):
                tap = 8 * a + r - (HALO - pad)
                if 0 <= tap < CONV_K:
                    rows = pl.ds(base + 8 * a, half)
                    src = zs[c, rows, :] if r == 0 else zsh[r - 1, c, rows, :]
                    acc = acc + src * wdw_ref[c, tap:tap + 1, :]
        cb[c, pl.ds(base, half), :] = acc
        return carry

    lax.fori_loop(0, 2 * n_lane, conv_chunk, 0)
    z = jnp.concatenate([cb[c] for c in range(n_lane)], axis=1) + bdw_ref[...]
    z = _layernorm(z, gcl_ref[...], bcl_ref[...])
    z = z * jax.nn.sigmoid(z)
    o_b = jnp.dot(z.astype(BF16), wpw_ref[...], preferred_element_type=F32) + bpw_ref[...]
    o_ref[:, 0:cw] = o_b.astype(BF16)

    u = gm_ref[:, :cw].astype(F32)
    vg = _layernorm(gm_ref[:, cw:].astype(F32), gsl_ref[...], bsl_ref[...]).astype(BF16)
    hc = cw // GMLP_HEADS
    for n in range(tm // CHUNK):
        rows = slice(n * CHUNK, (n + 1) * CHUNK)
        for h in range(GMLP_HEADS):
            cols = slice(h * hc, (h + 1) * hc)
            mixed = jnp.dot(wsp_ref[h], vg[rows, cols], preferred_element_type=F32) + bsp_ref[:, cols]
            o_ref[rows, cw + h * hc:cw + (h + 1) * hc] = (u[rows, cols] * mixed).astype(BF16)


def _mixers(proj, weights, tp, seq, dec_seq, tm):
    t = proj.shape[0]
    cw = 512
    npt = tp // tm
    hb = tm // HALO
    last = t // HALO - 1
    const2 = lambda i: (0, 0)

    def main(col):
        return pl.BlockSpec((tm, 2 * cw), lambda i: (i, col))

    def prev(col):
        return pl.BlockSpec((HALO, 2 * cw), lambda i: (jnp.maximum(i * hb - 1, 0), col))

    def nxt(col):
        return pl.BlockSpec((HALO, 2 * cw), lambda i: (jnp.minimum((i + 1) * hb, last), col))

    w_specs = [pl.BlockSpec(w.shape, const2 if w.ndim == 2 else (lambda i: (0, 0, 0))) for w in weights]
    return pl.pallas_call(
        functools.partial(_mixers_kernel, npt, seq // tm, dec_seq // tm),
        grid=(t // tm,),
        in_specs=[main(0), prev(0), nxt(0), main(1), main(2), prev(2), nxt(2),
                  pl.BlockSpec((tm, cw), lambda i: (i, 6))] + w_specs,
        out_specs=pl.BlockSpec((tm, 3 * cw), lambda i: (i, 0)),
        out_shape=jax.ShapeDtypeStruct((t, 3 * cw), BF16),
        scratch_shapes=[pltpu.VMEM((cw // LANE, tm + 2 * HALO, LANE), F32), pltpu.VMEM((tm + 2 * HALO, cw), F32),
                        pltpu.VMEM((7, cw // LANE, tm + 24, LANE), F32), pltpu.VMEM((cw // LANE, tm, LANE), F32)],
        compiler_params=_cparams(("arbitrary",), 40),
        name="mixers",
    )(proj, proj, proj, proj, proj, proj, proj, proj, *weights)


def _out_proj_kernel(npt, n, oap_ref, oas_ref, ob_ref, xp_ref, xs_ref, mod_ref, g_ref, wo_ref, wr_ref,
                     xn_ref, hp_ref, r_ref, mix_a, mix_b):
    i = pl.program_id(0)
    ka = oap_ref.shape[1]
    mm_tile = jnp.minimum(i, n - 1)
    ep_tile = jnp.maximum(i - 1, 0)

    @pl.when(i == 0)
    def _():
        mix_b[...] = jnp.zeros_like(mix_b)

    def step(mix_w, mix_r):
        oa = jnp.where(mm_tile < npt, oap_ref[...], oas_ref[...])
        lhs = jnp.concatenate([oa, ob_ref[...]], axis=1)
        tm, d = mix_w.shape
        parts = OUT_PARTS
        for k in range(parts):
            cols = slice(k * d // parts, (k + 1) * d // parts)
            rows = slice(k * tm // parts, (k + 1) * tm // parts)
            mix_w[:, cols] = jnp.dot(lhs, wo_ref[:, cols], preferred_element_type=F32)
            x = jnp.where(ep_tile < npt, xp_ref[rows, :], xs_ref[rows, :])
            xn = x + mod_ref[2:3, :] * mix_r[rows, :]
            xn_ref[rows, :] = xn
            h = xn * _inv_rms(xn, xn.shape[-1]) * g_ref[...]
            h = h * (1.0 + mod_ref[4:5, :]) + mod_ref[3:4, :]
            hp_ref[rows, :] = _pack_rows(h)
            h_hi, h_lo = _split_bf16(h)
            r_ref[:, rows] = (lax.dot_general(wr_ref[...], h_hi, _NT, preferred_element_type=F32)
                              + lax.dot_general(wr_ref[...], h_lo, _NT, preferred_element_type=F32))

    @pl.when(i % 2 == 0)
    def _():
        step(mix_a, mix_b)

    @pl.when(i % 2 == 1)
    def _():
        step(mix_b, mix_a)


def _out_proj(oap, oas, ob, xp, xs, mod_l, g, wo, wr, tm, mod_index):
    t = ob.shape[0]
    d = xp.shape[1]
    npt = oap.shape[0] // tm
    n = t // tm
    const = lambda i: (0, 0)
    mm = lambda i: jnp.minimum(i, n - 1)
    ep = lambda i: jnp.maximum(i - 1, 0)

    def pair(w, tile, stacked=False):
        second = (lambda i: (jnp.maximum(tile(i), npt), 0)) if stacked else (
            lambda i: (jnp.maximum(tile(i) - npt, 0), 0))
        return [pl.BlockSpec((tm, w), lambda i: (jnp.minimum(tile(i), npt - 1), 0)), pl.BlockSpec((tm, w), second)]

    x_specs = pair(d, ep, stacked=xs is None)
    if xs is None:
        xs = xp
    return pl.pallas_call(
        functools.partial(_out_proj_kernel, npt, n),
        grid=(n + 1,),
        in_specs=pair(oap.shape[1], mm) + [pl.BlockSpec((tm, ob.shape[1]), lambda i: (mm(i), 0))] + x_specs + [
                  pl.BlockSpec((None, 6, d), lambda i: (mod_index(ep(i), tm), 0, 0)),
                  pl.BlockSpec((1, d), const),
                  pl.BlockSpec(wo.shape, const, pipeline_mode=pl.Buffered(1)),
                  pl.BlockSpec(wr.shape, const, pipeline_mode=pl.Buffered(1))],
        out_specs=[pl.BlockSpec((tm, d), lambda i: (ep(i), 0)),
                   pl.BlockSpec((tm, d // 2), lambda i: (ep(i), 0)),
                   pl.BlockSpec((LANE, tm), lambda i: (0, ep(i)))],
        out_shape=[jax.ShapeDtypeStruct((t, d), F32),
                   jax.ShapeDtypeStruct((t, d // 2), U32),
                   jax.ShapeDtypeStruct((LANE, t), F32)],
        scratch_shapes=[pltpu.VMEM((tm, d), F32), pltpu.VMEM((tm, d), F32)],
        compiler_params=_cparams(("arbitrary",), 56),
        name="out_proj",
    )(oap, oas, ob, xp, xs, mod_l, g, wo, wr)


N_BUCKET_PAD = 64
PAIR_FIRST = (0, 0, 0, 1, 1, 3)
PAIR_SECOND = (1, 2, 3, 3, 2, 2)


def _route_kernel(rt_ref, b_ref, tri_ref, bucket_ref, rank_ref, gate_ref, cnt_ref, run):
    i = pl.program_id(0)
    tm = rt_ref.shape[1]
    ne = N_GROUPS * GROUP_SIZE

    @pl.when(i == 0)
    def _():
        run[...] = jnp.zeros_like(run)

    sc = jax.nn.sigmoid(rt_ref[0:ne, :] + rt_ref[ne:2 * ne, :])
    sel = sc + b_ref[...]
    s = [sel[r * N_GROUPS:(r + 1) * N_GROUPS, :] for r in range(GROUP_SIZE)]
    c = [sc[r * N_GROUPS:(r + 1) * N_GROUPS, :] for r in range(GROUP_SIZE)]
    hi01, lo01 = jnp.maximum(s[0], s[1]), jnp.minimum(s[0], s[1])
    hi23, lo23 = jnp.maximum(s[2], s[3]), jnp.minimum(s[2], s[3])
    g_score = jnp.maximum(hi01, hi23) + jnp.maximum(jnp.minimum(hi01, hi23), jnp.maximum(lo01, lo23))
    grp = lax.broadcasted_iota(jnp.int32, g_score.shape, 0)
    g_best = jnp.min(jnp.where(g_score == jnp.max(g_score, axis=0, keepdims=True), grp, N_GROUPS),
                     axis=0, keepdims=True)
    own = grp == g_best
    v = [jnp.sum(jnp.where(own, s[r], 0.0), axis=0, keepdims=True) for r in range(GROUP_SIZE)]
    w = [jnp.sum(jnp.where(own, c[r], 0.0), axis=0, keepdims=True) for r in range(GROUP_SIZE)]

    def first_max(vals):
        best, idx, gate = vals[0], jnp.zeros_like(g_best), w[0]
        for r in range(1, GROUP_SIZE):
            upd = vals[r] > best
            best = jnp.where(upd, vals[r], best)
            idx = jnp.where(upd, r, idx)
            gate = jnp.where(upd, w[r], gate)
        return idx, gate

    r1, w1 = first_max(v)
    r2, w2 = first_max([jnp.where(r1 == r, -jnp.inf, v[r]) for r in range(GROUP_SIZE)])
    wsum = w1 + w2
    swap = r1 > r2
    r_lo = jnp.where(swap, r2, r1)
    r_hi = jnp.where(swap, r1, r2)
    g_lo = jnp.where(swap, w2, w1) / wsum
    g_hi = jnp.where(swap, w1, w2) / wsum
    lex = r_lo * 3 - ((r_lo * (r_lo - 1)) >> 1) + (r_hi - r_lo - 1)
    pair = jnp.where(lex == 3, 4, jnp.where(lex == 4, 3, lex))
    hi_first = pair == N_PAIRS - 1
    gate_ref[0:1, :] = jnp.where(hi_first, g_hi, g_lo)
    gate_ref[1:2, :] = jnp.where(hi_first, g_lo, g_hi)
    bucket = g_best * N_PAIRS + pair
    bucket_ref[...] = bucket

    onehot = (lax.broadcasted_iota(jnp.int32, (N_BUCKET_PAD, tm), 0) == bucket).astype(F32)
    before = jnp.dot(onehot.astype(BF16), tri_ref[...], preferred_element_type=F32) + run[...]
    rank_ref[...] = jnp.sum(onehot * before, axis=0, keepdims=True).astype(jnp.int32)
    run[...] += jnp.sum(onehot, axis=1, keepdims=True)

    @pl.when(i == pl.num_programs(0) - 1)
    def _():
        cnt_ref[...] = run[...]


def _route(rt, router_b, n_blk, tm):
    t = rt.shape[1]
    ne = router_b.shape[0]
    b_col = router_b.astype(F32).reshape(N_GROUPS, GROUP_SIZE).T.reshape(ne, 1)
    tri = (jnp.arange(tm)[:, None] < jnp.arange(tm)[None, :]).astype(BF16)
    bucket, rank, gate, cnt = pl.pallas_call(
        _route_kernel,
        grid=(t // tm,),
        in_specs=[pl.BlockSpec((LANE, tm), lambda i: (0, i)),
                  pl.BlockSpec((ne, 1), lambda i: (0, 0)),
                  pl.BlockSpec((tm, tm), lambda i: (0, 0))],
        out_specs=[pl.BlockSpec((1, tm), lambda i: (0, i)),
                   pl.BlockSpec((1, tm), lambda i: (0, i)),
                   pl.BlockSpec((2, tm), lambda i: (0, i)),
                   pl.BlockSpec((N_BUCKET_PAD, 1), lambda i: (0, 0))],
        out_shape=[jax.ShapeDtypeStruct((1, t), jnp.int32),
                   jax.ShapeDtypeStruct((1, t), jnp.int32),
                   jax.ShapeDtypeStruct((2, t), F32),
                   jax.ShapeDtypeStruct((N_BUCKET_PAD, 1), F32)],
        scratch_shapes=[pltpu.VMEM((N_BUCKET_PAD, 1), F32)],
        compiler_params=_cparams(("arbitrary",), 32),
        name="route",
    )(rt, b_col, tri)

    n_bucket = N_GROUPS * N_PAIRS
    counts = cnt[:n_bucket, 0].astype(jnp.int32)
    padded = (counts + MOE_BM - 1) // MOE_BM * MOE_BM
    pad_end = jnp.cumsum(padded)
    pad_start = pad_end - padded
    bucket = bucket[0]
    sel = bucket[:, None] == jnp.arange(n_bucket, dtype=jnp.int32)[None, :]
    pos = jnp.sum(jnp.where(sel, pad_start[None, :], 0), axis=1).astype(jnp.int32) + rank[0]
    n_used = (pad_end[-1] // MOE_BM).astype(jnp.int32)
    blk = jnp.minimum(jnp.arange(n_blk, dtype=jnp.int32), n_used - 1)
    blk_bucket = jnp.sum((pad_end[None, :] <= (blk * MOE_BM)[:, None]).astype(jnp.int32), axis=1)
    blk_bucket = jnp.minimum(blk_bucket, n_bucket - 1)
    pair_lo = jnp.array(PAIR_FIRST, jnp.int32)
    pair_hi = jnp.array(PAIR_SECOND, jnp.int32)
    blk_pair = blk_bucket % N_PAIRS
    first = (blk_bucket // N_PAIRS) * GROUP_SIZE
    is_pair = blk_pair[:, None] == jnp.arange(N_PAIRS, dtype=jnp.int32)[None, :]
    blk_a = first + jnp.sum(jnp.where(is_pair, pair_lo[None, :], 0), axis=1)
    blk_b = first + jnp.sum(jnp.where(is_pair, pair_hi[None, :], 0), axis=1)
    is_bucket = blk_bucket[:, None] == jnp.arange(n_bucket, dtype=jnp.int32)[None, :]
    per_bucket = lambda v: jnp.sum(jnp.where(is_bucket, v[None, :], 0), axis=1)
    first_blk = per_bucket(pad_start // MOE_BM)
    n_in_bucket = jnp.maximum(per_bucket(padded // MOE_BM), 1)
    src = first_blk + (blk - first_blk + n_in_bucket - 1) % n_in_bucket
    steps = jnp.arange(n_blk, dtype=jnp.int32)
    used = steps < n_used
    blk_out = jnp.where(used, src, steps).astype(jnp.int32)
    blk_rows = jnp.where(used, jnp.clip(per_bucket(pad_start + counts) - src * MOE_BM, 0, MOE_BM), 0)
    gates = jnp.pad(gate.T, ((0, 0), (0, GATE_W - 2)))
    plan = (src.astype(jnp.int32), blk_a.astype(jnp.int32), blk_b.astype(jnp.int32), blk_rows.astype(jnp.int32),
            blk_out)
    rows_in_blk = jnp.clip(per_bucket(pad_start + counts) - blk * MOE_BM, 0, MOE_BM)
    fill = jnp.logical_or(jnp.logical_not(used), rows_in_blk < MOE_BM).astype(jnp.int32)
    return pos, gates, plan, fill


def _dispatch_kernel(fill_ref, pos_ref, hp_ref, gt_ref, xs_ref, buf, zeros, sem, zsem):
    i = pl.program_id(0)
    slot = i % 2
    tm = hp_ref.shape[0]
    n_data = hp_ref.shape[1] // LANE
    blk_rows = zeros.shape[0]

    def wait_all(s):
        pltpu.make_async_copy(buf.at[s], xs_ref.at[pl.ds(0, tm * X_ROWS)], sem.at[s]).wait()

    @pl.when(i == 0)
    def _():
        zeros[...] = jnp.zeros_like(zeros)

        def zero_block(b):
            return pltpu.make_async_copy(zeros, xs_ref.at[pl.ds(pl.multiple_of(b * blk_rows, blk_rows), blk_rows)],
                                         zsem)

        def start(b, carry):
            @pl.when(fill_ref[b] != 0)
            def _():
                zero_block(b).start()
            return carry

        def wait(b, carry):
            @pl.when(fill_ref[b] != 0)
            def _():
                zero_block(b).wait()
            return carry

        lax.fori_loop(0, fill_ref.shape[0], start, 0)
        lax.fori_loop(0, fill_ref.shape[0], wait, 0)

    for s in range(2):
        @pl.when(slot == s)
        def _():
            for j in range(n_data):
                buf[s, pl.ds(j, tm, stride=X_ROWS), :] = hp_ref[:, j * LANE:(j + 1) * LANE]
            buf[s, pl.ds(n_data, tm, stride=X_ROWS), :] = pltpu.bitcast(gt_ref[...], U32)
            for r in range(tm):
                pltpu.make_async_copy(buf.at[s, pl.ds(r * X_ROWS, X_ROWS)],
                                      xs_ref.at[pl.ds(pos_ref[0, 0, r] * X_ROWS, X_ROWS)],
                                      sem.at[s]).start(priority=r % N_DMA_PRIORITIES)

    @pl.when(i > 0)
    def _():
        wait_all(1 - slot)

    @pl.when(i == pl.num_programs(0) - 1)
    def _():
        wait_all(slot)


def _dispatch(hp, gates, pos, fill, tm):
    t, w = hp.shape
    assert w == (X_ROWS - 1) * LANE and GATE_W == LANE
    n_blk = fill.shape[0]
    grid_spec = pltpu.PrefetchScalarGridSpec(
        num_scalar_prefetch=1,
        grid=(t // tm,),
        in_specs=[pl.BlockSpec((1, 1, tm), lambda i, fill: (i, 0, 0), memory_space=pltpu.SMEM),
                  pl.BlockSpec((tm, w), lambda i, fill: (i, 0)),
                  pl.BlockSpec((tm, GATE_W), lambda i, fill: (i, 0))],
        out_specs=pl.BlockSpec(memory_space=pl.ANY),
        scratch_shapes=[pltpu.VMEM((2, tm * X_ROWS, LANE), U32), pltpu.VMEM((MOE_BM * X_ROWS, LANE), U32),
                        pltpu.SemaphoreType.DMA((2,)), pltpu.SemaphoreType.DMA(())],
    )
    return pl.pallas_call(
        _dispatch_kernel,
        grid_spec=grid_spec,
        out_shape=jax.ShapeDtypeStruct((n_blk * MOE_BM * X_ROWS, LANE), U32),
        compiler_params=_cparams(("arbitrary",), 32),
        name="dispatch",
    )(fill, pos.reshape(t // tm, 1, tm), hp, gates)


def _moe_kernel(src_ref, ea_ref, eb_ref, nv_ref, dst_ref, x_ref, w1a_ref, w3a_ref, w2a_ref, w1b_ref, w3b_ref,
                w2b_ref, y_ref):
    del src_ref, ea_ref, eb_ref, dst_ref
    nv = nv_ref[pl.program_id(0)]
    bm = x_ref.shape[0] // X_ROWS

    def experts(m):
        halves = [_unpack_rows(x_ref[pl.ds(j, m, stride=X_ROWS), :]) for j in range(X_ROWS - 1)]
        x = jnp.concatenate([a.astype(BF16) for a, _ in halves] + [b.astype(BF16) for _, b in halves], axis=1)
        gates = pltpu.bitcast(x_ref[pl.ds(X_ROWS - 1, m, stride=X_ROWS), :], F32)

        def hidden(w1_ref, w3_ref, g):
            a = jnp.dot(x, w1_ref[...], preferred_element_type=F32)
            b = jnp.dot(x, w3_ref[...], preferred_element_type=F32)
            return (a * jax.nn.sigmoid(a) * b * g).astype(BF16)

        y = jnp.dot(hidden(w1a_ref, w3a_ref, gates[:, 0:1]), w2a_ref[...], preferred_element_type=F32)
        y += jnp.dot(hidden(w1b_ref, w3b_ref, gates[:, 1:2]), w2b_ref[...], preferred_element_type=F32)
        yp = _pack_rows(y)
        for j in range(Y_ROWS):
            y_ref[pl.ds(j, m, stride=Y_ROWS), :] = yp[:, j * LANE:(j + 1) * LANE]
        if m < bm:
            y_ref[m * Y_ROWS:, :] = jnp.zeros(((bm - m) * Y_ROWS, LANE), y_ref.dtype)

    @pl.when(nv > bm // 2)
    def _():
        experts(bm)

    @pl.when(jnp.logical_and(nv > 0, nv <= bm // 2))
    def _():
        experts(bm // 2)

    @pl.when(nv == 0)
    def _():
        y_ref[...] = jnp.zeros_like(y_ref)


def _moe(xs, w1, w3, w2, plan):
    n_blk = xs.shape[0] // (MOE_BM * X_ROWS)
    _, d, de = w1.shape
    assert d == 2 * Y_ROWS * LANE

    def expert(second, rows, cols):
        if second:
            return pl.BlockSpec((None, rows, cols), lambda i, src, ea, eb, nv, dst: (eb[i], 0, 0))
        return pl.BlockSpec((None, rows, cols), lambda i, src, ea, eb, nv, dst: (ea[i], 0, 0))

    grid_spec = pltpu.PrefetchScalarGridSpec(
        num_scalar_prefetch=5,
        grid=(n_blk,),
        in_specs=[pl.BlockSpec((MOE_BM * X_ROWS, LANE), lambda i, src, ea, eb, nv, dst: (src[i], 0)),
                  expert(False, d, de), expert(False, d, de), expert(False, de, d),
                  expert(True, d, de), expert(True, d, de), expert(True, de, d)],
        out_specs=pl.BlockSpec((MOE_BM * Y_ROWS, LANE), lambda i, src, ea, eb, nv, dst: (dst[i], 0)),
    )
    return pl.pallas_call(
        _moe_kernel,
        grid_spec=grid_spec,
        out_shape=jax.ShapeDtypeStruct((n_blk * MOE_BM * Y_ROWS, LANE), U32),
        compiler_params=_cparams(("arbitrary",), 48),
        name="moe",
    )(*plan, xs, w1, w3, w2, w1, w3, w2)


def _combine_kernel(npt, pos_ref, nxt_ref, xn_ref, mod_ref, yb_ref, op_ref, os_ref, buf, sem):
    i = pl.program_id(0)
    out = xn_ref[...] + mod_ref[5:6, :] * _gathered_rows(pos_ref, nxt_ref, yb_ref, buf, sem)

    @pl.when(i < npt)
    def _():
        op_ref[...] = out

    @pl.when(i >= npt)
    def _():
        os_ref[...] = out


def _combine(xn, yb, pos, mod_l, tp, tm, mod_index):
    t, d = xn.shape
    npt = tp // tm
    n = t // tm
    pos3 = pos.reshape(n, 1, tm)
    return pl.pallas_call(
        functools.partial(_combine_kernel, npt),
        grid=(n,),
        in_specs=[pl.BlockSpec((1, 1, tm), lambda i: (i, 0, 0), memory_space=pltpu.SMEM),
                  pl.BlockSpec((1, 1, tm), lambda i: (jnp.minimum(i + 1, n - 1), 0, 0), memory_space=pltpu.SMEM),
                  pl.BlockSpec((tm, d), lambda i: (i, 0)),
                  pl.BlockSpec((None, 6, d), lambda i: (mod_index(i, tm), 0, 0)),
                  pl.BlockSpec(memory_space=pl.ANY)],
        out_specs=[pl.BlockSpec((tm, d), lambda i: (jnp.minimum(i, npt - 1), 0)),
                   pl.BlockSpec((tm, d), lambda i: (jnp.maximum(i - npt, 0), 0))],
        out_shape=[jax.ShapeDtypeStruct((tp, d), F32),
                   jax.ShapeDtypeStruct((t - tp, d), F32)],
        scratch_shapes=[pltpu.VMEM((2, tm * Y_ROWS, LANE), U32), pltpu.SemaphoreType.DMA((2,))],
        compiler_params=_cparams(("arbitrary",), 48),
        name="combine",
    )(pos3, pos3, xn, mod_l, yb)


def _pad_cols(x, n):
    return jnp.pad(x, ((0, 0), (0, n - x.shape[1])))


def _rope_tables(dec_seq, tm):
    n_freq = MLA_ROPE // 4
    pos = jnp.arange(dec_seq, dtype=jnp.int32)
    row = (pos // GRID_W).astype(F32)
    col = (pos % GRID_W).astype(F32)
    inv = ROPE_THETA ** (-jnp.arange(n_freq, dtype=F32) / n_freq)
    ar = row[:, None] * inv[None, :]
    ac = col[:, None] * inv[None, :]
    ones = jnp.ones((dec_seq, LANE - MLA_ROPE), F32)
    cos = jnp.concatenate([jnp.cos(ar), jnp.cos(ar), jnp.cos(ac), jnp.cos(ac), ones], axis=1)
    sin = jnp.concatenate([-jnp.sin(ar), jnp.sin(ar), -jnp.sin(ac), jnp.sin(ac), 0.0 * ones], axis=1)
    ident_c = jnp.ones((tm, LANE), F32)
    ident_s = jnp.zeros((tm, LANE), F32)
    return jnp.concatenate([cos, ident_c], axis=0), jnp.concatenate([sin, ident_s], axis=0)


def _layer_params(l, q_lora, kv_lora, w_in, g_q_lora, w_uq, g_kv_lora, w_ukv, g_qk_q, g_qk_k,
                  w_dw31, b_dw31, g_conv_ln, b_conv_ln, w_pw, b_pw, g_sgu_ln, b_sgu_ln,
                  w_spatial, b_spatial, w_sc3, w_out):
    d = w_in.shape[1]
    o1 = q_lora
    o2 = o1 + kv_lora
    o3 = o2 + MLA_ROPE
    o4 = o3 + 1024
    o5 = o4 + 1024
    w = w_in[l]
    zeros = lambda n: jnp.zeros((d, n), w.dtype)
    w_in_p = jnp.concatenate(
        [w[:, o3:o4], w[:, o4:o5], w[:, o5 + 512:o5 + 1536], w[:, o5:o5 + 512],
         w[:, :o1], zeros(512 - q_lora),
         w[:, o1:o2], w[:, o2:o3], zeros(PROJ_W - KV_COL - kv_lora - MLA_ROPE)], axis=1).astype(BF16)

    uq = w_uq[l].reshape(q_lora, MLA_HEADS, MLA_QK)
    uq = jnp.pad(uq, ((0, 512 - q_lora), (0, 0), (0, HEAD_PAD - MLA_QK)))
    w_uq_p = uq.reshape(512, MLA_HEADS * HEAD_PAD).astype(BF16)
    ukv = w_ukv[l].reshape(kv_lora, MLA_HEADS, MLA_NOPE + MLA_V)
    w_ukv_p = jnp.concatenate([ukv[:, :, :MLA_NOPE].reshape(kv_lora, -1),
                               ukv[:, :, MLA_NOPE:].reshape(kv_lora, -1)], axis=1).astype(BF16)
    row = lambda v: v.reshape(1, -1).astype(F32)
    qkv_w = (w_uq_p, w_ukv_p, _pad_cols(row(g_q_lora[l]), 512), row(g_kv_lora[l]),
             _pad_cols(row(g_qk_q[l]), HEAD_PAD), _pad_cols(row(g_qk_k[l]), HEAD_PAD))

    hc = w_pw.shape[1] // GMLP_HEADS
    bsp = jnp.repeat(b_spatial[l].T.astype(F32), hc, axis=1)
    w_dw = jnp.pad(w_dw31[l].astype(F32), ((0, 32 - CONV_K), (0, 0)))
    w_dw = w_dw.reshape(32, -1, LANE).transpose(1, 0, 2)
    mix_w = (w_dw, row(b_dw31[l]),
             row(g_conv_ln[l]), row(b_conv_ln[l]), w_pw[l].astype(BF16), row(b_pw[l]),
             row(g_sgu_ln[l]), row(b_sgu_ln[l]), w_spatial[l].astype(BF16), bsp,
             jnp.pad(w_sc3[l].astype(F32), ((0, 8 - SC_K), (0, 0))))
    return w_in_p, qkv_w, mix_w, w_out[l].astype(BF16)


def _tile(limit, *sizes):
    tm = limit
    while any(s % tm for s in sizes):
        tm //= 2
    return tm


def kernel(x_prompt, x_sample, cache_mla_ckv, cache_mla_krope, c, c_ctx, w_mod, b_mod, g_norm1, g_norm2, w_in, g_q_lora, w_uq, g_kv_lora, w_ukv, g_qk_q, g_qk_k, w_dw31, b_dw31, g_conv_ln, b_conv_ln, w_pw, b_pw, g_sgu_ln, b_sgu_ln, w_spatial, b_spatial, w_sc3, w_out, router_w, router_b, w1, w3, w2):
    nseq, seq, d = x_prompt.shape
    nb, dec_seq, _ = x_sample.shape
    depth = w_mod.shape[0]
    past = cache_mla_ckv.shape[2]
    q_lora = g_q_lora.shape[1]
    kv_lora = g_kv_lora.shape[1]
    n_exp = router_w.shape[1]
    tp = nseq * seq
    ts = nb * dec_seq
    t = tp + ts
    assert tp % dec_seq == 0 and seq % CHUNK == 0 and dec_seq % GRID_W == 0
    assert nb + 1 <= 8 and kv_lora == MLA_NOPE and n_exp == N_GROUPS * GROUP_SIZE

    tm_in = _tile(512, tp, dec_seq)
    tm_qkv = _tile(512, tp, dec_seq)
    tm_row = _tile(256, seq, dec_seq)
    tq = _tile(1024, dec_seq)

    def mod_index(i, tm):
        npt = tp // tm
        return jnp.where(i < npt, 0, 1 + (i - npt) // (dec_seq // tm))

    cvec = jnp.concatenate([c_ctx[None, :], c, jnp.zeros((8 - 1 - nb, d), F32)], axis=0)
    mod = _modulation(cvec, w_mod, b_mod).reshape(depth, 8, 6, d)
    cos_t, sin_t = _rope_tables(dec_seq, tm_qkv)

    rw = router_w.T.reshape(N_GROUPS, GROUP_SIZE, d).transpose(1, 0, 2).reshape(n_exp, d)
    rw_hi = rw.astype(BF16)
    rw_lo = (rw - rw_hi.astype(F32)).astype(BF16)
    wr = jnp.pad(jnp.concatenate([rw_hi, rw_lo], axis=0), ((0, LANE - 2 * n_exp), (0, 0)))
    n_blk = (t + N_GROUPS * N_PAIRS * (MOE_BM - 1) + MOE_BM - 1) // MOE_BM

    xp = x_prompt.reshape(tp, d)
    xs = x_sample.reshape(ts, d)
    ckv_states = []
    krope_states = []
    for l in range(depth):
        w_in_p, qkv_w, mix_w, w_out_b = _layer_params(
            l, q_lora, kv_lora, w_in, g_q_lora, w_uq, g_kv_lora, w_ukv, g_qk_q, g_qk_k,
            w_dw31, b_dw31, g_conv_ln, b_conv_ln, w_pw, b_pw, g_sgu_ln, b_sgu_ln,
            w_spatial, b_spatial, w_sc3, w_out)
        mod_l = mod[l]

        if l == 0:
            proj, kv = _in_proj(xp, xs, mod_l, g_norm1[l].reshape(1, d), w_in_p, tm_in, mod_index)
        else:
            proj, kv, xp = _in_proj_fused(xn, yb, pos, mod[l - 1], mod_l, g_norm1[l].reshape(1, d), w_in_p,
                                          tm_in, mod_index)
            xs = None
        q, k, v, state = _qkv(proj, kv, cos_t, sin_t, *qkv_w, tp, dec_seq, q_lora, tm_qkv)
        kvc = jnp.concatenate([cache_mla_ckv[:, l], cache_mla_krope[:, l],
                               jnp.zeros((nb, past, KV_W - kv_lora - MLA_ROPE), F32)], axis=-1)
        kc, vc = _ctx_kv(kvc.reshape(nb * past, KV_W), qkv_w[1], qkv_w[5], _tile(512, past))
        o_as, w1b, w3b, w2b = _attention_latent(q, k, v, kc, vc, w1, w3, w2, l, tp, nb, dec_seq, past, tq)
        o_ap = _attention_context(q, k, v, nseq, seq)
        o_bcd = _mixers(proj, mix_w, tp, seq, dec_seq, tm_row)

        xn, hp, r = _out_proj(o_ap, o_as, o_bcd, xp, xs, mod_l, g_norm2[l].reshape(1, d), w_out_b, wr,
                              tm_qkv, mod_index)
        pos, gates, plan, fill = _route(r, router_b, n_blk, tm_qkv)
        xd = _dispatch(hp, gates, pos, fill, tm_qkv)
        yb = _moe(xd, w1b, w3b, w2b, plan)
        if l == depth - 1:
            xp, xs = _combine(xn, yb, pos, mod_l, tp, tm_qkv, mod_index)

        ckv_states.append(state[:, :kv_lora].reshape(nseq, seq, kv_lora))
        krope_states.append(state[:, kv_lora:kv_lora + MLA_ROPE].reshape(nseq, seq, MLA_ROPE))

    return (xp.reshape(nseq, seq, d), xs.reshape(nb, dec_seq, d),
            jnp.stack(ckv_states, axis=1), jnp.stack(krope_states, axis=1))
```

```python
import functools

import jax
import jax.numpy as jnp
from jax import lax
from jax.experimental import pallas as pl
from jax.experimental.pallas import tpu as pltpu

F32 = jnp.float32
BF16 = jnp.bfloat16
U32 = jnp.uint32

MLA_HEADS = 4
MLA_NOPE = 128
MLA_ROPE = 64
MLA_V = 128
MLA_QK = MLA_NOPE + MLA_ROPE
HEAD_PAD = 256
GRID_W = 64
ROPE_THETA = 10000.0
CONV_K = 31
SC_K = 3
CHUNK = 128
GMLP_HEADS = 4
N_GROUPS = 8
GROUP_SIZE = 4
N_PAIRS = 6
NORM_EPS = 1e-6

LANE = 128
HALO = 16
PROJ_W = 4352
CQ_COL = 3584
KV_COL = 4096
KV_W = 256
IN_TN = 1536
MOE_BM = 256
OUT_PARTS = 4
N_DMA_PRIORITIES = 2
ATTN_QSUB = 1024
ATTN_KCHUNK = 1024
LOG2_E = 1.4426950408889634
GATE_W = 128
X_ROWS = 9
Y_ROWS = 8
VMEM_CAP = 56 * 1024 * 1024


def _cparams(sem, vmem_mb):
    return pltpu.CompilerParams(dimension_semantics=sem,
                                vmem_limit_bytes=min(vmem_mb * 1024 * 1024, VMEM_CAP))


def _inv_rms(x, n):
    return lax.rsqrt(jnp.sum(x * x, axis=-1, keepdims=True) * (1.0 / n) + NORM_EPS)


def _layernorm(x, g, b):
    mu = jnp.mean(x, axis=-1, keepdims=True)
    xc = x - mu
    var = jnp.mean(xc * xc, axis=-1, keepdims=True)
    return xc * lax.rsqrt(var + NORM_EPS) * g + b


def _split_bf16(x):
    hi = x.astype(BF16)
    lo = (x - hi.astype(F32)).astype(BF16)
    return hi, lo


def _pack_rows(x):
    n = x.shape[1] // 2
    hi = pltpu.bitcast(x[:, :n].astype(BF16).astype(F32), U32)
    lo = pltpu.bitcast(x[:, n:].astype(BF16).astype(F32), U32)
    return (hi & jnp.uint32(0xFFFF0000)) | (lo >> 16)


def _unpack_rows(u):
    a = pltpu.bitcast(u & jnp.uint32(0xFFFF0000), F32)
    b = pltpu.bitcast(u << 16, F32)
    return a, b


def _mod_kernel(c_ref, w_ref, b_ref, o_ref):
    c = c_ref[...]
    a_hi, a_lo = _split_bf16(c * jax.nn.sigmoid(c))
    w_hi, w_lo = _split_bf16(w_ref[...])
    acc = jnp.dot(a_hi, w_hi, preferred_element_type=F32)
    acc += jnp.dot(a_lo, w_hi, preferred_element_type=F32)
    acc += jnp.dot(a_hi, w_lo, preferred_element_type=F32)
    o_ref[...] = acc + b_ref[...]


def _modulation(cvec, w_mod, b_mod):
    depth, d, n = w_mod.shape
    tn = 1024
    return pl.pallas_call(
        _mod_kernel,
        grid=(depth, n // tn),
        in_specs=[pl.BlockSpec((8, d), lambda l, j: (0, 0)),
                  pl.BlockSpec((None, d, tn), lambda l, j: (l, 0, j)),
                  pl.BlockSpec((None, 1, tn), lambda l, j: (l, 0, j))],
        out_specs=pl.BlockSpec((None, 8, tn), lambda l, j: (l, 0, j)),
        out_shape=jax.ShapeDtypeStruct((depth, 8, n), F32),
        compiler_params=_cparams(("arbitrary", "arbitrary"), 40),
        name="modulation",
    )(cvec, w_mod, b_mod.reshape(depth, 1, n))


def _norm_project(x, mod_ref, g_ref, w_ref, proj_ref, kv_ref):
    y = x * _inv_rms(x, x.shape[-1]) * g_ref[...]
    h = (y * (1.0 + mod_ref[1:2, :]) + mod_ref[0:1, :]).astype(BF16)
    for c0 in range(0, PROJ_W, IN_TN):
        c1 = min(c0 + IN_TN, PROJ_W)
        acc = jnp.dot(h, w_ref[:, c0:c1], preferred_element_type=F32)
        proj_ref[:, c0:c1] = acc.astype(BF16)
        if c0 <= KV_COL < c1:
            kv_ref[...] = acc[:, KV_COL - c0:KV_COL - c0 + KV_W]


def _in_proj_kernel(npt, xp_ref, xs_ref, mod_ref, g_ref, w_ref, proj_ref, kv_ref):
    x = jnp.where(pl.program_id(0) < npt, xp_ref[...], xs_ref[...])
    _norm_project(x, mod_ref, g_ref, w_ref, proj_ref, kv_ref)


def _gathered_rows(pos_ref, nxt_ref, yb_ref, buf, sem):
    i = pl.program_id(0)
    slot = i % 2
    tm = buf.shape[1] // Y_ROWS

    def gather(idx_ref, s):
        for r in range(tm):
            src = pl.multiple_of(idx_ref[0, 0, r] * Y_ROWS, Y_ROWS)
            pltpu.make_async_copy(yb_ref.at[pl.ds(src, Y_ROWS)], buf.at[s, pl.ds(r * Y_ROWS, Y_ROWS)],
                                  sem.at[s]).start(priority=r % N_DMA_PRIORITIES)

    @pl.when(i == 0)
    def _():
        gather(pos_ref, 0)

    for s in range(2):
        @pl.when(jnp.logical_and(i + 1 < pl.num_programs(0), slot == 1 - s))
        def _():
            gather(nxt_ref, s)

    pltpu.make_async_copy(yb_ref.at[pl.ds(0, tm * Y_ROWS)], buf.at[slot], sem.at[slot]).wait()
    halves = [_unpack_rows(buf[slot, pl.ds(j, tm, stride=Y_ROWS), :]) for j in range(Y_ROWS)]
    return jnp.concatenate([a for a, _ in halves] + [b for _, b in halves], axis=1)


def _in_proj_fused_kernel(pos_ref, nxt_ref, xn_ref, prev_mod_ref, mod_ref, g_ref, w_ref, yb_ref,
                          proj_ref, kv_ref, x_ref, buf, sem):
    x = xn_ref[...] + prev_mod_ref[5:6, :] * _gathered_rows(pos_ref, nxt_ref, yb_ref, buf, sem)
    x_ref[...] = x
    _norm_project(x, mod_ref, g_ref, w_ref, proj_ref, kv_ref)


def _in_proj_fused(xn, yb, pos, prev_mod, mod_l, g, w, tm, mod_index):
    t, d = xn.shape
    n = t // tm
    pos3 = pos.reshape(n, 1, tm)
    mod_spec = pl.BlockSpec((None, 6, d), lambda i: (mod_index(i, tm), 0, 0))
    return pl.pallas_call(
        _in_proj_fused_kernel,
        grid=(n,),
        in_specs=[pl.BlockSpec((1, 1, tm), lambda i: (i, 0, 0), memory_space=pltpu.SMEM),
                  pl.BlockSpec((1, 1, tm), lambda i: (jnp.minimum(i + 1, n - 1), 0, 0), memory_space=pltpu.SMEM),
                  pl.BlockSpec((tm, d), lambda i: (i, 0)),
                  mod_spec, mod_spec,
                  pl.BlockSpec((1, d), lambda i: (0, 0)),
                  pl.BlockSpec(w.shape, lambda i: (0, 0), pipeline_mode=pl.Buffered(1)),
                  pl.BlockSpec(memory_space=pl.ANY)],
        out_specs=[pl.BlockSpec((tm, PROJ_W), lambda i: (i, 0)),
                   pl.BlockSpec((tm, KV_W), lambda i: (i, 0)),
                   pl.BlockSpec((tm, d), lambda i: (i, 0))],
        out_shape=[jax.ShapeDtypeStruct((t, PROJ_W), BF16),
                   jax.ShapeDtypeStruct((t, KV_W), F32),
                   jax.ShapeDtypeStruct((t, d), F32)],
        scratch_shapes=[pltpu.VMEM((2, tm * Y_ROWS, LANE), U32), pltpu.SemaphoreType.DMA((2,))],
        compiler_params=_cparams(("arbitrary",), 56),
        name="in_proj_fused",
    )(pos3, pos3, xn, prev_mod, mod_l, g, w, yb)


def _in_proj(xp, xs, mod_l, g, w, tm, mod_index):
    tp, d = xp.shape
    t = tp + xs.shape[0]
    npt = tp // tm
    return pl.pallas_call(
        functools.partial(_in_proj_kernel, npt),
        grid=(t // tm,),
        in_specs=[pl.BlockSpec((tm, d), lambda i: (jnp.minimum(i, npt - 1), 0)),
                  pl.BlockSpec((tm, d), lambda i: (jnp.maximum(i - npt, 0), 0)),
                  pl.BlockSpec((None, 6, d), lambda i: (mod_index(i, tm), 0, 0)),
                  pl.BlockSpec((1, d), lambda i: (0, 0)),
                  pl.BlockSpec(w.shape, lambda i: (0, 0), pipeline_mode=pl.Buffered(1))],
        out_specs=[pl.BlockSpec((tm, PROJ_W), lambda i: (i, 0)),
                   pl.BlockSpec((tm, KV_W), lambda i: (i, 0))],
        out_shape=[jax.ShapeDtypeStruct((t, PROJ_W), BF16),
                   jax.ShapeDtypeStruct((t, KV_W), F32)],
        compiler_params=_cparams(("arbitrary",), 56),
        name="in_proj",
    )(xp, xs, mod_l, g, w)


def _rope_tile(t, cos, sin):
    lane = lax.broadcasted_iota(jnp.int32, t.shape, 1)
    first = (lane & 16) == 0
    swapped = jnp.where(first, pltpu.roll(t, LANE - 16, 1), pltpu.roll(t, 16, 1))
    return t * cos + swapped * sin


def _row_ss(x):
    sq = (x * x).astype(BF16)
    return jnp.dot(sq, jnp.ones((x.shape[1], LANE), BF16), preferred_element_type=F32)


def _keys_values(ckv_n, krope, w_ukv_ref, gk_ref, cos, sin, k_ref, v_ref):
    kvf = jnp.dot(ckv_n.astype(BF16), w_ukv_ref[...], preferred_element_type=F32)
    kr_ss = _row_ss(krope)
    gk = gk_ref[...]
    tail = krope * gk[:, MLA_NOPE:]
    if cos is not None:
        tail = _rope_tile(tail, cos, sin)
    for h in range(MLA_HEADS):
        kn = kvf[:, h * MLA_NOPE:(h + 1) * MLA_NOPE]
        r = lax.rsqrt((_row_ss(kn) + kr_ss) * (1.0 / MLA_QK) + NORM_EPS)
        k_ref[:, h * HEAD_PAD:h * HEAD_PAD + MLA_NOPE] = (kn * r * gk[:, :MLA_NOPE]).astype(BF16)
        k_ref[:, h * HEAD_PAD + MLA_NOPE:(h + 1) * HEAD_PAD] = (tail * r).astype(BF16)
    v_ref[...] = kvf[:, MLA_HEADS * MLA_NOPE:].astype(BF16)


def _qkv_kernel(npt, q_lora, cq_ref, kv_ref, cos_ref, sin_ref, w_uq_ref, w_ukv_ref, gql_ref, gkv_ref,
                gq_ref, gk_ref, q_ref, k_ref, v_ref, st_ref):
    i = pl.program_id(0)
    cos = cos_ref[...]
    sin = sin_ref[...]
    cq = cq_ref[...].astype(F32)
    r_cq = lax.rsqrt(_row_ss(cq) * (1.0 / q_lora) + NORM_EPS)
    cqn = cq * jnp.concatenate([r_cq] * (cq.shape[1] // LANE), axis=1) * gql_ref[...]
    qf = jnp.dot(cqn.astype(BF16), w_uq_ref[...], preferred_element_type=F32)
    gq = gq_ref[...]
    scale = MLA_QK ** -0.5 * LOG2_E
    for h in range(MLA_HEADS):
        qh = qf[:, h * HEAD_PAD:(h + 1) * HEAD_PAD]
        r = lax.rsqrt(_row_ss(qh) * (1.0 / MLA_QK) + NORM_EPS) * scale
        head = qh[:, :MLA_NOPE] * r * gq[:, :MLA_NOPE]
        tail = qh[:, MLA_NOPE:] * r * gq[:, MLA_NOPE:]
        q_ref[:, h * HEAD_PAD:h * HEAD_PAD + MLA_NOPE] = head.astype(BF16)
        q_ref[:, h * HEAD_PAD + MLA_NOPE:(h + 1) * HEAD_PAD] = _rope_tile(tail, cos, sin).astype(BF16)

    kv = kv_ref[...]
    ckv = kv[:, :MLA_NOPE]
    krope = kv[:, MLA_NOPE:]
    ckv_n = ckv * _inv_rms(ckv, ckv.shape[-1]) * gkv_ref[...]
    _keys_values(ckv_n, krope, w_ukv_ref, gk_ref, cos, sin, k_ref, v_ref)

    @pl.when(i < npt)
    def _():
        st_ref[:, :MLA_NOPE] = ckv_n
        st_ref[:, MLA_NOPE:] = krope


def _qkv(proj, kv, cos_t, sin_t, w_uq, w_ukv, gql, gkv, gq, gk, tp, dec_seq, q_lora, tm):
    t = proj.shape[0]
    npt = tp // tm
    nseq = dec_seq // tm
    cq_blk = CQ_COL // 512

    def tab(i):
        return (jnp.where(i < npt, nseq, (i - npt) % nseq), 0)

    const = lambda i: (0, 0)
    return pl.pallas_call(
        functools.partial(_qkv_kernel, npt, q_lora),
        grid=(t // tm,),
        in_specs=[pl.BlockSpec((tm, 512), lambda i: (i, cq_blk)),
                  pl.BlockSpec((tm, KV_W), lambda i: (i, 0)),
                  pl.BlockSpec((tm, LANE), tab),
                  pl.BlockSpec((tm, LANE), tab),
                  pl.BlockSpec(w_uq.shape, const),
                  pl.BlockSpec(w_ukv.shape, const),
                  pl.BlockSpec(gql.shape, const),
                  pl.BlockSpec(gkv.shape, const),
                  pl.BlockSpec(gq.shape, const),
                  pl.BlockSpec(gk.shape, const)],
        out_specs=[pl.BlockSpec((tm, MLA_HEADS * HEAD_PAD), lambda i: (i, 0)),
                   pl.BlockSpec((tm, MLA_HEADS * HEAD_PAD), lambda i: (i, 0)),
                   pl.BlockSpec((tm, MLA_HEADS * MLA_V), lambda i: (i, 0)),
                   pl.BlockSpec((tm, KV_W), lambda i: (jnp.minimum(i, npt - 1), 0))],
        out_shape=[jax.ShapeDtypeStruct((t, MLA_HEADS * HEAD_PAD), BF16),
                   jax.ShapeDtypeStruct((t, MLA_HEADS * HEAD_PAD), BF16),
                   jax.ShapeDtypeStruct((t, MLA_HEADS * MLA_V), BF16),
                   jax.ShapeDtypeStruct((tp, KV_W), F32)],
        compiler_params=_cparams(("arbitrary",), 40),
        name="qkv",
    )(proj, kv, cos_t, sin_t, w_uq, w_ukv, gql, gkv, gq, gk)


def _ctx_kv_kernel(kv_ref, w_ukv_ref, gk_ref, k_ref, v_ref):
    kv = kv_ref[...]
    _keys_values(kv[:, :MLA_NOPE], kv[:, MLA_NOPE:], w_ukv_ref, gk_ref, None, None, k_ref, v_ref)


def _ctx_kv(kvc, w_ukv, gk, tm):
    r = kvc.shape[0]
    const = lambda i: (0, 0)
    return pl.pallas_call(
        _ctx_kv_kernel,
        grid=(r // tm,),
        in_specs=[pl.BlockSpec((tm, KV_W), lambda i: (i, 0)),
                  pl.BlockSpec(w_ukv.shape, const),
                  pl.BlockSpec(gk.shape, const)],
        out_specs=[pl.BlockSpec((tm, MLA_HEADS * HEAD_PAD), lambda i: (i, 0)),
                   pl.BlockSpec((tm, MLA_HEADS * MLA_V), lambda i: (i, 0))],
        out_shape=[jax.ShapeDtypeStruct((r, MLA_HEADS * HEAD_PAD), BF16),
                   jax.ShapeDtypeStruct((r, MLA_HEADS * MLA_V), BF16)],
        compiler_params=_cparams(("arbitrary",), 32),
        name="ctx_kv",
    )(kvc, w_ukv, gk)


_NT = (((1,), (1,)), ((), ()))


def _attn_kernel(has_ctx, *refs):
    if has_ctx:
        q_ref, kc_ref, vc_ref, k_ref, v_ref, w1_ref, w3_ref, w2_ref, o_ref, w1o_ref, w3o_ref, w2o_ref = refs
        w1o_ref[...] = w1_ref[...].astype(BF16)
        w3o_ref[...] = w3_ref[...].astype(BF16)
        w2o_ref[...] = w2_ref[...].astype(BF16)
    else:
        q_ref, k_ref, v_ref, o_ref = refs
    n_own = k_ref.shape[0]
    step = min(n_own, ATTN_KCHUNK)
    chunks = [(k_ref, v_ref, c * step, step) for c in range(n_own // step)]
    if has_ctx:
        chunks = chunks + [(kc_ref, vc_ref, 0, kc_ref.shape[0])]
    for h in range(q_ref.shape[1] // HEAD_PAD):
        qk = slice(h * HEAD_PAD, (h + 1) * HEAD_PAD)
        hv = slice(h * MLA_V, (h + 1) * MLA_V)
        tq = q_ref.shape[0]
        sub = min(tq, ATTN_QSUB)
        state = [None] * (tq // sub)
        for kr, vr, off, n in chunks:
            for t in range(tq // sub):
                rows = slice(t * sub, (t + 1) * sub)
                s = lax.dot_general(q_ref[rows, qk], kr[off:off + n, qk], _NT, preferred_element_type=F32)
                mc = jnp.max(s, axis=-1, keepdims=True)
                m_new = mc if state[t] is None else jnp.maximum(state[t][0], mc)
                p = jnp.exp2(s - m_new).astype(BF16)
                v1 = jnp.concatenate([vr[off:off + n, hv], jnp.ones((n, MLA_V), BF16)], axis=1)
                pv = jnp.dot(p, v1, preferred_element_type=F32)
                if state[t] is None:
                    state[t] = (m_new, pv)
                else:
                    m, acc = state[t]
                    state[t] = (m_new, jnp.exp2(m - m_new) * acc + pv)
        for t in range(tq // sub):
            _, acc = state[t]
            o_ref[t * sub:(t + 1) * sub, hv] = (acc[:, :MLA_V] / acc[:, MLA_V:]).astype(BF16)


def _attention_latent(q, k, v, kc, vc, w1, w3, w2, layer, tp, nb, dec_seq, past, tq):
    t = q.shape[0]
    nq = dec_seq // tq
    row0 = tp // tq
    kblk0 = tp // dec_seq
    steps = nb * MLA_HEADS * nq
    depth, n_exp, d, de = w1.shape
    flat = [w1.reshape(depth, n_exp * d, de), w3.reshape(depth, n_exp * d, de), w2.reshape(depth, n_exp * de, d)]
    assert all(w.shape[1] % (8 * steps) == 0 for w in flat)
    step = lambda b, h, i: (b * MLA_HEADS + h) * nq + i
    w_in = [pl.BlockSpec((None, w.shape[1] // steps, w.shape[2]), lambda b, h, i: (layer, step(b, h, i), 0))
            for w in flat]
    w_out = [pl.BlockSpec((w.shape[1] // steps, w.shape[2]), lambda b, h, i: (step(b, h, i), 0)) for w in flat]
    o, w1b, w3b, w2b = pl.pallas_call(
        functools.partial(_attn_kernel, True),
        grid=(nb, MLA_HEADS, nq),
        in_specs=[pl.BlockSpec((tq, HEAD_PAD), lambda b, h, i: (row0 + b * nq + i, h)),
                  pl.BlockSpec((past, HEAD_PAD), lambda b, h, i: (b, h)),
                  pl.BlockSpec((past, MLA_V), lambda b, h, i: (b, h)),
                  pl.BlockSpec((dec_seq, HEAD_PAD), lambda b, h, i: (kblk0 + b, h)),
                  pl.BlockSpec((dec_seq, MLA_V), lambda b, h, i: (kblk0 + b, h))] + w_in,
        out_specs=[pl.BlockSpec((tq, MLA_V), lambda b, h, i: (b * nq + i, h))] + w_out,
        out_shape=[jax.ShapeDtypeStruct((t - tp, MLA_HEADS * MLA_V), BF16)]
                  + [jax.ShapeDtypeStruct(w.shape[1:], BF16) for w in flat],
        compiler_params=_cparams(("arbitrary", "arbitrary", "arbitrary"), 56),
        name="attn_latent",
    )(q, kc, vc, k, v, *flat)
    return o, w1b.reshape(n_exp, d, de), w3b.reshape(n_exp, d, de), w2b.reshape(n_exp, de, d)


def _attention_context(q, k, v, nseq, seq):
    return pl.pallas_call(
        functools.partial(_attn_kernel, False),
        grid=(nseq,),
        in_specs=[pl.BlockSpec((seq, MLA_HEADS * HEAD_PAD), lambda b: (b, 0)),
                  pl.BlockSpec((seq, MLA_HEADS * HEAD_PAD), lambda b: (b, 0)),
                  pl.BlockSpec((seq, MLA_HEADS * MLA_V), lambda b: (b, 0))],
        out_specs=pl.BlockSpec((seq, MLA_HEADS * MLA_V), lambda b: (b, 0)),
        out_shape=jax.ShapeDtypeStruct((nseq * seq, MLA_HEADS * MLA_V), BF16),
        compiler_params=_cparams(("arbitrary",), 32),
        name="attn_context",
    )(q, k, v)


def _mixers_kernel(npt, p_tiles, s_tiles,
                   cv_ref, cvp_ref, cvn_ref, gm_ref, sc_ref, scp_ref, scn_ref, bg_ref,
                   wdw_ref, bdw_ref, gcl_ref, bcl_ref, wpw_ref, bpw_ref,
                   gsl_ref, bsl_ref, wsp_ref, bsp_ref, wsc_ref,
                   o_ref, zs, ys, zsh, cb):
    i = pl.program_id(0)
    tm = cv_ref.shape[0]
    cw = cv_ref.shape[1] // 2
    pos = jnp.where(i < npt, i % p_tiles, (i - npt) % s_tiles)
    n_tiles = jnp.where(i < npt, p_tiles, s_tiles)
    keep_prev = (pos > 0).astype(F32)
    keep_next = (pos < n_tiles - 1).astype(F32)

    def glu(ref):
        a = ref[:, :cw].astype(F32)
        g = ref[:, cw:].astype(F32)
        return a * jax.nn.sigmoid(g)

    def prod(ref):
        return ref[:, :cw].astype(F32) * ref[:, cw:].astype(F32)

    ys[0:HALO, :] = prod(scp_ref) * keep_prev
    ys[HALO:HALO + tm, :] = prod(sc_ref)
    ys[HALO + tm:, :] = prod(scn_ref) * keep_next
    acc = jnp.zeros((tm, cw), F32)
    for tap in range(SC_K):
        acc = acc + ys[pl.ds(HALO - SC_K // 2 + tap, tm), :] * wsc_ref[tap:tap + 1, :]
    o_ref[:, 2 * cw:3 * cw] = (bg_ref[...].astype(F32) * acc).astype(BF16)

    n_lane = cw // LANE
    z_prev, z_main, z_next = glu(cvp_ref) * keep_prev, glu(cv_ref), glu(cvn_ref) * keep_next
    for c in range(n_lane):
        cs = slice(c * LANE, (c + 1) * LANE)
        zs[c, 0:HALO, :] = z_prev[:, cs]
        zs[c, HALO:HALO + tm, :] = z_main[:, cs]
        zs[c, HALO + tm:, :] = z_next[:, cs]
        for r in range(1, 8):
            zsh[r - 1, c] = zs[c, pl.ds(r, tm + 24), :]
    pad = CONV_K // 2
    half = tm // 2

    def conv_chunk(idx, carry):
        c = idx // 2
        base = pl.multiple_of((idx % 2) * half, half)
        acc = jnp.zeros((half, LANE), F32)
        for r in range(8):
            for a in range(4):
                tap = 8 * a + r - (HALO - pad)
                if 0 <= tap < CONV_K:
                    rows = pl.ds(base + 8 * a, half)
                    src = zs[c, rows, :] if r == 0 else zsh[r - 1, c, rows, :]
                    acc = acc + src * wdw_ref[c, tap:tap + 1, :]
        cb[c, pl.ds(base, half), :] = acc
        return carry

    lax.fori_loop(0, 2 * n_lane, conv_chunk, 0)
    z = jnp.concatenate([cb[c] for c in range(n_lane)], axis=1) + bdw_ref[...]
    z = _layernorm(z, gcl_ref[...], bcl_ref[...])
    z = z * jax.nn.sigmoid(z)
    o_b = jnp.dot(z.astype(BF16), wpw_ref[...], preferred_element_type=F32) + bpw_ref[...]
    o_ref[:, 0:cw] = o_b.astype(BF16)

    u = gm_ref[:, :cw].astype(F32)
    vg = _layernorm(gm_ref[:, cw:].astype(F32), gsl_ref[...], bsl_ref[...]).astype(BF16)
    hc = cw // GMLP_HEADS
    for n in range(tm // CHUNK):
        rows = slice(n * CHUNK, (n + 1) * CHUNK)
        for h in range(GMLP_HEADS):
            cols = slice(h * hc, (h + 1) * hc)
            mixed = jnp.dot(wsp_ref[h], vg[rows, cols], preferred_element_type=F32) + bsp_ref[:, cols]
            o_ref[rows, cw + h * hc:cw + (h + 1) * hc] = (u[rows, cols] * mixed).astype(BF16)


def _mixers(proj, weights, tp, seq, dec_seq, tm):
    t = proj.shape[0]
    cw = 512
    npt = tp // tm
    hb = tm // HALO
    last = t // HALO - 1
    const2 = lambda i: (0, 0)

    def main(col):
        return pl.BlockSpec((tm, 2 * cw), lambda i: (i, col))

    def prev(col):
        return pl.BlockSpec((HALO, 2 * cw), lambda i: (jnp.maximum(i * hb - 1, 0), col))

    def nxt(col):
        return pl.BlockSpec((HALO, 2 * cw), lambda i: (jnp.minimum((i + 1) * hb, last), col))

    w_specs = [pl.BlockSpec(w.shape, const2 if w.ndim == 2 else (lambda i: (0, 0, 0))) for w in weights]
    return pl.pallas_call(
        functools.partial(_mixers_kernel, npt, seq // tm, dec_seq // tm),
        grid=(t // tm,),
        in_specs=[main(0), prev(0), nxt(0), main(1), main(2), prev(2), nxt(2),
                  pl.BlockSpec((tm, cw), lambda i: (i, 6))] + w_specs,
        out_specs=pl.BlockSpec((tm, 3 * cw), lambda i: (i, 0)),
        out_shape=jax.ShapeDtypeStruct((t, 3 * cw), BF16),
        scratch_shapes=[pltpu.VMEM((cw // LANE, tm + 2 * HALO, LANE), F32), pltpu.VMEM((tm + 2 * HALO, cw), F32),
                        pltpu.VMEM((7, cw // LANE, tm + 24, LANE), F32), pltpu.VMEM((cw // LANE, tm, LANE), F32)],
        compiler_params=_cparams(("arbitrary",), 40),
        name="mixers",
    )(proj, proj, proj, proj, proj, proj, proj, proj, *weights)


def _out_proj_kernel(npt, n, oap_ref, oas_ref, ob_ref, xp_ref, xs_ref, mod_ref, g_ref, wo_ref, wr_ref,
                     xn_ref, hp_ref, r_ref, mix_a, mix_b):
    i = pl.program_id(0)
    ka = oap_ref.shape[1]
    mm_tile = jnp.minimum(i, n - 1)
    ep_tile = jnp.maximum(i - 1, 0)

    @pl.when(i == 0)
    def _():
        mix_b[...] = jnp.zeros_like(mix_b)

    def step(mix_w, mix_r):
        oa = jnp.where(mm_tile < npt, oap_ref[...], oas_ref[...])
        lhs = jnp.concatenate([oa, ob_ref[...]], axis=1)
        tm, d = mix_w.shape
        parts = OUT_PARTS
        for k in range(parts):
            cols = slice(k * d // parts, (k + 1) * d // parts)
            rows = slice(k * tm // parts, (k + 1) * tm // parts)
            mix_w[:, cols] = jnp.dot(lhs, wo_ref[:, cols], preferred_element_type=F32)
            x = jnp.where(ep_tile < npt, xp_ref[rows, :], xs_ref[rows, :])
            xn = x + mod_ref[2:3, :] * mix_r[rows, :]
            xn_ref[rows, :] = xn
            h = xn * _inv_rms(xn, xn.shape[-1]) * g_ref[...]
            h = h * (1.0 + mod_ref[4:5, :]) + mod_ref[3:4, :]
            hp_ref[rows, :] = _pack_rows(h)
            h_hi, h_lo = _split_bf16(h)
            r_ref[:, rows] = (lax.dot_general(wr_ref[...], h_hi, _NT, preferred_element_type=F32)
                              + lax.dot_general(wr_ref[...], h_lo, _NT, preferred_element_type=F32))

    @pl.when(i % 2 == 0)
    def _():
        step(mix_a, mix_b)

    @pl.when(i % 2 == 1)
    def _():
        step(mix_b, mix_a)


def _out_proj(oap, oas, ob, xp, xs, mod_l, g, wo, wr, tm, mod_index):
    t = ob.shape[0]
    d = xp.shape[1]
    npt = oap.shape[0] // tm
    n = t // tm
    const = lambda i: (0, 0)
    mm = lambda i: jnp.minimum(i, n - 1)
    ep = lambda i: jnp.maximum(i - 1, 0)

    def pair(w, tile, stacked=False):
        second = (lambda i: (jnp.maximum(tile(i), npt), 0)) if stacked else (
            lambda i: (jnp.maximum(tile(i) - npt, 0), 0))
        return [pl.BlockSpec((tm, w), lambda i: (jnp.minimum(tile(i), npt - 1), 0)), pl.BlockSpec((tm, w), second)]

    x_specs = pair(d, ep, stacked=xs is None)
    if xs is None:
        xs = xp
    return pl.pallas_call(
        functools.partial(_out_proj_kernel, npt, n),
        grid=(n + 1,),
        in_specs=pair(oap.shape[1], mm) + [pl.BlockSpec((tm, ob.shape[1]), lambda i: (mm(i), 0))] + x_specs + [
                  pl.BlockSpec((None, 6, d), lambda i: (mod_index(ep(i), tm), 0, 0)),
                  pl.BlockSpec((1, d), const),
                  pl.BlockSpec(wo.shape, const, pipeline_mode=pl.Buffered(1)),
                  pl.BlockSpec(wr.shape, const, pipeline_mode=pl.Buffered(1))],
        out_specs=[pl.BlockSpec((tm, d), lambda i: (ep(i), 0)),
                   pl.BlockSpec((tm, d // 2), lambda i: (ep(i), 0)),
                   pl.BlockSpec((LANE, tm), lambda i: (0, ep(i)))],
        out_shape=[jax.ShapeDtypeStruct((t, d), F32),
                   jax.ShapeDtypeStruct((t, d // 2), U32),
                   jax.ShapeDtypeStruct((LANE, t), F32)],
        scratch_shapes=[pltpu.VMEM((tm, d), F32), pltpu.VMEM((tm, d), F32)],
        compiler_params=_cparams(("arbitrary",), 56),
        name="out_proj",
    )(oap, oas, ob, xp, xs, mod_l, g, wo, wr)


N_BUCKET_PAD = 64
PAIR_FIRST = (0, 0, 0, 1, 1, 3)
PAIR_SECOND = (1, 2, 3, 3, 2, 2)


def _route_kernel(rt_ref, b_ref, tri_ref, bucket_ref, rank_ref, gate_ref, cnt_ref, run):
    i = pl.program_id(0)
    tm = rt_ref.shape[1]
    ne = N_GROUPS * GROUP_SIZE

    @pl.when(i == 0)
    def _():
        run[...] = jnp.zeros_like(run)

    sc = jax.nn.sigmoid(rt_ref[0:ne, :] + rt_ref[ne:2 * ne, :])
    sel = sc + b_ref[...]
    s = [sel[r * N_GROUPS:(r + 1) * N_GROUPS, :] for r in range(GROUP_SIZE)]
    c = [sc[r * N_GROUPS:(r + 1) * N_GROUPS, :] for r in range(GROUP_SIZE)]
    hi01, lo01 = jnp.maximum(s[0], s[1]), jnp.minimum(s[0], s[1])
    hi23, lo23 = jnp.maximum(s[2], s[3]), jnp.minimum(s[2], s[3])
    g_score = jnp.maximum(hi01, hi23) + jnp.maximum(jnp.minimum(hi01, hi23), jnp.maximum(lo01, lo23))
    grp = lax.broadcasted_iota(jnp.int32, g_score.shape, 0)
    g_best = jnp.min(jnp.where(g_score == jnp.max(g_score, axis=0, keepdims=True), grp, N_GROUPS),
                     axis=0, keepdims=True)
    own = grp == g_best
    v = [jnp.sum(jnp.where(own, s[r], 0.0), axis=0, keepdims=True) for r in range(GROUP_SIZE)]
    w = [jnp.sum(jnp.where(own, c[r], 0.0), axis=0, keepdims=True) for r in range(GROUP_SIZE)]

    def first_max(vals):
        best, idx, gate = vals[0], jnp.zeros_like(g_best), w[0]
        for r in range(1, GROUP_SIZE):
            upd = vals[r] > best
            best = jnp.where(upd, vals[r], best)
            idx = jnp.where(upd, r, idx)
            gate = jnp.where(upd, w[r], gate)
        return idx, gate

    r1, w1 = first_max(v)
    r2, w2 = first_max([jnp.where(r1 == r, -jnp.inf, v[r]) for r in range(GROUP_SIZE)])
    wsum = w1 + w2
    swap = r1 > r2
    r_lo = jnp.where(swap, r2, r1)
    r_hi = jnp.where(swap, r1, r2)
    g_lo = jnp.where(swap, w2, w1) / wsum
    g_hi = jnp.where(swap, w1, w2) / wsum
    lex = r_lo * 3 - ((r_lo * (r_lo - 1)) >> 1) + (r_hi - r_lo - 1)
    pair = jnp.where(lex == 3, 4, jnp.where(lex == 4, 3, lex))
    hi_first = pair == N_PAIRS - 1
    gate_ref[0:1, :] = jnp.where(hi_first, g_hi, g_lo)
    gate_ref[1:2, :] = jnp.where(hi_first, g_lo, g_hi)
    bucket = g_best * N_PAIRS + pair
    bucket_ref[...] = bucket

    onehot = (lax.broadcasted_iota(jnp.int32, (N_BUCKET_PAD, tm), 0) == bucket).astype(F32)
    before = jnp.dot(onehot.astype(BF16), tri_ref[...], preferred_element_type=F32) + run[...]
    rank_ref[...] = jnp.sum(onehot * before, axis=0, keepdims=True).astype(jnp.int32)
    run[...] += jnp.sum(onehot, axis=1, keepdims=True)

    @pl.when(i == pl.num_programs(0) - 1)
    def _():
        cnt_ref[...] = run[...]


def _route(rt, router_b, n_blk, tm):
    t = rt.shape[1]
    ne = router_b.shape[0]
    b_col = router_b.astype(F32).reshape(N_GROUPS, GROUP_SIZE).T.reshape(ne, 1)
    tri = (jnp.arange(tm)[:, None] < jnp.arange(tm)[None, :]).astype(BF16)
    bucket, rank, gate, cnt = pl.pallas_call(
        _route_kernel,
        grid=(t // tm,),
        in_specs=[pl.BlockSpec((LANE, tm), lambda i: (0, i)),
                  pl.BlockSpec((ne, 1), lambda i: (0, 0)),
                  pl.BlockSpec((tm, tm), lambda i: (0, 0))],
        out_specs=[pl.BlockSpec((1, tm), lambda i: (0, i)),
                   pl.BlockSpec((1, tm), lambda i: (0, i)),
                   pl.BlockSpec((2, tm), lambda i: (0, i)),
                   pl.BlockSpec((N_BUCKET_PAD, 1), lambda i: (0, 0))],
        out_shape=[jax.ShapeDtypeStruct((1, t), jnp.int32),
                   jax.ShapeDtypeStruct((1, t), jnp.int32),
                   jax.ShapeDtypeStruct((2, t), F32),
                   jax.ShapeDtypeStruct((N_BUCKET_PAD, 1), F32)],
        scratch_shapes=[pltpu.VMEM((N_BUCKET_PAD, 1), F32)],
        compiler_params=_cparams(("arbitrary",), 32),
        name="route",
    )(rt, b_col, tri)

    n_bucket = N_GROUPS * N_PAIRS
    counts = cnt[:n_bucket, 0].astype(jnp.int32)
    padded = (counts + MOE_BM - 1) // MOE_BM * MOE_BM
    pad_end = jnp.cumsum(padded)
    pad_start = pad_end - padded
    bucket = bucket[0]
    sel = bucket[:, None] == jnp.arange(n_bucket, dtype=jnp.int32)[None, :]
    pos = jnp.sum(jnp.where(sel, pad_start[None, :], 0), axis=1).astype(jnp.int32) + rank[0]
    n_used = (pad_end[-1] // MOE_BM).astype(jnp.int32)
    blk = jnp.minimum(jnp.arange(n_blk, dtype=jnp.int32), n_used - 1)
    blk_bucket = jnp.sum((pad_end[None, :] <= (blk * MOE_BM)[:, None]).astype(jnp.int32), axis=1)
    blk_bucket = jnp.minimum(blk_bucket, n_bucket - 1)
    pair_lo = jnp.array(PAIR_FIRST, jnp.int32)
    pair_hi = jnp.array(PAIR_SECOND, jnp.int32)
    blk_pair = blk_bucket % N_PAIRS
    first = (blk_bucket // N_PAIRS) * GROUP_SIZE
    is_pair = blk_pair[:, None] == jnp.arange(N_PAIRS, dtype=jnp.int32)[None, :]
    blk_a = first + jnp.sum(jnp.where(is_pair, pair_lo[None, :], 0), axis=1)
    blk_b = first + jnp.sum(jnp.where(is_pair, pair_hi[None, :], 0), axis=1)
    is_bucket = blk_bucket[:, None] == jnp.arange(n_bucket, dtype=jnp.int32)[None, :]
    per_bucket = lambda v: jnp.sum(jnp.where(is_bucket, v[None, :], 0), axis=1)
    first_blk = per_bucket(pad_start // MOE_BM)
    n_in_bucket = jnp.maximum(per_bucket(padded // MOE_BM), 1)
    src = first_blk + (blk - first_blk + n_in_bucket - 1) % n_in_bucket
    steps = jnp.arange(n_blk, dtype=jnp.int32)
    used = steps < n_used
    blk_out = jnp.where(used, src, steps).astype(jnp.int32)
    blk_rows = jnp.where(used, jnp.clip(per_bucket(pad_start + counts) - src * MOE_BM, 0, MOE_BM), 0)
    gates = jnp.pad(gate.T, ((0, 0), (0, GATE_W - 2)))
    plan = (src.astype(jnp.int32), blk_a.astype(jnp.int32), blk_b.astype(jnp.int32), blk_rows.astype(jnp.int32),
            blk_out)
    rows_in_blk = jnp.clip(per_bucket(pad_start + counts) - blk * MOE_BM, 0, MOE_BM)
    fill = jnp.logical_or(jnp.logical_not(used), rows_in_blk < MOE_BM).astype(jnp.int32)
    return pos, gates, plan, fill


def _dispatch_kernel(fill_ref, pos_ref, hp_ref, gt_ref, xs_ref, buf, zeros, sem, zsem):
    i = pl.program_id(0)
    slot = i % 2
    tm = hp_ref.shape[0]
    n_data = hp_ref.shape[1] // LANE
    blk_rows = zeros.shape[0]

    def wait_all(s):
        pltpu.make_async_copy(buf.at[s], xs_ref.at[pl.ds(0, tm * X_ROWS)], sem.at[s]).wait()

    @pl.when(i == 0)
    def _():
        zeros[...] = jnp.zeros_like(zeros)

        def zero_block(b):
            return pltpu.make_async_copy(zeros, xs_ref.at[pl.ds(pl.multiple_of(b * blk_rows, blk_rows), blk_rows)],
                                         zsem)

        def start(b, carry):
            @pl.when(fill_ref[b] != 0)
            def _():
                zero_block(b).start()
            return carry

        def wait(b, carry):
            @pl.when(fill_ref[b] != 0)
            def _():
                zero_block(b).wait()
            return carry

        lax.fori_loop(0, fill_ref.shape[0], start, 0)
        lax.fori_loop(0, fill_ref.shape[0], wait, 0)

    for s in range(2):
        @pl.when(slot == s)
        def _():
            for j in range(n_data):
                buf[s, pl.ds(j, tm, stride=X_ROWS), :] = hp_ref[:, j * LANE:(j + 1) * LANE]
            buf[s, pl.ds(n_data, tm, stride=X_ROWS), :] = pltpu.bitcast(gt_ref[...], U32)
            for r in range(tm):
                pltpu.make_async_copy(buf.at[s, pl.ds(r * X_ROWS, X_ROWS)],
                                      xs_ref.at[pl.ds(pos_ref[0, 0, r] * X_ROWS, X_ROWS)],
                                      sem.at[s]).start(priority=r % N_DMA_PRIORITIES)

    @pl.when(i > 0)
    def _():
        wait_all(1 - slot)

    @pl.when(i == pl.num_programs(0) - 1)
    def _():
        wait_all(slot)


def _dispatch(hp, gates, pos, fill, tm):
    t, w = hp.shape
    assert w == (X_ROWS - 1) * LANE and GATE_W == LANE
    n_blk = fill.shape[0]
    grid_spec = pltpu.PrefetchScalarGridSpec(
        num_scalar_prefetch=1,
        grid=(t // tm,),
        in_specs=[pl.BlockSpec((1, 1, tm), lambda i, fill: (i, 0, 0), memory_space=pltpu.SMEM),
                  pl.BlockSpec((tm, w), lambda i, fill: (i, 0)),
                  pl.BlockSpec((tm, GATE_W), lambda i, fill: (i, 0))],
        out_specs=pl.BlockSpec(memory_space=pl.ANY),
        scratch_shapes=[pltpu.VMEM((2, tm * X_ROWS, LANE), U32), pltpu.VMEM((MOE_BM * X_ROWS, LANE), U32),
                        pltpu.SemaphoreType.DMA((2,)), pltpu.SemaphoreType.DMA(())],
    )
    return pl.pallas_call(
        _dispatch_kernel,
        grid_spec=grid_spec,
        out_shape=jax.ShapeDtypeStruct((n_blk * MOE_BM * X_ROWS, LANE), U32),
        compiler_params=_cparams(("arbitrary",), 48),
        name="dispatch",
    )(fill, pos.reshape(t // tm, 1, tm), hp, gates)


def _moe_kernel(src_ref, ea_ref, eb_ref, nv_ref, dst_ref, x_ref, w1a_ref, w3a_ref, w2a_ref, w1b_ref, w3b_ref,
                w2b_ref, y_ref):
    del src_ref, ea_ref, eb_ref, dst_ref
    nv = nv_ref[pl.program_id(0)]
    bm = x_ref.shape[0] // X_ROWS

    def experts(m):
        halves = [_unpack_rows(x_ref[pl.ds(j, m, stride=X_ROWS), :]) for j in range(X_ROWS - 1)]
        x = jnp.concatenate([a.astype(BF16) for a, _ in halves] + [b.astype(BF16) for _, b in halves], axis=1)
        gates = pltpu.bitcast(x_ref[pl.ds(X_ROWS - 1, m, stride=X_ROWS), :], F32)

        def hidden(w1_ref, w3_ref, g):
            a = jnp.dot(x, w1_ref[...], preferred_element_type=F32)
            b = jnp.dot(x, w3_ref[...], preferred_element_type=F32)
            return (a * jax.nn.sigmoid(a) * b * g).astype(BF16)

        y = jnp.dot(hidden(w1a_ref, w3a_ref, gates[:, 0:1]), w2a_ref[...], preferred_element_type=F32)
        y += jnp.dot(hidden(w1b_ref, w3b_ref, gates[:, 1:2]), w2b_ref[...], preferred_element_type=F32)
        yp = _pack_rows(y)
        for j in range(Y_ROWS):
            y_ref[pl.ds(j, m, stride=Y_ROWS), :] = yp[:, j * LANE:(j + 1) * LANE]
        if m < bm:
            y_ref[m * Y_ROWS:, :] = jnp.zeros(((bm - m) * Y_ROWS, LANE), y_ref.dtype)

    @pl.when(nv > bm // 2)
    def _():
        experts(bm)

    @pl.when(jnp.logical_and(nv > 0, nv <= bm // 2))
    def _():
        experts(bm // 2)

    @pl.when(nv == 0)
    def _():
        y_ref[...] = jnp.zeros_like(y_ref)


def _moe(xs, w1, w3, w2, plan):
    n_blk = xs.shape[0] // (MOE_BM * X_ROWS)
    _, d, de = w1.shape
    assert d == 2 * Y_ROWS * LANE

    def expert(second, rows, cols):
        if second:
            return pl.BlockSpec((None, rows, cols), lambda i, src, ea, eb, nv, dst: (eb[i], 0, 0))
        return pl.BlockSpec((None, rows, cols), lambda i, src, ea, eb, nv, dst: (ea[i], 0, 0))

    grid_spec = pltpu.PrefetchScalarGridSpec(
        num_scalar_prefetch=5,
        grid=(n_blk,),
        in_specs=[pl.BlockSpec((MOE_BM * X_ROWS, LANE), lambda i, src, ea, eb, nv, dst: (src[i], 0)),
                  expert(False, d, de), expert(False, d, de), expert(False, de, d),
                  expert(True, d, de), expert(True, d, de), expert(True, de, d)],
        out_specs=pl.BlockSpec((MOE_BM * Y_ROWS, LANE), lambda i, src, ea, eb, nv, dst: (dst[i], 0)),
    )
    return pl.pallas_call(
        _moe_kernel,
        grid_spec=grid_spec,
        out_shape=jax.ShapeDtypeStruct((n_blk * MOE_BM * Y_ROWS, LANE), U32),
        compiler_params=_cparams(("arbitrary",), 48),
        name="moe",
    )(*plan, xs, w1, w3, w2, w1, w3, w2)


def _combine_kernel(npt, pos_ref, nxt_ref, xn_ref, mod_ref, yb_ref, op_ref, os_ref, buf, sem):
    i = pl.program_id(0)
    out = xn_ref[...] + mod_ref[5:6, :] * _gathered_rows(pos_ref, nxt_ref, yb_ref, buf, sem)

    @pl.when(i < npt)
    def _():
        op_ref[...] = out

    @pl.when(i >= npt)
    def _():
        os_ref[...] = out


def _combine(xn, yb, pos, mod_l, tp, tm, mod_index):
    t, d = xn.shape
    npt = tp // tm
    n = t // tm
    pos3 = pos.reshape(n, 1, tm)
    return pl.pallas_call(
        functools.partial(_combine_kernel, npt),
        grid=(n,),
        in_specs=[pl.BlockSpec((1, 1, tm), lambda i: (i, 0, 0), memory_space=pltpu.SMEM),
                  pl.BlockSpec((1, 1, tm), lambda i: (jnp.minimum(i + 1, n - 1), 0, 0), memory_space=pltpu.SMEM),
                  pl.BlockSpec((tm, d), lambda i: (i, 0)),
                  pl.BlockSpec((None, 6, d), lambda i: (mod_index(i, tm), 0, 0)),
                  pl.BlockSpec(memory_space=pl.ANY)],
        out_specs=[pl.BlockSpec((tm, d), lambda i: (jnp.minimum(i, npt - 1), 0)),
                   pl.BlockSpec((tm, d), lambda i: (jnp.maximum(i - npt, 0), 0))],
        out_shape=[jax.ShapeDtypeStruct((tp, d), F32),
                   jax.ShapeDtypeStruct((t - tp, d), F32)],
        scratch_shapes=[pltpu.VMEM((2, tm * Y_ROWS, LANE), U32), pltpu.SemaphoreType.DMA((2,))],
        compiler_params=_cparams(("arbitrary",), 48),
        name="combine",
    )(pos3, pos3, xn, mod_l, yb)


def _pad_cols(x, n):
    return jnp.pad(x, ((0, 0), (0, n - x.shape[1])))


def _rope_tables(dec_seq, tm):
    n_freq = MLA_ROPE // 4
    pos = jnp.arange(dec_seq, dtype=jnp.int32)
    row = (pos // GRID_W).astype(F32)
    col = (pos % GRID_W).astype(F32)
    inv = ROPE_THETA ** (-jnp.arange(n_freq, dtype=F32) / n_freq)
    ar = row[:, None] * inv[None, :]
    ac = col[:, None] * inv[None, :]
    ones = jnp.ones((dec_seq, LANE - MLA_ROPE), F32)
    cos = jnp.concatenate([jnp.cos(ar), jnp.cos(ar), jnp.cos(ac), jnp.cos(ac), ones], axis=1)
    sin = jnp.concatenate([-jnp.sin(ar), jnp.sin(ar), -jnp.sin(ac), jnp.sin(ac), 0.0 * ones], axis=1)
    ident_c = jnp.ones((tm, LANE), F32)
    ident_s = jnp.zeros((tm, LANE), F32)
    return jnp.concatenate([cos, ident_c], axis=0), jnp.concatenate([sin, ident_s], axis=0)


def _layer_params(l, q_lora, kv_lora, w_in, g_q_lora, w_uq, g_kv_lora, w_ukv, g_qk_q, g_qk_k,
                  w_dw31, b_dw31, g_conv_ln, b_conv_ln, w_pw, b_pw, g_sgu_ln, b_sgu_ln,
                  w_spatial, b_spatial, w_sc3, w_out):
    d = w_in.shape[1]
    o1 = q_lora
    o2 = o1 + kv_lora
    o3 = o2 + MLA_ROPE
    o4 = o3 + 1024
    o5 = o4 + 1024
    w = w_in[l]
    zeros = lambda n: jnp.zeros((d, n), w.dtype)
    w_in_p = jnp.concatenate(
        [w[:, o3:o4], w[:, o4:o5], w[:, o5 + 512:o5 + 1536], w[:, o5:o5 + 512],
         w[:, :o1], zeros(512 - q_lora),
         w[:, o1:o2], w[:, o2:o3], zeros(PROJ_W - KV_COL - kv_lora - MLA_ROPE)], axis=1).astype(BF16)

    uq = w_uq[l].reshape(q_lora, MLA_HEADS, MLA_QK)
    uq = jnp.pad(uq, ((0, 512 - q_lora), (0, 0), (0, HEAD_PAD - MLA_QK)))
    w_uq_p = uq.reshape(512, MLA_HEADS * HEAD_PAD).astype(BF16)
    ukv = w_ukv[l].reshape(kv_lora, MLA_HEADS, MLA_NOPE + MLA_V)
    w_ukv_p = jnp.concatenate([ukv[:, :, :MLA_NOPE].reshape(kv_lora, -1),
                               ukv[:, :, MLA_NOPE:].reshape(kv_lora, -1)], axis=1).astype(BF16)
    row = lambda v: v.reshape(1, -1).astype(F32)
    qkv_w = (w_uq_p, w_ukv_p, _pad_cols(row(g_q_lora[l]), 512), row(g_kv_lora[l]),
             _pad_cols(row(g_qk_q[l]), HEAD_PAD), _pad_cols(row(g_qk_k[l]), HEAD_PAD))

    hc = w_pw.shape[1] // GMLP_HEADS
    bsp = jnp.repeat(b_spatial[l].T.astype(F32), hc, axis=1)
    w_dw = jnp.pad(w_dw31[l].astype(F32), ((0, 32 - CONV_K), (0, 0)))
    w_dw = w_dw.reshape(32, -1, LANE).transpose(1, 0, 2)
    mix_w = (w_dw, row(b_dw31[l]),
             row(g_conv_ln[l]), row(b_conv_ln[l]), w_pw[l].astype(BF16), row(b_pw[l]),
             row(g_sgu_ln[l]), row(b_sgu_ln[l]), w_spatial[l].astype(BF16), bsp,
             jnp.pad(w_sc3[l].astype(F32), ((0, 8 - SC_K), (0, 0))))
    return w_in_p, qkv_w, mix_w, w_out[l].astype(BF16)


def _tile(limit, *sizes):
    tm = limit
    while any(s % tm for s in sizes):
        tm //= 2
    return tm


def kernel(x_prompt, x_sample, cache_mla_ckv, cache_mla_krope, c, c_ctx, w_mod, b_mod, g_norm1, g_norm2, w_in, g_q_lora, w_uq, g_kv_lora, w_ukv, g_qk_q, g_qk_k, w_dw31, b_dw31, g_conv_ln, b_conv_ln, w_pw, b_pw, g_sgu_ln, b_sgu_ln, w_spatial, b_spatial, w_sc3, w_out, router_w, router_b, w1, w3, w2):
    nseq, seq, d = x_prompt.shape
    nb, dec_seq, _ = x_sample.shape
    depth = w_mod.shape[0]
    past = cache_mla_ckv.shape[2]
    q_lora = g_q_lora.shape[1]
    kv_lora = g_kv_lora.shape[1]
    n_exp = router_w.shape[1]
    tp = nseq * seq
    ts = nb * dec_seq
    t = tp + ts
    assert tp % dec_seq == 0 and seq % CHUNK == 0 and dec_seq % GRID_W == 0
    assert nb + 1 <= 8 and kv_lora == MLA_NOPE and n_exp == N_GROUPS * GROUP_SIZE

    tm_in = _tile(512, tp, dec_seq)
    tm_qkv = _tile(512, tp, dec_seq)
    tm_big = _tile(1024, tp, dec_seq)
    tm_row = _tile(256, seq, dec_seq)
    tq = _tile(1024, dec_seq)

    def mod_index(i, tm):
        npt = tp // tm
        return jnp.where(i < npt, 0, 1 + (i - npt) // (dec_seq // tm))

    cvec = jnp.concatenate([c_ctx[None, :], c, jnp.zeros((8 - 1 - nb, d), F32)], axis=0)
    mod = _modulation(cvec, w_mod, b_mod).reshape(depth, 8, 6, d)
    cos_t, sin_t = _rope_tables(dec_seq, tm_big)

    rw = router_w.T.reshape(N_GROUPS, GROUP_SIZE, d).transpose(1, 0, 2).reshape(n_exp, d)
    rw_hi = rw.astype(BF16)
    rw_lo = (rw - rw_hi.astype(F32)).astype(BF16)
    wr = jnp.pad(jnp.concatenate([rw_hi, rw_lo], axis=0), ((0, LANE - 2 * n_exp), (0, 0)))
    n_blk = (t + N_GROUPS * N_PAIRS * (MOE_BM - 1) + MOE_BM - 1) // MOE_BM

    xp = x_prompt.reshape(tp, d)
    xs = x_sample.reshape(ts, d)
    ckv_states = []
    krope_states = []
    for l in range(depth):
        w_in_p, qkv_w, mix_w, w_out_b = _layer_params(
            l, q_lora, kv_lora, w_in, g_q_lora, w_uq, g_kv_lora, w_ukv, g_qk_q, g_qk_k,
            w_dw31, b_dw31, g_conv_ln, b_conv_ln, w_pw, b_pw, g_sgu_ln, b_sgu_ln,
            w_spatial, b_spatial, w_sc3, w_out)
        mod_l = mod[l]

        if l == 0:
            proj, kv = _in_proj(xp, xs, mod_l, g_norm1[l].reshape(1, d), w_in_p, tm_in, mod_index)
        else:
            proj, kv, xp = _in_proj_fused(xn, yb, pos, mod[l - 1], mod_l, g_norm1[l].reshape(1, d), w_in_p,
                                          tm_in, mod_index)
            xs = None
        q, k, v, state = _qkv(proj, kv, cos_t, sin_t, *qkv_w, tp, dec_seq, q_lora, tm_big)
        kvc = jnp.concatenate([cache_mla_ckv[:, l], cache_mla_krope[:, l],
                               jnp.zeros((nb, past, KV_W - kv_lora - MLA_ROPE), F32)], axis=-1)
        kc, vc = _ctx_kv(kvc.reshape(nb * past, KV_W), qkv_w[1], qkv_w[5], _tile(512, past))
        o_as, w1b, w3b, w2b = _attention_latent(q, k, v, kc, vc, w1, w3, w2, l, tp, nb, dec_seq, past, tq)
        o_ap = _attention_context(q, k, v, nseq, seq)
        o_bcd = _mixers(proj, mix_w, tp, seq, dec_seq, tm_row)

        xn, hp, r = _out_proj(o_ap, o_as, o_bcd, xp, xs, mod_l, g_norm2[l].reshape(1, d), w_out_b, wr,
                              tm_qkv, mod_index)
        pos, gates, plan, fill = _route(r, router_b, n_blk, tm_big)
        xd = _dispatch(hp, gates, pos, fill, tm_big)
        yb = _moe(xd, w1b, w3b, w2b, plan)
        if l == depth - 1:
            xp, xs = _combine(xn, yb, pos, mod_l, tp, tm_qkv, mod_index)

        ckv_states.append(state[:, :kv_lora].reshape(nseq, seq, kv_lora))
        krope_states.append(state[:, kv_lora:kv_lora + MLA_ROPE].reshape(nseq, seq, MLA_ROPE))

    return (xp.reshape(nseq, seq, d), xs.reshape(nb, dec_seq, d),
            jnp.stack(ckv_states, axis=1), jnp.stack(krope_states, axis=1))
```
